```python
import math
import jax
import jax.numpy as jnp
from jax import lax
import numpy as np

D_MODEL = 1024
BATCH = 2
SEQ = 8192
DEPTH = 2

GRID_W = 64
CTX_LEN = 256
EPS = 1e-6

HEAD_DIM = 128
ATT_HEADS = 8
ATT_KV_HEADS = 2
GQA_REP = ATT_HEADS // ATT_KV_HEADS
ATT_WIDTH = ATT_HEADS * HEAD_DIM
KV_WIDTH = ATT_KV_HEADS * HEAD_DIM
ATT_SCALE = HEAD_DIM ** -0.5
ROPE_THETA = 10000.0
Q_BLOCK = 128
WINDOW = 128

SSD_HEADS = 16
SSD_HEAD_DIM = 64
SSD_WIDTH = SSD_HEADS * SSD_HEAD_DIM
SSD_GROUPS = 2
SSD_STATE = 64
SSD_CHUNK = 128
SSD_CONV_CH = SSD_WIDTH + 2 * SSD_GROUPS * SSD_STATE

LRU_WIDTH = 1024
LRU_BLOCKS = 16
LRU_BLOCK_DIM = LRU_WIDTH // LRU_BLOCKS
LRU_C = 8.0

CONV_W = 5

MIX_WIDTH = SSD_WIDTH + ATT_WIDTH
AB_SPLITS = (SSD_WIDTH,
             SSD_WIDTH + SSD_CONV_CH,
             SSD_WIDTH + SSD_CONV_CH + 2 * SSD_HEADS,
             SSD_WIDTH + SSD_CONV_CH + 2 * SSD_HEADS + ATT_WIDTH,
             SSD_WIDTH + SSD_CONV_CH + 2 * SSD_HEADS + ATT_WIDTH + KV_WIDTH)
AB_IN = AB_SPLITS[-1] + KV_WIDTH
CD_SPLITS = (LRU_WIDTH, 2 * LRU_WIDTH, 2 * LRU_WIDTH + ATT_WIDTH, 2 * LRU_WIDTH + ATT_WIDTH + KV_WIDTH)
CD_IN = CD_SPLITS[-1] + KV_WIDTH

MOE_GROUPS = 4
EXPERTS_PER_GROUP = 4
N_EXPERTS = MOE_GROUPS * EXPERTS_PER_GROUP
TOP_K = 2
D_EXPERT = 256

N_EVEN = (DEPTH + 1) // 2
N_ODD = DEPTH // 2

kernel_name = 'hybrid_ssd_gqa_rglru_swa_hmoe_dit'

F32 = jnp.float32


def rmsnorm(x, g):
    xf = x.astype(F32)
    y = xf * lax.rsqrt(jnp.mean(xf * xf, axis=-1, keepdims=True) + EPS)
    return (y * g.astype(F32)).astype(x.dtype)


def dwconv_centred(x, w, b):
    pad = CONV_W // 2
    y = lax.conv_general_dilated(x, w[:, None, :].astype(x.dtype), window_strides=(1,),
                                 padding=[(pad, pad)], dimension_numbers=('NWC', 'WIO', 'NWC'),
                                 feature_group_count=x.shape[-1])
    return y + b


def axial_rope(L):
    rows = L // GRID_W
    row = jnp.repeat(jnp.arange(rows), GRID_W).astype(F32)
    col = jnp.tile(jnp.arange(GRID_W), rows).astype(F32)
    n_freq = HEAD_DIM // 4
    inv = ROPE_THETA ** (-jnp.arange(n_freq, dtype=F32) / n_freq)
    ang = jnp.concatenate([row[:, None] * inv, col[:, None] * inv], axis=-1)
    return jnp.cos(ang), jnp.sin(ang)


def apply_rope(t, cos, sin):
    t1, t2 = jnp.split(t, 2, axis=-1)
    c = cos[None, :, None, :].astype(t.dtype)
    s = sin[None, :, None, :].astype(t.dtype)
    return jnp.concatenate([t1 * c - t2 * s, t1 * s + t2 * c], axis=-1)


def attend_full(q, k, v, sink=None):
    Bsz, Lq = q.shape[:2]
    qg = q.reshape(Bsz, Lq, ATT_KV_HEADS, GQA_REP, HEAD_DIM)
    s = jnp.einsum('bqgrd,bkgd->bgrqk', qg, k, preferred_element_type=F32) * ATT_SCALE
    if sink is not None:
        col = sink.astype(F32).reshape(ATT_KV_HEADS, GQA_REP)[None, :, :, None, None]
        s = jnp.concatenate([s, jnp.broadcast_to(col, s.shape[:-1] + (1,))], axis=-1)
    p = jax.nn.softmax(s, axis=-1)
    if sink is not None:
        p = p[..., :-1]
    o = jnp.einsum('bgrqk,bkgd->bqgrd', p.astype(v.dtype), v)
    return o.reshape(Bsz, Lq, ATT_WIDTH)


def attend_blocked(q, k, v, kc, vc):
    Bsz, L = q.shape[:2]
    nb = L // Q_BLOCK
    keys = jnp.concatenate([kc, k], axis=1)
    vals = jnp.concatenate([vc, v], axis=1)
    qb = jnp.moveaxis(q.reshape(Bsz, nb, Q_BLOCK, ATT_HEADS, HEAD_DIM), 1, 0)
    o = lax.map(lambda qi: attend_full(qi, keys, vals), qb)
    return jnp.moveaxis(o, 0, 1).reshape(Bsz, L, ATT_WIDTH)


def attend_window_sink(q, k, v, kc, vc, sink):
    Bsz, L = q.shape[:2]
    nb = L // Q_BLOCK
    Q = Q_BLOCK
    qb = q.reshape(Bsz, nb, Q, ATT_KV_HEADS, GQA_REP, HEAD_DIM)

    def band(t):
        tp = jnp.pad(t, ((0, 0), (Q, Q), (0, 0), (0, 0))).reshape(Bsz, nb + 2, Q, ATT_KV_HEADS, HEAD_DIM)
        return jnp.concatenate([tp[:, :-2], tp[:, 1:-1], tp[:, 2:]], axis=2)

    kb, vb = band(k), band(v)
    qpos = jnp.arange(L).reshape(nb, Q)
    kpos = (jnp.arange(nb)[:, None] - 1) * Q + jnp.arange(3 * Q)[None, :]
    valid = ((jnp.abs(qpos[:, :, None] - kpos[:, None, :]) <= WINDOW)
             & (kpos >= 0)[:, None, :] & (kpos < L)[:, None, :])
    s_loc = jnp.einsum('bnqgrd,bnkgd->bngrqk', qb, kb, preferred_element_type=F32) * ATT_SCALE
    s_loc = jnp.where(valid[None, :, None, None], s_loc, -jnp.inf)
    s_ctx = jnp.einsum('bnqgrd,bkgd->bngrqk', qb, kc, preferred_element_type=F32) * ATT_SCALE
    col = sink.astype(F32).reshape(ATT_KV_HEADS, GQA_REP)[None, None, :, :, None, None]
    s_sink = jnp.broadcast_to(col, s_loc.shape[:-1] + (1,))
    p = jax.nn.softmax(jnp.concatenate([s_loc, s_ctx, s_sink], axis=-1), axis=-1).astype(v.dtype)
    n_ctx = kc.shape[1]
    o = (jnp.einsum('bngrqk,bnkgd->bnqgrd', p[..., :3 * Q], vb)
         + jnp.einsum('bngrqk,bkgd->bnqgrd', p[..., 3 * Q:3 * Q + n_ctx], vc))
    return o.reshape(Bsz, L, ATT_WIDTH)


def ssd_scan(xs, dt, a_neg, bm, cm, h0):
    Bsz, L, H, P = xs.shape
    G, N = bm.shape[2], bm.shape[3]
    R = H // G
    Q = SSD_CHUNK
    nc = L // Q
    x = xs.astype(F32).reshape(Bsz, nc, Q, G, R, P)
    dt = dt.astype(F32).reshape(Bsz, nc, Q, G, R)
    bc = bm.astype(F32).reshape(Bsz, nc, Q, G, N)
    cc = cm.astype(F32).reshape(Bsz, nc, Q, G, N)
    acs = jnp.cumsum(dt * a_neg.reshape(G, R), axis=2)
    tri = jnp.tril(jnp.ones((Q, Q), dtype=bool))
    seg = acs[:, :, :, None] - acs[:, :, None, :]
    decay = jnp.exp(jnp.where(tri[:, :, None, None], seg, -jnp.inf))
    cb = jnp.einsum('bcign,bcjgn->bcijg', cc, bc)
    w = cb[..., None] * decay * dt[:, :, None]
    y_diag = jnp.einsum('bcijgr,bcjgrp->bcigrp', w, x)
    to_end = jnp.exp(acs[:, :, -1:] - acs) * dt
    states = jnp.einsum('bcjgn,bcjgr,bcjgrp->bcgrpn', bc, to_end, x)
    chunk_decay = jnp.exp(acs[:, :, -1])

    def step(h, inp):
        s_c, d_c = inp
        return d_c[..., None, None] * h + s_c, h

    h_last, h_prev = lax.scan(step, h0.astype(F32).reshape(Bsz, G, R, P, N),
                              (jnp.moveaxis(states, 1, 0), jnp.moveaxis(chunk_decay, 1, 0)))
    h_prev = jnp.moveaxis(h_prev, 0, 1)
    y_off = jnp.einsum('bcign,bcgrpn,bcigr->bcigrp', cc, h_prev, jnp.exp(acs))
    y = (y_diag + y_off).reshape(Bsz, L, H, P)
    return y, h_last.reshape(Bsz, H, P, N)


def flip_seq(t):
    return jnp.flip(t, axis=1)


def bidir_ssd(xs, dt, bm, cm, a_neg, h0f, h0b):
    yf, hf = ssd_scan(xs, dt[:, :, 0], a_neg[0], bm, cm, h0f)
    yb, hb = ssd_scan(flip_seq(xs), flip_seq(dt[:, :, 1]), a_neg[1], flip_seq(bm), flip_seq(cm), h0b)
    return yf + flip_seq(yb), hf, hb


def linear_scan(a, b, h0):
    def combine(left, right):
        return left[0] * right[0], right[0] * left[1] + right[1]
    a_cum, b_cum = lax.associative_scan(combine, (a, b), axis=1)
    h = a_cum * h0[:, None] + b_cum
    return h, h[:, -1]


def ssd_attn_mixer(hl, hc, cos, sin, w_in, conv_w, conv_b, dt_bias, a_log, d_skip, norm_g, q_g, k_g, ctx_out):
    a_neg = -jnp.exp(a_log.astype(F32))

    def project(h, rope):
        Bsz, L, _ = h.shape
        z, xbc, dt, q, k, v = jnp.split(h @ w_in, AB_SPLITS, axis=-1)
        xbc = jax.nn.silu(dwconv_centred(xbc, conv_w, conv_b))
        xs, bm, cm = jnp.split(xbc, [SSD_WIDTH, SSD_WIDTH + SSD_GROUPS * SSD_STATE], axis=-1)
        xs = xs.reshape(Bsz, L, SSD_HEADS, SSD_HEAD_DIM)
        bm = bm.reshape(Bsz, L, SSD_GROUPS, SSD_STATE)
        cm = cm.reshape(Bsz, L, SSD_GROUPS, SSD_STATE)
        dt = jax.nn.softplus(dt.reshape(Bsz, L, 2, SSD_HEADS).astype(F32) + dt_bias.astype(F32))
        q = rmsnorm(q.reshape(Bsz, L, ATT_HEADS, HEAD_DIM), q_g)
        k = rmsnorm(k.reshape(Bsz, L, ATT_KV_HEADS, HEAD_DIM), k_g)
        v = v.reshape(Bsz, L, ATT_KV_HEADS, HEAD_DIM)
        if rope:
            q, k = apply_rope(q, cos, sin), apply_rope(k, cos, sin)
        return z, xs, bm, cm, dt, q, k, v

    def ssd_out(y, xs, z):
        Bsz, L = xs.shape[:2]
        y = (y + d_skip.astype(F32)[:, None] * xs.astype(F32)).reshape(Bsz, L, SSD_WIDTH).astype(z.dtype)
        return rmsnorm(y * jax.nn.silu(z), norm_g)

    zc, xsc, bc, cc, dtc, qc, kc, vc = project(hc, False)
    zl, xsl, bl, cl, dtl, ql, kl, vl = project(hl, True)
    h0 = jnp.zeros((hl.shape[0], SSD_HEADS, SSD_HEAD_DIM, SSD_STATE), F32)
    yc_s, hf, hb = bidir_ssd(xsc, dtc, bc, cc, a_neg, h0, h0)
    yl_s, _, _ = bidir_ssd(xsl, dtl, bl, cl, a_neg, hf, hb)
    y_lat = jnp.concatenate([ssd_out(yl_s, xsl, zl), attend_blocked(ql, kl, vl, kc, vc)], axis=-1)
    y_ctx = None
    if ctx_out:
        y_ctx = jnp.concatenate([ssd_out(yc_s, xsc, zc), attend_full(qc, kc, vc)], axis=-1)
    return y_lat, y_ctx


def lru_swa_mixer(hl, hc, cos, sin, w_in, conv_w, conv_b, w_a, b_a, w_x, b_x, lam, sink, ctx_out):
    log_base = -LRU_C * jax.nn.softplus(-lam.astype(F32))

    def project(h, rope):
        Bsz, L, _ = h.shape
        gate, xr, q, k, v = jnp.split(h @ w_in, CD_SPLITS, axis=-1)
        xr = dwconv_centred(xr, conv_w, conv_b)
        q = q.reshape(Bsz, L, ATT_HEADS, HEAD_DIM)
        k = k.reshape(Bsz, L, ATT_KV_HEADS, HEAD_DIM)
        v = v.reshape(Bsz, L, ATT_KV_HEADS, HEAD_DIM)
        if rope:
            q, k = apply_rope(q, cos, sin), apply_rope(k, cos, sin)
        return gate, xr, q, k, v

    def bidir_lru(xr, h0f, h0b):
        Bsz, L, _ = xr.shape
        xf = xr.astype(F32)
        xb = xf.reshape(Bsz, L, LRU_BLOCKS, LRU_BLOCK_DIM)
        r = jax.nn.sigmoid(jnp.einsum('blhi,dhij->dblhj', xb, w_a.astype(F32)).reshape(2, Bsz, L, LRU_WIDTH)
                           + b_a.astype(F32)[:, None, None])
        ig = jax.nn.sigmoid(jnp.einsum('blhi,dhij->dblhj', xb, w_x.astype(F32)).reshape(2, Bsz, L, LRU_WIDTH)
                            + b_x.astype(F32)[:, None, None])
        log_a = r * log_base[:, None, None]
        a = jnp.exp(log_a)
        b = jnp.sqrt(-jnp.expm1(2.0 * log_a)) * (ig * xf)
        hf, lf = linear_scan(a[0], b[0], h0f)
        hb, lb = linear_scan(flip_seq(a[1]), flip_seq(b[1]), h0b)
        return hf + flip_seq(hb), lf, lb

    gc, xrc, qc, kc, vc = project(hc, False)
    gl, xrl, ql, kl, vl = project(hl, True)
    h0 = jnp.zeros((hl.shape[0], LRU_WIDTH), F32)
    yc_r, hf, hb = bidir_lru(xrc, h0, h0)
    yl_r, _, _ = bidir_lru(xrl, hf, hb)
    y_lat = jnp.concatenate([jax.nn.gelu(gl) * yl_r.astype(gl.dtype),
                             attend_window_sink(ql, kl, vl, kc, vc, sink)], axis=-1)
    y_ctx = None
    if ctx_out:
        y_ctx = jnp.concatenate([jax.nn.gelu(gc) * yc_r.astype(gc.dtype),
                                 attend_full(qc, kc, vc, sink)], axis=-1)
    return y_lat, y_ctx


def hier_moe(h, w_grp, b_grp, w_rt, b_rt, w1, w3, w2):
    Bsz, T, _ = h.shape
    gl = jnp.einsum('btd,dg->btg', h, w_grp, preferred_element_type=F32) + b_grp.astype(F32)
    pg = jax.nn.softmax(gl, axis=-1)
    g_idx = jnp.argmax(gl, axis=-1)
    p_grp = jnp.take_along_axis(pg, g_idx[..., None], axis=-1)
    el = (jnp.einsum('btd,de->bte', h, w_rt, preferred_element_type=F32) + b_rt.astype(F32))
    el = el.reshape(Bsz, T, MOE_GROUPS, EXPERTS_PER_GROUP)
    el_g = jnp.take_along_axis(el, g_idx[..., None, None], axis=2)[:, :, 0]
    top_v, top_i = lax.top_k(el_g, TOP_K)
    w_top = jax.nn.softmax(top_v, axis=-1) * p_grp
    eid = g_idx[..., None] * EXPERTS_PER_GROUP + top_i
    gates = jnp.sum(jax.nn.one_hot(eid, N_EXPERTS, dtype=F32) * w_top[..., None], axis=2)
    a = jnp.einsum('btd,edf->btef', h, w1)
    u = jnp.einsum('btd,edf->btef', h, w3)
    hid = jax.nn.silu(a) * u * gates[..., None].astype(h.dtype)
    return jnp.einsum('btef,efd->btd', hid, w2)


def setup_inputs(seed: int = 0) -> dict:
    key = jax.random.key(seed)
    ks = list(jax.random.split(key, 40))
    cnt = [0]

    def nk():
        cnt[0] += 1
        return ks[cnt[0] - 1]

    def nrm(shape, scale):
        return scale * jax.random.normal(nk(), shape, F32)

    def gain(shape):
        return 1.0 + nrm(shape, 0.02)

    D = D_MODEL
    dt0 = jnp.exp(jax.random.uniform(nk(), (N_EVEN, 2, SSD_HEADS), F32, math.log(1e-3), math.log(1e-1)))
    ssd_dt_bias = dt0 + jnp.log(-jnp.expm1(-dt0))
    ssd_a_log = jnp.log(jax.random.uniform(nk(), (N_EVEN, 2, SSD_HEADS), F32, 1.0, 16.0))
    a0 = jax.random.uniform(nk(), (N_ODD, 2, LRU_WIDTH), F32, 0.9, 0.999)
    sg = a0 ** (1.0 / LRU_C)
    lru_lam = jnp.log(sg) - jnp.log1p(-sg)
    return {
        'x': nrm((BATCH, SEQ, D), 1.0),
        'c': nrm((BATCH, D), 1.0),
        'ctx': nrm((BATCH, CTX_LEN, D), 1.0),
        'c_ctx': nrm((D,), 1.0),
        'w_mod': nrm((DEPTH, D, 6 * D), 0.5 * D ** -0.5),
        'b_mod': nrm((DEPTH, 6 * D), 0.01),
        'g_mix': gain((DEPTH, D)),
        'g_ffn': gain((DEPTH, D)),
        'moe_w_grp': nrm((DEPTH, D, MOE_GROUPS), D ** -0.5),
        'moe_b_grp': nrm((DEPTH, MOE_GROUPS), 0.01),
        'moe_w_rt': nrm((DEPTH, D, N_EXPERTS), D ** -0.5),
        'moe_b_rt': nrm((DEPTH, N_EXPERTS), 0.01),
        'moe_w1': nrm((DEPTH, N_EXPERTS, D, D_EXPERT), D ** -0.5),
        'moe_w3': nrm((DEPTH, N_EXPERTS, D, D_EXPERT), D ** -0.5),
        'moe_w2': nrm((DEPTH, N_EXPERTS, D_EXPERT, D), D_EXPERT ** -0.5),
        'ab_w_in': nrm((N_EVEN, D, AB_IN), D ** -0.5),
        'ab_w_out': nrm((N_EVEN, MIX_WIDTH, D), MIX_WIDTH ** -0.5),
        'ssd_conv_w': nrm((N_EVEN, CONV_W, SSD_CONV_CH), CONV_W ** -0.5),
        'ssd_conv_b': nrm((N_EVEN, SSD_CONV_CH), 0.01),
        'ssd_dt_bias': ssd_dt_bias,
        'ssd_a_log': ssd_a_log,
        'ssd_d': gain((N_EVEN, SSD_HEADS)),
        'ssd_norm_g': gain((N_EVEN, SSD_WIDTH)),
        'att_q_g': gain((N_EVEN, HEAD_DIM)),
        'att_k_g': gain((N_EVEN, HEAD_DIM)),
        'cd_w_in': nrm((N_ODD, D, CD_IN), D ** -0.5),
        'cd_w_out': nrm((N_ODD, MIX_WIDTH, D), MIX_WIDTH ** -0.5),
        'lru_conv_w': nrm((N_ODD, CONV_W, LRU_WIDTH), CONV_W ** -0.5),
        'lru_conv_b': nrm((N_ODD, LRU_WIDTH), 0.01),
        'lru_w_a': nrm((N_ODD, 2, LRU_BLOCKS, LRU_BLOCK_DIM, LRU_BLOCK_DIM), LRU_BLOCK_DIM ** -0.5),
        'lru_b_a': nrm((N_ODD, 2, LRU_WIDTH), 0.01),
        'lru_w_x': nrm((N_ODD, 2, LRU_BLOCKS, LRU_BLOCK_DIM, LRU_BLOCK_DIM), LRU_BLOCK_DIM ** -0.5),
        'lru_b_x': nrm((N_ODD, 2, LRU_WIDTH), 0.01),
        'lru_lam': lru_lam,
        'swa_sink': nrm((N_ODD, ATT_HEADS), 0.5),
        'g_final': gain((D,)),
    }


def reference(x, c, ctx, c_ctx, w_mod, b_mod, g_mix, g_ffn, moe_w_grp, moe_b_grp, moe_w_rt, moe_b_rt,
              moe_w1, moe_w3, moe_w2, ab_w_in, ab_w_out, ssd_conv_w, ssd_conv_b, ssd_dt_bias, ssd_a_log,
              ssd_d, ssd_norm_g, att_q_g, att_k_g, cd_w_in, cd_w_out, lru_conv_w, lru_conv_b, lru_w_a,
              lru_b_a, lru_w_x, lru_b_x, lru_lam, swa_sink, g_final):
    L = x.shape[1]
    cos, sin = axial_rope(L)
    c_act = jax.nn.silu(c)[:, None, :]
    cc_act = jax.nn.silu(c_ctx)[None, None, :]
    xc = ctx
    for i in range(DEPTH):
        last = i == DEPTH - 1
        j = i // 2
        sh1, sc1, g1, sh2, sc2, g2 = jnp.split(c_act @ w_mod[i] + b_mod[i], 6, axis=-1)
        ch1, cs1, cg1, ch2, cs2, cg2 = jnp.split(cc_act @ w_mod[i] + b_mod[i], 6, axis=-1)
        hl = rmsnorm(x, g_mix[i]) * (1.0 + sc1) + sh1
        hc = rmsnorm(xc, g_mix[i]) * (1.0 + cs1) + ch1
        if i % 2 == 0:
            yl, yc = ssd_attn_mixer(hl, hc, cos, sin, ab_w_in[j], ssd_conv_w[j], ssd_conv_b[j], ssd_dt_bias[j],
                                    ssd_a_log[j], ssd_d[j], ssd_norm_g[j], att_q_g[j], att_k_g[j], not last)
            w_out = ab_w_out[j]
        else:
            yl, yc = lru_swa_mixer(hl, hc, cos, sin, cd_w_in[j], lru_conv_w[j], lru_conv_b[j], lru_w_a[j],
                                   lru_b_a[j], lru_w_x[j], lru_b_x[j], lru_lam[j], swa_sink[j], not last)
            w_out = cd_w_out[j]
        moe = (moe_w_grp[i], moe_b_grp[i], moe_w_rt[i], moe_b_rt[i], moe_w1[i], moe_w3[i], moe_w2[i])
        x = x + g1 * (yl @ w_out)
        x = x + g2 * hier_moe(rmsnorm(x, g_ffn[i]) * (1.0 + sc2) + sh2, *moe)
        if not last:
            xc = xc + cg1 * (yc @ w_out)
            xc = xc + cg2 * hier_moe(rmsnorm(xc, g_ffn[i]) * (1.0 + cs2) + ch2, *moe)
    return rmsnorm(x, g_final)
```

```python
import functools
import math

import jax
import jax.numpy as jnp
from jax import lax
from jax.experimental import pallas as pl
from jax.experimental.pallas import tpu as pltpu

F32 = jnp.float32
BF16 = jnp.bfloat16

D_MODEL = 1024
GRID_W = 64
EPS = 1e-6
HEAD_DIM = 128
ATT_HEADS = 8
ATT_KV_HEADS = 2
GQA_REP = ATT_HEADS // ATT_KV_HEADS
ATT_WIDTH = ATT_HEADS * HEAD_DIM
KV_WIDTH = ATT_KV_HEADS * HEAD_DIM
ATT_SCALE = HEAD_DIM ** -0.5
ROPE_THETA = 10000.0
Q_BLOCK = 128
WINDOW = 128
SSD_HEADS = 16
SSD_HEAD_DIM = 64
SSD_WIDTH = SSD_HEADS * SSD_HEAD_DIM
SSD_GROUPS = 2
SSD_STATE = 64
SSD_CHUNK = 128
SSD_CONV_CH = SSD_WIDTH + 2 * SSD_GROUPS * SSD_STATE
LRU_WIDTH = 1024
LRU_BLOCKS = 16
LRU_BLOCK_DIM = LRU_WIDTH // LRU_BLOCKS
LRU_C = 8.0
CONV_W = 5
MOE_GROUPS = 4
EXPERTS_PER_GROUP = 4
N_EXPERTS = MOE_GROUPS * EXPERTS_PER_GROUP
D_EXPERT = 256

LANES = 128
SUBLANES = 8
LRU_SLAB = 256
VMEM_LIMIT = 56 * 1024 * 1024


def _params(sem):
    return pltpu.CompilerParams(dimension_semantics=sem, vmem_limit_bytes=VMEM_LIMIT)


def _sigmoid(x):
    return 1.0 / (1.0 + jnp.exp(-x))


def _softplus(x):
    return jnp.maximum(x, 0.0) + jnp.log1p(jnp.exp(-jnp.abs(x)))


def _split_bf16(v, n):
    parts = []
    r = v
    for _ in range(n):
        p = r.astype(BF16)
        parts.append(p)
        r = r - p.astype(F32)
    return parts


def _dot_nt(a, b):
    return lax.dot_general(a, b, (((1,), (1,)), ((), ())), preferred_element_type=F32)


def _row_select(row0, tm, ctx_len, mod_ref, k):
    rows = row0 + lax.broadcasted_iota(jnp.int32, (tm, 1), 0)
    return jnp.where(rows < ctx_len, mod_ref[0, 0, k:k + 1, :], mod_ref[0, 1, k:k + 1, :])


def _rms_mod(x, g, row0, ctx_len, mod_ref, k_shift, k_scale):
    tm = x.shape[0]
    y = x * lax.rsqrt(jnp.mean(x * x, axis=-1, keepdims=True) + EPS) * g
    sh = _row_select(row0, tm, ctx_len, mod_ref, k_shift)
    sc = _row_select(row0, tm, ctx_len, mod_ref, k_scale)
    return y * (1.0 + sc) + sh


def _mod_kernel(ct_ref, w_ref, b_ref, o_ref, *, rows):
    w = w_ref[0]
    outs = []
    for r in range(rows):
        c = ct_ref[:, r:r + 1]
        act = c * _sigmoid(c)
        outs.append(jnp.sum(w * act, axis=0, keepdims=True) + b_ref[0])
    o_ref[0] = jnp.concatenate(outs, axis=0)


def _modulation(c, c_ctx, w_mod, b_mod):
    depth, d, n = w_mod.shape
    rows = c.shape[0] + 1
    ct = jnp.concatenate([c_ctx[None, :], c], axis=0).T
    tn = 1536
    return pl.pallas_call(
        functools.partial(_mod_kernel, rows=rows),
        grid=(depth, n // tn),
        in_specs=[pl.BlockSpec((d, rows), lambda i, j: (0, 0)),
                  pl.BlockSpec((1, d, tn), lambda i, j: (i, 0, j)),
                  pl.BlockSpec((1, 1, tn), lambda i, j: (i, 0, j))],
        out_specs=pl.BlockSpec((1, rows, tn), lambda i, j: (i, 0, j)),
        out_shape=jax.ShapeDtypeStruct((depth, rows, n), F32),
        compiler_params=_params(("arbitrary", "arbitrary")),
        name="modulation",
    )(ct, w_mod, b_mod.reshape(depth, 1, n))


def _inproj_kernel(x_ref, g_ref, mod_ref, w_ref, qg_ref, kg_ref, cc_ref, ss_ref, *out_refs,
                   tm, ctx_len, plain, q0, qk_norm):
    i = pl.program_id(1)
    h = _rms_mod(x_ref[0], g_ref[...], i * tm, ctx_len, mod_ref, 0, 1)
    res = jnp.dot(h.astype(BF16), w_ref[...], preferred_element_type=F32)
    for ref, (c0, cw) in zip(out_refs[:len(plain)], plain):
        ref[0] = res[:, c0:c0 + cw]
    q_ref, k_ref, v_ref = out_refs[len(plain):]
    cc = cc_ref[...]
    ss = ss_ref[...]

    def head(t, gain, scale):
        if qk_norm:
            t = t * lax.rsqrt(jnp.mean(t * t, axis=-1, keepdims=True) + EPS) * gain
        t = t * cc + pltpu.roll(t, HEAD_DIM // 2, 1) * ss
        return (t * scale).astype(BF16)

    for hh in range(ATT_HEADS):
        c0 = q0 + hh * HEAD_DIM
        q_ref[0, :, hh * HEAD_DIM:(hh + 1) * HEAD_DIM] = head(res[:, c0:c0 + HEAD_DIM], qg_ref[...], ATT_SCALE)
    k0 = q0 + ATT_WIDTH
    for hh in range(ATT_KV_HEADS):
        c0 = k0 + hh * HEAD_DIM
        k_ref[0, :, hh * HEAD_DIM:(hh + 1) * HEAD_DIM] = head(res[:, c0:c0 + HEAD_DIM], kg_ref[...], 1.0)
    v0 = k0 + KV_WIDTH
    v_ref[0] = res[:, v0:v0 + KV_WIDTH].astype(BF16)


def _inproj(xx, g, mod, w, qg, kg, cc, ss, *, ctx_len, plain, q0, qk_norm, tm):
    bsz, lc, d = xx.shape
    n = w.shape[1]
    nt = lc // tm
    row = lambda b, i: (b, i, 0)
    out_shape = [jax.ShapeDtypeStruct((bsz, lc, cw), F32) for _, cw in plain]
    out_specs = [pl.BlockSpec((1, tm, cw), row) for _, cw in plain]
    for cw in (ATT_WIDTH, KV_WIDTH, KV_WIDTH):
        out_shape.append(jax.ShapeDtypeStruct((bsz, lc, cw), BF16))
        out_specs.append(pl.BlockSpec((1, tm, cw), row))
    return pl.pallas_call(
        functools.partial(_inproj_kernel, tm=tm, ctx_len=ctx_len, plain=plain, q0=q0, qk_norm=qk_norm),
        grid=(bsz, nt),
        in_specs=[pl.BlockSpec((1, tm, d), row),
                  pl.BlockSpec((1, d), lambda b, i: (0, 0)),
                  pl.BlockSpec((1, 2, 6, d), lambda b, i: (b, 0, 0, 0)),
                  pl.BlockSpec((d, n), lambda b, i: (0, 0)),
                  pl.BlockSpec((1, HEAD_DIM), lambda b, i: (0, 0)),
                  pl.BlockSpec((1, HEAD_DIM), lambda b, i: (0, 0)),
                  pl.BlockSpec((tm, HEAD_DIM), lambda b, i: (i, 0)),
                  pl.BlockSpec((tm, HEAD_DIM), lambda b, i: (i, 0))],
        out_specs=out_specs,
        out_shape=out_shape,
        compiler_params=_params(("parallel", "arbitrary")),
        name="inproj",
    )(xx, g, mod, w, qg, kg, cc, ss)


def _conv_kernel(prev_ref, cur_ref, next_ref, w_ref, b_ref, o_ref, *, tm, nt, ct, silu):
    i = pl.program_id(1)
    seg_start = jnp.logical_or(i == 0, i == ct)
    seg_end = jnp.logical_or(i == ct - 1, i == nt - 1)
    prev = jnp.where(seg_start, 0.0, prev_ref[0])
    nxt = jnp.where(seg_end, 0.0, next_ref[0])
    ext = jnp.concatenate([prev, cur_ref[0], nxt], axis=0)
    n = tm + 2 * SUBLANES
    pad = CONV_W // 2
    acc = jnp.broadcast_to(b_ref[...], (tm, ext.shape[1]))
    for k in range(CONV_W):
        sh = (pad - k) % n
        r = ext if sh == 0 else pltpu.roll(ext, sh, 0)
        acc = acc + r[SUBLANES:SUBLANES + tm] * w_ref[k:k + 1, :]
    if silu:
        acc = acc * _sigmoid(acc)
    o_ref[0] = acc


def _dwconv(u, w, b, *, ctx_len, silu, tm=256):
    bsz, lc, c = u.shape
    nt = lc // tm
    ct = ctx_len // tm
    hb = tm // SUBLANES
    nhb = lc // SUBLANES
    return pl.pallas_call(
        functools.partial(_conv_kernel, tm=tm, nt=nt, ct=ct, silu=silu),
        grid=(bsz, nt),
        in_specs=[pl.BlockSpec((1, SUBLANES, c), lambda b_, i: (b_, jnp.maximum(i * hb - 1, 0), 0)),
                  pl.BlockSpec((1, tm, c), lambda b_, i: (b_, i, 0)),
                  pl.BlockSpec((1, SUBLANES, c), lambda b_, i: (b_, jnp.minimum((i + 1) * hb, nhb - 1), 0)),
                  pl.BlockSpec((CONV_W, c), lambda b_, i: (0, 0)),
                  pl.BlockSpec((1, c), lambda b_, i: (0, 0))],
        out_specs=pl.BlockSpec((1, tm, c), lambda b_, i: (b_, i, 0)),
        out_shape=jax.ShapeDtypeStruct((bsz, lc, c), F32),
        compiler_params=_params(("parallel", "arbitrary")),
        name="dwconv",
    )(u, u, u, w, b.reshape(1, c))


def _expand_heads(v, e_ref):
    hi, lo = _split_bf16(v, 2)
    e = e_ref[...]
    return (jnp.dot(hi, e, preferred_element_type=F32) + jnp.dot(lo, e, preferred_element_type=F32))


def _ssd_kernel(xs_ref, bm_ref, cm_ref, dt_ref, dtb_ref, alog_ref, e_ref, y_ref, hst_ref):
    d = pl.program_id(1)
    s = pl.program_id(2)
    q = SSD_CHUNK

    @pl.when(s == 0)
    def _():
        hst_ref[...] = jnp.zeros_like(hst_ref)

    dt = _softplus(dt_ref[0] + dtb_ref[0])
    a_neg = -jnp.exp(alog_ref[0])
    da = dt * a_neg
    ii = lax.broadcasted_iota(jnp.int32, (q, q), 0)
    jj = lax.broadcasted_iota(jnp.int32, (q, q), 1)
    causal = (jj - ii) * jnp.where(d == 0, 1, -1) <= 0
    mask_b = jnp.where(causal, 1.0, 0.0).astype(BF16)
    da_t = da.T
    dt_t = dt.T
    acs = sum(jnp.dot(mask_b, p, preferred_element_type=F32) for p in _split_bf16(da, 3))
    acs_t = sum(_dot_nt(p, mask_b) for p in _split_bf16(da_t, 3))
    total = jnp.sum(da, axis=0, keepdims=True)

    bm = bm_ref[0]
    cm = cm_ref[0]
    bm_b = bm.astype(BF16)
    cm_b = cm.astype(BF16)
    lane = lax.broadcasted_iota(jnp.int32, (1, LANES), 1)
    in_group = [jnp.logical_and(lane >= g * SSD_STATE, lane < (g + 1) * SSD_STATE) for g in range(SSD_GROUPS)]
    cb = [_dot_nt(jnp.where(in_group[g], cm, 0.0).astype(BF16), bm_b) for g in range(SSD_GROUPS)]

    x = xs_ref[0]
    x_b = x.astype(BF16)
    heads_per_group = SSD_HEADS // SSD_GROUPS
    pairs = []
    for hp in range(SSD_HEADS // 2):
        xp = x_b[:, hp * LANES:(hp + 1) * LANES]
        outs = []
        for t in range(2):
            h = 2 * hp + t
            seg = acs[:, h:h + 1] - acs_t[h:h + 1, :]
            dec = jnp.exp(jnp.where(causal, seg, -jnp.inf))
            w = (cb[h // heads_per_group] * dec * dt_t[h:h + 1, :]).astype(BF16)
            outs.append(jnp.dot(w, xp, preferred_element_type=F32))
        pairs.append(jnp.where(lane < SSD_HEAD_DIM, outs[0], outs[1]))
    y_diag = jnp.concatenate(pairs, axis=1)

    h_prev = hst_ref[...]
    y_off = jnp.dot(cm_b, h_prev.astype(BF16), preferred_element_type=F32) * _expand_heads(jnp.exp(acs), e_ref)
    y_ref[0, 0] = y_diag + y_off

    to_end = jnp.exp(total - acs) * dt
    xw = (x * _expand_heads(to_end, e_ref)).astype(BF16)
    st = jnp.dot(bm.T.astype(BF16), xw, preferred_element_type=F32)
    row_g0 = lax.broadcasted_iota(jnp.int32, st.shape, 0) < SSD_STATE
    col_g0 = lax.broadcasted_iota(jnp.int32, st.shape, 1) < heads_per_group * SSD_HEAD_DIM
    chunk_decay = _expand_heads(jnp.broadcast_to(jnp.exp(total), (SUBLANES, LANES)), e_ref)[0:1]
    hst_ref[...] = h_prev * chunk_decay + jnp.where(row_g0 == col_g0, st, 0.0)


def _scan_order(s, n_ctx, n_all):
    return jnp.where(s < n_ctx, n_ctx - 1 - s, n_all + n_ctx - 1 - s)


def _ssd(xbc, dt, dt_bias, a_log, expand, *, ctx_len):
    bsz, lc, _ = xbc.shape
    q = SSD_CHUNK
    nc = lc // q
    ncx = ctx_len // q
    xcol = SSD_WIDTH // LANES

    def chunk(d, s):
        return jnp.where(d == 0, s, _scan_order(s, ncx, nc))

    pad = lambda t: jnp.pad(t, ((0, 0), (0, LANES - SSD_HEADS))).reshape(2, 1, LANES)
    return pl.pallas_call(
        _ssd_kernel,
        grid=(bsz, 2, nc),
        in_specs=[pl.BlockSpec((1, q, SSD_WIDTH), lambda b, d, s: (b, chunk(d, s), 0)),
                  pl.BlockSpec((1, q, LANES), lambda b, d, s: (b, chunk(d, s), xcol)),
                  pl.BlockSpec((1, q, LANES), lambda b, d, s: (b, chunk(d, s), xcol + 1)),
                  pl.BlockSpec((1, q, LANES), lambda b, d, s: (b, chunk(d, s), d)),
                  pl.BlockSpec((1, 1, LANES), lambda b, d, s: (d, 0, 0)),
                  pl.BlockSpec((1, 1, LANES), lambda b, d, s: (d, 0, 0)),
                  pl.BlockSpec((LANES, SSD_WIDTH), lambda b, d, s: (0, 0))],
        out_specs=pl.BlockSpec((1, 1, q, SSD_WIDTH), lambda b, d, s: (b, d, chunk(d, s), 0)),
        out_shape=jax.ShapeDtypeStruct((bsz, 2, lc, SSD_WIDTH), F32),
        scratch_shapes=[pltpu.VMEM((SSD_GROUPS * SSD_STATE, SSD_WIDTH), F32)],
        compiler_params=_params(("parallel", "arbitrary", "arbitrary")),
        name="ssd_scan",
    )(xbc, xbc, xbc, dt, pad(dt_bias), pad(a_log), expand)


def _ssd_out_kernel(y_ref, xs_ref, z_ref, dsk_ref, ng_ref, o_ref):
    y = y_ref[0, 0] + y_ref[0, 1] + dsk_ref[...] * xs_ref[0]
    z = z_ref[0]
    y = y * (z * _sigmoid(z))
    y = y * lax.rsqrt(jnp.mean(y * y, axis=-1, keepdims=True) + EPS) * ng_ref[...]
    o_ref[0] = y.astype(BF16)


def _ssd_out(y, xbc, z, d_skip, norm_g, *, tm=256):
    bsz, lc, w = z.shape
    return pl.pallas_call(
        _ssd_out_kernel,
        grid=(bsz, lc // tm),
        in_specs=[pl.BlockSpec((1, 2, tm, w), lambda b, i: (b, 0, i, 0)),
                  pl.BlockSpec((1, tm, w), lambda b, i: (b, i, 0)),
                  pl.BlockSpec((1, tm, w), lambda b, i: (b, i, 0)),
                  pl.BlockSpec((1, w), lambda b, i: (0, 0)),
                  pl.BlockSpec((1, w), lambda b, i: (0, 0))],
        out_specs=pl.BlockSpec((1, tm, w), lambda b, i: (b, i, 0)),
        out_shape=jax.ShapeDtypeStruct((bsz, lc, w), BF16),
        compiler_params=_params(("parallel", "arbitrary")),
        name="ssd_out",
    )(y, xbc, z, jnp.repeat(d_skip, SSD_HEAD_DIM).reshape(1, w), norm_g.reshape(1, w))


def _stack_heads(q):
    return jnp.concatenate([q[:, r * HEAD_DIM:(r + 1) * HEAD_DIM] for r in range(GQA_REP)], axis=0)


def _attn_dense_kernel(q_ref, k_ref, v_ref, o_ref, m_scr, l_scr, acc_scr, *, tq, tk, ctx_len, lk):
    i = pl.program_id(2)
    q4 = _stack_heads(q_ref[0])
    nk = jnp.where(i < ctx_len // tq, ctx_len // tk, lk // tk)
    m_scr[...] = jnp.full_like(m_scr, -jnp.inf)
    l_scr[...] = jnp.zeros_like(l_scr)
    acc_scr[...] = jnp.zeros_like(acc_scr)

    def body(j, carry):
        off = pl.multiple_of(j * tk, tk)
        kj = k_ref[0, pl.ds(off, tk), :]
        vj = v_ref[0, pl.ds(off, tk), :]
        s = _dot_nt(q4, kj)
        m_prev = m_scr[...]
        m_new = jnp.maximum(m_prev, jnp.max(s, axis=-1, keepdims=True))
        alpha = jnp.exp(m_prev - m_new)
        p = jnp.exp(s - m_new)
        l_scr[...] = alpha * l_scr[...] + jnp.sum(p, axis=-1, keepdims=True)
        acc_scr[...] = alpha * acc_scr[...] + jnp.dot(p.astype(BF16), vj, preferred_element_type=F32)
        m_scr[...] = m_new
        return carry

    lax.fori_loop(0, nk, body, 0)
    o = acc_scr[...] / l_scr[...]
    for r in range(GQA_REP):
        o_ref[0, :, r * HEAD_DIM:(r + 1) * HEAD_DIM] = o[r * tq:(r + 1) * tq].astype(BF16)


def _attn_dense(q, k, v, *, ctx_len, tq=128, tk=256):
    bsz, lc, _ = q.shape
    gw = GQA_REP * HEAD_DIM
    return pl.pallas_call(
        functools.partial(_attn_dense_kernel, tq=tq, tk=tk, ctx_len=ctx_len, lk=lc),
        grid=(bsz, ATT_KV_HEADS, lc // tq),
        in_specs=[pl.BlockSpec((1, tq, gw), lambda b, g, i: (b, i, g)),
                  pl.BlockSpec((1, lc, HEAD_DIM), lambda b, g, i: (b, 0, g)),
                  pl.BlockSpec((1, lc, HEAD_DIM), lambda b, g, i: (b, 0, g))],
        out_specs=pl.BlockSpec((1, tq, gw), lambda b, g, i: (b, i, g)),
        out_shape=jax.ShapeDtypeStruct((bsz, lc, ATT_WIDTH), BF16),
        scratch_shapes=[pltpu.VMEM((GQA_REP * tq, 1), F32),
                        pltpu.VMEM((GQA_REP * tq, 1), F32),
                        pltpu.VMEM((GQA_REP * tq, HEAD_DIM), F32)],
        compiler_params=_params(("parallel", "parallel", "arbitrary")),
        name="attn_dense",
    )(q, k, v)


def _attn_win_kernel(q_ref, k_ref, v_ref, sink_ref, o_ref, *, ctx_len, lat_len):
    n = pl.program_id(2)
    tq = Q_BLOCK
    band = 3 * Q_BLOCK
    nb = n - ctx_len // Q_BLOCK
    q4 = _stack_heads(q_ref[0])
    lo = jnp.clip((nb - 1) * Q_BLOCK, 0, lat_len - band)
    start = pl.multiple_of(ctx_len + lo, Q_BLOCK)
    kl = k_ref[0, pl.ds(start, band), :]
    vl = v_ref[0, pl.ds(start, band), :]
    kc = k_ref[0, 0:ctx_len, :]
    vc = v_ref[0, 0:ctx_len, :]
    rows = lax.broadcasted_iota(jnp.int32, (GQA_REP * tq, 1), 0)
    qpos = nb * Q_BLOCK + (rows & (tq - 1))
    kpos = lo + lax.broadcasted_iota(jnp.int32, (1, band), 1)
    valid = jnp.logical_and(jnp.abs(qpos - kpos) <= WINDOW, nb >= 0)
    s_loc = jnp.where(valid, _dot_nt(q4, kl), -jnp.inf)
    s_ctx = _dot_nt(q4, kc)
    sink = sink_ref[0]
    m = jnp.maximum(jnp.maximum(jnp.max(s_loc, axis=-1, keepdims=True), jnp.max(s_ctx, axis=-1, keepdims=True)), sink)
    p_loc = jnp.exp(s_loc - m)
    p_ctx = jnp.exp(s_ctx - m)
    denom = jnp.sum(p_loc, axis=-1, keepdims=True) + jnp.sum(p_ctx, axis=-1, keepdims=True) + jnp.exp(sink - m)
    o = (jnp.dot(p_loc.astype(BF16), vl, preferred_element_type=F32)
         + jnp.dot(p_ctx.astype(BF16), vc, preferred_element_type=F32)) / denom
    for r in range(GQA_REP):
        o_ref[0, :, r * HEAD_DIM:(r + 1) * HEAD_DIM] = o[r * tq:(r + 1) * tq].astype(BF16)


def _attn_window(q, k, v, sink, *, ctx_len):
    bsz, lc, _ = q.shape
    lat_len = lc - ctx_len
    gw = GQA_REP * HEAD_DIM
    sink_col = jnp.repeat(sink.reshape(ATT_KV_HEADS, GQA_REP), Q_BLOCK, axis=1).reshape(ATT_KV_HEADS, gw, 1)
    return pl.pallas_call(
        functools.partial(_attn_win_kernel, ctx_len=ctx_len, lat_len=lat_len),
        grid=(bsz, ATT_KV_HEADS, lc // Q_BLOCK),
        in_specs=[pl.BlockSpec((1, Q_BLOCK, gw), lambda b, g, n: (b, n, g)),
                  pl.BlockSpec((1, lc, HEAD_DIM), lambda b, g, n: (b, 0, g)),
                  pl.BlockSpec((1, lc, HEAD_DIM), lambda b, g, n: (b, 0, g)),
                  pl.BlockSpec((1, gw, 1), lambda b, g, n: (g, 0, 0))],
        out_specs=pl.BlockSpec((1, Q_BLOCK, gw), lambda b, g, n: (b, n, g)),
        out_shape=jax.ShapeDtypeStruct((bsz, lc, ATT_WIDTH), BF16),
        compiler_params=_params(("parallel", "parallel", "arbitrary")),
        name="attn_window",
    )(q, k, v, sink_col)


def _lru_kernel(x_ref, wa_ref, wx_ref, ba_ref, bx_ref, lam_ref, h_ref, carry_ref, *, t, reverse):
    s = pl.program_id(1)

    @pl.when(s == 0)
    def _():
        carry_ref[...] = jnp.zeros_like(carry_ref)

    x = x_ref[0]
    x_b = x.astype(BF16)
    nslab = LRU_WIDTH // LRU_SLAB

    def gate(w_ref, b_ref):
        pre = jnp.concatenate(
            [jnp.dot(x_b[:, c * LRU_SLAB:(c + 1) * LRU_SLAB], w_ref[0, c], preferred_element_type=F32)
             for c in range(nslab)], axis=1)
        return _sigmoid(pre + b_ref[0])

    r = gate(wa_ref, ba_ref)
    ig = gate(wx_ref, bx_ref)
    log_base = -LRU_C * _softplus(-lam_ref[0])
    log_a = r * log_base
    a = jnp.exp(log_a)
    b = jnp.sqrt(1.0 - a * a) * (ig * x)
    pos = lax.broadcasted_iota(jnp.int32, (t, 1), 0)
    k = 1
    while k < t:
        if reverse:
            keep = pos < t - k
            sh = t - k
        else:
            keep = pos >= k
            sh = k
        a_sh = jnp.where(keep, pltpu.roll(a, sh, 0), 1.0)
        b_sh = jnp.where(keep, pltpu.roll(b, sh, 0), 0.0)
        b = b + a * b_sh
        a = a * a_sh
        k *= 2
    h = a * carry_ref[...] + b
    h_ref[0] = h
    carry_ref[...] = h[0:1] if reverse else h[t - 1:t]


def _lru(xr, wa, wx, ba, bx, lam, *, ctx_len, direction, t=256):
    bsz, lc, w = xr.shape
    nt = lc // t
    ct = ctx_len // t
    reverse = direction == 1
    nslab = w // LRU_SLAB

    def tile(s):
        return _scan_order(s, ct, nt) if reverse else s

    vec = lambda v: v.reshape(2, 1, w)
    return pl.pallas_call(
        functools.partial(_lru_kernel, t=t, reverse=reverse),
        grid=(bsz, nt),
        in_specs=[pl.BlockSpec((1, t, w), lambda b, s: (b, tile(s), 0)),
                  pl.BlockSpec((1, nslab, LRU_SLAB, LRU_SLAB), lambda b, s: (direction, 0, 0, 0)),
                  pl.BlockSpec((1, nslab, LRU_SLAB, LRU_SLAB), lambda b, s: (direction, 0, 0, 0)),
                  pl.BlockSpec((1, 1, w), lambda b, s: (direction, 0, 0)),
                  pl.BlockSpec((1, 1, w), lambda b, s: (direction, 0, 0)),
                  pl.BlockSpec((1, 1, w), lambda b, s: (direction, 0, 0))],
        out_specs=pl.BlockSpec((1, t, w), lambda b, s: (b, tile(s), 0)),
        out_shape=jax.ShapeDtypeStruct((bsz, lc, w), F32),
        scratch_shapes=[pltpu.VMEM((1, w), F32)],
        compiler_params=_params(("parallel", "arbitrary")),
        name="lru_scan",
    )(xr, wa, wx, vec(ba), vec(bx), vec(lam))


def _lru_out_kernel(g_ref, hf_ref, hb_ref, o_ref):
    g = g_ref[0]
    gelu = 0.5 * g * (1.0 + jnp.tanh(math.sqrt(2.0 / math.pi) * (g + 0.044715 * (g * g * g))))
    o_ref[0] = (gelu * (hf_ref[0] + hb_ref[0])).astype(BF16)


def _lru_out(gate, hf, hb, *, tm=256):
    bsz, lc, w = gate.shape
    spec = pl.BlockSpec((1, tm, w), lambda b, i: (b, i, 0))
    return pl.pallas_call(
        _lru_out_kernel,
        grid=(bsz, lc // tm),
        in_specs=[spec, spec, spec],
        out_specs=spec,
        out_shape=jax.ShapeDtypeStruct((bsz, lc, w), BF16),
        compiler_params=_params(("parallel", "arbitrary")),
        name="lru_out",
    )(gate, hf, hb)


def _lru_slabs(w):
    per = LRU_SLAB // LRU_BLOCK_DIM
    nslab = LRU_BLOCKS // per
    w = w.reshape(2, nslab, per, LRU_BLOCK_DIM, LRU_BLOCK_DIM)
    eye = jnp.eye(per, dtype=w.dtype)
    full = jnp.einsum('dspij,pq->dspiqj', w, eye)
    return full.reshape(2, nslab, LRU_SLAB, LRU_SLAB).astype(BF16)


def _outproj_kernel(ya_ref, yb_ref, x_ref, mod_ref, w_ref, o_ref, *, tm, ctx_len):
    i = pl.program_id(1)
    half = ya_ref.shape[2]
    acc = (jnp.dot(ya_ref[0], w_ref[0:half, :], preferred_element_type=F32)
           + jnp.dot(yb_ref[0], w_ref[half:, :], preferred_element_type=F32))
    gate = _row_select(i * tm, tm, ctx_len, mod_ref, 2)
    o_ref[0] = x_ref[0] + gate * acc


def _outproj(ya, yb, xx, mod, w, *, ctx_len, tm=256):
    bsz, lc, d = xx.shape
    half = ya.shape[2]
    row = lambda b, i: (b, i, 0)
    return pl.pallas_call(
        functools.partial(_outproj_kernel, tm=tm, ctx_len=ctx_len),
        grid=(bsz, lc // tm),
        in_specs=[pl.BlockSpec((1, tm, half), row),
                  pl.BlockSpec((1, tm, half), row),
                  pl.BlockSpec((1, tm, d), row),
                  pl.BlockSpec((1, 2, 6, d), lambda b, i: (b, 0, 0, 0)),
                  pl.BlockSpec((2 * half, d), lambda b, i: (0, 0))],
        out_specs=pl.BlockSpec((1, tm, d), row),
        out_shape=jax.ShapeDtypeStruct((bsz, lc, d), F32),
        compiler_params=_params(("parallel", "arbitrary")),
        name="outproj",
    )(ya, yb, xx, mod, w)


def _first_max(vals):
    m = vals[0]
    for v in vals[1:]:
        m = jnp.maximum(m, v)
    hot = []
    taken = None
    for v in vals:
        hit = v == m
        if taken is None:
            hot.append(hit)
            taken = hit
        else:
            hot.append(jnp.logical_and(hit, jnp.logical_not(taken)))
            taken = jnp.logical_or(taken, hit)
    return m, hot


def _moe_kernel(x_ref, g_ref, mod_ref, wr_ref, br_ref, w13_ref, w2_ref, o_ref, hn_scr, gate_scr, acc_scr,
                *, tm, ctx_len):
    i = pl.program_id(1)
    e = pl.program_id(2)

    @pl.when(e == 0)
    def _():
        hn = _rms_mod(x_ref[0], g_ref[...], i * tm, ctx_len, mod_ref, 3, 4)
        hn_scr[...] = hn.astype(BF16)
        logits = jnp.dot(hn, wr_ref[...], preferred_element_type=F32, precision=lax.Precision.HIGHEST) + br_ref[...]
        gl = [logits[:, j:j + 1] for j in range(MOE_GROUPS)]
        gmax, ghot = _first_max(gl)
        gsum = sum(jnp.exp(v - gmax) for v in gl)
        p_grp = 1.0 / gsum
        elg = []
        for j in range(EXPERTS_PER_GROUP):
            v = jnp.zeros_like(gmax)
            for gi in range(MOE_GROUPS):
                c = MOE_GROUPS + gi * EXPERTS_PER_GROUP + j
                v = jnp.where(ghot[gi], logits[:, c:c + 1], v)
            elg.append(v)
        v1, hot1 = _first_max(elg)
        rest = [jnp.where(hh, -jnp.inf, v) for hh, v in zip(hot1, elg)]
        v2, hot2 = _first_max(rest)
        ex = jnp.exp(v2 - v1)
        w1 = p_grp / (1.0 + ex)
        w2 = p_grp * ex / (1.0 + ex)
        for gi in range(MOE_GROUPS):
            for j in range(EXPERTS_PER_GROUP):
                val = jnp.where(hot1[j], w1, 0.0) + jnp.where(hot2[j], w2, 0.0)
                gate_scr[gi * EXPERTS_PER_GROUP + j] = jnp.where(ghot[gi], val, 0.0)
        acc_scr[...] = jnp.zeros_like(acc_scr)

    au = jnp.dot(hn_scr[...], w13_ref[0], preferred_element_type=F32)
    a = au[:, :D_EXPERT]
    u = au[:, D_EXPERT:]
    hid = (a * _sigmoid(a)) * u * gate_scr[e]
    acc_scr[...] += jnp.dot(hid.astype(BF16), w2_ref[0], preferred_element_type=F32)

    @pl.when(e == N_EXPERTS - 1)
    def _():
        gate = _row_select(i * tm, tm, ctx_len, mod_ref, 5)
        o_ref[0] = x_ref[0] + gate * acc_scr[...]


def _moe(xx, g, mod, wr, br, w13, w2, *, ctx_len, tm):
    bsz, lc, d = xx.shape
    row = lambda b, i, e: (b, i, 0)
    return pl.pallas_call(
        functools.partial(_moe_kernel, tm=tm, ctx_len=ctx_len),
        grid=(bsz, lc // tm, N_EXPERTS),
        in_specs=[pl.BlockSpec((1, tm, d), row),
                  pl.BlockSpec((1, d), lambda b, i, e: (0, 0)),
                  pl.BlockSpec((1, 2, 6, d), lambda b, i, e: (b, 0, 0, 0)),
                  pl.BlockSpec((d, LANES), lambda b, i, e: (0, 0)),
                  pl.BlockSpec((1, LANES), lambda b, i, e: (0, 0)),
                  pl.BlockSpec((1, d, 2 * D_EXPERT), lambda b, i, e: (e, 0, 0)),
                  pl.BlockSpec((1, D_EXPERT, d), lambda b, i, e: (e, 0, 0))],
        out_specs=pl.BlockSpec((1, tm, d), row),
        out_shape=jax.ShapeDtypeStruct((bsz, lc, d), F32),
        scratch_shapes=[pltpu.VMEM((tm, d), BF16),
                        pltpu.VMEM((N_EXPERTS, tm, 1), F32),
                        pltpu.VMEM((tm, d), F32)],
        compiler_params=_params(("parallel", "arbitrary", "arbitrary")),
        name="moe",
    )(xx, g, mod, wr, br, w13, w2)


def _final_kernel(x_ref, g_ref, o_ref):
    x = x_ref[0]
    o_ref[0] = x * lax.rsqrt(jnp.mean(x * x, axis=-1, keepdims=True) + EPS) * g_ref[...]


def _final_norm(xx, g, *, ctx_len, tm=256):
    bsz, lc, d = xx.shape
    lat = lc - ctx_len
    cb = ctx_len // tm
    return pl.pallas_call(
        _final_kernel,
        grid=(bsz, lat // tm),
        in_specs=[pl.BlockSpec((1, tm, d), lambda b, i: (b, cb + i, 0)),
                  pl.BlockSpec((1, d), lambda b, i: (0, 0))],
        out_specs=pl.BlockSpec((1, tm, d), lambda b, i: (b, i, 0)),
        out_shape=jax.ShapeDtypeStruct((bsz, lat, d), F32),
        compiler_params=_params(("parallel", "arbitrary")),
        name="final_norm",
    )(xx, g.reshape(1, d))


def _rope_tables(ctx_len, lat_len):
    rows = lat_len // GRID_W
    row = jnp.repeat(jnp.arange(rows), GRID_W).astype(F32)
    col = jnp.tile(jnp.arange(GRID_W), rows).astype(F32)
    n_freq = HEAD_DIM // 4
    inv = ROPE_THETA ** (-jnp.arange(n_freq, dtype=F32) / n_freq)
    ang = jnp.concatenate([row[:, None] * inv, col[:, None] * inv], axis=-1)
    cos, sin = jnp.cos(ang), jnp.sin(ang)
    cc = jnp.concatenate([cos, cos], axis=-1)
    ss = jnp.concatenate([-sin, sin], axis=-1)
    cc = jnp.concatenate([jnp.ones((ctx_len, HEAD_DIM), F32), cc], axis=0)
    ss = jnp.concatenate([jnp.zeros((ctx_len, HEAD_DIM), F32), ss], axis=0)
    return cc, ss


def _pick_tile(n, prefs):
    for t in prefs:
        if n % t == 0:
            return t
    raise ValueError(f"no tile for {n}")


def kernel(x, c, ctx, c_ctx, w_mod, b_mod, g_mix, g_ffn, moe_w_grp, moe_b_grp, moe_w_rt, moe_b_rt, moe_w1, moe_w3, moe_w2, ab_w_in, ab_w_out, ssd_conv_w, ssd_conv_b, ssd_dt_bias, ssd_a_log, ssd_d, ssd_norm_g, att_q_g, att_k_g, cd_w_in, cd_w_out, lru_conv_w, lru_conv_b, lru_w_a, lru_b_a, lru_w_x, lru_b_x, lru_lam, swa_sink, g_final):
    bsz, lat_len, d = x.shape
    ctx_len = ctx.shape[1]
    depth = w_mod.shape[0]
    lc = ctx_len + lat_len
    assert d == D_MODEL and ctx_len % 256 == 0 and lat_len % 256 == 0 and lat_len >= 3 * Q_BLOCK

    xx = jnp.concatenate([ctx, x], axis=1)
    cc, ss = _rope_tables(ctx_len, lat_len)
    mods = _modulation(c, c_ctx, w_mod, b_mod)
    expand = (jnp.arange(SSD_WIDTH)[None, :] // SSD_HEAD_DIM == jnp.arange(LANES)[:, None]).astype(BF16)
    tm_moe = _pick_tile(lc, (768, 512, 256))
    ones_h = jnp.ones((1, HEAD_DIM), F32)

    for i in range(depth):
        j = i // 2
        m = mods[i].reshape(bsz + 1, 6, d)
        mod = jnp.stack([jnp.broadcast_to(m[0], (bsz, 6, d)), m[1:]], axis=1)
        if i % 2 == 0:
            w = ab_w_in[j]
            zc = jnp.zeros((d, LANES - SSD_HEADS), w.dtype)
            o_dt = SSD_WIDTH + SSD_CONV_CH
            o_q = o_dt + 2 * SSD_HEADS
            w_packed = jnp.concatenate([w[:, :o_dt], w[:, o_dt:o_dt + SSD_HEADS], zc,
                                        w[:, o_dt + SSD_HEADS:o_q], zc, w[:, o_q:]], axis=1).astype(BF16)
            plain = ((0, SSD_WIDTH), (SSD_WIDTH, SSD_CONV_CH), (o_dt, 2 * LANES))
            z, xbc, dt, q, k, v = _inproj(xx, g_mix[i].reshape(1, d), mod, w_packed,
                                          att_q_g[j].reshape(1, HEAD_DIM), att_k_g[j].reshape(1, HEAD_DIM), cc, ss,
                                          ctx_len=ctx_len, plain=plain, q0=o_dt + 2 * LANES, qk_norm=True, tm=256)
            xbc = _dwconv(xbc, ssd_conv_w[j], ssd_conv_b[j], ctx_len=ctx_len, silu=True)
            y = _ssd(xbc, dt, ssd_dt_bias[j], ssd_a_log[j], expand, ctx_len=ctx_len)
            ya = _ssd_out(y, xbc, z, ssd_d[j], ssd_norm_g[j])
            yb = _attn_dense(q, k, v, ctx_len=ctx_len)
            w_out = ab_w_out[j]
        else:
            w_packed = cd_w_in[j].astype(BF16)
            plain = ((0, LRU_WIDTH), (LRU_WIDTH, LRU_WIDTH))
            gate, xr, q, k, v = _inproj(xx, g_mix[i].reshape(1, d), mod, w_packed, ones_h, ones_h, cc, ss,
                                        ctx_len=ctx_len, plain=plain, q0=2 * LRU_WIDTH, qk_norm=False, tm=256)
            xr = _dwconv(xr, lru_conv_w[j], lru_conv_b[j], ctx_len=ctx_len, silu=False)
            wa = _lru_slabs(lru_w_a[j])
            wx = _lru_slabs(lru_w_x[j])
            hf = _lru(xr, wa, wx, lru_b_a[j], lru_b_x[j], lru_lam[j], ctx_len=ctx_len, direction=0)
            hb = _lru(xr, wa, wx, lru_b_a[j], lru_b_x[j], lru_lam[j], ctx_len=ctx_len, direction=1)
            ya = _lru_out(gate, hf, hb)
            yb = _attn_window(q, k, v, swa_sink[j], ctx_len=ctx_len)
            w_out = cd_w_out[j]
        xx = _outproj(ya, yb, xx, mod, w_out.astype(BF16), ctx_len=ctx_len)
        wr = jnp.concatenate([moe_w_grp[i], moe_w_rt[i],
                              jnp.zeros((d, LANES - MOE_GROUPS - N_EXPERTS), F32)], axis=1)
        br = jnp.concatenate([moe_b_grp[i], moe_b_rt[i],
                              jnp.zeros((LANES - MOE_GROUPS - N_EXPERTS,), F32)]).reshape(1, LANES)
        w13 = jnp.concatenate([moe_w1[i], moe_w3[i]], axis=-1).astype(BF16)
        xx = _moe(xx, g_ffn[i].reshape(1, d), mod, wr, br, w13, moe_w2[i].astype(BF16), ctx_len=ctx_len, tm=tm_moe)
    return _final_norm(xx, g_final, ctx_len=ctx_len)
```

```python
import functools
import math

import jax
import jax.numpy as jnp
from jax import lax
from jax.experimental import pallas as pl
from jax.experimental.pallas import tpu as pltpu

F32 = jnp.float32
BF16 = jnp.bfloat16

D_MODEL = 1024
GRID_W = 64
EPS = 1e-6
HEAD_DIM = 128
ATT_HEADS = 8
ATT_KV_HEADS = 2
GQA_REP = ATT_HEADS // ATT_KV_HEADS
ATT_WIDTH = ATT_HEADS * HEAD_DIM
KV_WIDTH = ATT_KV_HEADS * HEAD_DIM
ATT_SCALE = HEAD_DIM ** -0.5
ROPE_THETA = 10000.0
Q_BLOCK = 128
WINDOW = 128
SSD_HEADS = 16
SSD_HEAD_DIM = 64
SSD_WIDTH = SSD_HEADS * SSD_HEAD_DIM
SSD_GROUPS = 2
SSD_STATE = 64
SSD_CHUNK = 128
SSD_CONV_CH = SSD_WIDTH + 2 * SSD_GROUPS * SSD_STATE
LRU_WIDTH = 1024
LRU_BLOCKS = 16
LRU_BLOCK_DIM = LRU_WIDTH // LRU_BLOCKS
LRU_C = 8.0
CONV_W = 5
MOE_GROUPS = 4
EXPERTS_PER_GROUP = 4
N_EXPERTS = MOE_GROUPS * EXPERTS_PER_GROUP
D_EXPERT = 256

LANES = 128
SUBLANES = 8
LRU_SLAB = 256
VMEM_LIMIT = 56 * 1024 * 1024
MOE_EXPERT_BLOCK = 2
ATT_TQ = 256
ATT_TK_PREFS = (2816, 1408, 768, 256)


def _params(sem):
    return pltpu.CompilerParams(dimension_semantics=sem, vmem_limit_bytes=VMEM_LIMIT)


def _sigmoid(x):
    return 1.0 / (1.0 + jnp.exp(-x))


def _softplus(x):
    return jnp.maximum(x, 0.0) + jnp.log1p(jnp.exp(-jnp.abs(x)))


def _split_bf16(v, n):
    parts = []
    r = v
    for _ in range(n):
        p = r.astype(BF16)
        parts.append(p)
        r = r - p.astype(F32)
    return parts


def _dot_nt(a, b):
    return lax.dot_general(a, b, (((1,), (1,)), ((), ())), preferred_element_type=F32)


def _row_select(row0, tm, ctx_len, mod_ref, k):
    rows = row0 + lax.broadcasted_iota(jnp.int32, (tm, 1), 0)
    return jnp.where(rows < ctx_len, mod_ref[0, 0, k:k + 1, :], mod_ref[0, 1, k:k + 1, :])


def _rms_mod(x, g, row0, ctx_len, mod_ref, k_shift, k_scale):
    tm = x.shape[0]
    y = x * lax.rsqrt(jnp.mean(x * x, axis=-1, keepdims=True) + EPS) * g
    sh = _row_select(row0, tm, ctx_len, mod_ref, k_shift)
    sc = _row_select(row0, tm, ctx_len, mod_ref, k_scale)
    return y * (1.0 + sc) + sh


def _mod_kernel(ct_ref, w_ref, b_ref, o_ref, *, rows):
    w = w_ref[0]
    outs = []
    for r in range(rows):
        c = ct_ref[:, r:r + 1]
        act = c * _sigmoid(c)
        outs.append(jnp.sum(w * act, axis=0, keepdims=True) + b_ref[0])
    o_ref[0] = jnp.concatenate(outs, axis=0)


def _modulation(c, c_ctx, w_mod, b_mod):
    depth, d, n = w_mod.shape
    rows = c.shape[0] + 1
    ct = jnp.concatenate([c_ctx[None, :], c], axis=0).T
    tn = 1536
    return pl.pallas_call(
        functools.partial(_mod_kernel, rows=rows),
        grid=(depth, n // tn),
        in_specs=[pl.BlockSpec((d, rows), lambda i, j: (0, 0)),
                  pl.BlockSpec((1, d, tn), lambda i, j: (i, 0, j)),
                  pl.BlockSpec((1, 1, tn), lambda i, j: (i, 0, j))],
        out_specs=pl.BlockSpec((1, rows, tn), lambda i, j: (i, 0, j)),
        out_shape=jax.ShapeDtypeStruct((depth, rows, n), F32),
        compiler_params=_params(("arbitrary", "arbitrary")),
        name="modulation",
    )(ct, w_mod, b_mod.reshape(depth, 1, n))


def _inproj_kernel(x_ref, g_ref, mod_ref, w_ref, qg_ref, kg_ref, cc_ref, ss_ref, *out_refs,
                   tm, ctx_len, plain, q0, qk_norm):
    i = pl.program_id(1)
    h = _rms_mod(x_ref[0], g_ref[...], i * tm, ctx_len, mod_ref, 0, 1)
    res = jnp.dot(h.astype(BF16), w_ref[...], preferred_element_type=F32)
    for ref, (c0, cw) in zip(out_refs[:len(plain)], plain):
        ref[0] = res[:, c0:c0 + cw]
    q_ref, k_ref, v_ref = out_refs[len(plain):]
    cc = cc_ref[...]
    ss = ss_ref[...]

    def head(t, gain, scale):
        if qk_norm:
            t = t * lax.rsqrt(jnp.mean(t * t, axis=-1, keepdims=True) + EPS) * gain
        t = t * cc + pltpu.roll(t, HEAD_DIM // 2, 1) * ss
        return (t * scale).astype(BF16)

    for hh in range(ATT_HEADS):
        c0 = q0 + hh * HEAD_DIM
        q_ref[0, :, hh * HEAD_DIM:(hh + 1) * HEAD_DIM] = head(res[:, c0:c0 + HEAD_DIM], qg_ref[...], ATT_SCALE)
    k0 = q0 + ATT_WIDTH
    for hh in range(ATT_KV_HEADS):
        c0 = k0 + hh * HEAD_DIM
        k_ref[0, :, hh * HEAD_DIM:(hh + 1) * HEAD_DIM] = head(res[:, c0:c0 + HEAD_DIM], kg_ref[...], 1.0)
    v0 = k0 + KV_WIDTH
    v_ref[0] = res[:, v0:v0 + KV_WIDTH].astype(BF16)


def _inproj(xx, g, mod, w, qg, kg, cc, ss, *, ctx_len, plain, q0, qk_norm, tm):
    bsz, lc, d = xx.shape
    n = w.shape[1]
    nt = lc // tm
    row = lambda b, i: (b, i, 0)
    out_shape = [jax.ShapeDtypeStruct((bsz, lc, cw), F32) for _, cw in plain]
    out_specs = [pl.BlockSpec((1, tm, cw), row) for _, cw in plain]
    for cw in (ATT_WIDTH, KV_WIDTH, KV_WIDTH):
        out_shape.append(jax.ShapeDtypeStruct((bsz, lc, cw), BF16))
        out_specs.append(pl.BlockSpec((1, tm, cw), row))
    return pl.pallas_call(
        functools.partial(_inproj_kernel, tm=tm, ctx_len=ctx_len, plain=plain, q0=q0, qk_norm=qk_norm),
        grid=(bsz, nt),
        in_specs=[pl.BlockSpec((1, tm, d), row),
                  pl.BlockSpec((1, d), lambda b, i: (0, 0)),
                  pl.BlockSpec((1, 2, 6, d), lambda b, i: (b, 0, 0, 0)),
                  pl.BlockSpec((d, n), lambda b, i: (0, 0)),
                  pl.BlockSpec((1, HEAD_DIM), lambda b, i: (0, 0)),
                  pl.BlockSpec((1, HEAD_DIM), lambda b, i: (0, 0)),
                  pl.BlockSpec((tm, HEAD_DIM), lambda b, i: (i, 0)),
                  pl.BlockSpec((tm, HEAD_DIM), lambda b, i: (i, 0))],
        out_specs=out_specs,
        out_shape=out_shape,
        compiler_params=_params(("parallel", "arbitrary")),
        name="inproj",
    )(xx, g, mod, w, qg, kg, cc, ss)


def _conv_kernel(prev_ref, cur_ref, next_ref, w_ref, b_ref, o_ref, *, tm, nt, ct, silu):
    i = pl.program_id(1)
    seg_start = jnp.logical_or(i == 0, i == ct)
    seg_end = jnp.logical_or(i == ct - 1, i == nt - 1)
    prev = jnp.where(seg_start, 0.0, prev_ref[0])
    nxt = jnp.where(seg_end, 0.0, next_ref[0])
    ext = jnp.concatenate([prev, cur_ref[0], nxt], axis=0)
    n = tm + 2 * SUBLANES
    pad = CONV_W // 2
    acc = jnp.broadcast_to(b_ref[...], (tm, ext.shape[1]))
    for k in range(CONV_W):
        sh = (pad - k) % n
        r = ext if sh == 0 else pltpu.roll(ext, sh, 0)
        acc = acc + r[SUBLANES:SUBLANES + tm] * w_ref[k:k + 1, :]
    if silu:
        acc = acc * _sigmoid(acc)
    o_ref[0] = acc


def _dwconv(u, w, b, *, ctx_len, silu, tm=256):
    bsz, lc, c = u.shape
    nt = lc // tm
    ct = ctx_len // tm
    hb = tm // SUBLANES
    nhb = lc // SUBLANES
    return pl.pallas_call(
        functools.partial(_conv_kernel, tm=tm, nt=nt, ct=ct, silu=silu),
        grid=(bsz, nt),
        in_specs=[pl.BlockSpec((1, SUBLANES, c), lambda b_, i: (b_, jnp.maximum(i * hb - 1, 0), 0)),
                  pl.BlockSpec((1, tm, c), lambda b_, i: (b_, i, 0)),
                  pl.BlockSpec((1, SUBLANES, c), lambda b_, i: (b_, jnp.minimum((i + 1) * hb, nhb - 1), 0)),
                  pl.BlockSpec((CONV_W, c), lambda b_, i: (0, 0)),
                  pl.BlockSpec((1, c), lambda b_, i: (0, 0))],
        out_specs=pl.BlockSpec((1, tm, c), lambda b_, i: (b_, i, 0)),
        out_shape=jax.ShapeDtypeStruct((bsz, lc, c), F32),
        compiler_params=_params(("parallel", "arbitrary")),
        name="dwconv",
    )(u, u, u, w, b.reshape(1, c))


def _expand_heads(v, e_ref):
    hi, lo = _split_bf16(v, 2)
    e = e_ref[...]
    return (jnp.dot(hi, e, preferred_element_type=F32) + jnp.dot(lo, e, preferred_element_type=F32))


def _ssd_kernel(xs_ref, bm_ref, cm_ref, dt_ref, dtb_ref, alog_ref, e_ref, y_ref, hst_ref):
    d = pl.program_id(1)
    s = pl.program_id(2)
    q = SSD_CHUNK

    @pl.when(s == 0)
    def _():
        hst_ref[...] = jnp.zeros_like(hst_ref)

    dt = _softplus(dt_ref[0] + dtb_ref[0])
    a_neg = -jnp.exp(alog_ref[0])
    da = dt * a_neg
    ii = lax.broadcasted_iota(jnp.int32, (q, q), 0)
    jj = lax.broadcasted_iota(jnp.int32, (q, q), 1)
    causal = (jj - ii) * jnp.where(d == 0, 1, -1) <= 0
    mask_b = jnp.where(causal, 1.0, 0.0).astype(BF16)
    da_t = da.T
    dt_t = dt.T
    acs = sum(jnp.dot(mask_b, p, preferred_element_type=F32) for p in _split_bf16(da, 3))
    acs_t = sum(_dot_nt(p, mask_b) for p in _split_bf16(da_t, 3))
    total = jnp.sum(da, axis=0, keepdims=True)

    bm = bm_ref[0]
    cm = cm_ref[0]
    bm_b = bm.astype(BF16)
    cm_b = cm.astype(BF16)
    lane = lax.broadcasted_iota(jnp.int32, (1, LANES), 1)
    in_group = [jnp.logical_and(lane >= g * SSD_STATE, lane < (g + 1) * SSD_STATE) for g in range(SSD_GROUPS)]
    cb = [_dot_nt(jnp.where(in_group[g], cm, 0.0).astype(BF16), bm_b) for g in range(SSD_GROUPS)]

    x = xs_ref[0]
    x_b = x.astype(BF16)
    heads_per_group = SSD_HEADS // SSD_GROUPS
    pairs = []
    for hp in range(SSD_HEADS // 2):
        xp = x_b[:, hp * LANES:(hp + 1) * LANES]
        outs = []
        for t in range(2):
            h = 2 * hp + t
            seg = acs[:, h:h + 1] - acs_t[h:h + 1, :]
            dec = jnp.exp(jnp.where(causal, seg, -jnp.inf))
            w = (cb[h // heads_per_group] * dec * dt_t[h:h + 1, :]).astype(BF16)
            outs.append(jnp.dot(w, xp, preferred_element_type=F32))
        pairs.append(jnp.where(lane < SSD_HEAD_DIM, outs[0], outs[1]))
    y_diag = jnp.concatenate(pairs, axis=1)

    h_prev = hst_ref[...]
    y_off = jnp.dot(cm_b, h_prev.astype(BF16), preferred_element_type=F32) * _expand_heads(jnp.exp(acs), e_ref)
    y_ref[0, 0] = y_diag + y_off

    to_end = jnp.exp(total - acs) * dt
    xw = (x * _expand_heads(to_end, e_ref)).astype(BF16)
    st = jnp.dot(bm.T.astype(BF16), xw, preferred_element_type=F32)
    row_g0 = lax.broadcasted_iota(jnp.int32, st.shape, 0) < SSD_STATE
    col_g0 = lax.broadcasted_iota(jnp.int32, st.shape, 1) < heads_per_group * SSD_HEAD_DIM
    chunk_decay = _expand_heads(jnp.broadcast_to(jnp.exp(total), (SUBLANES, LANES)), e_ref)[0:1]
    hst_ref[...] = h_prev * chunk_decay + jnp.where(row_g0 == col_g0, st, 0.0)


def _scan_order(s, n_ctx, n_all):
    return jnp.where(s < n_ctx, n_ctx - 1 - s, n_all + n_ctx - 1 - s)


def _ssd(xbc, dt, dt_bias, a_log, expand, *, ctx_len):
    bsz, lc, _ = xbc.shape
    q = SSD_CHUNK
    nc = lc // q
    ncx = ctx_len // q
    xcol = SSD_WIDTH // LANES

    def chunk(d, s):
        return jnp.where(d == 0, s, _scan_order(s, ncx, nc))

    pad = lambda t: jnp.pad(t, ((0, 0), (0, LANES - SSD_HEADS))).reshape(2, 1, LANES)
    return pl.pallas_call(
        _ssd_kernel,
        grid=(bsz, 2, nc),
        in_specs=[pl.BlockSpec((1, q, SSD_WIDTH), lambda b, d, s: (b, chunk(d, s), 0)),
                  pl.BlockSpec((1, q, LANES), lambda b, d, s: (b, chunk(d, s), xcol)),
                  pl.BlockSpec((1, q, LANES), lambda b, d, s: (b, chunk(d, s), xcol + 1)),
                  pl.BlockSpec((1, q, LANES), lambda b, d, s: (b, chunk(d, s), d)),
                  pl.BlockSpec((1, 1, LANES), lambda b, d, s: (d, 0, 0)),
                  pl.BlockSpec((1, 1, LANES), lambda b, d, s: (d, 0, 0)),
                  pl.BlockSpec((LANES, SSD_WIDTH), lambda b, d, s: (0, 0))],
        out_specs=pl.BlockSpec((1, 1, q, SSD_WIDTH), lambda b, d, s: (b, d, chunk(d, s), 0)),
        out_shape=jax.ShapeDtypeStruct((bsz, 2, lc, SSD_WIDTH), F32),
        scratch_shapes=[pltpu.VMEM((SSD_GROUPS * SSD_STATE, SSD_WIDTH), F32)],
        compiler_params=_params(("parallel", "arbitrary", "arbitrary")),
        name="ssd_scan",
    )(xbc, xbc, xbc, dt, pad(dt_bias), pad(a_log), expand)


def _ssd_out_kernel(y_ref, xs_ref, z_ref, dsk_ref, ng_ref, o_ref):
    y = y_ref[0, 0] + y_ref[0, 1] + dsk_ref[...] * xs_ref[0]
    z = z_ref[0]
    y = y * (z * _sigmoid(z))
    y = y * lax.rsqrt(jnp.mean(y * y, axis=-1, keepdims=True) + EPS) * ng_ref[...]
    o_ref[0] = y.astype(BF16)


def _ssd_out(y, xbc, z, d_skip, norm_g, *, tm=256):
    bsz, lc, w = z.shape
    return pl.pallas_call(
        _ssd_out_kernel,
        grid=(bsz, lc // tm),
        in_specs=[pl.BlockSpec((1, 2, tm, w), lambda b, i: (b, 0, i, 0)),
                  pl.BlockSpec((1, tm, w), lambda b, i: (b, i, 0)),
                  pl.BlockSpec((1, tm, w), lambda b, i: (b, i, 0)),
                  pl.BlockSpec((1, w), lambda b, i: (0, 0)),
                  pl.BlockSpec((1, w), lambda b, i: (0, 0))],
        out_specs=pl.BlockSpec((1, tm, w), lambda b, i: (b, i, 0)),
        out_shape=jax.ShapeDtypeStruct((bsz, lc, w), BF16),
        compiler_params=_params(("parallel", "arbitrary")),
        name="ssd_out",
    )(y, xbc, z, jnp.repeat(d_skip, SSD_HEAD_DIM).reshape(1, w), norm_g.reshape(1, w))


def _stack_heads(q):
    return jnp.concatenate([q[:, r * HEAD_DIM:(r + 1) * HEAD_DIM] for r in range(GQA_REP)], axis=0)


def _lane_parts(s):
    return [s[:, c * LANES:(c + 1) * LANES] for c in range(s.shape[1] // LANES)]


def _softmax_parts(parts, sink=None):
    pm = parts[0]
    for t in parts[1:]:
        pm = jnp.maximum(pm, t)
    m = jnp.max(pm, axis=-1, keepdims=True)
    if sink is not None:
        m = jnp.maximum(m, sink)
    mb = jnp.broadcast_to(m, pm.shape)
    ps = [jnp.exp(t - mb) for t in parts]
    lsum = ps[0]
    for t in ps[1:]:
        lsum = lsum + t
    l = jnp.sum(lsum, axis=-1, keepdims=True)
    if sink is not None:
        l = l + jnp.exp(sink - m)
    return jnp.concatenate([t.astype(BF16) for t in ps], axis=1), l


def _store_heads(o_ref, o, tq):
    for r in range(GQA_REP):
        o_ref[0, :, r * HEAD_DIM:(r + 1) * HEAD_DIM] = o[r * tq:(r + 1) * tq].astype(BF16)


def _attn_dense_kernel(q_ref, k_ref, v_ref, o_ref, m_scr, l_scr, acc_scr, *, tq, tk, ctx_len, lk):
    i = pl.program_id(2)
    q4 = _stack_heads(q_ref[0])

    @pl.when(i < ctx_len // tq)
    def _():
        p, l = _softmax_parts(_lane_parts(_dot_nt(q4, k_ref[0, 0:ctx_len, :])))
        _store_heads(o_ref, jnp.dot(p, v_ref[0, 0:ctx_len, :], preferred_element_type=F32) / l, tq)

    @pl.when(i >= ctx_len // tq)
    def _():
        m_scr[...] = jnp.full_like(m_scr, -jnp.inf)
        l_scr[...] = jnp.zeros_like(l_scr)
        acc_scr[...] = jnp.zeros_like(acc_scr)

        def body(j, carry):
            off = pl.multiple_of(j * tk, tk)
            parts = _lane_parts(_dot_nt(q4, k_ref[0, pl.ds(off, tk), :]))
            pm = parts[0]
            for t in parts[1:]:
                pm = jnp.maximum(pm, t)
            m_prev = m_scr[...]
            m_new = jnp.maximum(m_prev, jnp.max(pm, axis=-1, keepdims=True))
            alpha = jnp.exp(m_prev - m_new)
            ps = [jnp.exp(t - m_new) for t in parts]
            lsum = ps[0]
            for t in ps[1:]:
                lsum = lsum + t
            l_scr[...] = alpha * l_scr[...] + lsum
            p = jnp.concatenate([t.astype(BF16) for t in ps], axis=1)
            acc_scr[...] = alpha * acc_scr[...] + jnp.dot(p, v_ref[0, pl.ds(off, tk), :], preferred_element_type=F32)
            m_scr[...] = m_new
            return carry

        lax.fori_loop(0, lk // tk, body, 0)
        _store_heads(o_ref, acc_scr[...] / jnp.sum(l_scr[...], axis=-1, keepdims=True), tq)


def _attn_dense(q, k, v, *, ctx_len, tq, tk):
    bsz, lc, _ = q.shape
    gw = GQA_REP * HEAD_DIM
    rows = GQA_REP * tq
    return pl.pallas_call(
        functools.partial(_attn_dense_kernel, tq=tq, tk=tk, ctx_len=ctx_len, lk=lc),
        grid=(bsz, ATT_KV_HEADS, lc // tq),
        in_specs=[pl.BlockSpec((1, tq, gw), lambda b, g, i: (b, i, g)),
                  pl.BlockSpec((1, lc, HEAD_DIM), lambda b, g, i: (b, 0, g)),
                  pl.BlockSpec((1, lc, HEAD_DIM), lambda b, g, i: (b, 0, g))],
        out_specs=pl.BlockSpec((1, tq, gw), lambda b, g, i: (b, i, g)),
        out_shape=jax.ShapeDtypeStruct((bsz, lc, ATT_WIDTH), BF16),
        scratch_shapes=[pltpu.VMEM((rows, LANES), F32),
                        pltpu.VMEM((rows, LANES), F32),
                        pltpu.VMEM((rows, HEAD_DIM), F32)],
        compiler_params=_params(("parallel", "parallel", "arbitrary")),
        name="attn_dense",
    )(q, k, v)


def _attn_win_kernel(q_ref, k_ref, v_ref, sink_ref, o_ref, *, ctx_len, lat_len, tq):
    n = pl.program_id(2)
    band = tq + 2 * WINDOW
    il = n - ctx_len // tq
    q4 = _stack_heads(q_ref[0])
    lo = jnp.clip(il * tq - WINDOW, 0, lat_len - band)
    start = pl.multiple_of(ctx_len + lo, WINDOW)
    rows = lax.broadcasted_iota(jnp.int32, (GQA_REP * tq, 1), 0)
    qpos = il * tq + (rows & (tq - 1))
    kpos = lo + lax.broadcasted_iota(jnp.int32, (1, band), 1)
    valid = jnp.logical_and(jnp.abs(qpos - kpos) <= WINDOW, il >= 0)
    s_loc = jnp.where(valid, _dot_nt(q4, k_ref[0, pl.ds(start, band), :]), -jnp.inf)
    s_ctx = _dot_nt(q4, k_ref[0, 0:ctx_len, :])
    p, l = _softmax_parts(_lane_parts(s_loc) + _lane_parts(s_ctx), sink_ref[0])
    o = (jnp.dot(p[:, :band], v_ref[0, pl.ds(start, band), :], preferred_element_type=F32)
         + jnp.dot(p[:, band:], v_ref[0, 0:ctx_len, :], preferred_element_type=F32)) / l
    _store_heads(o_ref, o, tq)


def _attn_window(q, k, v, sink, *, ctx_len, tq):
    bsz, lc, _ = q.shape
    lat_len = lc - ctx_len
    gw = GQA_REP * HEAD_DIM
    assert tq & (tq - 1) == 0 and ctx_len % tq == 0 and lat_len >= tq + 2 * WINDOW
    sink_col = jnp.repeat(sink.reshape(ATT_KV_HEADS, GQA_REP), tq, axis=1).reshape(ATT_KV_HEADS, GQA_REP * tq, 1)
    return pl.pallas_call(
        functools.partial(_attn_win_kernel, ctx_len=ctx_len, lat_len=lat_len, tq=tq),
        grid=(bsz, ATT_KV_HEADS, lc // tq),
        in_specs=[pl.BlockSpec((1, tq, gw), lambda b, g, n: (b, n, g)),
                  pl.BlockSpec((1, lc, HEAD_DIM), lambda b, g, n: (b, 0, g)),
                  pl.BlockSpec((1, lc, HEAD_DIM), lambda b, g, n: (b, 0, g)),
                  pl.BlockSpec((1, GQA_REP * tq, 1), lambda b, g, n: (g, 0, 0))],
        out_specs=pl.BlockSpec((1, tq, gw), lambda b, g, n: (b, n, g)),
        out_shape=jax.ShapeDtypeStruct((bsz, lc, ATT_WIDTH), BF16),
        compiler_params=_params(("parallel", "parallel", "arbitrary")),
        name="attn_window",
    )(q, k, v, sink_col)


def _lru_kernel(x_ref, wa_ref, wx_ref, ba_ref, bx_ref, lam_ref, h_ref, carry_ref, *, t, reverse):
    s = pl.program_id(1)

    @pl.when(s == 0)
    def _():
        carry_ref[...] = jnp.zeros_like(carry_ref)

    x = x_ref[0]
    x_b = x.astype(BF16)
    nslab = LRU_WIDTH // LRU_SLAB

    def gate(w_ref, b_ref):
        pre = jnp.concatenate(
            [jnp.dot(x_b[:, c * LRU_SLAB:(c + 1) * LRU_SLAB], w_ref[0, c], preferred_element_type=F32)
             for c in range(nslab)], axis=1)
        return _sigmoid(pre + b_ref[0])

    r = gate(wa_ref, ba_ref)
    ig = gate(wx_ref, bx_ref)
    log_base = -LRU_C * _softplus(-lam_ref[0])
    log_a = r * log_base
    a = jnp.exp(log_a)
    b = jnp.sqrt(1.0 - a * a) * (ig * x)
    pos = lax.broadcasted_iota(jnp.int32, (t, 1), 0)
    k = 1
    while k < t:
        if reverse:
            keep = pos < t - k
            sh = t - k
        else:
            keep = pos >= k
            sh = k
        a_sh = jnp.where(keep, pltpu.roll(a, sh, 0), 1.0)
        b_sh = jnp.where(keep, pltpu.roll(b, sh, 0), 0.0)
        b = b + a * b_sh
        a = a * a_sh
        k *= 2
    h = a * carry_ref[...] + b
    h_ref[0] = h
    carry_ref[...] = h[0:1] if reverse else h[t - 1:t]


def _lru(xr, wa, wx, ba, bx, lam, *, ctx_len, direction, t=256):
    bsz, lc, w = xr.shape
    nt = lc // t
    ct = ctx_len // t
    reverse = direction == 1
    nslab = w // LRU_SLAB

    def tile(s):
        return _scan_order(s, ct, nt) if reverse else s

    vec = lambda v: v.reshape(2, 1, w)
    return pl.pallas_call(
        functools.partial(_lru_kernel, t=t, reverse=reverse),
        grid=(bsz, nt),
        in_specs=[pl.BlockSpec((1, t, w), lambda b, s: (b, tile(s), 0)),
                  pl.BlockSpec((1, nslab, LRU_SLAB, LRU_SLAB), lambda b, s: (direction, 0, 0, 0)),
                  pl.BlockSpec((1, nslab, LRU_SLAB, LRU_SLAB), lambda b, s: (direction, 0, 0, 0)),
                  pl.BlockSpec((1, 1, w), lambda b, s: (direction, 0, 0)),
                  pl.BlockSpec((1, 1, w), lambda b, s: (direction, 0, 0)),
                  pl.BlockSpec((1, 1, w), lambda b, s: (direction, 0, 0))],
        out_specs=pl.BlockSpec((1, t, w), lambda b, s: (b, tile(s), 0)),
        out_shape=jax.ShapeDtypeStruct((bsz, lc, w), F32),
        scratch_shapes=[pltpu.VMEM((1, w), F32)],
        compiler_params=_params(("parallel", "arbitrary")),
        name="lru_scan",
    )(xr, wa, wx, vec(ba), vec(bx), vec(lam))


def _lru_out_kernel(g_ref, hf_ref, hb_ref, o_ref):
    g = g_ref[0]
    gelu = 0.5 * g * (1.0 + jnp.tanh(math.sqrt(2.0 / math.pi) * (g + 0.044715 * (g * g * g))))
    o_ref[0] = (gelu * (hf_ref[0] + hb_ref[0])).astype(BF16)


def _lru_out(gate, hf, hb, *, tm=256):
    bsz, lc, w = gate.shape
    spec = pl.BlockSpec((1, tm, w), lambda b, i: (b, i, 0))
    return pl.pallas_call(
        _lru_out_kernel,
        grid=(bsz, lc // tm),
        in_specs=[spec, spec, spec],
        out_specs=spec,
        out_shape=jax.ShapeDtypeStruct((bsz, lc, w), BF16),
        compiler_params=_params(("parallel", "arbitrary")),
        name="lru_out",
    )(gate, hf, hb)


def _lru_slabs(w):
    per = LRU_SLAB // LRU_BLOCK_DIM
    nslab = LRU_BLOCKS // per
    w = w.reshape(2, nslab, per, LRU_BLOCK_DIM, LRU_BLOCK_DIM)
    eye = jnp.eye(per, dtype=w.dtype)
    full = jnp.einsum('dspij,pq->dspiqj', w, eye)
    return full.reshape(2, nslab, LRU_SLAB, LRU_SLAB).astype(BF16)


def _outproj_kernel(ya_ref, yb_ref, x_ref, mod_ref, w_ref, o_ref, *, tm, ctx_len):
    i = pl.program_id(1)
    half = ya_ref.shape[2]
    acc = (jnp.dot(ya_ref[0], w_ref[0:half, :], preferred_element_type=F32)
           + jnp.dot(yb_ref[0], w_ref[half:, :], preferred_element_type=F32))
    gate = _row_select(i * tm, tm, ctx_len, mod_ref, 2)
    o_ref[0] = x_ref[0] + gate * acc


def _outproj(ya, yb, xx, mod, w, *, ctx_len, tm=256):
    bsz, lc, d = xx.shape
    half = ya.shape[2]
    row = lambda b, i: (b, i, 0)
    return pl.pallas_call(
        functools.partial(_outproj_kernel, tm=tm, ctx_len=ctx_len),
        grid=(bsz, lc // tm),
        in_specs=[pl.BlockSpec((1, tm, half), row),
                  pl.BlockSpec((1, tm, half), row),
                  pl.BlockSpec((1, tm, d), row),
                  pl.BlockSpec((1, 2, 6, d), lambda b, i: (b, 0, 0, 0)),
                  pl.BlockSpec((2 * half, d), lambda b, i: (0, 0))],
        out_specs=pl.BlockSpec((1, tm, d), row),
        out_shape=jax.ShapeDtypeStruct((bsz, lc, d), F32),
        compiler_params=_params(("parallel", "arbitrary")),
        name="outproj",
    )(ya, yb, xx, mod, w)


def _first_max(vals):
    m = vals[0]
    for v in vals[1:]:
        m = jnp.maximum(m, v)
    hot = []
    taken = None
    for v in vals:
        hit = v == m
        if taken is None:
            hot.append(hit)
            taken = hit
        else:
            hot.append(jnp.logical_and(hit, jnp.logical_not(taken)))
            taken = jnp.logical_or(taken, hit)
    return m, hot


def _moe_kernel(x_ref, g_ref, mod_ref, wr_ref, br_ref, w13_ref, w2_ref, o_ref, hn_scr, gate_scr, acc_scr,
                *, tm, ctx_len, eb):
    i = pl.program_id(1)
    e = pl.program_id(2)

    @pl.when(e == 0)
    def _():
        hn = _rms_mod(x_ref[0], g_ref[...], i * tm, ctx_len, mod_ref, 3, 4)
        hn_scr[...] = hn.astype(BF16)
        h_hi, h_lo = _split_bf16(hn, 2)
        r_hi = jnp.dot(h_hi, wr_ref[...], preferred_element_type=F32)
        r_lo = jnp.dot(h_lo, wr_ref[...], preferred_element_type=F32)
        logits = (r_hi[:, :LANES] + r_hi[:, LANES:]) + (r_lo[:, :LANES] + r_lo[:, LANES:]) + br_ref[...]
        lt = jnp.concatenate([logits[c * LANES:(c + 1) * LANES, :].T for c in range(tm // LANES)], axis=1)
        gl = [lt[j:j + 1, :] for j in range(MOE_GROUPS)]
        gmax, ghot = _first_max(gl)
        gsum = sum(jnp.exp(v - gmax) for v in gl)
        p_grp = 1.0 / gsum
        elg = []
        for j in range(EXPERTS_PER_GROUP):
            v = jnp.zeros_like(gmax)
            for gi in range(MOE_GROUPS):
                c = MOE_GROUPS + gi * EXPERTS_PER_GROUP + j
                v = jnp.where(ghot[gi], lt[c:c + 1, :], v)
            elg.append(v)
        v1, hot1 = _first_max(elg)
        rest = [jnp.where(hh, -jnp.inf, v) for hh, v in zip(hot1, elg)]
        v2, hot2 = _first_max(rest)
        ex = jnp.exp(v2 - v1)
        w1 = p_grp / (1.0 + ex)
        w2 = p_grp * ex / (1.0 + ex)
        rows = []
        for gi in range(MOE_GROUPS):
            for j in range(EXPERTS_PER_GROUP):
                val = jnp.where(hot1[j], w1, 0.0) + jnp.where(hot2[j], w2, 0.0)
                rows.append(jnp.where(ghot[gi], val, 0.0))
        gates_t = jnp.concatenate(rows + [jnp.zeros((LANES - N_EXPERTS, tm), F32)], axis=0)
        for c in range(tm // LANES):
            gc = gates_t[:, c * LANES:(c + 1) * LANES].T
            for ee in range(N_EXPERTS):
                gate_scr[ee, c * LANES:(c + 1) * LANES, :] = gc[:, ee:ee + 1]
        acc_scr[...] = jnp.zeros_like(acc_scr)

    hn_b = hn_scr[...]
    upd = None
    for kk in range(eb):
        au = jnp.dot(hn_b, w13_ref[kk], preferred_element_type=F32)
        a = au[:, :D_EXPERT]
        u = au[:, D_EXPERT:]
        hid = (a * _sigmoid(a)) * u * gate_scr[e * eb + kk]
        part = jnp.dot(hid.astype(BF16), w2_ref[kk], preferred_element_type=F32)
        upd = part if upd is None else upd + part
    acc_scr[...] += upd

    @pl.when(e == N_EXPERTS // eb - 1)
    def _():
        gate = _row_select(i * tm, tm, ctx_len, mod_ref, 5)
        o_ref[0] = x_ref[0] + gate * acc_scr[...]


def _moe(xx, g, mod, wr, br, w13, w2, *, ctx_len, tm, eb=MOE_EXPERT_BLOCK):
    bsz, lc, d = xx.shape
    row = lambda b, i, e: (b, i, 0)
    return pl.pallas_call(
        functools.partial(_moe_kernel, tm=tm, ctx_len=ctx_len, eb=eb),
        grid=(bsz, lc // tm, N_EXPERTS // eb),
        in_specs=[pl.BlockSpec((1, tm, d), row),
                  pl.BlockSpec((1, d), lambda b, i, e: (0, 0)),
                  pl.BlockSpec((1, 2, 6, d), lambda b, i, e: (b, 0, 0, 0)),
                  pl.BlockSpec((d, 2 * LANES), lambda b, i, e: (0, 0)),
                  pl.BlockSpec((1, LANES), lambda b, i, e: (0, 0)),
                  pl.BlockSpec((eb, d, 2 * D_EXPERT), lambda b, i, e: (e, 0, 0)),
                  pl.BlockSpec((eb, D_EXPERT, d), lambda b, i, e: (e, 0, 0))],
        out_specs=pl.BlockSpec((1, tm, d), row),
        out_shape=jax.ShapeDtypeStruct((bsz, lc, d), F32),
        scratch_shapes=[pltpu.VMEM((tm, d), BF16),
                        pltpu.VMEM((N_EXPERTS, tm, 1), F32),
                        pltpu.VMEM((tm, d), F32)],
        compiler_params=_params(("parallel", "arbitrary", "arbitrary")),
        name="moe",
    )(xx, g, mod, wr, br, w13, w2)


def _final_kernel(x_ref, g_ref, o_ref):
    x = x_ref[0]
    o_ref[0] = x * lax.rsqrt(jnp.mean(x * x, axis=-1, keepdims=True) + EPS) * g_ref[...]


def _final_norm(xx, g, *, ctx_len, tm=256):
    bsz, lc, d = xx.shape
    lat = lc - ctx_len
    cb = ctx_len // tm
    return pl.pallas_call(
        _final_kernel,
        grid=(bsz, lat // tm),
        in_specs=[pl.BlockSpec((1, tm, d), lambda b, i: (b, cb + i, 0)),
                  pl.BlockSpec((1, d), lambda b, i: (0, 0))],
        out_specs=pl.BlockSpec((1, tm, d), lambda b, i: (b, i, 0)),
        out_shape=jax.ShapeDtypeStruct((bsz, lat, d), F32),
        compiler_params=_params(("parallel", "arbitrary")),
        name="final_norm",
    )(xx, g.reshape(1, d))


def _rope_tables(ctx_len, lat_len):
    rows = lat_len // GRID_W
    row = jnp.repeat(jnp.arange(rows), GRID_W).astype(F32)
    col = jnp.tile(jnp.arange(GRID_W), rows).astype(F32)
    n_freq = HEAD_DIM // 4
    inv = ROPE_THETA ** (-jnp.arange(n_freq, dtype=F32) / n_freq)
    ang = jnp.concatenate([row[:, None] * inv, col[:, None] * inv], axis=-1)
    cos, sin = jnp.cos(ang), jnp.sin(ang)
    cc = jnp.concatenate([cos, cos], axis=-1)
    ss = jnp.concatenate([-sin, sin], axis=-1)
    cc = jnp.concatenate([jnp.ones((ctx_len, HEAD_DIM), F32), cc], axis=0)
    ss = jnp.concatenate([jnp.zeros((ctx_len, HEAD_DIM), F32), ss], axis=0)
    return cc, ss


def _pick_tile(n, prefs):
    for t in prefs:
        if n % t == 0:
            return t
    raise ValueError(f"no tile for {n}")


def kernel(x, c, ctx, c_ctx, w_mod, b_mod, g_mix, g_ffn, moe_w_grp, moe_b_grp, moe_w_rt, moe_b_rt, moe_w1, moe_w3, moe_w2, ab_w_in, ab_w_out, ssd_conv_w, ssd_conv_b, ssd_dt_bias, ssd_a_log, ssd_d, ssd_norm_g, att_q_g, att_k_g, cd_w_in, cd_w_out, lru_conv_w, lru_conv_b, lru_w_a, lru_b_a, lru_w_x, lru_b_x, lru_lam, swa_sink, g_final):
    bsz, lat_len, d = x.shape
    ctx_len = ctx.shape[1]
    depth = w_mod.shape[0]
    lc = ctx_len + lat_len
    assert d == D_MODEL and ctx_len % 256 == 0 and lat_len % 256 == 0 and lat_len >= 3 * Q_BLOCK

    xx = jnp.concatenate([ctx, x], axis=1)
    cc, ss = _rope_tables(ctx_len, lat_len)
    mods = _modulation(c, c_ctx, w_mod, b_mod)
    expand = (jnp.arange(SSD_WIDTH)[None, :] // SSD_HEAD_DIM == jnp.arange(LANES)[:, None]).astype(BF16)
    tm_moe = _pick_tile(lc, (768, 512, 256))
    ones_h = jnp.ones((1, HEAD_DIM), F32)

    for i in range(depth):
        j = i // 2
        m = mods[i].reshape(bsz + 1, 6, d)
        mod = jnp.stack([jnp.broadcast_to(m[0], (bsz, 6, d)), m[1:]], axis=1)
        if i % 2 == 0:
            w = ab_w_in[j]
            zc = jnp.zeros((d, LANES - SSD_HEADS), w.dtype)
            o_dt = SSD_WIDTH + SSD_CONV_CH
            o_q = o_dt + 2 * SSD_HEADS
            w_packed = jnp.concatenate([w[:, :o_dt], w[:, o_dt:o_dt + SSD_HEADS], zc,
                                        w[:, o_dt + SSD_HEADS:o_q], zc, w[:, o_q:]], axis=1).astype(BF16)
            plain = ((0, SSD_WIDTH), (SSD_WIDTH, SSD_CONV_CH), (o_dt, 2 * LANES))
            z, xbc, dt, q, k, v = _inproj(xx, g_mix[i].reshape(1, d), mod, w_packed,
                                          att_q_g[j].reshape(1, HEAD_DIM), att_k_g[j].reshape(1, HEAD_DIM), cc, ss,
                                          ctx_len=ctx_len, plain=plain, q0=o_dt + 2 * LANES, qk_norm=True, tm=256)
            xbc = _dwconv(xbc, ssd_conv_w[j], ssd_conv_b[j], ctx_len=ctx_len, silu=True)
            y = _ssd(xbc, dt, ssd_dt_bias[j], ssd_a_log[j], expand, ctx_len=ctx_len)
            ya = _ssd_out(y, xbc, z, ssd_d[j], ssd_norm_g[j])
            yb = _attn_dense(q, k, v, ctx_len=ctx_len, tq=ATT_TQ, tk=_pick_tile(lc, ATT_TK_PREFS))
            w_out = ab_w_out[j]
        else:
            w_packed = cd_w_in[j].astype(BF16)
            plain = ((0, LRU_WIDTH), (LRU_WIDTH, LRU_WIDTH))
            gate, xr, q, k, v = _inproj(xx, g_mix[i].reshape(1, d), mod, w_packed, ones_h, ones_h, cc, ss,
                                        ctx_len=ctx_len, plain=plain, q0=2 * LRU_WIDTH, qk_norm=False, tm=256)
            xr = _dwconv(xr, lru_conv_w[j], lru_conv_b[j], ctx_len=ctx_len, silu=False)
            wa = _lru_slabs(lru_w_a[j])
            wx = _lru_slabs(lru_w_x[j])
            hf = _lru(xr, wa, wx, lru_b_a[j], lru_b_x[j], lru_lam[j], ctx_len=ctx_len, direction=0)
            hb = _lru(xr, wa, wx, lru_b_a[j], lru_b_x[j], lru_lam[j], ctx_len=ctx_len, direction=1)
            ya = _lru_out(gate, hf, hb)
            yb = _attn_window(q, k, v, swa_sink[j], ctx_len=ctx_len, tq=ATT_TQ)
            w_out = cd_w_out[j]
        xx = _outproj(ya, yb, xx, mod, w_out.astype(BF16), ctx_len=ctx_len)
        wr = jnp.concatenate([moe_w_grp[i], moe_w_rt[i],
                              jnp.zeros((d, LANES - MOE_GROUPS - N_EXPERTS), F32)], axis=1)
        br = jnp.concatenate([moe_b_grp[i], moe_b_rt[i],
                              jnp.zeros((LANES - MOE_GROUPS - N_EXPERTS,), F32)]).reshape(1, LANES)
        wr_hi = wr.astype(BF16)
        wr = jnp.concatenate([wr_hi, (wr - wr_hi.astype(F32)).astype(BF16)], axis=1)
        w13 = jnp.concatenate([moe_w1[i], moe_w3[i]], axis=-1).astype(BF16)
        xx = _moe(xx, g_ffn[i].reshape(1, d), mod, wr, br, w13, moe_w2[i].astype(BF16), ctx_len=ctx_len, tm=tm_moe)
    return _final_norm(xx, g_final, ctx_len=ctx_len)
```

```python
import functools
import math

import numpy as np
import jax
import jax.numpy as jnp
from jax import lax
from jax.experimental import pallas as pl
from jax.experimental.pallas import tpu as pltpu

F32 = jnp.float32
BF16 = jnp.bfloat16

D_MODEL = 1024
GRID_W = 64
EPS = 1e-6
HEAD_DIM = 128
ATT_HEADS = 8
ATT_KV_HEADS = 2
GQA_REP = ATT_HEADS // ATT_KV_HEADS
ATT_WIDTH = ATT_HEADS * HEAD_DIM
KV_WIDTH = ATT_KV_HEADS * HEAD_DIM
ATT_SCALE = HEAD_DIM ** -0.5
ROPE_THETA = 10000.0
WINDOW = 128
SSD_HEADS = 16
SSD_HEAD_DIM = 64
SSD_WIDTH = SSD_HEADS * SSD_HEAD_DIM
SSD_GROUPS = 2
SSD_STATE = 64
SSD_CHUNK = 128
SSD_CONV_CH = SSD_WIDTH + 2 * SSD_GROUPS * SSD_STATE
LRU_WIDTH = 1024
LRU_BLOCKS = 16
LRU_BLOCK_DIM = LRU_WIDTH // LRU_BLOCKS
LRU_C = 8.0
CONV_W = 5
MOE_GROUPS = 4
EXPERTS_PER_GROUP = 4
N_EXPERTS = MOE_GROUPS * EXPERTS_PER_GROUP
D_EXPERT = 256

LANES = 128
SUBLANES = 8
LRU_SLAB = 256
VMEM_LIMIT = 56 * 1024 * 1024
ROW_TILE = 256
MOE_TILE_PREFS = (1024, 768, 512, 256)
MOE_EXPERT_BLOCK = 2
ATT_TQ = 256
ATT_TK_PREFS = (2816, 1408, 768, 256)


def _params(sem):
    return pltpu.CompilerParams(dimension_semantics=sem, vmem_limit_bytes=VMEM_LIMIT)


def _pick_tile(n, prefs):
    for t in prefs:
        if n % t == 0:
            return t
    raise ValueError(f"no tile for {n}")


def _sigmoid(x):
    return 1.0 / (1.0 + jnp.exp(-x))


def _softplus(x):
    return jnp.maximum(x, 0.0) + jnp.log1p(jnp.exp(-jnp.abs(x)))


def _split_bf16(v, n):
    parts = []
    r = v
    for _ in range(n):
        p = r.astype(BF16)
        parts.append(p)
        r = r - p.astype(F32)
    return parts


def _dot_nt(a, b):
    return lax.dot_general(a, b, (((1,), (1,)), ((), ())), preferred_element_type=F32)


def _rms(x):
    return x * lax.rsqrt(jnp.mean(x * x, axis=-1, keepdims=True) + EPS)


def _row_select(row0, tm, ctx_len, mod_ref, k):
    if ctx_len == 0:
        return mod_ref[0, 1, k:k + 1, :]
    rows = row0 + lax.broadcasted_iota(jnp.int32, (tm, 1), 0)
    return jnp.where(rows < ctx_len, mod_ref[0, 0, k:k + 1, :], mod_ref[0, 1, k:k + 1, :])


def _rms_mod(x, g, row0, ctx_len, mod_ref, k_shift, k_scale):
    tm = x.shape[0]
    sh = _row_select(row0, tm, ctx_len, mod_ref, k_shift)
    sc = _row_select(row0, tm, ctx_len, mod_ref, k_scale)
    return _rms(x) * g * (1.0 + sc) + sh


def _scan_order(s, n_ctx, n_all):
    return jnp.where(s < n_ctx, n_ctx - 1 - s, n_all + n_ctx - 1 - s)


def _row_specs(tm, d, ct, split):
    if split:
        return [pl.BlockSpec((1, tm, d), lambda b, i: (b, jnp.minimum(i, ct - 1), 0)),
                pl.BlockSpec((1, tm, d), lambda b, i: (b, jnp.maximum(i - ct, 0), 0))]
    return [pl.BlockSpec((1, tm, d), lambda b, i: (b, i, 0))]


def _load_rows(refs, i, ct):
    if len(refs) == 2:
        return jnp.where(i < ct, refs[0][0], refs[1][0])
    return refs[0][0]


def _mod_kernel(ct_ref, w_ref, b_ref, o_ref, *, rows):
    w = w_ref[0]
    outs = []
    for r in range(rows):
        c = ct_ref[:, r:r + 1]
        act = c * _sigmoid(c)
        outs.append(jnp.sum(w * act, axis=0, keepdims=True) + b_ref[0])
    o_ref[0] = jnp.concatenate(outs, axis=0)


def _modulation(c, c_ctx, w_mod, b_mod):
    depth, d, n = w_mod.shape
    rows = c.shape[0] + 1
    ct = jnp.concatenate([c_ctx[None, :], c], axis=0).T
    tn = 1536
    return pl.pallas_call(
        functools.partial(_mod_kernel, rows=rows),
        grid=(depth, n // tn),
        in_specs=[pl.BlockSpec((d, rows), lambda i, j: (0, 0)),
                  pl.BlockSpec((1, d, tn), lambda i, j: (i, 0, j)),
                  pl.BlockSpec((1, 1, tn), lambda i, j: (i, 0, j))],
        out_specs=pl.BlockSpec((1, rows, tn), lambda i, j: (i, 0, j)),
        out_shape=jax.ShapeDtypeStruct((depth, rows, n), F32),
        compiler_params=_params(("arbitrary", "arbitrary")),
        name="modulation",
    )(ct, w_mod, b_mod.reshape(depth, 1, n))


def _inproj_kernel(*refs, tm, ctx_len, plain, q0, qk_norm, n_src):
    x_refs = refs[:n_src]
    g_ref, mod_ref, w_ref, qg_ref, kg_ref, cc_ref, ss_ref = refs[n_src:n_src + 7]
    out_refs = refs[n_src + 7:]
    i = pl.program_id(1)
    x = _load_rows(x_refs, i, ctx_len // tm)
    h = _rms_mod(x, g_ref[...], i * tm, ctx_len, mod_ref, 0, 1)
    res = jnp.dot(h.astype(BF16), w_ref[...], preferred_element_type=F32)
    for ref, (c0, cw) in zip(out_refs[:len(plain)], plain):
        ref[0] = res[:, c0:c0 + cw]
    q_ref, k_ref, v_ref = out_refs[len(plain):]
    cc = cc_ref[...]
    ss = ss_ref[...]

    def head(t, gain, scale):
        if qk_norm:
            t = _rms(t) * gain
        t = t * cc + pltpu.roll(t, HEAD_DIM // 2, 1) * ss
        return (t * scale).astype(BF16)

    for hh in range(ATT_HEADS):
        c0 = q0 + hh * HEAD_DIM
        q_ref[0, :, hh * HEAD_DIM:(hh + 1) * HEAD_DIM] = head(res[:, c0:c0 + HEAD_DIM], qg_ref[...], ATT_SCALE)
    k0 = q0 + ATT_WIDTH
    for hh in range(ATT_KV_HEADS):
        c0 = k0 + hh * HEAD_DIM
        k_ref[0, :, hh * HEAD_DIM:(hh + 1) * HEAD_DIM] = head(res[:, c0:c0 + HEAD_DIM], kg_ref[...], 1.0)
    v0 = k0 + KV_WIDTH
    v_ref[0] = res[:, v0:v0 + KV_WIDTH].astype(BF16)


def _inproj(srcs, lc, g, mod, w, qg, kg, cc, ss, *, ctx_len, plain, q0, qk_norm, tm=ROW_TILE):
    bsz, _, d = srcs[0].shape
    n = w.shape[1]
    row = lambda b, i: (b, i, 0)
    const = lambda b, i: (0, 0)
    out_shape = [jax.ShapeDtypeStruct((bsz, lc, cw), F32) for _, cw in plain]
    out_specs = [pl.BlockSpec((1, tm, cw), row) for _, cw in plain]
    for cw in (ATT_WIDTH, KV_WIDTH, KV_WIDTH):
        out_shape.append(jax.ShapeDtypeStruct((bsz, lc, cw), BF16))
        out_specs.append(pl.BlockSpec((1, tm, cw), row))
    return pl.pallas_call(
        functools.partial(_inproj_kernel, tm=tm, ctx_len=ctx_len, plain=plain, q0=q0, qk_norm=qk_norm,
                          n_src=len(srcs)),
        grid=(bsz, lc // tm),
        in_specs=_row_specs(tm, d, ctx_len // tm, len(srcs) == 2) + [
            pl.BlockSpec((1, d), const),
            pl.BlockSpec((1, 2, 6, d), lambda b, i: (b, 0, 0, 0)),
            pl.BlockSpec((d, n), const),
            pl.BlockSpec((1, HEAD_DIM), const),
            pl.BlockSpec((1, HEAD_DIM), const),
            pl.BlockSpec((tm, HEAD_DIM), lambda b, i: (i, 0)),
            pl.BlockSpec((tm, HEAD_DIM), lambda b, i: (i, 0))],
        out_specs=out_specs,
        out_shape=out_shape,
        compiler_params=_params(("parallel", "arbitrary")),
        name="inproj",
    )(*srcs, g, mod, w, qg, kg, cc, ss)


def _conv_rows(prev, cur, nxt, w_ref, b_ref, seg_start, seg_end, silu):
    tm = cur.shape[0]
    prev = jnp.where(seg_start, 0.0, prev)
    nxt = jnp.where(seg_end, 0.0, nxt)
    ext = jnp.concatenate([prev, cur, nxt], axis=0)
    n = tm + 2 * SUBLANES
    pad = CONV_W // 2
    acc = jnp.broadcast_to(b_ref[...], cur.shape)
    for k in range(CONV_W):
        sh = (pad - k) % n
        r = ext if sh == 0 else pltpu.roll(ext, sh, 0)
        acc = acc + r[SUBLANES:SUBLANES + tm] * w_ref[k:k + 1, :]
    if silu:
        acc = acc * _sigmoid(acc)
    return acc


def _dwconv_kernel(prev_ref, cur_ref, next_ref, w_ref, b_ref, o_ref, *, nt, ct, silu):
    i = pl.program_id(1)
    seg_start = jnp.logical_or(i == 0, i == ct)
    seg_end = jnp.logical_or(i == ct - 1, i == nt - 1)
    o_ref[0] = _conv_rows(prev_ref[0], cur_ref[0], next_ref[0], w_ref, b_ref, seg_start, seg_end, silu)


def _dwconv(u, w, b, *, ctx_len, silu, tm=ROW_TILE):
    bsz, lc, c = u.shape
    nt = lc // tm
    hb = tm // SUBLANES
    nhb = lc // SUBLANES
    return pl.pallas_call(
        functools.partial(_dwconv_kernel, nt=nt, ct=ctx_len // tm, silu=silu),
        grid=(bsz, nt),
        in_specs=[pl.BlockSpec((1, SUBLANES, c), lambda b_, i: (b_, jnp.maximum(i * hb - 1, 0), 0)),
                  pl.BlockSpec((1, tm, c), lambda b_, i: (b_, i, 0)),
                  pl.BlockSpec((1, SUBLANES, c), lambda b_, i: (b_, jnp.minimum((i + 1) * hb, nhb - 1), 0)),
                  pl.BlockSpec((CONV_W, c), lambda b_, i: (0, 0)),
                  pl.BlockSpec((1, c), lambda b_, i: (0, 0))],
        out_specs=pl.BlockSpec((1, tm, c), lambda b_, i: (b_, i, 0)),
        out_shape=jax.ShapeDtypeStruct((bsz, lc, c), F32),
        compiler_params=_params(("parallel", "arbitrary")),
        name="dwconv",
    )(u, u, u, w, b.reshape(1, c))


def _expand_heads(v, e_ref):
    hi, lo = _split_bf16(v, 2)
    e = e_ref[...]
    return (jnp.dot(hi, e, preferred_element_type=F32) + jnp.dot(lo, e, preferred_element_type=F32))


def _ssd_chunk(xbc, dt_raw, dtb, alog, e_ref, h_prev, reverse):
    q = SSD_CHUNK
    heads_per_group = SSD_HEADS // SSD_GROUPS
    x = xbc[:, :SSD_WIDTH]
    bm = xbc[:, SSD_WIDTH:SSD_WIDTH + LANES]
    cm = xbc[:, SSD_WIDTH + LANES:]
    dt = _softplus(dt_raw + dtb)
    da = dt * (-jnp.exp(alog))
    ii = lax.broadcasted_iota(jnp.int32, (q, q), 0)
    jj = lax.broadcasted_iota(jnp.int32, (q, q), 1)
    causal = (jj >= ii) if reverse else (jj <= ii)
    mask_b = jnp.where(causal, 1.0, 0.0).astype(BF16)
    da_t = da.T
    dt_t = dt.T
    acs = sum(jnp.dot(mask_b, p, preferred_element_type=F32) for p in _split_bf16(da, 3))
    acs_t = sum(_dot_nt(p, mask_b) for p in _split_bf16(da_t, 3))
    total = jnp.sum(da, axis=0, keepdims=True)
    eacs = jnp.exp(acs)

    bm_b = bm.astype(BF16)
    lane = lax.broadcasted_iota(jnp.int32, (1, LANES), 1)
    in_group = [jnp.logical_and(lane >= g * SSD_STATE, lane < (g + 1) * SSD_STATE) for g in range(SSD_GROUPS)]
    cb = [_dot_nt(jnp.where(in_group[g], cm, 0.0).astype(BF16), bm_b) for g in range(SSD_GROUPS)]

    x_b = x.astype(BF16)
    h_b = h_prev.astype(BF16)
    pairs = []
    for hp in range(SSD_HEADS // 2):
        rhs = jnp.concatenate([x_b[:, hp * LANES:(hp + 1) * LANES], h_b[:, hp * LANES:(hp + 1) * LANES]], axis=0)
        outs = []
        for t in range(2):
            h = 2 * hp + t
            seg = acs[:, h:h + 1] - acs_t[h:h + 1, :]
            dec = jnp.exp(jnp.where(causal, seg, -jnp.inf))
            w = cb[h // heads_per_group] * dec * dt_t[h:h + 1, :]
            lhs = jnp.concatenate([w.astype(BF16), (cm * eacs[:, h:h + 1]).astype(BF16)], axis=1)
            outs.append(jnp.dot(lhs, rhs, preferred_element_type=F32))
        pairs.append(jnp.where(lane < SSD_HEAD_DIM, outs[0], outs[1]))
    y = jnp.concatenate(pairs, axis=1)

    to_end = jnp.exp(total - acs) * dt
    xw = (x * _expand_heads(to_end, e_ref)).astype(BF16)
    st = jnp.dot(bm.T.astype(BF16), xw, preferred_element_type=F32)
    row_g0 = lax.broadcasted_iota(jnp.int32, st.shape, 0) < SSD_STATE
    col_g0 = lax.broadcasted_iota(jnp.int32, st.shape, 1) < heads_per_group * SSD_HEAD_DIM
    chunk_decay = _expand_heads(jnp.broadcast_to(jnp.exp(total), (SUBLANES, LANES)), e_ref)[0:1]
    return y, h_prev * chunk_decay + jnp.where(row_g0 == col_g0, st, 0.0)


def _ssd_kernel(*refs, reverse, ncx, nc, bsz):
    prev_ref, cur_ref, next_ref, cw_ref, cb_ref, dt_ref, dtb_ref, alog_ref, e_ref = refs[:9]
    if reverse:
        o_ref, hst_ref = refs[9:]
    else:
        yb_ref, z_ref, dsk_ref, ng_ref, o_ref, hst_ref = refs[9:]
    s = pl.program_id(0)
    c = _scan_order(s, ncx, nc) if reverse else s

    @pl.when(s == 0)
    def _():
        hst_ref[...] = jnp.zeros_like(hst_ref)

    seg_start = jnp.logical_or(c == 0, c == ncx)
    seg_end = jnp.logical_or(c == ncx - 1, c == nc - 1)
    for bb in range(bsz):
        xbc = _conv_rows(prev_ref[bb], cur_ref[bb], next_ref[bb], cw_ref, cb_ref, seg_start, seg_end, True)
        y, h_new = _ssd_chunk(xbc, dt_ref[bb], dtb_ref[...], alog_ref[...], e_ref, hst_ref[bb], reverse)
        hst_ref[bb] = h_new
        if reverse:
            o_ref[bb] = y
        else:
            z = z_ref[bb]
            y = (y + yb_ref[bb] + dsk_ref[...] * xbc[:, :SSD_WIDTH]) * (z * _sigmoid(z))
            o_ref[bb] = (_rms(y) * ng_ref[...]).astype(BF16)


def _ssd(xbc_raw, dt, z, conv_w, conv_b, dt_bias, a_log, d_skip, norm_g, expand, *, ctx_len):
    bsz, lc, cch = xbc_raw.shape
    q = SSD_CHUNK
    nc = lc // q
    ncx = ctx_len // q
    hb = q // SUBLANES
    nhb = lc // SUBLANES
    pad = lambda t: jnp.pad(t, (0, LANES - SSD_HEADS)).reshape(1, LANES)
    const = lambda s: (0, 0)

    def call(reverse, extra_in, extra_specs, out_dtype):
        ch = (lambda s: _scan_order(s, ncx, nc)) if reverse else (lambda s: s)
        d = 1 if reverse else 0
        w = SSD_WIDTH
        return pl.pallas_call(
            functools.partial(_ssd_kernel, reverse=reverse, ncx=ncx, nc=nc, bsz=bsz),
            grid=(nc,),
            in_specs=[pl.BlockSpec((bsz, SUBLANES, cch), lambda s: (0, jnp.maximum(ch(s) * hb - 1, 0), 0)),
                      pl.BlockSpec((bsz, q, cch), lambda s: (0, ch(s), 0)),
                      pl.BlockSpec((bsz, SUBLANES, cch), lambda s: (0, jnp.minimum((ch(s) + 1) * hb, nhb - 1), 0)),
                      pl.BlockSpec((CONV_W, cch), const),
                      pl.BlockSpec((1, cch), const),
                      pl.BlockSpec((bsz, q, LANES), lambda s: (0, ch(s), d)),
                      pl.BlockSpec((1, LANES), const),
                      pl.BlockSpec((1, LANES), const),
                      pl.BlockSpec((LANES, w), const)] + extra_specs(ch),
            out_specs=pl.BlockSpec((bsz, q, w), lambda s: (0, ch(s), 0)),
            out_shape=jax.ShapeDtypeStruct((bsz, lc, w), out_dtype),
            scratch_shapes=[pltpu.VMEM((bsz, SSD_GROUPS * SSD_STATE, w), F32)],
            compiler_params=_params(("arbitrary",)),
            name="ssd_bwd" if reverse else "ssd_fwd",
        )(xbc_raw, xbc_raw, xbc_raw, conv_w, conv_b.reshape(1, cch), dt, pad(dt_bias[d]), pad(a_log[d]), expand,
          *extra_in)

    yb = call(True, (), lambda ch: [], F32)
    w = SSD_WIDTH
    fin_specs = lambda ch: [pl.BlockSpec((bsz, q, w), lambda s: (0, ch(s), 0)),
                            pl.BlockSpec((bsz, q, w), lambda s: (0, ch(s), 0)),
                            pl.BlockSpec((1, w), const),
                            pl.BlockSpec((1, w), const)]
    return call(False, (yb, z, jnp.repeat(d_skip, SSD_HEAD_DIM).reshape(1, w), norm_g.reshape(1, w)), fin_specs, BF16)


def _stack_heads(q):
    return jnp.concatenate([q[:, r * HEAD_DIM:(r + 1) * HEAD_DIM] for r in range(q.shape[1] // HEAD_DIM)], axis=0)


def _lane_parts(s):
    return [s[:, c * LANES:(c + 1) * LANES] for c in range(s.shape[1] // LANES)]


def _softmax_parts(parts, sink=None):
    pm = parts[0]
    for t in parts[1:]:
        pm = jnp.maximum(pm, t)
    m = jnp.max(pm, axis=-1, keepdims=True)
    if sink is not None:
        m = jnp.maximum(m, sink)
    mb = jnp.broadcast_to(m, pm.shape)
    ps = [jnp.exp(t - mb) for t in parts]
    lsum = ps[0]
    for t in ps[1:]:
        lsum = lsum + t
    l = jnp.sum(lsum, axis=-1, keepdims=True)
    if sink is not None:
        l = l + jnp.exp(sink - m)
    return jnp.concatenate([t.astype(BF16) for t in ps], axis=1), l


def _store_heads(o_ref, o, tq, col0=0):
    for r in range(GQA_REP):
        c0 = col0 + r * HEAD_DIM
        o_ref[0, :, c0:c0 + HEAD_DIM] = o[r * tq:(r + 1) * tq].astype(BF16)


def _attn_dense_kernel(q_ref, k_ref, v_ref, o_ref, m_scr, l_scr, acc_scr, *, tq, tk, ctx_len, lk):
    i = pl.program_id(2)
    q4 = _stack_heads(q_ref[0])

    @pl.when(i < ctx_len // tq)
    def _():
        p, l = _softmax_parts(_lane_parts(_dot_nt(q4, k_ref[0, 0:ctx_len, :])))
        _store_heads(o_ref, jnp.dot(p, v_ref[0, 0:ctx_len, :], preferred_element_type=F32) / l, tq)

    @pl.when(i >= ctx_len // tq)
    def _():
        m_scr[...] = jnp.full_like(m_scr, -jnp.inf)
        l_scr[...] = jnp.zeros_like(l_scr)
        acc_scr[...] = jnp.zeros_like(acc_scr)

        def body(j, carry):
            off = pl.multiple_of(j * tk, tk)
            parts = _lane_parts(_dot_nt(q4, k_ref[0, pl.ds(off, tk), :]))
            pm = parts[0]
            for t in parts[1:]:
                pm = jnp.maximum(pm, t)
            m_prev = m_scr[...]
            m_new = jnp.maximum(m_prev, jnp.max(pm, axis=-1, keepdims=True))
            alpha = jnp.exp(m_prev - m_new)
            ps = [jnp.exp(t - m_new) for t in parts]
            lsum = ps[0]
            for t in ps[1:]:
                lsum = lsum + t
            l_scr[...] = alpha * l_scr[...] + lsum
            p = jnp.concatenate([t.astype(BF16) for t in ps], axis=1)
            acc_scr[...] = alpha * acc_scr[...] + jnp.dot(p, v_ref[0, pl.ds(off, tk), :], preferred_element_type=F32)
            m_scr[...] = m_new
            return carry

        lax.fori_loop(0, lk // tk, body, 0)
        _store_heads(o_ref, acc_scr[...] / jnp.sum(l_scr[...], axis=-1, keepdims=True), tq)


def _attn_dense(q, k, v, *, ctx_len, tq, tk):
    bsz, lc, _ = q.shape
    gw = GQA_REP * HEAD_DIM
    rows = GQA_REP * tq
    return pl.pallas_call(
        functools.partial(_attn_dense_kernel, tq=tq, tk=tk, ctx_len=ctx_len, lk=lc),
        grid=(bsz, ATT_KV_HEADS, lc // tq),
        in_specs=[pl.BlockSpec((1, tq, gw), lambda b, g, i: (b, i, g)),
                  pl.BlockSpec((1, lc, HEAD_DIM), lambda b, g, i: (b, 0, g)),
                  pl.BlockSpec((1, lc, HEAD_DIM), lambda b, g, i: (b, 0, g))],
        out_specs=pl.BlockSpec((1, tq, gw), lambda b, g, i: (b, i, g)),
        out_shape=jax.ShapeDtypeStruct((bsz, lc, ATT_WIDTH), BF16),
        scratch_shapes=[pltpu.VMEM((rows, LANES), F32),
                        pltpu.VMEM((rows, LANES), F32),
                        pltpu.VMEM((rows, HEAD_DIM), F32)],
        compiler_params=_params(("parallel", "parallel", "arbitrary")),
        name="attn_dense",
    )(q, k, v)


def _attn_win_kernel(q_ref, k_ref, v_ref, sink_ref, o_ref, *, ctx_len, lat_len, tq):
    n = pl.program_id(1)
    band = tq + 2 * WINDOW
    il = n - ctx_len // tq
    lo = jnp.clip(il * tq - WINDOW, 0, lat_len - band)
    start = pl.multiple_of(ctx_len + lo, WINDOW)
    rows = lax.broadcasted_iota(jnp.int32, (GQA_REP * tq, 1), 0)
    qpos = il * tq + (rows & (tq - 1))
    kpos = lo + lax.broadcasted_iota(jnp.int32, (1, band), 1)
    valid = jnp.logical_and(jnp.abs(qpos - kpos) <= WINDOW, il >= 0)
    gw = GQA_REP * HEAD_DIM
    for g in range(ATT_KV_HEADS):
        kc0 = g * HEAD_DIM
        q4 = _stack_heads(q_ref[0, :, g * gw:(g + 1) * gw])
        s_loc = jnp.where(valid, _dot_nt(q4, k_ref[0, pl.ds(start, band), kc0:kc0 + HEAD_DIM]), -jnp.inf)
        s_ctx = _dot_nt(q4, k_ref[0, 0:ctx_len, kc0:kc0 + HEAD_DIM])
        p, l = _softmax_parts(_lane_parts(s_loc) + _lane_parts(s_ctx), sink_ref[g])
        o = (jnp.dot(p[:, :band], v_ref[0, pl.ds(start, band), kc0:kc0 + HEAD_DIM], preferred_element_type=F32)
             + jnp.dot(p[:, band:], v_ref[0, 0:ctx_len, kc0:kc0 + HEAD_DIM], preferred_element_type=F32)) / l
        _store_heads(o_ref, o, tq, col0=g * gw)


def _attn_window(q, k, v, sink, *, ctx_len, tq):
    bsz, lc, _ = q.shape
    lat_len = lc - ctx_len
    rows = GQA_REP * tq
    assert tq & (tq - 1) == 0 and ctx_len % tq == 0 and lat_len >= tq + 2 * WINDOW
    sink_col = jnp.repeat(sink.reshape(ATT_KV_HEADS, GQA_REP), tq, axis=1).reshape(ATT_KV_HEADS, rows, 1)
    return pl.pallas_call(
        functools.partial(_attn_win_kernel, ctx_len=ctx_len, lat_len=lat_len, tq=tq),
        grid=(bsz, lc // tq),
        in_specs=[pl.BlockSpec((1, tq, ATT_WIDTH), lambda b, n: (b, n, 0)),
                  pl.BlockSpec((1, lc, KV_WIDTH), lambda b, n: (b, 0, 0)),
                  pl.BlockSpec((1, lc, KV_WIDTH), lambda b, n: (b, 0, 0)),
                  pl.BlockSpec((ATT_KV_HEADS, rows, 1), lambda b, n: (0, 0, 0))],
        out_specs=pl.BlockSpec((1, tq, ATT_WIDTH), lambda b, n: (b, n, 0)),
        out_shape=jax.ShapeDtypeStruct((bsz, lc, ATT_WIDTH), BF16),
        compiler_params=_params(("parallel", "arbitrary")),
        name="attn_window",
    )(q, k, v, sink_col)


def _lru_scan_rows(a, b, carry, reverse):
    t = a.shape[0]
    pos8 = lax.broadcasted_iota(jnp.int32, (t, 1), 0) & (SUBLANES - 1)
    k = 1
    while k < SUBLANES:
        if reverse:
            keep = pos8 < SUBLANES - k
            sh = t - k
        else:
            keep = pos8 >= k
            sh = k
        a_sh = jnp.where(keep, pltpu.roll(a, sh, 0), 1.0)
        b_sh = jnp.where(keep, pltpu.roll(b, sh, 0), 0.0)
        b = b + a * b_sh
        a = a * a_sh
        k *= 2
    ngroups = t // SUBLANES
    hs = [None] * ngroups
    for g in (range(ngroups - 1, -1, -1) if reverse else range(ngroups)):
        sl = slice(g * SUBLANES, (g + 1) * SUBLANES)
        hg = a[sl] * carry + b[sl]
        carry = hg[0:1] if reverse else hg[SUBLANES - 1:SUBLANES]
        hs[g] = hg
    return jnp.concatenate(hs, axis=0), carry


def _lru_kernel(*refs, reverse):
    x_ref, wa_ref, wx_ref, ba_ref, bx_ref, lam_ref = refs[:6]
    if reverse:
        o_ref, carry_ref = refs[6:]
    else:
        hb_ref, gate_ref, o_ref, carry_ref = refs[6:]
    s = pl.program_id(1)

    @pl.when(s == 0)
    def _():
        carry_ref[...] = jnp.zeros_like(carry_ref)

    x = x_ref[0]
    x_b = x.astype(BF16)
    nslab = LRU_WIDTH // LRU_SLAB

    def gate(w_ref, b_ref):
        pre = jnp.concatenate(
            [jnp.dot(x_b[:, c * LRU_SLAB:(c + 1) * LRU_SLAB], w_ref[0, c], preferred_element_type=F32)
             for c in range(nslab)], axis=1)
        return _sigmoid(pre + b_ref[0])

    r = gate(wa_ref, ba_ref)
    ig = gate(wx_ref, bx_ref)
    log_base = -LRU_C * _softplus(-lam_ref[0])
    a = jnp.exp(r * log_base)
    b = jnp.sqrt(1.0 - a * a) * (ig * x)
    h, carry = _lru_scan_rows(a, b, carry_ref[...], reverse)
    carry_ref[...] = carry
    if reverse:
        o_ref[0] = h
    else:
        g = gate_ref[0]
        gelu = 0.5 * g * (1.0 + jnp.tanh(math.sqrt(2.0 / math.pi) * (g + 0.044715 * (g * g * g))))
        o_ref[0] = (gelu * (h + hb_ref[0])).astype(BF16)


def _lru(xr, gate, wa, wx, ba, bx, lam, *, ctx_len, t=ROW_TILE):
    bsz, lc, w = xr.shape
    nt = lc // t
    ct = ctx_len // t
    nslab = w // LRU_SLAB
    vec = lambda v: v.reshape(2, 1, w)

    def call(reverse, extra_in, out_dtype):
        tile = (lambda s: _scan_order(s, ct, nt)) if reverse else (lambda s: s)
        d = 1 if reverse else 0
        row = pl.BlockSpec((1, t, w), lambda b, s: (b, tile(s), 0))
        return pl.pallas_call(
            functools.partial(_lru_kernel, reverse=reverse),
            grid=(bsz, nt),
            in_specs=[row,
                      pl.BlockSpec((1, nslab, LRU_SLAB, LRU_SLAB), lambda b, s: (d, 0, 0, 0)),
                      pl.BlockSpec((1, nslab, LRU_SLAB, LRU_SLAB), lambda b, s: (d, 0, 0, 0)),
                      pl.BlockSpec((1, 1, w), lambda b, s: (d, 0, 0)),
                      pl.BlockSpec((1, 1, w), lambda b, s: (d, 0, 0)),
                      pl.BlockSpec((1, 1, w), lambda b, s: (d, 0, 0))] + [row] * len(extra_in),
            out_specs=row,
            out_shape=jax.ShapeDtypeStruct((bsz, lc, w), out_dtype),
            scratch_shapes=[pltpu.VMEM((1, w), F32)],
            compiler_params=_params(("parallel", "arbitrary")),
            name="lru_bwd" if reverse else "lru_fwd",
        )(xr, wa, wx, vec(ba), vec(bx), vec(lam), *extra_in)

    hb = call(True, (), F32)
    return call(False, (hb, gate), BF16)


def _lru_slabs(w):
    per = LRU_SLAB // LRU_BLOCK_DIM
    nslab = LRU_BLOCKS // per
    w = w.reshape(2, nslab, per, LRU_BLOCK_DIM, LRU_BLOCK_DIM)
    eye = jnp.eye(per, dtype=w.dtype)
    full = jnp.einsum('dspij,pq->dspiqj', w, eye)
    return full.reshape(2, nslab, LRU_SLAB, LRU_SLAB).astype(BF16)


def _outproj_kernel(*refs, tm, ctx_len, t0, n_src):
    ya_ref, yb_ref = refs[:2]
    x_refs = refs[2:2 + n_src]
    mod_ref, w_ref, o_ref = refs[2 + n_src:]
    i = pl.program_id(1) + t0
    half = ya_ref.shape[2]
    acc = (jnp.dot(ya_ref[0], w_ref[0:half, :], preferred_element_type=F32)
           + jnp.dot(yb_ref[0], w_ref[half:, :], preferred_element_type=F32))
    gate = _row_select(i * tm, tm, ctx_len, mod_ref, 2)
    o_ref[0] = _load_rows(x_refs, i, ctx_len // tm) + gate * acc


def _outproj(ya, yb, srcs, mod, w, *, ctx_len, latent_only, tm=ROW_TILE):
    bsz, lc, half = ya.shape
    d = w.shape[1]
    ct = ctx_len // tm
    t0 = ct if latent_only else 0
    row = lambda b, i: (b, i + t0, 0)
    if len(srcs) == 2:
        x_specs = [pl.BlockSpec((1, tm, d), lambda b, i: (b, jnp.minimum(i + t0, ct - 1), 0)),
                   pl.BlockSpec((1, tm, d), lambda b, i: (b, jnp.maximum(i + t0 - ct, 0), 0))]
    else:
        x_specs = [pl.BlockSpec((1, tm, d), row)]
    return pl.pallas_call(
        functools.partial(_outproj_kernel, tm=tm, ctx_len=ctx_len, t0=t0, n_src=len(srcs)),
        grid=(bsz, lc // tm - t0),
        in_specs=[pl.BlockSpec((1, tm, half), row), pl.BlockSpec((1, tm, half), row)] + x_specs + [
            pl.BlockSpec((1, 2, 6, d), lambda b, i: (b, 0, 0, 0)),
            pl.BlockSpec((2 * half, d), lambda b, i: (0, 0))],
        out_specs=pl.BlockSpec((1, tm, d), lambda b, i: (b, i, 0)),
        out_shape=jax.ShapeDtypeStruct((bsz, lc - t0 * tm, d), F32),
        compiler_params=_params(("parallel", "arbitrary")),
        name="outproj",
    )(ya, yb, *srcs, mod, w)


def _first_max(vals):
    m = vals[0]
    for v in vals[1:]:
        m = jnp.maximum(m, v)
    hot = []
    taken = None
    for v in vals:
        hit = v == m
        if taken is None:
            hot.append(hit)
            taken = hit
        else:
            hot.append(jnp.logical_and(hit, jnp.logical_not(taken)))
            taken = jnp.logical_or(taken, hit)
    return m, hot


def _moe_kernel(*refs, tm, ctx_len, eb, final):
    x_ref, g_ref, mod_ref, wr_ref, br_ref, w13_ref, w2_ref = refs[:7]
    if final:
        gf_ref, o_ref, hn_scr, gate_scr, acc_scr = refs[7:]
    else:
        o_ref, hn_scr, gate_scr, acc_scr = refs[7:]
    i = pl.program_id(1)
    e = pl.program_id(2)

    @pl.when(e == 0)
    def _():
        hn = _rms_mod(x_ref[0], g_ref[...], i * tm, ctx_len, mod_ref, 3, 4)
        hn_scr[...] = hn.astype(BF16)
        h_hi, h_lo = _split_bf16(hn, 2)
        r_hi = jnp.dot(h_hi, wr_ref[...], preferred_element_type=F32)
        r_lo = jnp.dot(h_lo, wr_ref[...], preferred_element_type=F32)
        logits = (r_hi[:, :LANES] + r_hi[:, LANES:]) + (r_lo[:, :LANES] + r_lo[:, LANES:]) + br_ref[...]
        lt = jnp.concatenate([logits[c * LANES:(c + 1) * LANES, :].T for c in range(tm // LANES)], axis=1)
        gl = [lt[j:j + 1, :] for j in range(MOE_GROUPS)]
        gmax, ghot = _first_max(gl)
        gsum = sum(jnp.exp(v - gmax) for v in gl)
        p_grp = 1.0 / gsum
        elg = []
        for j in range(EXPERTS_PER_GROUP):
            v = jnp.zeros_like(gmax)
            for gi in range(MOE_GROUPS):
                c = MOE_GROUPS + gi * EXPERTS_PER_GROUP + j
                v = jnp.where(ghot[gi], lt[c:c + 1, :], v)
            elg.append(v)
        v1, hot1 = _first_max(elg)
        rest = [jnp.where(hh, -jnp.inf, v) for hh, v in zip(hot1, elg)]
        v2, hot2 = _first_max(rest)
        ex = jnp.exp(v2 - v1)
        w1 = p_grp / (1.0 + ex)
        w2 = p_grp * ex / (1.0 + ex)
        rows = []
        for gi in range(MOE_GROUPS):
            for j in range(EXPERTS_PER_GROUP):
                val = jnp.where(hot1[j], w1, 0.0) + jnp.where(hot2[j], w2, 0.0)
                rows.append(jnp.where(ghot[gi], val, 0.0))
        gates_t = jnp.concatenate(rows + [jnp.zeros((LANES - N_EXPERTS, tm), F32)], axis=0)
        for c in range(tm // LANES):
            gc = gates_t[:, c * LANES:(c + 1) * LANES].T
            for ee in range(N_EXPERTS):
                gate_scr[ee, c * LANES:(c + 1) * LANES, :] = gc[:, ee:ee + 1]
        acc_scr[...] = jnp.zeros_like(acc_scr)

    hn_b = hn_scr[...]
    upd = None
    for kk in range(eb):
        au = jnp.dot(hn_b, w13_ref[kk], preferred_element_type=F32)
        a = au[:, :D_EXPERT]
        u = au[:, D_EXPERT:]
        hid = (a * _sigmoid(a)) * u * gate_scr[e * eb + kk]
        part = jnp.dot(hid.astype(BF16), w2_ref[kk], preferred_element_type=F32)
        upd = part if upd is None else upd + part
    acc_scr[...] += upd

    @pl.when(e == N_EXPERTS // eb - 1)
    def _():
        gate = _row_select(i * tm, tm, ctx_len, mod_ref, 5)
        out = x_ref[0] + gate * acc_scr[...]
        if final:
            out = _rms(out) * gf_ref[...]
        o_ref[0] = out


def _moe(xx, g, mod, wr, br, w13, w2, g_final, *, ctx_len, eb=MOE_EXPERT_BLOCK):
    bsz, rows, d = xx.shape
    tm = _pick_tile(rows, MOE_TILE_PREFS)
    row = lambda b, i, e: (b, i, 0)
    const = lambda b, i, e: (0, 0)
    final = g_final is not None
    return pl.pallas_call(
        functools.partial(_moe_kernel, tm=tm, ctx_len=ctx_len, eb=eb, final=final),
        grid=(bsz, rows // tm, N_EXPERTS // eb),
        in_specs=[pl.BlockSpec((1, tm, d), row),
                  pl.BlockSpec((1, d), const),
                  pl.BlockSpec((1, 2, 6, d), lambda b, i, e: (b, 0, 0, 0)),
                  pl.BlockSpec((d, 2 * LANES), const),
                  pl.BlockSpec((1, LANES), const),
                  pl.BlockSpec((eb, d, 2 * D_EXPERT), lambda b, i, e: (e, 0, 0)),
                  pl.BlockSpec((eb, D_EXPERT, d), lambda b, i, e: (e, 0, 0))]
                 + ([pl.BlockSpec((1, d), const)] if final else []),
        out_specs=pl.BlockSpec((1, tm, d), row),
        out_shape=jax.ShapeDtypeStruct((bsz, rows, d), F32),
        scratch_shapes=[pltpu.VMEM((tm, d), BF16),
                        pltpu.VMEM((N_EXPERTS, tm, 1), F32),
                        pltpu.VMEM((tm, d), F32)],
        compiler_params=_params(("parallel", "arbitrary", "arbitrary")),
        name="moe",
    )(xx, g, mod, wr, br, w13, w2, *((g_final.reshape(1, d),) if final else ()))


def _rope_tables(ctx_len, lat_len):
    rows = lat_len // GRID_W
    row = np.repeat(np.arange(rows), GRID_W).astype(np.float64)
    col = np.tile(np.arange(GRID_W), rows).astype(np.float64)
    n_freq = HEAD_DIM // 4
    inv = ROPE_THETA ** (-np.arange(n_freq, dtype=np.float64) / n_freq)
    ang = np.concatenate([row[:, None] * inv, col[:, None] * inv], axis=-1)
    cos, sin = np.cos(ang), np.sin(ang)
    cc = np.concatenate([np.ones((ctx_len, HEAD_DIM)), np.concatenate([cos, cos], axis=-1)], axis=0)
    ss = np.concatenate([np.zeros((ctx_len, HEAD_DIM)), np.concatenate([-sin, sin], axis=-1)], axis=0)
    return jnp.asarray(cc, F32), jnp.asarray(ss, F32)


def kernel(x, c, ctx, c_ctx, w_mod, b_mod, g_mix, g_ffn, moe_w_grp, moe_b_grp, moe_w_rt, moe_b_rt, moe_w1, moe_w3, moe_w2, ab_w_in, ab_w_out, ssd_conv_w, ssd_conv_b, ssd_dt_bias, ssd_a_log, ssd_d, ssd_norm_g, att_q_g, att_k_g, cd_w_in, cd_w_out, lru_conv_w, lru_conv_b, lru_w_a, lru_b_a, lru_w_x, lru_b_x, lru_lam, swa_sink, g_final):
    bsz, lat_len, d = x.shape
    ctx_len = ctx.shape[1]
    depth = w_mod.shape[0]
    lc = ctx_len + lat_len
    assert d == D_MODEL and ctx_len % ROW_TILE == 0 and lat_len % ROW_TILE == 0 and ctx_len > 0

    cc, ss = _rope_tables(ctx_len, lat_len)
    mods = _modulation(c, c_ctx, w_mod, b_mod)
    expand = jnp.asarray(np.arange(SSD_WIDTH)[None, :] // SSD_HEAD_DIM == np.arange(LANES)[:, None], BF16)
    ones_h = jnp.ones((1, HEAD_DIM), F32)
    srcs = (ctx, x)

    for i in range(depth):
        j = i // 2
        last = i == depth - 1
        m = mods[i].reshape(bsz + 1, 6, d)
        mod = jnp.stack([jnp.broadcast_to(m[0], (bsz, 6, d)), m[1:]], axis=1)
        if i % 2 == 0:
            w = ab_w_in[j]
            zc = jnp.zeros((d, LANES - SSD_HEADS), w.dtype)
            o_dt = SSD_WIDTH + SSD_CONV_CH
            o_q = o_dt + 2 * SSD_HEADS
            w_packed = jnp.concatenate([w[:, :o_dt], w[:, o_dt:o_dt + SSD_HEADS], zc,
                                        w[:, o_dt + SSD_HEADS:o_q], zc, w[:, o_q:]], axis=1).astype(BF16)
            plain = ((0, SSD_WIDTH), (SSD_WIDTH, SSD_CONV_CH), (o_dt, 2 * LANES))
            z, xbc, dt, q, k, v = _inproj(srcs, lc, g_mix[i].reshape(1, d), mod, w_packed,
                                          att_q_g[j].reshape(1, HEAD_DIM), att_k_g[j].reshape(1, HEAD_DIM), cc, ss,
                                          ctx_len=ctx_len, plain=plain, q0=o_dt + 2 * LANES, qk_norm=True)
            ya = _ssd(xbc, dt, z, ssd_conv_w[j], ssd_conv_b[j], ssd_dt_bias[j], ssd_a_log[j], ssd_d[j],
                      ssd_norm_g[j], expand, ctx_len=ctx_len)
            yb = _attn_dense(q, k, v, ctx_len=ctx_len, tq=ATT_TQ, tk=_pick_tile(lc, ATT_TK_PREFS))
            w_out = ab_w_out[j]
        else:
            plain = ((0, LRU_WIDTH), (LRU_WIDTH, LRU_WIDTH))
            gate, xr, q, k, v = _inproj(srcs, lc, g_mix[i].reshape(1, d), mod, cd_w_in[j].astype(BF16),
                                        ones_h, ones_h, cc, ss,
                                        ctx_len=ctx_len, plain=plain, q0=2 * LRU_WIDTH, qk_norm=False)
            xr = _dwconv(xr, lru_conv_w[j], lru_conv_b[j], ctx_len=ctx_len, silu=False)
            ya = _lru(xr, gate, _lru_slabs(lru_w_a[j]), _lru_slabs(lru_w_x[j]), lru_b_a[j], lru_b_x[j], lru_lam[j],
                      ctx_len=ctx_len)
            yb = _attn_window(q, k, v, swa_sink[j], ctx_len=ctx_len, tq=ATT_TQ)
            w_out = cd_w_out[j]
        xs = _outproj(ya, yb, srcs, mod, w_out.astype(BF16), ctx_len=ctx_len, latent_only=last)
        wr = jnp.concatenate([moe_w_grp[i], moe_w_rt[i],
                              jnp.zeros((d, LANES - MOE_GROUPS - N_EXPERTS), F32)], axis=1)
        br = jnp.concatenate([moe_b_grp[i], moe_b_rt[i],
                              jnp.zeros((LANES - MOE_GROUPS - N_EXPERTS,), F32)]).reshape(1, LANES)
        wr_hi = wr.astype(BF16)
        wr = jnp.concatenate([wr_hi, (wr - wr_hi.astype(F32)).astype(BF16)], axis=1)
        w13 = jnp.concatenate([moe_w1[i], moe_w3[i]], axis=-1).astype(BF16)
        xs = _moe(xs, g_ffn[i].reshape(1, d), mod, wr, br, w13, moe_w2[i].astype(BF16),
                  g_final if last else None, ctx_len=0 if last else ctx_len)
        srcs = (xs,)
    return xs
```

```python
import functools
import math

import numpy as np
import jax
import jax.numpy as jnp
from jax import lax
from jax.experimental import pallas as pl
from jax.experimental.pallas import tpu as pltpu

F32 = jnp.float32
BF16 = jnp.bfloat16

D_MODEL = 1024
GRID_W = 64
EPS = 1e-6
HEAD_DIM = 128
ATT_HEADS = 8
ATT_KV_HEADS = 2
GQA_REP = ATT_HEADS // ATT_KV_HEADS
ATT_WIDTH = ATT_HEADS * HEAD_DIM
KV_WIDTH = ATT_KV_HEADS * HEAD_DIM
ATT_SCALE = HEAD_DIM ** -0.5
ROPE_THETA = 10000.0
WINDOW = 128
SSD_HEADS = 16
SSD_HEAD_DIM = 64
SSD_WIDTH = SSD_HEADS * SSD_HEAD_DIM
SSD_GROUPS = 2
SSD_STATE = 64
SSD_CHUNK = 128
SSD_CONV_CH = SSD_WIDTH + 2 * SSD_GROUPS * SSD_STATE
LRU_WIDTH = 1024
LRU_BLOCKS = 16
LRU_BLOCK_DIM = LRU_WIDTH // LRU_BLOCKS
LRU_C = 8.0
CONV_W = 5
MOE_GROUPS = 4
EXPERTS_PER_GROUP = 4
N_EXPERTS = MOE_GROUPS * EXPERTS_PER_GROUP
D_EXPERT = 256

LANES = 128
SUBLANES = 8
LRU_SLAB = 256
VMEM_LIMIT = 56 * 1024 * 1024
ROW_TILE = 256
MOE_TILE_PREFS = (1024, 768, 512, 256)
ATT_TQ = 256
ATT_TK_PREFS = (2816, 1408, 768, 256)


def _params(sem):
    return pltpu.CompilerParams(dimension_semantics=sem, vmem_limit_bytes=VMEM_LIMIT)


def _pick_tile(n, prefs):
    for t in prefs:
        if n % t == 0:
            return t
    raise ValueError(f"no tile for {n}")


def _sigmoid(x):
    return 1.0 / (1.0 + jnp.exp(-x))


def _softplus(x):
    return jnp.maximum(x, 0.0) + jnp.log1p(jnp.exp(-jnp.abs(x)))


def _split_bf16(v, n):
    parts = []
    r = v
    for _ in range(n):
        p = r.astype(BF16)
        parts.append(p)
        r = r - p.astype(F32)
    return parts


def _dot_nt(a, b):
    return lax.dot_general(a, b, (((1,), (1,)), ((), ())), preferred_element_type=F32)


def _rms(x):
    return x * lax.rsqrt(jnp.mean(x * x, axis=-1, keepdims=True) + EPS)


def _row_select(row0, tm, ctx_len, mod_ref, k):
    if ctx_len == 0:
        return mod_ref[0, 1, k:k + 1, :]
    rows = row0 + lax.broadcasted_iota(jnp.int32, (tm, 1), 0)
    return jnp.where(rows < ctx_len, mod_ref[0, 0, k:k + 1, :], mod_ref[0, 1, k:k + 1, :])


def _rms_mod(x, g, row0, ctx_len, mod_ref, k_shift, k_scale):
    tm = x.shape[0]
    sh = _row_select(row0, tm, ctx_len, mod_ref, k_shift)
    sc = _row_select(row0, tm, ctx_len, mod_ref, k_scale)
    return _rms(x) * g * (1.0 + sc) + sh


def _scan_order(s, n_ctx, n_all):
    return jnp.where(s < n_ctx, n_ctx - 1 - s, n_all + n_ctx - 1 - s)


def _row_specs(tm, d, ct, split):
    if split:
        return [pl.BlockSpec((1, tm, d), lambda b, i: (b, jnp.minimum(i, ct - 1), 0)),
                pl.BlockSpec((1, tm, d), lambda b, i: (b, jnp.maximum(i - ct, 0), 0))]
    return [pl.BlockSpec((1, tm, d), lambda b, i: (b, i, 0))]


def _load_rows(refs, i, ct):
    if len(refs) == 2:
        return jnp.where(i < ct, refs[0][0], refs[1][0])
    return refs[0][0]


def _mod_kernel(ct_ref, w_ref, b_ref, o_ref, *, rows):
    w = w_ref[0]
    outs = []
    for r in range(rows):
        c = ct_ref[:, r:r + 1]
        act = c * _sigmoid(c)
        outs.append(jnp.sum(w * act, axis=0, keepdims=True) + b_ref[0])
    o_ref[0] = jnp.concatenate(outs, axis=0)


def _modulation(c, c_ctx, w_mod, b_mod):
    depth, d, n = w_mod.shape
    rows = c.shape[0] + 1
    ct = jnp.concatenate([c_ctx[None, :], c], axis=0).T
    tn = 1536
    return pl.pallas_call(
        functools.partial(_mod_kernel, rows=rows),
        grid=(depth, n // tn),
        in_specs=[pl.BlockSpec((d, rows), lambda i, j: (0, 0)),
                  pl.BlockSpec((1, d, tn), lambda i, j: (i, 0, j)),
                  pl.BlockSpec((1, 1, tn), lambda i, j: (i, 0, j))],
        out_specs=pl.BlockSpec((1, rows, tn), lambda i, j: (i, 0, j)),
        out_shape=jax.ShapeDtypeStruct((depth, rows, n), F32),
        compiler_params=_params(("arbitrary", "arbitrary")),
        name="modulation",
    )(ct, w_mod, b_mod.reshape(depth, 1, n))


def _inproj_kernel(*refs, tm, ctx_len, plain, q0, qk_norm, n_src):
    x_refs = refs[:n_src]
    g_ref, mod_ref, w_ref, qg_ref, kg_ref, cc_ref, ss_ref = refs[n_src:n_src + 7]
    out_refs = refs[n_src + 7:]
    i = pl.program_id(1)
    x = _load_rows(x_refs, i, ctx_len // tm)
    h = _rms_mod(x, g_ref[...], i * tm, ctx_len, mod_ref, 0, 1)
    res = jnp.dot(h.astype(BF16), w_ref[...], preferred_element_type=F32)
    for ref, (c0, cw) in zip(out_refs[:len(plain)], plain):
        ref[0] = res[:, c0:c0 + cw]
    q_ref, k_ref, v_ref = out_refs[len(plain):]
    cc = cc_ref[...]
    ss = ss_ref[...]

    def head(t, gain, scale):
        if qk_norm:
            t = _rms(t) * gain
        t = t * cc + pltpu.roll(t, HEAD_DIM // 2, 1) * ss
        return (t * scale).astype(BF16)

    for hh in range(ATT_HEADS):
        c0 = q0 + hh * HEAD_DIM
        q_ref[0, :, hh * HEAD_DIM:(hh + 1) * HEAD_DIM] = head(res[:, c0:c0 + HEAD_DIM], qg_ref[...], ATT_SCALE)
    k0 = q0 + ATT_WIDTH
    for hh in range(ATT_KV_HEADS):
        c0 = k0 + hh * HEAD_DIM
        k_ref[0, :, hh * HEAD_DIM:(hh + 1) * HEAD_DIM] = head(res[:, c0:c0 + HEAD_DIM], kg_ref[...], 1.0)
    v0 = k0 + KV_WIDTH
    v_ref[0] = res[:, v0:v0 + KV_WIDTH].astype(BF16)


def _inproj(srcs, lc, g, mod, w, qg, kg, cc, ss, *, ctx_len, plain, q0, qk_norm, tm=ROW_TILE):
    bsz, _, d = srcs[0].shape
    n = w.shape[1]
    row = lambda b, i: (b, i, 0)
    const = lambda b, i: (0, 0)
    out_shape = [jax.ShapeDtypeStruct((bsz, lc, cw), F32) for _, cw in plain]
    out_specs = [pl.BlockSpec((1, tm, cw), row) for _, cw in plain]
    for cw in (ATT_WIDTH, KV_WIDTH, KV_WIDTH):
        out_shape.append(jax.ShapeDtypeStruct((bsz, lc, cw), BF16))
        out_specs.append(pl.BlockSpec((1, tm, cw), row))
    return pl.pallas_call(
        functools.partial(_inproj_kernel, tm=tm, ctx_len=ctx_len, plain=plain, q0=q0, qk_norm=qk_norm,
                          n_src=len(srcs)),
        grid=(bsz, lc // tm),
        in_specs=_row_specs(tm, d, ctx_len // tm, len(srcs) == 2) + [
            pl.BlockSpec((1, d), const),
            pl.BlockSpec((1, 2, 6, d), lambda b, i: (b, 0, 0, 0)),
            pl.BlockSpec((d, n), const),
            pl.BlockSpec((1, HEAD_DIM), const),
            pl.BlockSpec((1, HEAD_DIM), const),
            pl.BlockSpec((tm, HEAD_DIM), lambda b, i: (i, 0)),
            pl.BlockSpec((tm, HEAD_DIM), lambda b, i: (i, 0))],
        out_specs=out_specs,
        out_shape=out_shape,
        compiler_params=_params(("parallel", "arbitrary")),
        name="inproj",
    )(*srcs, g, mod, w, qg, kg, cc, ss)


def _conv_rows(prev, cur, nxt, w_ref, b_ref, seg_start, seg_end, silu):
    tm = cur.shape[0]
    prev = jnp.where(seg_start, 0.0, prev)
    nxt = jnp.where(seg_end, 0.0, nxt)
    ext = jnp.concatenate([prev, cur, nxt], axis=0)
    n = tm + 2 * SUBLANES
    pad = CONV_W // 2
    acc = jnp.broadcast_to(b_ref[...], cur.shape)
    for k in range(CONV_W):
        sh = (pad - k) % n
        r = ext if sh == 0 else pltpu.roll(ext, sh, 0)
        acc = acc + r[SUBLANES:SUBLANES + tm] * w_ref[k:k + 1, :]
    if silu:
        acc = acc * _sigmoid(acc)
    return acc


def _dwconv_kernel(prev_ref, cur_ref, next_ref, w_ref, b_ref, o_ref, *, nt, ct, silu):
    i = pl.program_id(1)
    seg_start = jnp.logical_or(i == 0, i == ct)
    seg_end = jnp.logical_or(i == ct - 1, i == nt - 1)
    o_ref[0] = _conv_rows(prev_ref[0], cur_ref[0], next_ref[0], w_ref, b_ref, seg_start, seg_end, silu)


def _dwconv(u, w, b, *, ctx_len, silu, tm=ROW_TILE):
    bsz, lc, c = u.shape
    nt = lc // tm
    hb = tm // SUBLANES
    nhb = lc // SUBLANES
    return pl.pallas_call(
        functools.partial(_dwconv_kernel, nt=nt, ct=ctx_len // tm, silu=silu),
        grid=(bsz, nt),
        in_specs=[pl.BlockSpec((1, SUBLANES, c), lambda b_, i: (b_, jnp.maximum(i * hb - 1, 0), 0)),
                  pl.BlockSpec((1, tm, c), lambda b_, i: (b_, i, 0)),
                  pl.BlockSpec((1, SUBLANES, c), lambda b_, i: (b_, jnp.minimum((i + 1) * hb, nhb - 1), 0)),
                  pl.BlockSpec((CONV_W, c), lambda b_, i: (0, 0)),
                  pl.BlockSpec((1, c), lambda b_, i: (0, 0))],
        out_specs=pl.BlockSpec((1, tm, c), lambda b_, i: (b_, i, 0)),
        out_shape=jax.ShapeDtypeStruct((bsz, lc, c), F32),
        compiler_params=_params(("parallel", "arbitrary")),
        name="dwconv",
    )(u, u, u, w, b.reshape(1, c))


def _expand_heads(v, e_ref):
    hi, lo = _split_bf16(v, 2)
    e = e_ref[...]
    return (jnp.dot(hi, e, preferred_element_type=F32) + jnp.dot(lo, e, preferred_element_type=F32))


def _ssd_chunk(xbc, dt_raw, dtb, alog, e_ref, h_prev, reverse):
    q = SSD_CHUNK
    heads_per_group = SSD_HEADS // SSD_GROUPS
    x = xbc[:, :SSD_WIDTH]
    bm = xbc[:, SSD_WIDTH:SSD_WIDTH + LANES]
    cm = xbc[:, SSD_WIDTH + LANES:]
    dt = _softplus(dt_raw + dtb)
    da = dt * (-jnp.exp(alog))
    ii = lax.broadcasted_iota(jnp.int32, (q, q), 0)
    jj = lax.broadcasted_iota(jnp.int32, (q, q), 1)
    causal = (jj >= ii) if reverse else (jj <= ii)
    mask_b = jnp.where(causal, 1.0, 0.0).astype(BF16)
    da_t = da.T
    dt_t = dt.T
    acs = sum(jnp.dot(mask_b, p, preferred_element_type=F32) for p in _split_bf16(da, 3))
    acs_t = sum(_dot_nt(p, mask_b) for p in _split_bf16(da_t, 3))
    total = jnp.sum(da, axis=0, keepdims=True)
    eacs = jnp.exp(acs)

    bm_b = bm.astype(BF16)
    lane = lax.broadcasted_iota(jnp.int32, (1, LANES), 1)
    in_group = [jnp.logical_and(lane >= g * SSD_STATE, lane < (g + 1) * SSD_STATE) for g in range(SSD_GROUPS)]
    cb = [_dot_nt(jnp.where(in_group[g], cm, 0.0).astype(BF16), bm_b) for g in range(SSD_GROUPS)]

    x_b = x.astype(BF16)
    h_b = h_prev.astype(BF16)
    pairs = []
    for hp in range(SSD_HEADS // 2):
        rhs = jnp.concatenate([x_b[:, hp * LANES:(hp + 1) * LANES], h_b[:, hp * LANES:(hp + 1) * LANES]], axis=0)
        outs = []
        for t in range(2):
            h = 2 * hp + t
            seg = acs[:, h:h + 1] - acs_t[h:h + 1, :]
            dec = jnp.exp(jnp.where(causal, seg, -jnp.inf))
            w = cb[h // heads_per_group] * dec * dt_t[h:h + 1, :]
            lhs = jnp.concatenate([w.astype(BF16), (cm * eacs[:, h:h + 1]).astype(BF16)], axis=1)
            outs.append(jnp.dot(lhs, rhs, preferred_element_type=F32))
        pairs.append(jnp.where(lane < SSD_HEAD_DIM, outs[0], outs[1]))
    y = jnp.concatenate(pairs, axis=1)

    to_end = jnp.exp(total - acs) * dt
    xw = (x * _expand_heads(to_end, e_ref)).astype(BF16)
    st = jnp.dot(bm.T.astype(BF16), xw, preferred_element_type=F32)
    row_g0 = lax.broadcasted_iota(jnp.int32, st.shape, 0) < SSD_STATE
    col_g0 = lax.broadcasted_iota(jnp.int32, st.shape, 1) < heads_per_group * SSD_HEAD_DIM
    chunk_decay = _expand_heads(jnp.broadcast_to(jnp.exp(total), (SUBLANES, LANES)), e_ref)[0:1]
    return y, h_prev * chunk_decay + jnp.where(row_g0 == col_g0, st, 0.0)


def _ssd_kernel(*refs, reverse, ncx, nc, bsz):
    prev_ref, cur_ref, next_ref, cw_ref, cb_ref, dt_ref, dtb_ref, alog_ref, e_ref = refs[:9]
    if reverse:
        o_ref, hst_ref = refs[9:]
    else:
        yb_ref, z_ref, dsk_ref, ng_ref, o_ref, hst_ref = refs[9:]
    s = pl.program_id(0)
    c = _scan_order(s, ncx, nc) if reverse else s

    @pl.when(s == 0)
    def _():
        hst_ref[...] = jnp.zeros_like(hst_ref)

    seg_start = jnp.logical_or(c == 0, c == ncx)
    seg_end = jnp.logical_or(c == ncx - 1, c == nc - 1)
    for bb in range(bsz):
        xbc = _conv_rows(prev_ref[bb], cur_ref[bb], next_ref[bb], cw_ref, cb_ref, seg_start, seg_end, True)
        y, h_new = _ssd_chunk(xbc, dt_ref[bb], dtb_ref[...], alog_ref[...], e_ref, hst_ref[bb], reverse)
        hst_ref[bb] = h_new
        if reverse:
            o_ref[bb] = y
        else:
            z = z_ref[bb]
            y = (y + yb_ref[bb] + dsk_ref[...] * xbc[:, :SSD_WIDTH]) * (z * _sigmoid(z))
            o_ref[bb] = (_rms(y) * ng_ref[...]).astype(BF16)


def _ssd(xbc_raw, dt, z, conv_w, conv_b, dt_bias, a_log, d_skip, norm_g, expand, *, ctx_len):
    bsz, lc, cch = xbc_raw.shape
    q = SSD_CHUNK
    nc = lc // q
    ncx = ctx_len // q
    hb = q // SUBLANES
    nhb = lc // SUBLANES
    pad = lambda t: jnp.pad(t, (0, LANES - SSD_HEADS)).reshape(1, LANES)
    const = lambda s: (0, 0)

    def call(reverse, extra_in, extra_specs, out_dtype):
        ch = (lambda s: _scan_order(s, ncx, nc)) if reverse else (lambda s: s)
        d = 1 if reverse else 0
        w = SSD_WIDTH
        return pl.pallas_call(
            functools.partial(_ssd_kernel, reverse=reverse, ncx=ncx, nc=nc, bsz=bsz),
            grid=(nc,),
            in_specs=[pl.BlockSpec((bsz, SUBLANES, cch), lambda s: (0, jnp.maximum(ch(s) * hb - 1, 0), 0)),
                      pl.BlockSpec((bsz, q, cch), lambda s: (0, ch(s), 0)),
                      pl.BlockSpec((bsz, SUBLANES, cch), lambda s: (0, jnp.minimum((ch(s) + 1) * hb, nhb - 1), 0)),
                      pl.BlockSpec((CONV_W, cch), const),
                      pl.BlockSpec((1, cch), const),
                      pl.BlockSpec((bsz, q, LANES), lambda s: (0, ch(s), d)),
                      pl.BlockSpec((1, LANES), const),
                      pl.BlockSpec((1, LANES), const),
                      pl.BlockSpec((LANES, w), const)] + extra_specs(ch),
            out_specs=pl.BlockSpec((bsz, q, w), lambda s: (0, ch(s), 0)),
            out_shape=jax.ShapeDtypeStruct((bsz, lc, w), out_dtype),
            scratch_shapes=[pltpu.VMEM((bsz, SSD_GROUPS * SSD_STATE, w), F32)],
            compiler_params=_params(("arbitrary",)),
            name="ssd_bwd" if reverse else "ssd_fwd",
        )(xbc_raw, xbc_raw, xbc_raw, conv_w, conv_b.reshape(1, cch), dt, pad(dt_bias[d]), pad(a_log[d]), expand,
          *extra_in)

    yb = call(True, (), lambda ch: [], F32)
    w = SSD_WIDTH
    fin_specs = lambda ch: [pl.BlockSpec((bsz, q, w), lambda s: (0, ch(s), 0)),
                            pl.BlockSpec((bsz, q, w), lambda s: (0, ch(s), 0)),
                            pl.BlockSpec((1, w), const),
                            pl.BlockSpec((1, w), const)]
    return call(False, (yb, z, jnp.repeat(d_skip, SSD_HEAD_DIM).reshape(1, w), norm_g.reshape(1, w)), fin_specs, BF16)


def _stack_heads(q):
    return jnp.concatenate([q[:, r * HEAD_DIM:(r + 1) * HEAD_DIM] for r in range(q.shape[1] // HEAD_DIM)], axis=0)


def _lane_parts(s):
    return [s[:, c * LANES:(c + 1) * LANES] for c in range(s.shape[1] // LANES)]


def _softmax_parts(parts, sink=None):
    pm = parts[0]
    for t in parts[1:]:
        pm = jnp.maximum(pm, t)
    m = jnp.max(pm, axis=-1, keepdims=True)
    if sink is not None:
        m = jnp.maximum(m, sink)
    mb = jnp.broadcast_to(m, pm.shape)
    ps = [jnp.exp(t - mb) for t in parts]
    lsum = ps[0]
    for t in ps[1:]:
        lsum = lsum + t
    l = jnp.sum(lsum, axis=-1, keepdims=True)
    if sink is not None:
        l = l + jnp.exp(sink - m)
    return jnp.concatenate([t.astype(BF16) for t in ps], axis=1), l


def _store_heads(o_ref, o, tq, col0=0):
    for r in range(GQA_REP):
        c0 = col0 + r * HEAD_DIM
        o_ref[0, :, c0:c0 + HEAD_DIM] = o[r * tq:(r + 1) * tq].astype(BF16)


def _attn_dense_kernel(q_ref, k_ref, v_ref, o_ref, m_scr, l_scr, acc_scr, *, tq, tk, ctx_len, lk):
    i = pl.program_id(2)
    q4 = _stack_heads(q_ref[0])

    @pl.when(i < ctx_len // tq)
    def _():
        p, l = _softmax_parts(_lane_parts(_dot_nt(q4, k_ref[0, 0:ctx_len, :])))
        _store_heads(o_ref, jnp.dot(p, v_ref[0, 0:ctx_len, :], preferred_element_type=F32) / l, tq)

    @pl.when(i >= ctx_len // tq)
    def _():
        m_scr[...] = jnp.full_like(m_scr, -jnp.inf)
        l_scr[...] = jnp.zeros_like(l_scr)
        acc_scr[...] = jnp.zeros_like(acc_scr)

        def body(j, carry):
            off = pl.multiple_of(j * tk, tk)
            parts = _lane_parts(_dot_nt(q4, k_ref[0, pl.ds(off, tk), :]))
            pm = parts[0]
            for t in parts[1:]:
                pm = jnp.maximum(pm, t)
            m_prev = m_scr[...]
            m_new = jnp.maximum(m_prev, jnp.max(pm, axis=-1, keepdims=True))
            alpha = jnp.exp(m_prev - m_new)
            ps = [jnp.exp(t - m_new) for t in parts]
            lsum = ps[0]
            for t in ps[1:]:
                lsum = lsum + t
            l_scr[...] = alpha * l_scr[...] + lsum
            p = jnp.concatenate([t.astype(BF16) for t in ps], axis=1)
            acc_scr[...] = alpha * acc_scr[...] + jnp.dot(p, v_ref[0, pl.ds(off, tk), :], preferred_element_type=F32)
            m_scr[...] = m_new
            return carry

        lax.fori_loop(0, lk // tk, body, 0)
        _store_heads(o_ref, acc_scr[...] / jnp.sum(l_scr[...], axis=-1, keepdims=True), tq)


def _attn_dense(q, k, v, *, ctx_len, tq, tk):
    bsz, lc, _ = q.shape
    gw = GQA_REP * HEAD_DIM
    rows = GQA_REP * tq
    return pl.pallas_call(
        functools.partial(_attn_dense_kernel, tq=tq, tk=tk, ctx_len=ctx_len, lk=lc),
        grid=(bsz, ATT_KV_HEADS, lc // tq),
        in_specs=[pl.BlockSpec((1, tq, gw), lambda b, g, i: (b, i, g)),
                  pl.BlockSpec((1, lc, HEAD_DIM), lambda b, g, i: (b, 0, g)),
                  pl.BlockSpec((1, lc, HEAD_DIM), lambda b, g, i: (b, 0, g))],
        out_specs=pl.BlockSpec((1, tq, gw), lambda b, g, i: (b, i, g)),
        out_shape=jax.ShapeDtypeStruct((bsz, lc, ATT_WIDTH), BF16),
        scratch_shapes=[pltpu.VMEM((rows, LANES), F32),
                        pltpu.VMEM((rows, LANES), F32),
                        pltpu.VMEM((rows, HEAD_DIM), F32)],
        compiler_params=_params(("parallel", "parallel", "arbitrary")),
        name="attn_dense",
    )(q, k, v)


def _attn_win_kernel(q_ref, k_ref, v_ref, sink_ref, o_ref, *, ctx_len, lat_len, tq):
    n = pl.program_id(1)
    band = tq + 2 * WINDOW
    il = n - ctx_len // tq
    lo = jnp.clip(il * tq - WINDOW, 0, lat_len - band)
    start = pl.multiple_of(ctx_len + lo, WINDOW)
    rows = lax.broadcasted_iota(jnp.int32, (GQA_REP * tq, 1), 0)
    qpos = il * tq + (rows & (tq - 1))
    kpos = lo + lax.broadcasted_iota(jnp.int32, (1, band), 1)
    valid = jnp.logical_and(jnp.abs(qpos - kpos) <= WINDOW, il >= 0)
    gw = GQA_REP * HEAD_DIM
    for g in range(ATT_KV_HEADS):
        kc0 = g * HEAD_DIM
        q4 = _stack_heads(q_ref[0, :, g * gw:(g + 1) * gw])
        s_loc = jnp.where(valid, _dot_nt(q4, k_ref[0, pl.ds(start, band), kc0:kc0 + HEAD_DIM]), -jnp.inf)
        s_ctx = _dot_nt(q4, k_ref[0, 0:ctx_len, kc0:kc0 + HEAD_DIM])
        p, l = _softmax_parts(_lane_parts(s_loc) + _lane_parts(s_ctx), sink_ref[g])
        o = (jnp.dot(p[:, :band], v_ref[0, pl.ds(start, band), kc0:kc0 + HEAD_DIM], preferred_element_type=F32)
             + jnp.dot(p[:, band:], v_ref[0, 0:ctx_len, kc0:kc0 + HEAD_DIM], preferred_element_type=F32)) / l
        _store_heads(o_ref, o, tq, col0=g * gw)


def _attn_window(q, k, v, sink, *, ctx_len, tq):
    bsz, lc, _ = q.shape
    lat_len = lc - ctx_len
    rows = GQA_REP * tq
    assert tq & (tq - 1) == 0 and ctx_len % tq == 0 and lat_len >= tq + 2 * WINDOW
    sink_col = jnp.repeat(sink.reshape(ATT_KV_HEADS, GQA_REP), tq, axis=1).reshape(ATT_KV_HEADS, rows, 1)
    return pl.pallas_call(
        functools.partial(_attn_win_kernel, ctx_len=ctx_len, lat_len=lat_len, tq=tq),
        grid=(bsz, lc // tq),
        in_specs=[pl.BlockSpec((1, tq, ATT_WIDTH), lambda b, n: (b, n, 0)),
                  pl.BlockSpec((1, lc, KV_WIDTH), lambda b, n: (b, 0, 0)),
                  pl.BlockSpec((1, lc, KV_WIDTH), lambda b, n: (b, 0, 0)),
                  pl.BlockSpec((ATT_KV_HEADS, rows, 1), lambda b, n: (0, 0, 0))],
        out_specs=pl.BlockSpec((1, tq, ATT_WIDTH), lambda b, n: (b, n, 0)),
        out_shape=jax.ShapeDtypeStruct((bsz, lc, ATT_WIDTH), BF16),
        compiler_params=_params(("parallel", "arbitrary")),
        name="attn_window",
    )(q, k, v, sink_col)


def _lru_scan_rows(a, b, carry, reverse):
    t = a.shape[0]
    pos8 = lax.broadcasted_iota(jnp.int32, (t, 1), 0) & (SUBLANES - 1)
    k = 1
    while k < SUBLANES:
        if reverse:
            keep = pos8 < SUBLANES - k
            sh = t - k
        else:
            keep = pos8 >= k
            sh = k
        a_sh = jnp.where(keep, pltpu.roll(a, sh, 0), 1.0)
        b_sh = jnp.where(keep, pltpu.roll(b, sh, 0), 0.0)
        b = b + a * b_sh
        a = a * a_sh
        k *= 2
    ngroups = t // SUBLANES
    hs = [None] * ngroups
    for g in (range(ngroups - 1, -1, -1) if reverse else range(ngroups)):
        sl = slice(g * SUBLANES, (g + 1) * SUBLANES)
        hg = a[sl] * carry + b[sl]
        carry = hg[0:1] if reverse else hg[SUBLANES - 1:SUBLANES]
        hs[g] = hg
    return jnp.concatenate(hs, axis=0), carry


def _lru_kernel(*refs, reverse):
    x_ref, wa_ref, wx_ref, ba_ref, bx_ref, lam_ref = refs[:6]
    if reverse:
        o_ref, carry_ref = refs[6:]
    else:
        hb_ref, gate_ref, o_ref, carry_ref = refs[6:]
    s = pl.program_id(1)

    @pl.when(s == 0)
    def _():
        carry_ref[...] = jnp.zeros_like(carry_ref)

    x = x_ref[0]
    x_b = x.astype(BF16)
    nslab = LRU_WIDTH // LRU_SLAB

    def gate(w_ref, b_ref):
        pre = jnp.concatenate(
            [jnp.dot(x_b[:, c * LRU_SLAB:(c + 1) * LRU_SLAB], w_ref[0, c], preferred_element_type=F32)
             for c in range(nslab)], axis=1)
        return _sigmoid(pre + b_ref[0])

    r = gate(wa_ref, ba_ref)
    ig = gate(wx_ref, bx_ref)
    log_base = -LRU_C * _softplus(-lam_ref[0])
    a = jnp.exp(r * log_base)
    b = jnp.sqrt(1.0 - a * a) * (ig * x)
    h, carry = _lru_scan_rows(a, b, carry_ref[...], reverse)
    carry_ref[...] = carry
    if reverse:
        o_ref[0] = h
    else:
        g = gate_ref[0]
        gelu = 0.5 * g * (1.0 + jnp.tanh(math.sqrt(2.0 / math.pi) * (g + 0.044715 * (g * g * g))))
        o_ref[0] = (gelu * (h + hb_ref[0])).astype(BF16)


def _lru(xr, gate, wa, wx, ba, bx, lam, *, ctx_len, t=ROW_TILE):
    bsz, lc, w = xr.shape
    nt = lc // t
    ct = ctx_len // t
    nslab = w // LRU_SLAB
    vec = lambda v: v.reshape(2, 1, w)

    def call(reverse, extra_in, out_dtype):
        tile = (lambda s: _scan_order(s, ct, nt)) if reverse else (lambda s: s)
        d = 1 if reverse else 0
        row = pl.BlockSpec((1, t, w), lambda b, s: (b, tile(s), 0))
        return pl.pallas_call(
            functools.partial(_lru_kernel, reverse=reverse),
            grid=(bsz, nt),
            in_specs=[row,
                      pl.BlockSpec((1, nslab, LRU_SLAB, LRU_SLAB), lambda b, s: (d, 0, 0, 0)),
                      pl.BlockSpec((1, nslab, LRU_SLAB, LRU_SLAB), lambda b, s: (d, 0, 0, 0)),
                      pl.BlockSpec((1, 1, w), lambda b, s: (d, 0, 0)),
                      pl.BlockSpec((1, 1, w), lambda b, s: (d, 0, 0)),
                      pl.BlockSpec((1, 1, w), lambda b, s: (d, 0, 0))] + [row] * len(extra_in),
            out_specs=row,
            out_shape=jax.ShapeDtypeStruct((bsz, lc, w), out_dtype),
            scratch_shapes=[pltpu.VMEM((1, w), F32)],
            compiler_params=_params(("parallel", "arbitrary")),
            name="lru_bwd" if reverse else "lru_fwd",
        )(xr, wa, wx, vec(ba), vec(bx), vec(lam), *extra_in)

    hb = call(True, (), F32)
    return call(False, (hb, gate), BF16)


def _lru_slabs(w):
    per = LRU_SLAB // LRU_BLOCK_DIM
    nslab = LRU_BLOCKS // per
    w = w.reshape(2, nslab, per, LRU_BLOCK_DIM, LRU_BLOCK_DIM)
    eye = jnp.eye(per, dtype=w.dtype)
    full = jnp.einsum('dspij,pq->dspiqj', w, eye)
    return full.reshape(2, nslab, LRU_SLAB, LRU_SLAB).astype(BF16)


def _outproj_kernel(*refs, tm, ctx_len, t0, n_src):
    ya_ref, yb_ref = refs[:2]
    x_refs = refs[2:2 + n_src]
    mod_ref, w_ref, o_ref = refs[2 + n_src:]
    i = pl.program_id(1) + t0
    half = ya_ref.shape[2]
    acc = (jnp.dot(ya_ref[0], w_ref[0:half, :], preferred_element_type=F32)
           + jnp.dot(yb_ref[0], w_ref[half:, :], preferred_element_type=F32))
    gate = _row_select(i * tm, tm, ctx_len, mod_ref, 2)
    o_ref[0] = _load_rows(x_refs, i, ctx_len // tm) + gate * acc


def _outproj(ya, yb, srcs, mod, w, *, ctx_len, latent_only, tm=ROW_TILE):
    bsz, lc, half = ya.shape
    d = w.shape[1]
    ct = ctx_len // tm
    t0 = ct if latent_only else 0
    row = lambda b, i: (b, i + t0, 0)
    if len(srcs) == 2:
        x_specs = [pl.BlockSpec((1, tm, d), lambda b, i: (b, jnp.minimum(i + t0, ct - 1), 0)),
                   pl.BlockSpec((1, tm, d), lambda b, i: (b, jnp.maximum(i + t0 - ct, 0), 0))]
    else:
        x_specs = [pl.BlockSpec((1, tm, d), row)]
    return pl.pallas_call(
        functools.partial(_outproj_kernel, tm=tm, ctx_len=ctx_len, t0=t0, n_src=len(srcs)),
        grid=(bsz, lc // tm - t0),
        in_specs=[pl.BlockSpec((1, tm, half), row), pl.BlockSpec((1, tm, half), row)] + x_specs + [
            pl.BlockSpec((1, 2, 6, d), lambda b, i: (b, 0, 0, 0)),
            pl.BlockSpec((2 * half, d), lambda b, i: (0, 0))],
        out_specs=pl.BlockSpec((1, tm, d), lambda b, i: (b, i, 0)),
        out_shape=jax.ShapeDtypeStruct((bsz, lc - t0 * tm, d), F32),
        compiler_params=_params(("parallel", "arbitrary")),
        name="outproj",
    )(ya, yb, *srcs, mod, w)


def _first_max(vals):
    m = vals[0]
    for v in vals[1:]:
        m = jnp.maximum(m, v)
    hot = []
    taken = None
    for v in vals:
        hit = v == m
        if taken is None:
            hot.append(hit)
            taken = hit
        else:
            hot.append(jnp.logical_and(hit, jnp.logical_not(taken)))
            taken = jnp.logical_or(taken, hit)
    return m, hot


def _moe_kernel(*refs, tm, ctx_len, win, final):
    x_ref, g_ref, mod_ref, wr_ref, br_ref, w13_ref, w2_ref = refs[:7]
    if final:
        gf_ref = refs[7]
    o_ref, hn_scr, gate_scr, rank_row_scr, rank_col_scr, cnt_ref, acc_scr = refs[-7:]
    i = pl.program_id(1)
    e = pl.program_id(2)

    @pl.when(e == 0)
    def _():
        hn = _rms_mod(x_ref[0], g_ref[...], i * tm, ctx_len, mod_ref, 3, 4)
        hn_scr[...] = hn.astype(BF16)
        h_hi, h_lo = _split_bf16(hn, 2)
        r_hi = jnp.dot(h_hi, wr_ref[...], preferred_element_type=F32)
        r_lo = jnp.dot(h_lo, wr_ref[...], preferred_element_type=F32)
        logits = (r_hi[:, :LANES] + r_hi[:, LANES:]) + (r_lo[:, :LANES] + r_lo[:, LANES:]) + br_ref[...]
        lt = jnp.concatenate([logits[c * LANES:(c + 1) * LANES, :].T for c in range(tm // LANES)], axis=1)
        gl = [lt[j:j + 1, :] for j in range(MOE_GROUPS)]
        gmax, ghot = _first_max(gl)
        gsum = sum(jnp.exp(v - gmax) for v in gl)
        p_grp = 1.0 / gsum
        elg = []
        for j in range(EXPERTS_PER_GROUP):
            v = jnp.zeros_like(gmax)
            for gi in range(MOE_GROUPS):
                c = MOE_GROUPS + gi * EXPERTS_PER_GROUP + j
                v = jnp.where(ghot[gi], lt[c:c + 1, :], v)
            elg.append(v)
        v1, hot1 = _first_max(elg)
        rest = [jnp.where(hh, -jnp.inf, v) for hh, v in zip(hot1, elg)]
        v2, hot2 = _first_max(rest)
        ex = jnp.exp(v2 - v1)
        w1 = p_grp / (1.0 + ex)
        w2 = p_grp * ex / (1.0 + ex)
        own = [jnp.where(hot1[j], w1, 0.0) + jnp.where(hot2[j], w2, 0.0) for j in range(EXPERTS_PER_GROUP)]
        pad_rows = jnp.zeros((LANES - MOE_GROUPS, tm), F32)
        g_t = jnp.concatenate([jnp.where(h_, 1.0, 0.0) for h_ in ghot] + [pad_rows], axis=0)
        t0 = lax.broadcasted_iota(jnp.int32, (tm, tm), 0)
        t1 = lax.broadcasted_iota(jnp.int32, (tm, tm), 1)
        before = jnp.where(t0 < t1, 1.0, 0.0).astype(BF16)
        rank_t = jnp.where(g_t > 0.0, jnp.dot(g_t.astype(BF16), before, preferred_element_type=F32), -1.0)
        rank_row_scr[...] = rank_t[0:SUBLANES]
        for gi in range(MOE_GROUPS):
            cnt_ref[gi] = jnp.sum(g_t[gi:gi + 1, :]).astype(jnp.int32)
            slab_t = jnp.concatenate([jnp.where(ghot[gi], own[j], 0.0) for j in range(EXPERTS_PER_GROUP)]
                                     + [pad_rows], axis=0)
            for c in range(tm // LANES):
                rows_c = slice(c * LANES, (c + 1) * LANES)
                slab = slab_t[:, rows_c].T
                hi, lo = _split_bf16(slab, 2)
                gate_scr[gi, rows_c, 0:LANES] = hi
                gate_scr[gi, rows_c, LANES:] = lo
                if gi == 0:
                    rank_c = rank_t[:, rows_c].T
                    for gj in range(MOE_GROUPS):
                        rank_col_scr[gj, rows_c, :] = rank_c[:, gj:gj + 1]
        acc_scr[...] = jnp.zeros_like(acc_scr)

    cnt = cnt_ref[e]
    rank_row = rank_row_scr[pl.ds(e, 1), :]
    rank_col = rank_col_scr[e]
    slot_r = lax.broadcasted_iota(jnp.int32, (win, 1), 0).astype(F32)
    slot_c = lax.broadcasted_iota(jnp.int32, (1, win), 1).astype(F32)

    def window(wi, carry):
        base = (wi * win).astype(F32)
        sel = jnp.where(rank_row - base == slot_r, 1.0, 0.0).astype(BF16)
        sel_t = jnp.where(rank_col - base == slot_c, 1.0, 0.0).astype(BF16)
        xs = jnp.dot(sel, hn_scr[...], preferred_element_type=F32).astype(BF16)
        gs = jnp.dot(sel, gate_scr[e], preferred_element_type=F32)
        gsel = gs[:, :LANES] + gs[:, LANES:]
        yw = None
        for kk in range(EXPERTS_PER_GROUP):
            au = jnp.dot(xs, w13_ref[kk], preferred_element_type=F32)
            a = au[:, :D_EXPERT]
            u = au[:, D_EXPERT:]
            hid = (a * _sigmoid(a)) * u * gsel[:, kk:kk + 1]
            part = jnp.dot(hid.astype(BF16), w2_ref[kk], preferred_element_type=F32)
            yw = part if yw is None else yw + part
        acc_scr[...] += jnp.dot(sel_t, yw.astype(BF16), preferred_element_type=F32)
        return carry

    lax.fori_loop(0, (cnt + (win - 1)) // win, window, 0)

    @pl.when(e == MOE_GROUPS - 1)
    def _():
        gate = _row_select(i * tm, tm, ctx_len, mod_ref, 5)
        out = x_ref[0] + gate * acc_scr[...]
        if final:
            out = _rms(out) * gf_ref[...]
        o_ref[0] = out


def _moe(xx, g, mod, wr, br, w13, w2, g_final, *, ctx_len):
    bsz, rows, d = xx.shape
    tm = _pick_tile(rows, MOE_TILE_PREFS)
    win = -(-(tm // MOE_GROUPS + tm // 16) // LANES) * LANES
    eb = EXPERTS_PER_GROUP
    row = lambda b, i, e: (b, i, 0)
    const = lambda b, i, e: (0, 0)
    final = g_final is not None
    return pl.pallas_call(
        functools.partial(_moe_kernel, tm=tm, ctx_len=ctx_len, win=win, final=final),
        grid=(bsz, rows // tm, MOE_GROUPS),
        in_specs=[pl.BlockSpec((1, tm, d), row),
                  pl.BlockSpec((1, d), const),
                  pl.BlockSpec((1, 2, 6, d), lambda b, i, e: (b, 0, 0, 0)),
                  pl.BlockSpec((d, 2 * LANES), const),
                  pl.BlockSpec((1, LANES), const),
                  pl.BlockSpec((eb, d, 2 * D_EXPERT), lambda b, i, e: (e, 0, 0)),
                  pl.BlockSpec((eb, D_EXPERT, d), lambda b, i, e: (e, 0, 0))]
                 + ([pl.BlockSpec((1, d), const)] if final else []),
        out_specs=pl.BlockSpec((1, tm, d), row),
        out_shape=jax.ShapeDtypeStruct((bsz, rows, d), F32),
        scratch_shapes=[pltpu.VMEM((tm, d), BF16),
                        pltpu.VMEM((MOE_GROUPS, tm, 2 * LANES), BF16),
                        pltpu.VMEM((SUBLANES, tm), F32),
                        pltpu.VMEM((MOE_GROUPS, tm, 1), F32),
                        pltpu.SMEM((MOE_GROUPS,), jnp.int32),
                        pltpu.VMEM((tm, d), F32)],
        compiler_params=_params(("parallel", "arbitrary", "arbitrary")),
        name="moe",
    )(xx, g, mod, wr, br, w13, w2, *((g_final.reshape(1, d),) if final else ()))


def _rope_tables(ctx_len, lat_len):
    rows = lat_len // GRID_W
    row = np.repeat(np.arange(rows), GRID_W).astype(np.float64)
    col = np.tile(np.arange(GRID_W), rows).astype(np.float64)
    n_freq = HEAD_DIM // 4
    inv = ROPE_THETA ** (-np.arange(n_freq, dtype=np.float64) / n_freq)
    ang = np.concatenate([row[:, None] * inv, col[:, None] * inv], axis=-1)
    cos, sin = np.cos(ang), np.sin(ang)
    cc = np.concatenate([np.ones((ctx_len, HEAD_DIM)), np.concatenate([cos, cos], axis=-1)], axis=0)
    ss = np.concatenate([np.zeros((ctx_len, HEAD_DIM)), np.concatenate([-sin, sin], axis=-1)], axis=0)
    return jnp.asarray(cc, F32), jnp.asarray(ss, F32)


def kernel(x, c, ctx, c_ctx, w_mod, b_mod, g_mix, g_ffn, moe_w_grp, moe_b_grp, moe_w_rt, moe_b_rt, moe_w1, moe_w3, moe_w2, ab_w_in, ab_w_out, ssd_conv_w, ssd_conv_b, ssd_dt_bias, ssd_a_log, ssd_d, ssd_norm_g, att_q_g, att_k_g, cd_w_in, cd_w_out, lru_conv_w, lru_conv_b, lru_w_a, lru_b_a, lru_w_x, lru_b_x, lru_lam, swa_sink, g_final):
    bsz, lat_len, d = x.shape
    ctx_len = ctx.shape[1]
    depth = w_mod.shape[0]
    lc = ctx_len + lat_len
    assert d == D_MODEL and ctx_len % ROW_TILE == 0 and lat_len % ROW_TILE == 0 and ctx_len > 0

    cc, ss = _rope_tables(ctx_len, lat_len)
    mods = _modulation(c, c_ctx, w_mod, b_mod)
    expand = jnp.asarray(np.arange(SSD_WIDTH)[None, :] // SSD_HEAD_DIM == np.arange(LANES)[:, None], BF16)
    ones_h = jnp.ones((1, HEAD_DIM), F32)
    srcs = (ctx, x)

    for i in range(depth):
        j = i // 2
        last = i == depth - 1
        m = mods[i].reshape(bsz + 1, 6, d)
        mod = jnp.stack([jnp.broadcast_to(m[0], (bsz, 6, d)), m[1:]], axis=1)
        if i % 2 == 0:
            w = ab_w_in[j]
            zc = jnp.zeros((d, LANES - SSD_HEADS), w.dtype)
            o_dt = SSD_WIDTH + SSD_CONV_CH
            o_q = o_dt + 2 * SSD_HEADS
            w_packed = jnp.concatenate([w[:, :o_dt], w[:, o_dt:o_dt + SSD_HEADS], zc,
                                        w[:, o_dt + SSD_HEADS:o_q], zc, w[:, o_q:]], axis=1).astype(BF16)
            plain = ((0, SSD_WIDTH), (SSD_WIDTH, SSD_CONV_CH), (o_dt, 2 * LANES))
            z, xbc, dt, q, k, v = _inproj(srcs, lc, g_mix[i].reshape(1, d), mod, w_packed,
                                          att_q_g[j].reshape(1, HEAD_DIM), att_k_g[j].reshape(1, HEAD_DIM), cc, ss,
                                          ctx_len=ctx_len, plain=plain, q0=o_dt + 2 * LANES, qk_norm=True)
            ya = _ssd(xbc, dt, z, ssd_conv_w[j], ssd_conv_b[j], ssd_dt_bias[j], ssd_a_log[j], ssd_d[j],
                      ssd_norm_g[j], expand, ctx_len=ctx_len)
            yb = _attn_dense(q, k, v, ctx_len=ctx_len, tq=ATT_TQ, tk=_pick_tile(lc, ATT_TK_PREFS))
            w_out = ab_w_out[j]
        else:
            plain = ((0, LRU_WIDTH), (LRU_WIDTH, LRU_WIDTH))
            gate, xr, q, k, v = _inproj(srcs, lc, g_mix[i].reshape(1, d), mod, cd_w_in[j].astype(BF16),
                                        ones_h, ones_h, cc, ss,
                                        ctx_len=ctx_len, plain=plain, q0=2 * LRU_WIDTH, qk_norm=False)
            xr = _dwconv(xr, lru_conv_w[j], lru_conv_b[j], ctx_len=ctx_len, silu=False)
            ya = _lru(xr, gate, _lru_slabs(lru_w_a[j]), _lru_slabs(lru_w_x[j]), lru_b_a[j], lru_b_x[j], lru_lam[j],
                      ctx_len=ctx_len)
            yb = _attn_window(q, k, v, swa_sink[j], ctx_len=ctx_len, tq=ATT_TQ)
            w_out = cd_w_out[j]
        xs = _outproj(ya, yb, srcs, mod, w_out.astype(BF16), ctx_len=ctx_len, latent_only=last)
        wr = jnp.concatenate([moe_w_grp[i], moe_w_rt[i],
                              jnp.zeros((d, LANES - MOE_GROUPS - N_EXPERTS), F32)], axis=1)
        br = jnp.concatenate([moe_b_grp[i], moe_b_rt[i],
                              jnp.zeros((LANES - MOE_GROUPS - N_EXPERTS,), F32)]).reshape(1, LANES)
        wr_hi = wr.astype(BF16)
        wr = jnp.concatenate([wr_hi, (wr - wr_hi.astype(F32)).astype(BF16)], axis=1)
        w13 = jnp.concatenate([moe_w1[i], moe_w3[i]], axis=-1).astype(BF16)
        xs = _moe(xs, g_ffn[i].reshape(1, d), mod, wr, br, w13, moe_w2[i].astype(BF16),
                  g_final if last else None, ctx_len=0 if last else ctx_len)
        srcs = (xs,)
    return xs
```

```python
import functools
import math

import numpy as np
import jax
import jax.numpy as jnp
from jax import lax
from jax.experimental import pallas as pl
from jax.experimental.pallas import tpu as pltpu

F32 = jnp.float32
BF16 = jnp.bfloat16

D_MODEL = 1024
GRID_W = 64
EPS = 1e-6
HEAD_DIM = 128
ATT_HEADS = 8
ATT_KV_HEADS = 2
GQA_REP = ATT_HEADS // ATT_KV_HEADS
ATT_WIDTH = ATT_HEADS * HEAD_DIM
KV_WIDTH = ATT_KV_HEADS * HEAD_DIM
ATT_SCALE = HEAD_DIM ** -0.5
ROPE_THETA = 10000.0
WINDOW = 128
SSD_HEADS = 16
SSD_HEAD_DIM = 64
SSD_WIDTH = SSD_HEADS * SSD_HEAD_DIM
SSD_GROUPS = 2
SSD_STATE = 64
SSD_CHUNK = 128
SSD_CONV_CH = SSD_WIDTH + 2 * SSD_GROUPS * SSD_STATE
LRU_WIDTH = 1024
LRU_BLOCKS = 16
LRU_BLOCK_DIM = LRU_WIDTH // LRU_BLOCKS
LRU_C = 8.0
CONV_W = 5
MOE_GROUPS = 4
EXPERTS_PER_GROUP = 4
N_EXPERTS = MOE_GROUPS * EXPERTS_PER_GROUP
D_EXPERT = 256

LANES = 128
SUBLANES = 8
LRU_SLAB = 256
VMEM_LIMIT = 56 * 1024 * 1024
ROW_TILE = 256
MOE_TILE_PREFS = (768, 512, 256)
ATT_TQ = 256
ATT_TK_PREFS = (2816, 1408, 768, 256)


def _params(sem):
    return pltpu.CompilerParams(dimension_semantics=sem, vmem_limit_bytes=VMEM_LIMIT)


def _pick_tile(n, prefs):
    for t in prefs:
        if n % t == 0:
            return t
    raise ValueError(f"no tile for {n}")


def _sigmoid(x):
    return 0.5 * jnp.tanh(0.5 * x) + 0.5


def _softplus(x):
    return jnp.maximum(x, 0.0) + jnp.log1p(jnp.exp(-jnp.abs(x)))


def _split_bf16(v, n):
    parts = []
    r = v
    for _ in range(n):
        p = r.astype(BF16)
        parts.append(p)
        r = r - p.astype(F32)
    return parts


def _dot_nt(a, b):
    return lax.dot_general(a, b, (((1,), (1,)), ((), ())), preferred_element_type=F32)


def _rms(x):
    return x * lax.rsqrt(jnp.mean(x * x, axis=-1, keepdims=True) + EPS)


def _row_select(row0, tm, ctx_len, mod_ref, k):
    if ctx_len == 0:
        return mod_ref[0, 1, k:k + 1, :]
    rows = row0 + lax.broadcasted_iota(jnp.int32, (tm, 1), 0)
    return jnp.where(rows < ctx_len, mod_ref[0, 0, k:k + 1, :], mod_ref[0, 1, k:k + 1, :])


def _rms_mod(x, g, row0, ctx_len, mod_ref, k_shift, k_scale):
    tm = x.shape[0]
    sh = _row_select(row0, tm, ctx_len, mod_ref, k_shift)
    sc = _row_select(row0, tm, ctx_len, mod_ref, k_scale)
    return _rms(x) * g * (1.0 + sc) + sh


def _scan_order(s, n_ctx, n_all):
    return jnp.where(s < n_ctx, n_ctx - 1 - s, n_all + n_ctx - 1 - s)


def _row_specs(tm, d, ct, split):
    if split:
        return [pl.BlockSpec((1, tm, d), lambda b, i: (b, jnp.minimum(i, ct - 1), 0)),
                pl.BlockSpec((1, tm, d), lambda b, i: (b, jnp.maximum(i - ct, 0), 0))]
    return [pl.BlockSpec((1, tm, d), lambda b, i: (b, i, 0))]


def _load_rows(refs, i, ct):
    if len(refs) == 2:
        return jnp.where(i < ct, refs[0][0], refs[1][0])
    return refs[0][0]


def _mod_kernel(ct_ref, w_ref, b_ref, o_ref, *, rows):
    w = w_ref[0]
    outs = []
    for r in range(rows):
        c = ct_ref[:, r:r + 1]
        act = c * _sigmoid(c)
        outs.append(jnp.sum(w * act, axis=0, keepdims=True) + b_ref[0])
    o_ref[0] = jnp.concatenate(outs, axis=0)


def _modulation(c, c_ctx, w_mod, b_mod):
    depth, d, n = w_mod.shape
    rows = c.shape[0] + 1
    ct = jnp.concatenate([c_ctx[None, :], c], axis=0).T
    tn = 1536
    return pl.pallas_call(
        functools.partial(_mod_kernel, rows=rows),
        grid=(depth, n // tn),
        in_specs=[pl.BlockSpec((d, rows), lambda i, j: (0, 0)),
                  pl.BlockSpec((1, d, tn), lambda i, j: (i, 0, j)),
                  pl.BlockSpec((1, 1, tn), lambda i, j: (i, 0, j))],
        out_specs=pl.BlockSpec((1, rows, tn), lambda i, j: (i, 0, j)),
        out_shape=jax.ShapeDtypeStruct((depth, rows, n), F32),
        compiler_params=_params(("arbitrary", "arbitrary")),
        name="modulation",
    )(ct, w_mod, b_mod.reshape(depth, 1, n))


def _inproj_kernel(*refs, tm, ctx_len, plain, q0, qk_norm, n_src):
    x_refs = refs[:n_src]
    g_ref, mod_ref, w_ref, qg_ref, kg_ref, cc_ref, ss_ref = refs[n_src:n_src + 7]
    out_refs = refs[n_src + 7:]
    i = pl.program_id(1)
    x = _load_rows(x_refs, i, ctx_len // tm)
    h = _rms_mod(x, g_ref[...], i * tm, ctx_len, mod_ref, 0, 1)
    res = jnp.dot(h.astype(BF16), w_ref[...], preferred_element_type=F32)
    for ref, (c0, cw) in zip(out_refs[:len(plain)], plain):
        ref[0] = res[:, c0:c0 + cw]
    q_ref, k_ref, v_ref = out_refs[len(plain):]
    cc = cc_ref[...]
    ss = ss_ref[...]

    def head(t, gain, scale):
        if qk_norm:
            t = _rms(t) * gain
        t = t * cc + pltpu.roll(t, HEAD_DIM // 2, 1) * ss
        return (t * scale).astype(BF16)

    for hh in range(ATT_HEADS):
        c0 = q0 + hh * HEAD_DIM
        q_ref[0, :, hh * HEAD_DIM:(hh + 1) * HEAD_DIM] = head(res[:, c0:c0 + HEAD_DIM], qg_ref[...], ATT_SCALE)
    k0 = q0 + ATT_WIDTH
    for hh in range(ATT_KV_HEADS):
        c0 = k0 + hh * HEAD_DIM
        k_ref[0, :, hh * HEAD_DIM:(hh + 1) * HEAD_DIM] = head(res[:, c0:c0 + HEAD_DIM], kg_ref[...], 1.0)
    v0 = k0 + KV_WIDTH
    v_ref[0] = res[:, v0:v0 + KV_WIDTH].astype(BF16)


def _inproj(srcs, lc, g, mod, w, qg, kg, cc, ss, *, ctx_len, plain, q0, qk_norm, tm=ROW_TILE):
    bsz, _, d = srcs[0].shape
    n = w.shape[1]
    row = lambda b, i: (b, i, 0)
    const = lambda b, i: (0, 0)
    out_shape = [jax.ShapeDtypeStruct((bsz, lc, cw), F32) for _, cw in plain]
    out_specs = [pl.BlockSpec((1, tm, cw), row) for _, cw in plain]
    for cw in (ATT_WIDTH, KV_WIDTH, KV_WIDTH):
        out_shape.append(jax.ShapeDtypeStruct((bsz, lc, cw), BF16))
        out_specs.append(pl.BlockSpec((1, tm, cw), row))
    return pl.pallas_call(
        functools.partial(_inproj_kernel, tm=tm, ctx_len=ctx_len, plain=plain, q0=q0, qk_norm=qk_norm,
                          n_src=len(srcs)),
        grid=(bsz, lc // tm),
        in_specs=_row_specs(tm, d, ctx_len // tm, len(srcs) == 2) + [
            pl.BlockSpec((1, d), const),
            pl.BlockSpec((1, 2, 6, d), lambda b, i: (b, 0, 0, 0)),
            pl.BlockSpec((d, n), const),
            pl.BlockSpec((1, HEAD_DIM), const),
            pl.BlockSpec((1, HEAD_DIM), const),
            pl.BlockSpec((tm, HEAD_DIM), lambda b, i: (i, 0)),
            pl.BlockSpec((tm, HEAD_DIM), lambda b, i: (i, 0))],
        out_specs=out_specs,
        out_shape=out_shape,
        compiler_params=_params(("parallel", "arbitrary")),
        name="inproj",
    )(*srcs, g, mod, w, qg, kg, cc, ss)


def _conv_rows(prev, cur, nxt, w_ref, b_ref, seg_start, seg_end, silu):
    tm = cur.shape[0]
    prev = jnp.where(seg_start, 0.0, prev)
    nxt = jnp.where(seg_end, 0.0, nxt)
    ext = jnp.concatenate([prev, cur, nxt], axis=0)
    n = tm + 2 * SUBLANES
    pad = CONV_W // 2
    acc = jnp.broadcast_to(b_ref[...], cur.shape)
    for k in range(CONV_W):
        sh = (pad - k) % n
        r = ext if sh == 0 else pltpu.roll(ext, sh, 0)
        acc = acc + r[SUBLANES:SUBLANES + tm] * w_ref[k:k + 1, :]
    if silu:
        acc = acc * _sigmoid(acc)
    return acc


def _dwconv_kernel(prev_ref, cur_ref, next_ref, w_ref, b_ref, o_ref, *, nt, ct, silu):
    i = pl.program_id(1)
    seg_start = jnp.logical_or(i == 0, i == ct)
    seg_end = jnp.logical_or(i == ct - 1, i == nt - 1)
    o_ref[0] = _conv_rows(prev_ref[0], cur_ref[0], next_ref[0], w_ref, b_ref, seg_start, seg_end, silu)


def _dwconv(u, w, b, *, ctx_len, silu, tm=ROW_TILE):
    bsz, lc, c = u.shape
    nt = lc // tm
    hb = tm // SUBLANES
    nhb = lc // SUBLANES
    return pl.pallas_call(
        functools.partial(_dwconv_kernel, nt=nt, ct=ctx_len // tm, silu=silu),
        grid=(bsz, nt),
        in_specs=[pl.BlockSpec((1, SUBLANES, c), lambda b_, i: (b_, jnp.maximum(i * hb - 1, 0), 0)),
                  pl.BlockSpec((1, tm, c), lambda b_, i: (b_, i, 0)),
                  pl.BlockSpec((1, SUBLANES, c), lambda b_, i: (b_, jnp.minimum((i + 1) * hb, nhb - 1), 0)),
                  pl.BlockSpec((CONV_W, c), lambda b_, i: (0, 0)),
                  pl.BlockSpec((1, c), lambda b_, i: (0, 0))],
        out_specs=pl.BlockSpec((1, tm, c), lambda b_, i: (b_, i, 0)),
        out_shape=jax.ShapeDtypeStruct((bsz, lc, c), F32),
        compiler_params=_params(("parallel", "arbitrary")),
        name="dwconv",
    )(u, u, u, w, b.reshape(1, c))


def _expand_heads(v, e_ref):
    hi, lo = _split_bf16(v, 2)
    e = e_ref[...]
    return (jnp.dot(hi, e, preferred_element_type=F32) + jnp.dot(lo, e, preferred_element_type=F32))


def _ssd_chunk(xbc, dt_raw, dtb, alog, e_ref, h_prev, reverse):
    q = SSD_CHUNK
    heads_per_group = SSD_HEADS // SSD_GROUPS
    x = xbc[:, :SSD_WIDTH]
    bm = xbc[:, SSD_WIDTH:SSD_WIDTH + LANES]
    cm = xbc[:, SSD_WIDTH + LANES:]
    dt = _softplus(dt_raw + dtb)
    da = dt * (-jnp.exp(alog))
    ii = lax.broadcasted_iota(jnp.int32, (q, q), 0)
    jj = lax.broadcasted_iota(jnp.int32, (q, q), 1)
    causal = (jj >= ii) if reverse else (jj <= ii)
    mask_b = jnp.where(causal, 1.0, 0.0).astype(BF16)
    da_t = da.T
    dt_t = dt.T
    acs = sum(jnp.dot(mask_b, p, preferred_element_type=F32) for p in _split_bf16(da, 3))
    acs_t = sum(_dot_nt(p, mask_b) for p in _split_bf16(da_t, 3))
    total = jnp.sum(da, axis=0, keepdims=True)
    eacs = jnp.exp(acs)

    bm_b = bm.astype(BF16)
    lane = lax.broadcasted_iota(jnp.int32, (1, LANES), 1)
    in_group = [jnp.logical_and(lane >= g * SSD_STATE, lane < (g + 1) * SSD_STATE) for g in range(SSD_GROUPS)]
    cb = [_dot_nt(jnp.where(in_group[g], cm, 0.0).astype(BF16), bm_b) for g in range(SSD_GROUPS)]

    x_b = x.astype(BF16)
    h_b = h_prev.astype(BF16)
    pairs = []
    for hp in range(SSD_HEADS // 2):
        rhs = jnp.concatenate([x_b[:, hp * LANES:(hp + 1) * LANES], h_b[:, hp * LANES:(hp + 1) * LANES]], axis=0)
        outs = []
        for t in range(2):
            h = 2 * hp + t
            seg = acs[:, h:h + 1] - acs_t[h:h + 1, :]
            dec = jnp.exp(jnp.where(causal, seg, -jnp.inf))
            w = cb[h // heads_per_group] * dec * dt_t[h:h + 1, :]
            lhs = jnp.concatenate([w.astype(BF16), (cm * eacs[:, h:h + 1]).astype(BF16)], axis=1)
            outs.append(jnp.dot(lhs, rhs, preferred_element_type=F32))
        pairs.append(jnp.where(lane < SSD_HEAD_DIM, outs[0], outs[1]))
    y = jnp.concatenate(pairs, axis=1)

    to_end = jnp.exp(total - acs) * dt
    xw = (x * _expand_heads(to_end, e_ref)).astype(BF16)
    st = jnp.dot(bm.T.astype(BF16), xw, preferred_element_type=F32)
    row_g0 = lax.broadcasted_iota(jnp.int32, st.shape, 0) < SSD_STATE
    col_g0 = lax.broadcasted_iota(jnp.int32, st.shape, 1) < heads_per_group * SSD_HEAD_DIM
    chunk_decay = _expand_heads(jnp.broadcast_to(jnp.exp(total), (SUBLANES, LANES)), e_ref)[0:1]
    return y, h_prev * chunk_decay + jnp.where(row_g0 == col_g0, st, 0.0)


def _ssd_bwd_kernel(prev_ref, cur_ref, next_ref, cw_ref, cb_ref, dt_ref, dtb_ref, alog_ref, e_ref,
                    yb_ref, xc_ref, hst_ref, *, ncx, nc, bsz):
    s = pl.program_id(0)
    c = _scan_order(s, ncx, nc)

    @pl.when(s == 0)
    def _():
        hst_ref[...] = jnp.zeros_like(hst_ref)

    seg_start = jnp.logical_or(c == 0, c == ncx)
    seg_end = jnp.logical_or(c == ncx - 1, c == nc - 1)
    for bb in range(bsz):
        xbc = _conv_rows(prev_ref[bb], cur_ref[bb], next_ref[bb], cw_ref, cb_ref, seg_start, seg_end, True)
        xc_ref[bb] = xbc
        y, h_new = _ssd_chunk(xbc, dt_ref[bb], dtb_ref[...], alog_ref[...], e_ref, hst_ref[bb], True)
        hst_ref[bb] = h_new
        yb_ref[bb] = y


def _ssd_fwd_kernel(xc_ref, dt_ref, dtb_ref, alog_ref, e_ref, yb_ref, z_ref, dsk_ref, ng_ref,
                    o_ref, hst_ref, *, bsz):
    @pl.when(pl.program_id(0) == 0)
    def _():
        hst_ref[...] = jnp.zeros_like(hst_ref)

    for bb in range(bsz):
        xbc = xc_ref[bb]
        y, h_new = _ssd_chunk(xbc, dt_ref[bb], dtb_ref[...], alog_ref[...], e_ref, hst_ref[bb], False)
        hst_ref[bb] = h_new
        z = z_ref[bb]
        y = (y + yb_ref[bb] + dsk_ref[...] * xbc[:, :SSD_WIDTH]) * (z * _sigmoid(z))
        o_ref[bb] = (_rms(y) * ng_ref[...]).astype(BF16)


def _ssd(xbc_raw, dt, z, conv_w, conv_b, dt_bias, a_log, d_skip, norm_g, expand, *, ctx_len):
    bsz, lc, cch = xbc_raw.shape
    q = SSD_CHUNK
    w = SSD_WIDTH
    nc = lc // q
    ncx = ctx_len // q
    hb = q // SUBLANES
    nhb = lc // SUBLANES
    pad = lambda t: jnp.pad(t, (0, LANES - SSD_HEADS)).reshape(1, LANES)
    const = lambda s: (0, 0)
    rev = lambda s: _scan_order(s, ncx, nc)
    state = [pltpu.VMEM((bsz, SSD_GROUPS * SSD_STATE, w), F32)]
    yb, xc = pl.pallas_call(
        functools.partial(_ssd_bwd_kernel, ncx=ncx, nc=nc, bsz=bsz),
        grid=(nc,),
        in_specs=[pl.BlockSpec((bsz, SUBLANES, cch), lambda s: (0, jnp.maximum(rev(s) * hb - 1, 0), 0)),
                  pl.BlockSpec((bsz, q, cch), lambda s: (0, rev(s), 0)),
                  pl.BlockSpec((bsz, SUBLANES, cch), lambda s: (0, jnp.minimum((rev(s) + 1) * hb, nhb - 1), 0)),
                  pl.BlockSpec((CONV_W, cch), const),
                  pl.BlockSpec((1, cch), const),
                  pl.BlockSpec((bsz, q, LANES), lambda s: (0, rev(s), 1)),
                  pl.BlockSpec((1, LANES), const),
                  pl.BlockSpec((1, LANES), const),
                  pl.BlockSpec((LANES, w), const)],
        out_specs=[pl.BlockSpec((bsz, q, w), lambda s: (0, rev(s), 0)),
                   pl.BlockSpec((bsz, q, cch), lambda s: (0, rev(s), 0))],
        out_shape=[jax.ShapeDtypeStruct((bsz, lc, w), F32), jax.ShapeDtypeStruct((bsz, lc, cch), F32)],
        scratch_shapes=state,
        compiler_params=_params(("arbitrary",)),
        name="ssd_bwd",
    )(xbc_raw, xbc_raw, xbc_raw, conv_w, conv_b.reshape(1, cch), dt, pad(dt_bias[1]), pad(a_log[1]), expand)
    row = lambda cols: pl.BlockSpec((bsz, q, cols), lambda s: (0, s, 0))
    return pl.pallas_call(
        functools.partial(_ssd_fwd_kernel, bsz=bsz),
        grid=(nc,),
        in_specs=[row(cch), row(LANES), pl.BlockSpec((1, LANES), const), pl.BlockSpec((1, LANES), const),
                  pl.BlockSpec((LANES, w), const), row(w), row(w), pl.BlockSpec((1, w), const),
                  pl.BlockSpec((1, w), const)],
        out_specs=row(w),
        out_shape=jax.ShapeDtypeStruct((bsz, lc, w), BF16),
        scratch_shapes=state,
        compiler_params=_params(("arbitrary",)),
        name="ssd_fwd",
    )(xc, dt, pad(dt_bias[0]), pad(a_log[0]), expand, yb, z,
      jnp.repeat(d_skip, SSD_HEAD_DIM).reshape(1, w), norm_g.reshape(1, w))


def _stack_heads(q):
    return jnp.concatenate([q[:, r * HEAD_DIM:(r + 1) * HEAD_DIM] for r in range(q.shape[1] // HEAD_DIM)], axis=0)


def _lane_parts(s):
    return [s[:, c * LANES:(c + 1) * LANES] for c in range(s.shape[1] // LANES)]


def _softmax_parts(parts, sink=None):
    pm = parts[0]
    for t in parts[1:]:
        pm = jnp.maximum(pm, t)
    m = jnp.max(pm, axis=-1, keepdims=True)
    if sink is not None:
        m = jnp.maximum(m, sink)
    mb = jnp.broadcast_to(m, pm.shape)
    ps = [jnp.exp(t - mb) for t in parts]
    lsum = ps[0]
    for t in ps[1:]:
        lsum = lsum + t
    l = jnp.sum(lsum, axis=-1, keepdims=True)
    if sink is not None:
        l = l + jnp.exp(sink - m)
    return jnp.concatenate([t.astype(BF16) for t in ps], axis=1), l


def _store_heads(o_ref, o, tq, col0=0):
    for r in range(GQA_REP):
        c0 = col0 + r * HEAD_DIM
        o_ref[0, :, c0:c0 + HEAD_DIM] = o[r * tq:(r + 1) * tq].astype(BF16)


def _attn_dense_kernel(q_ref, k_ref, v_ref, o_ref, m_scr, l_scr, acc_scr, *, tq, tk, ctx_len, lk):
    i = pl.program_id(2)
    q4 = _stack_heads(q_ref[0])

    @pl.when(i < ctx_len // tq)
    def _():
        p, l = _softmax_parts(_lane_parts(_dot_nt(q4, k_ref[0, 0:ctx_len, :])))
        _store_heads(o_ref, jnp.dot(p, v_ref[0, 0:ctx_len, :], preferred_element_type=F32) / l, tq)

    @pl.when(i >= ctx_len // tq)
    def _():
        m_scr[...] = jnp.full_like(m_scr, -jnp.inf)
        l_scr[...] = jnp.zeros_like(l_scr)
        acc_scr[...] = jnp.zeros_like(acc_scr)

        def body(j, carry):
            off = pl.multiple_of(j * tk, tk)
            parts = _lane_parts(_dot_nt(q4, k_ref[0, pl.ds(off, tk), :]))
            pm = parts[0]
            for t in parts[1:]:
                pm = jnp.maximum(pm, t)
            m_prev = m_scr[...]
            m_new = jnp.maximum(m_prev, jnp.max(pm, axis=-1, keepdims=True))
            alpha = jnp.exp(m_prev - m_new)
            ps = [jnp.exp(t - m_new) for t in parts]
            lsum = ps[0]
            for t in ps[1:]:
                lsum = lsum + t
            l_scr[...] = alpha * l_scr[...] + lsum
            p = jnp.concatenate([t.astype(BF16) for t in ps], axis=1)
            acc_scr[...] = alpha * acc_scr[...] + jnp.dot(p, v_ref[0, pl.ds(off, tk), :], preferred_element_type=F32)
            m_scr[...] = m_new
            return carry

        lax.fori_loop(0, lk // tk, body, 0)
        _store_heads(o_ref, acc_scr[...] / jnp.sum(l_scr[...], axis=-1, keepdims=True), tq)


def _attn_dense(q, k, v, *, ctx_len, tq, tk):
    bsz, lc, _ = q.shape
    gw = GQA_REP * HEAD_DIM
    rows = GQA_REP * tq
    return pl.pallas_call(
        functools.partial(_attn_dense_kernel, tq=tq, tk=tk, ctx_len=ctx_len, lk=lc),
        grid=(bsz, ATT_KV_HEADS, lc // tq),
        in_specs=[pl.BlockSpec((1, tq, gw), lambda b, g, i: (b, i, g)),
                  pl.BlockSpec((1, lc, HEAD_DIM), lambda b, g, i: (b, 0, g)),
                  pl.BlockSpec((1, lc, HEAD_DIM), lambda b, g, i: (b, 0, g))],
        out_specs=pl.BlockSpec((1, tq, gw), lambda b, g, i: (b, i, g)),
        out_shape=jax.ShapeDtypeStruct((bsz, lc, ATT_WIDTH), BF16),
        scratch_shapes=[pltpu.VMEM((rows, LANES), F32),
                        pltpu.VMEM((rows, LANES), F32),
                        pltpu.VMEM((rows, HEAD_DIM), F32)],
        compiler_params=_params(("parallel", "parallel", "arbitrary")),
        name="attn_dense",
    )(q, k, v)


def _attn_win_kernel(q_ref, k_ref, v_ref, sink_ref, o_ref, *, ctx_len, lat_len, tq):
    n = pl.program_id(1)
    band = tq + 2 * WINDOW
    il = n - ctx_len // tq
    lo = jnp.clip(il * tq - WINDOW, 0, lat_len - band)
    start = pl.multiple_of(ctx_len + lo, WINDOW)
    rows = lax.broadcasted_iota(jnp.int32, (GQA_REP * tq, 1), 0)
    qpos = il * tq + (rows & (tq - 1))
    kpos = lo + lax.broadcasted_iota(jnp.int32, (1, band), 1)
    valid = jnp.logical_and(jnp.abs(qpos - kpos) <= WINDOW, il >= 0)
    gw = GQA_REP * HEAD_DIM
    for g in range(ATT_KV_HEADS):
        kc0 = g * HEAD_DIM
        q4 = _stack_heads(q_ref[0, :, g * gw:(g + 1) * gw])
        s_loc = jnp.where(valid, _dot_nt(q4, k_ref[0, pl.ds(start, band), kc0:kc0 + HEAD_DIM]), -jnp.inf)
        s_ctx = _dot_nt(q4, k_ref[0, 0:ctx_len, kc0:kc0 + HEAD_DIM])
        p, l = _softmax_parts(_lane_parts(s_loc) + _lane_parts(s_ctx), sink_ref[g])
        o = (jnp.dot(p[:, :band], v_ref[0, pl.ds(start, band), kc0:kc0 + HEAD_DIM], preferred_element_type=F32)
             + jnp.dot(p[:, band:], v_ref[0, 0:ctx_len, kc0:kc0 + HEAD_DIM], preferred_element_type=F32)) / l
        _store_heads(o_ref, o, tq, col0=g * gw)


def _attn_window(q, k, v, sink, *, ctx_len, tq):
    bsz, lc, _ = q.shape
    lat_len = lc - ctx_len
    rows = GQA_REP * tq
    assert tq & (tq - 1) == 0 and ctx_len % tq == 0 and lat_len >= tq + 2 * WINDOW
    sink_col = jnp.repeat(sink.reshape(ATT_KV_HEADS, GQA_REP), tq, axis=1).reshape(ATT_KV_HEADS, rows, 1)
    return pl.pallas_call(
        functools.partial(_attn_win_kernel, ctx_len=ctx_len, lat_len=lat_len, tq=tq),
        grid=(bsz, lc // tq),
        in_specs=[pl.BlockSpec((1, tq, ATT_WIDTH), lambda b, n: (b, n, 0)),
                  pl.BlockSpec((1, lc, KV_WIDTH), lambda b, n: (b, 0, 0)),
                  pl.BlockSpec((1, lc, KV_WIDTH), lambda b, n: (b, 0, 0)),
                  pl.BlockSpec((ATT_KV_HEADS, rows, 1), lambda b, n: (0, 0, 0))],
        out_specs=pl.BlockSpec((1, tq, ATT_WIDTH), lambda b, n: (b, n, 0)),
        out_shape=jax.ShapeDtypeStruct((bsz, lc, ATT_WIDTH), BF16),
        compiler_params=_params(("parallel", "arbitrary")),
        name="attn_window",
    )(q, k, v, sink_col)


def _lru_scan_rows(a, b, carry, reverse):
    t = a.shape[0]
    pos8 = lax.broadcasted_iota(jnp.int32, (t, 1), 0) & (SUBLANES - 1)
    k = 1
    while k < SUBLANES:
        if reverse:
            keep = pos8 < SUBLANES - k
            sh = t - k
        else:
            keep = pos8 >= k
            sh = k
        a_sh = jnp.where(keep, pltpu.roll(a, sh, 0), 1.0)
        b_sh = jnp.where(keep, pltpu.roll(b, sh, 0), 0.0)
        b = b + a * b_sh
        a = a * a_sh
        k *= 2
    ngroups = t // SUBLANES
    hs = [None] * ngroups
    for g in (range(ngroups - 1, -1, -1) if reverse else range(ngroups)):
        sl = slice(g * SUBLANES, (g + 1) * SUBLANES)
        hg = a[sl] * carry + b[sl]
        carry = hg[0:1] if reverse else hg[SUBLANES - 1:SUBLANES]
        hs[g] = hg
    return jnp.concatenate(hs, axis=0), carry


def _lru_kernel(*refs, reverse):
    x_ref, wa_ref, wx_ref, ba_ref, bx_ref, lam_ref = refs[:6]
    if reverse:
        o_ref, carry_ref = refs[6:]
    else:
        hb_ref, gate_ref, o_ref, carry_ref = refs[6:]
    s = pl.program_id(1)

    @pl.when(s == 0)
    def _():
        carry_ref[...] = jnp.zeros_like(carry_ref)

    x = x_ref[0]
    x_b = x.astype(BF16)
    nslab = LRU_WIDTH // LRU_SLAB

    def gate(w_ref, b_ref):
        pre = jnp.concatenate(
            [jnp.dot(x_b[:, c * LRU_SLAB:(c + 1) * LRU_SLAB], w_ref[0, c], preferred_element_type=F32)
             for c in range(nslab)], axis=1)
        return _sigmoid(pre + b_ref[0])

    r = gate(wa_ref, ba_ref)
    ig = gate(wx_ref, bx_ref)
    log_base = -LRU_C * _softplus(-lam_ref[0])
    a = jnp.exp(r * log_base)
    y = 1.0 - a * a
    b = jnp.where(y > 0.0, y * lax.rsqrt(y), 0.0) * (ig * x)
    h, carry = _lru_scan_rows(a, b, carry_ref[...], reverse)
    carry_ref[...] = carry
    if reverse:
        o_ref[0] = h
    else:
        g = gate_ref[0]
        gelu = 0.5 * g * (1.0 + jnp.tanh(math.sqrt(2.0 / math.pi) * (g + 0.044715 * (g * g * g))))
        o_ref[0] = (gelu * (h + hb_ref[0])).astype(BF16)


def _lru(xr, gate, wa, wx, ba, bx, lam, *, ctx_len, t=ROW_TILE):
    bsz, lc, w = xr.shape
    nt = lc // t
    ct = ctx_len // t
    nslab = w // LRU_SLAB
    vec = lambda v: v.reshape(2, 1, w)

    def call(reverse, extra_in, out_dtype):
        tile = (lambda s: _scan_order(s, ct, nt)) if reverse else (lambda s: s)
        d = 1 if reverse else 0
        row = pl.BlockSpec((1, t, w), lambda b, s: (b, tile(s), 0))
        return pl.pallas_call(
            functools.partial(_lru_kernel, reverse=reverse),
            grid=(bsz, nt),
            in_specs=[row,
                      pl.BlockSpec((1, nslab, LRU_SLAB, LRU_SLAB), lambda b, s: (d, 0, 0, 0)),
                      pl.BlockSpec((1, nslab, LRU_SLAB, LRU_SLAB), lambda b, s: (d, 0, 0, 0)),
                      pl.BlockSpec((1, 1, w), lambda b, s: (d, 0, 0)),
                      pl.BlockSpec((1, 1, w), lambda b, s: (d, 0, 0)),
                      pl.BlockSpec((1, 1, w), lambda b, s: (d, 0, 0))] + [row] * len(extra_in),
            out_specs=row,
            out_shape=jax.ShapeDtypeStruct((bsz, lc, w), out_dtype),
            scratch_shapes=[pltpu.VMEM((1, w), F32)],
            compiler_params=_params(("parallel", "arbitrary")),
            name="lru_bwd" if reverse else "lru_fwd",
        )(xr, wa, wx, vec(ba), vec(bx), vec(lam), *extra_in)

    hb = call(True, (), F32)
    return call(False, (hb, gate), BF16)


def _lru_slabs(w):
    per = LRU_SLAB // LRU_BLOCK_DIM
    nslab = LRU_BLOCKS // per
    w = w.reshape(2, nslab, per, LRU_BLOCK_DIM, LRU_BLOCK_DIM)
    eye = jnp.eye(per, dtype=w.dtype)
    full = jnp.einsum('dspij,pq->dspiqj', w, eye)
    return full.reshape(2, nslab, LRU_SLAB, LRU_SLAB).astype(BF16)


def _outproj_kernel(*refs, tm, ctx_len, t0, n_src):
    ya_ref, yb_ref = refs[:2]
    x_refs = refs[2:2 + n_src]
    mod_ref, w_ref, o_ref = refs[2 + n_src:]
    i = pl.program_id(1) + t0
    half = ya_ref.shape[2]
    acc = (jnp.dot(ya_ref[0], w_ref[0:half, :], preferred_element_type=F32)
           + jnp.dot(yb_ref[0], w_ref[half:, :], preferred_element_type=F32))
    gate = _row_select(i * tm, tm, ctx_len, mod_ref, 2)
    o_ref[0] = _load_rows(x_refs, i, ctx_len // tm) + gate * acc


def _outproj(ya, yb, srcs, mod, w, *, ctx_len, latent_only, tm=ROW_TILE):
    bsz, lc, half = ya.shape
    d = w.shape[1]
    ct = ctx_len // tm
    t0 = ct if latent_only else 0
    row = lambda b, i: (b, i + t0, 0)
    if len(srcs) == 2:
        x_specs = [pl.BlockSpec((1, tm, d), lambda b, i: (b, jnp.minimum(i + t0, ct - 1), 0)),
                   pl.BlockSpec((1, tm, d), lambda b, i: (b, jnp.maximum(i + t0 - ct, 0), 0))]
    else:
        x_specs = [pl.BlockSpec((1, tm, d), row)]
    return pl.pallas_call(
        functools.partial(_outproj_kernel, tm=tm, ctx_len=ctx_len, t0=t0, n_src=len(srcs)),
        grid=(bsz, lc // tm - t0),
        in_specs=[pl.BlockSpec((1, tm, half), row), pl.BlockSpec((1, tm, half), row)] + x_specs + [
            pl.BlockSpec((1, 2, 6, d), lambda b, i: (b, 0, 0, 0)),
            pl.BlockSpec((2 * half, d), lambda b, i: (0, 0))],
        out_specs=pl.BlockSpec((1, tm, d), lambda b, i: (b, i, 0)),
        out_shape=jax.ShapeDtypeStruct((bsz, lc - t0 * tm, d), F32),
        compiler_params=_params(("parallel", "arbitrary")),
        name="outproj",
    )(ya, yb, *srcs, mod, w)


def _first_max(vals):
    m = vals[0]
    for v in vals[1:]:
        m = jnp.maximum(m, v)
    hot = []
    taken = None
    for v in vals:
        hit = v == m
        if taken is None:
            hot.append(hit)
            taken = hit
        else:
            hot.append(jnp.logical_and(hit, jnp.logical_not(taken)))
            taken = jnp.logical_or(taken, hit)
    return m, hot


def _moe_kernel(*refs, tm, ctx_len, win, final):
    x_ref, g_ref, mod_ref, wr_ref, br_ref, w13_ref, w2_ref = refs[:7]
    if final:
        gf_ref = refs[7]
    o_ref, hn_scr, gate_scr, rank_row_scr, rank_col_scr, cnt_ref, acc_scr = refs[-7:]
    i = pl.program_id(1)
    e = pl.program_id(2)

    @pl.when(e == 0)
    def _():
        hn = _rms_mod(x_ref[0], g_ref[...], i * tm, ctx_len, mod_ref, 3, 4)
        hn_scr[...] = hn.astype(BF16)
        h_hi, h_lo = _split_bf16(hn, 2)
        r_hi = jnp.dot(h_hi, wr_ref[...], preferred_element_type=F32)
        r_lo = jnp.dot(h_lo, wr_ref[...], preferred_element_type=F32)
        logits = (r_hi[:, :LANES] + r_hi[:, LANES:]) + (r_lo[:, :LANES] + r_lo[:, LANES:]) + br_ref[...]
        lt = jnp.concatenate([logits[c * LANES:(c + 1) * LANES, :].T for c in range(tm // LANES)], axis=1)
        gl = [lt[j:j + 1, :] for j in range(MOE_GROUPS)]
        gmax, ghot = _first_max(gl)
        gsum = sum(jnp.exp(v - gmax) for v in gl)
        p_grp = 1.0 / gsum
        elg = []
        for j in range(EXPERTS_PER_GROUP):
            v = jnp.zeros_like(gmax)
            for gi in range(MOE_GROUPS):
                c = MOE_GROUPS + gi * EXPERTS_PER_GROUP + j
                v = jnp.where(ghot[gi], lt[c:c + 1, :], v)
            elg.append(v)
        v1, hot1 = _first_max(elg)
        rest = [jnp.where(hh, -jnp.inf, v) for hh, v in zip(hot1, elg)]
        v2, hot2 = _first_max(rest)
        ex = jnp.exp(v2 - v1)
        w1 = p_grp / (1.0 + ex)
        w2 = p_grp * ex / (1.0 + ex)
        own = [jnp.where(hot1[j], w1, 0.0) + jnp.where(hot2[j], w2, 0.0) for j in range(EXPERTS_PER_GROUP)]
        pad_rows = jnp.zeros((LANES - MOE_GROUPS, tm), F32)
        g_t = jnp.concatenate([jnp.where(h_, 1.0, 0.0) for h_ in ghot] + [pad_rows], axis=0)
        t0 = lax.broadcasted_iota(jnp.int32, (tm, tm), 0)
        t1 = lax.broadcasted_iota(jnp.int32, (tm, tm), 1)
        before = jnp.where(t0 < t1, 1.0, 0.0).astype(BF16)
        rank_t = jnp.where(g_t > 0.0, jnp.dot(g_t.astype(BF16), before, preferred_element_type=F32), -1.0)
        rank_row_scr[...] = rank_t[0:SUBLANES]
        for gi in range(MOE_GROUPS):
            cnt_ref[gi] = jnp.sum(g_t[gi:gi + 1, :]).astype(jnp.int32)
            slab_t = jnp.concatenate([jnp.where(ghot[gi], own[j], 0.0) for j in range(EXPERTS_PER_GROUP)]
                                     + [pad_rows], axis=0)
            for c in range(tm // LANES):
                rows_c = slice(c * LANES, (c + 1) * LANES)
                slab = slab_t[:, rows_c].T
                hi, lo = _split_bf16(slab, 2)
                gate_scr[gi, rows_c, 0:LANES] = hi
                gate_scr[gi, rows_c, LANES:] = lo
                if gi == 0:
                    rank_c = rank_t[:, rows_c].T
                    for gj in range(MOE_GROUPS):
                        rank_col_scr[gj, rows_c, :] = rank_c[:, gj:gj + 1]
        acc_scr[...] = jnp.zeros_like(acc_scr)

    cnt = cnt_ref[e]
    rank_row = rank_row_scr[pl.ds(e, 1), :]
    rank_col = rank_col_scr[e]
    slot_r = lax.broadcasted_iota(jnp.int32, (win, 1), 0).astype(F32)
    slot_c = lax.broadcasted_iota(jnp.int32, (1, win), 1).astype(F32)

    def window(wi, carry):
        base = (wi * win).astype(F32)
        sel = jnp.where(rank_row - base == slot_r, 1.0, 0.0).astype(BF16)
        sel_t = jnp.where(rank_col - base == slot_c, 1.0, 0.0).astype(BF16)
        xs = jnp.dot(sel, hn_scr[...], preferred_element_type=F32).astype(BF16)
        gs = jnp.dot(sel, gate_scr[e], preferred_element_type=F32)
        gsel = gs[:, :LANES] + gs[:, LANES:]
        yw = None
        for kk in range(EXPERTS_PER_GROUP):
            au = jnp.dot(xs, w13_ref[kk], preferred_element_type=F32)
            a = au[:, :D_EXPERT]
            u = au[:, D_EXPERT:]
            hid = (a * _sigmoid(a)) * u * gsel[:, kk:kk + 1]
            part = jnp.dot(hid.astype(BF16), w2_ref[kk], preferred_element_type=F32)
            yw = part if yw is None else yw + part
        acc_scr[...] += jnp.dot(sel_t, yw.astype(BF16), preferred_element_type=F32)
        return carry

    lax.fori_loop(0, (cnt + (win - 1)) // win, window, 0)

    @pl.when(e == MOE_GROUPS - 1)
    def _():
        gate = _row_select(i * tm, tm, ctx_len, mod_ref, 5)
        out = x_ref[0] + gate * acc_scr[...]
        if final:
            out = _rms(out) * gf_ref[...]
        o_ref[0] = out


def _moe(xx, g, mod, wr, br, w13, w2, g_final, *, ctx_len):
    bsz, rows, d = xx.shape
    tm = _pick_tile(rows, MOE_TILE_PREFS)
    win = -(-(tm // MOE_GROUPS + tm // 16) // LANES) * LANES
    eb = EXPERTS_PER_GROUP
    row = lambda b, i, e: (b, i, 0)
    const = lambda b, i, e: (0, 0)
    final = g_final is not None
    return pl.pallas_call(
        functools.partial(_moe_kernel, tm=tm, ctx_len=ctx_len, win=win, final=final),
        grid=(bsz, rows // tm, MOE_GROUPS),
        in_specs=[pl.BlockSpec((1, tm, d), row),
                  pl.BlockSpec((1, d), const),
                  pl.BlockSpec((1, 2, 6, d), lambda b, i, e: (b, 0, 0, 0)),
                  pl.BlockSpec((d, 2 * LANES), const),
                  pl.BlockSpec((1, LANES), const),
                  pl.BlockSpec((eb, d, 2 * D_EXPERT), lambda b, i, e: (e, 0, 0)),
                  pl.BlockSpec((eb, D_EXPERT, d), lambda b, i, e: (e, 0, 0))]
                 + ([pl.BlockSpec((1, d), const)] if final else []),
        out_specs=pl.BlockSpec((1, tm, d), row),
        out_shape=jax.ShapeDtypeStruct((bsz, rows, d), F32),
        scratch_shapes=[pltpu.VMEM((tm, d), BF16),
                        pltpu.VMEM((MOE_GROUPS, tm, 2 * LANES), BF16),
                        pltpu.VMEM((SUBLANES, tm), F32),
                        pltpu.VMEM((MOE_GROUPS, tm, 1), F32),
                        pltpu.SMEM((MOE_GROUPS,), jnp.int32),
                        pltpu.VMEM((tm, d), F32)],
        compiler_params=_params(("parallel", "arbitrary", "arbitrary")),
        name="moe",
    )(xx, g, mod, wr, br, w13, w2, *((g_final.reshape(1, d),) if final else ()))


def _rope_tables(ctx_len, lat_len):
    rows = lat_len // GRID_W
    row = np.repeat(np.arange(rows), GRID_W).astype(np.float64)
    col = np.tile(np.arange(GRID_W), rows).astype(np.float64)
    n_freq = HEAD_DIM // 4
    inv = ROPE_THETA ** (-np.arange(n_freq, dtype=np.float64) / n_freq)
    ang = np.concatenate([row[:, None] * inv, col[:, None] * inv], axis=-1)
    cos, sin = np.cos(ang), np.sin(ang)
    cc = np.concatenate([np.ones((ctx_len, HEAD_DIM)), np.concatenate([cos, cos], axis=-1)], axis=0)
    ss = np.concatenate([np.zeros((ctx_len, HEAD_DIM)), np.concatenate([-sin, sin], axis=-1)], axis=0)
    return jnp.asarray(cc, F32), jnp.asarray(ss, F32)


def kernel(x, c, ctx, c_ctx, w_mod, b_mod, g_mix, g_ffn, moe_w_grp, moe_b_grp, moe_w_rt, moe_b_rt, moe_w1, moe_w3, moe_w2, ab_w_in, ab_w_out, ssd_conv_w, ssd_conv_b, ssd_dt_bias, ssd_a_log, ssd_d, ssd_norm_g, att_q_g, att_k_g, cd_w_in, cd_w_out, lru_conv_w, lru_conv_b, lru_w_a, lru_b_a, lru_w_x, lru_b_x, lru_lam, swa_sink, g_final):
    bsz, lat_len, d = x.shape
    ctx_len = ctx.shape[1]
    depth = w_mod.shape[0]
    lc = ctx_len + lat_len
    assert d == D_MODEL and ctx_len % ROW_TILE == 0 and lat_len % ROW_TILE == 0 and ctx_len > 0

    cc, ss = _rope_tables(ctx_len, lat_len)
    mods = _modulation(c, c_ctx, w_mod, b_mod)
    expand = jnp.asarray(np.arange(SSD_WIDTH)[None, :] // SSD_HEAD_DIM == np.arange(LANES)[:, None], BF16)
    ones_h = jnp.ones((1, HEAD_DIM), F32)
    srcs = (ctx, x)

    for i in range(depth):
        j = i // 2
        last = i == depth - 1
        m = mods[i].reshape(bsz + 1, 6, d)
        mod = jnp.stack([jnp.broadcast_to(m[0], (bsz, 6, d)), m[1:]], axis=1)
        if i % 2 == 0:
            w = ab_w_in[j]
            zc = jnp.zeros((d, LANES - SSD_HEADS), w.dtype)
            o_dt = SSD_WIDTH + SSD_CONV_CH
            o_q = o_dt + 2 * SSD_HEADS
            w_packed = jnp.concatenate([w[:, :o_dt], w[:, o_dt:o_dt + SSD_HEADS], zc,
                                        w[:, o_dt + SSD_HEADS:o_q], zc, w[:, o_q:]], axis=1).astype(BF16)
            plain = ((0, SSD_WIDTH), (SSD_WIDTH, SSD_CONV_CH), (o_dt, 2 * LANES))
            z, xbc, dt, q, k, v = _inproj(srcs, lc, g_mix[i].reshape(1, d), mod, w_packed,
                                          att_q_g[j].reshape(1, HEAD_DIM), att_k_g[j].reshape(1, HEAD_DIM), cc, ss,
                                          ctx_len=ctx_len, plain=plain, q0=o_dt + 2 * LANES, qk_norm=True)
            ya = _ssd(xbc, dt, z, ssd_conv_w[j], ssd_conv_b[j], ssd_dt_bias[j], ssd_a_log[j], ssd_d[j],
                      ssd_norm_g[j], expand, ctx_len=ctx_len)
            yb = _attn_dense(q, k, v, ctx_len=ctx_len, tq=ATT_TQ, tk=_pick_tile(lc, ATT_TK_PREFS))
            w_out = ab_w_out[j]
        else:
            plain = ((0, LRU_WIDTH), (LRU_WIDTH, LRU_WIDTH))
            gate, xr, q, k, v = _inproj(srcs, lc, g_mix[i].reshape(1, d), mod, cd_w_in[j].astype(BF16),
                                        ones_h, ones_h, cc, ss,
                                        ctx_len=ctx_len, plain=plain, q0=2 * LRU_WIDTH, qk_norm=False)
            xr = _dwconv(xr, lru_conv_w[j], lru_conv_b[j], ctx_len=ctx_len, silu=False)
            ya = _lru(xr, gate, _lru_slabs(lru_w_a[j]), _lru_slabs(lru_w_x[j]), lru_b_a[j], lru_b_x[j], lru_lam[j],
                      ctx_len=ctx_len)
            yb = _attn_window(q, k, v, swa_sink[j], ctx_len=ctx_len, tq=ATT_TQ)
            w_out = cd_w_out[j]
        xs = _outproj(ya, yb, srcs, mod, w_out.astype(BF16), ctx_len=ctx_len, latent_only=last)
        wr = jnp.concatenate([moe_w_grp[i], moe_w_rt[i],
                              jnp.zeros((d, LANES - MOE_GROUPS - N_EXPERTS), F32)], axis=1)
        br = jnp.concatenate([moe_b_grp[i], moe_b_rt[i],
                              jnp.zeros((LANES - MOE_GROUPS - N_EXPERTS,), F32)]).reshape(1, LANES)
        wr_hi = wr.astype(BF16)
        wr = jnp.concatenate([wr_hi, (wr - wr_hi.astype(F32)).astype(BF16)], axis=1)
        w13 = jnp.concatenate([moe_w1[i], moe_w3[i]], axis=-1).astype(BF16)
        xs = _moe(xs, g_ffn[i].reshape(1, d), mod, wr, br, w13, moe_w2[i].astype(BF16),
                  g_final if last else None, ctx_len=0 if last else ctx_len)
        srcs = (xs,)
    return xs
```

```python
import functools
import math

import numpy as np
import jax
import jax.numpy as jnp
from jax import lax
from jax.experimental import pallas as pl
from jax.experimental.pallas import tpu as pltpu

F32 = jnp.float32
BF16 = jnp.bfloat16

D_MODEL = 1024
GRID_W = 64
EPS = 1e-6
HEAD_DIM = 128
ATT_HEADS = 8
ATT_KV_HEADS = 2
GQA_REP = ATT_HEADS // ATT_KV_HEADS
ATT_WIDTH = ATT_HEADS * HEAD_DIM
KV_WIDTH = ATT_KV_HEADS * HEAD_DIM
ATT_SCALE = HEAD_DIM ** -0.5
ROPE_THETA = 10000.0
WINDOW = 128
SSD_HEADS = 16
SSD_HEAD_DIM = 64
SSD_WIDTH = SSD_HEADS * SSD_HEAD_DIM
SSD_GROUPS = 2
SSD_STATE = 64
SSD_CHUNK = 128
SSD_CONV_CH = SSD_WIDTH + 2 * SSD_GROUPS * SSD_STATE
LRU_WIDTH = 1024
LRU_BLOCKS = 16
LRU_BLOCK_DIM = LRU_WIDTH // LRU_BLOCKS
LRU_C = 8.0
CONV_W = 5
MOE_GROUPS = 4
EXPERTS_PER_GROUP = 4
N_EXPERTS = MOE_GROUPS * EXPERTS_PER_GROUP
D_EXPERT = 256

LANES = 128
SUBLANES = 8
LRU_SLAB = 256
VMEM_LIMIT = 56 * 1024 * 1024
ROW_TILE = 256
MOE_TILE_PREFS = (768, 512, 256)
ATT_TQ = 256
ATT_TK_PREFS = (2816, 1408, 768, 256)


def _params(sem):
    return pltpu.CompilerParams(dimension_semantics=sem, vmem_limit_bytes=VMEM_LIMIT)


def _pick_tile(n, prefs):
    for t in prefs:
        if n % t == 0:
            return t
    raise ValueError(f"no tile for {n}")


def _sigmoid(x):
    return 0.5 * jnp.tanh(0.5 * x) + 0.5


def _softplus(x):
    return jnp.maximum(x, 0.0) + jnp.log1p(jnp.exp(-jnp.abs(x)))


def _split_bf16(v, n):
    parts = []
    r = v
    for _ in range(n):
        p = r.astype(BF16)
        parts.append(p)
        r = r - p.astype(F32)
    return parts


def _dot_nt(a, b):
    return lax.dot_general(a, b, (((1,), (1,)), ((), ())), preferred_element_type=F32)


def _rms(x):
    return x * lax.rsqrt(jnp.mean(x * x, axis=-1, keepdims=True) + EPS)


def _row_select(row0, tm, ctx_len, mod_ref, k):
    if ctx_len == 0:
        return mod_ref[0, 1, k:k + 1, :]
    rows = row0 + lax.broadcasted_iota(jnp.int32, (tm, 1), 0)
    return jnp.where(rows < ctx_len, mod_ref[0, 0, k:k + 1, :], mod_ref[0, 1, k:k + 1, :])


def _rms_mod(x, g, row0, ctx_len, mod_ref, k_shift, k_scale):
    tm = x.shape[0]
    sh = _row_select(row0, tm, ctx_len, mod_ref, k_shift)
    sc = _row_select(row0, tm, ctx_len, mod_ref, k_scale)
    return _rms(x) * g * (1.0 + sc) + sh


def _scan_order(s, n_ctx, n_all):
    return jnp.where(s < n_ctx, n_ctx - 1 - s, n_all + n_ctx - 1 - s)


def _row_specs(tm, d, ct, split):
    if split:
        return [pl.BlockSpec((1, tm, d), lambda b, i: (b, jnp.minimum(i, ct - 1), 0)),
                pl.BlockSpec((1, tm, d), lambda b, i: (b, jnp.maximum(i - ct, 0), 0))]
    return [pl.BlockSpec((1, tm, d), lambda b, i: (b, i, 0))]


def _load_rows(refs, i, ct):
    if len(refs) == 2:
        return jnp.where(i < ct, refs[0][0], refs[1][0])
    return refs[0][0]


def _mod_kernel(ct_ref, w_ref, b_ref, o_ref, *, rows):
    w = w_ref[0]
    outs = []
    for r in range(rows):
        c = ct_ref[:, r:r + 1]
        act = c * _sigmoid(c)
        outs.append(jnp.sum(w * act, axis=0, keepdims=True) + b_ref[0])
    o_ref[0] = jnp.concatenate(outs, axis=0)


def _modulation(c, c_ctx, w_mod, b_mod):
    depth, d, n = w_mod.shape
    rows = c.shape[0] + 1
    ct = jnp.concatenate([c_ctx[None, :], c], axis=0).T
    tn = 1536
    return pl.pallas_call(
        functools.partial(_mod_kernel, rows=rows),
        grid=(depth, n // tn),
        in_specs=[pl.BlockSpec((d, rows), lambda i, j: (0, 0)),
                  pl.BlockSpec((1, d, tn), lambda i, j: (i, 0, j)),
                  pl.BlockSpec((1, 1, tn), lambda i, j: (i, 0, j))],
        out_specs=pl.BlockSpec((1, rows, tn), lambda i, j: (i, 0, j)),
        out_shape=jax.ShapeDtypeStruct((depth, rows, n), F32),
        compiler_params=_params(("arbitrary", "arbitrary")),
        name="modulation",
    )(ct, w_mod, b_mod.reshape(depth, 1, n))


def _inproj_kernel(*refs, tm, ctx_len, plain, q0, qk_norm, n_src):
    x_refs = refs[:n_src]
    g_ref, mod_ref, w_ref, qg_ref, kg_ref, cc_ref, ss_ref = refs[n_src:n_src + 7]
    out_refs = refs[n_src + 7:]
    i = pl.program_id(1)
    x = _load_rows(x_refs, i, ctx_len // tm)
    h = _rms_mod(x, g_ref[...], i * tm, ctx_len, mod_ref, 0, 1)
    res = jnp.dot(h.astype(BF16), w_ref[...], preferred_element_type=F32)
    for ref, (c0, cw) in zip(out_refs[:len(plain)], plain):
        ref[0] = res[:, c0:c0 + cw]
    q_ref, k_ref, v_ref = out_refs[len(plain):]
    cc = cc_ref[...]
    ss = ss_ref[...]

    def head(t, gain, scale):
        if qk_norm:
            t = _rms(t) * gain
        t = t * cc + pltpu.roll(t, HEAD_DIM // 2, 1) * ss
        return (t * scale).astype(BF16)

    for hh in range(ATT_HEADS):
        c0 = q0 + hh * HEAD_DIM
        q_ref[0, :, hh * HEAD_DIM:(hh + 1) * HEAD_DIM] = head(res[:, c0:c0 + HEAD_DIM], qg_ref[...], ATT_SCALE)
    k0 = q0 + ATT_WIDTH
    for hh in range(ATT_KV_HEADS):
        c0 = k0 + hh * HEAD_DIM
        k_ref[0, :, hh * HEAD_DIM:(hh + 1) * HEAD_DIM] = head(res[:, c0:c0 + HEAD_DIM], kg_ref[...], 1.0)
    v0 = k0 + KV_WIDTH
    v_ref[0] = res[:, v0:v0 + KV_WIDTH].astype(BF16)


def _inproj(srcs, lc, g, mod, w, qg, kg, cc, ss, *, ctx_len, plain, q0, qk_norm, tm=ROW_TILE):
    bsz, _, d = srcs[0].shape
    n = w.shape[1]
    row = lambda b, i: (b, i, 0)
    const = lambda b, i: (0, 0)
    out_shape = [jax.ShapeDtypeStruct((bsz, lc, cw), F32) for _, cw in plain]
    out_specs = [pl.BlockSpec((1, tm, cw), row) for _, cw in plain]
    for cw in (ATT_WIDTH, KV_WIDTH, KV_WIDTH):
        out_shape.append(jax.ShapeDtypeStruct((bsz, lc, cw), BF16))
        out_specs.append(pl.BlockSpec((1, tm, cw), row))
    return pl.pallas_call(
        functools.partial(_inproj_kernel, tm=tm, ctx_len=ctx_len, plain=plain, q0=q0, qk_norm=qk_norm,
                          n_src=len(srcs)),
        grid=(bsz, lc // tm),
        in_specs=_row_specs(tm, d, ctx_len // tm, len(srcs) == 2) + [
            pl.BlockSpec((1, d), const),
            pl.BlockSpec((1, 2, 6, d), lambda b, i: (b, 0, 0, 0)),
            pl.BlockSpec((d, n), const),
            pl.BlockSpec((1, HEAD_DIM), const),
            pl.BlockSpec((1, HEAD_DIM), const),
            pl.BlockSpec((tm, HEAD_DIM), lambda b, i: (i, 0)),
            pl.BlockSpec((tm, HEAD_DIM), lambda b, i: (i, 0))],
        out_specs=out_specs,
        out_shape=out_shape,
        compiler_params=_params(("parallel", "arbitrary")),
        name="inproj",
    )(*srcs, g, mod, w, qg, kg, cc, ss)


def _conv_rows(prev, cur, nxt, w_ref, b_ref, seg_start, seg_end, silu):
    tm = cur.shape[0]
    prev = jnp.where(seg_start, 0.0, prev)
    nxt = jnp.where(seg_end, 0.0, nxt)
    ext = jnp.concatenate([prev, cur, nxt], axis=0)
    n = tm + 2 * SUBLANES
    pad = CONV_W // 2
    acc = jnp.broadcast_to(b_ref[...], cur.shape)
    for k in range(CONV_W):
        sh = (pad - k) % n
        r = ext if sh == 0 else pltpu.roll(ext, sh, 0)
        acc = acc + r[SUBLANES:SUBLANES + tm] * w_ref[k:k + 1, :]
    if silu:
        acc = acc * _sigmoid(acc)
    return acc


def _dwconv_kernel(prev_ref, cur_ref, next_ref, w_ref, b_ref, o_ref, *, nt, ct, silu):
    i = pl.program_id(1)
    seg_start = jnp.logical_or(i == 0, i == ct)
    seg_end = jnp.logical_or(i == ct - 1, i == nt - 1)
    o_ref[0] = _conv_rows(prev_ref[0], cur_ref[0], next_ref[0], w_ref, b_ref, seg_start, seg_end, silu)


def _dwconv(u, w, b, *, ctx_len, silu, tm=ROW_TILE):
    bsz, lc, c = u.shape
    nt = lc // tm
    hb = tm // SUBLANES
    nhb = lc // SUBLANES
    return pl.pallas_call(
        functools.partial(_dwconv_kernel, nt=nt, ct=ctx_len // tm, silu=silu),
        grid=(bsz, nt),
        in_specs=[pl.BlockSpec((1, SUBLANES, c), lambda b_, i: (b_, jnp.maximum(i * hb - 1, 0), 0)),
                  pl.BlockSpec((1, tm, c), lambda b_, i: (b_, i, 0)),
                  pl.BlockSpec((1, SUBLANES, c), lambda b_, i: (b_, jnp.minimum((i + 1) * hb, nhb - 1), 0)),
                  pl.BlockSpec((CONV_W, c), lambda b_, i: (0, 0)),
                  pl.BlockSpec((1, c), lambda b_, i: (0, 0))],
        out_specs=pl.BlockSpec((1, tm, c), lambda b_, i: (b_, i, 0)),
        out_shape=jax.ShapeDtypeStruct((bsz, lc, c), F32),
        compiler_params=_params(("parallel", "arbitrary")),
        name="dwconv",
    )(u, u, u, w, b.reshape(1, c))


def _expand_heads(v, e_ref):
    hi, lo = _split_bf16(v, 2)
    e = e_ref[...]
    return (jnp.dot(hi, e, preferred_element_type=F32) + jnp.dot(lo, e, preferred_element_type=F32))


def _ssd_chunk(xbc, dt_raw, dtb, alog, e_ref, h_prev, reverse):
    q = SSD_CHUNK
    heads_per_group = SSD_HEADS // SSD_GROUPS
    x = xbc[:, :SSD_WIDTH]
    bm = xbc[:, SSD_WIDTH:SSD_WIDTH + LANES]
    cm = xbc[:, SSD_WIDTH + LANES:]
    dt = _softplus(dt_raw + dtb)
    da = dt * (-jnp.exp(alog))
    ii = lax.broadcasted_iota(jnp.int32, (q, q), 0)
    jj = lax.broadcasted_iota(jnp.int32, (q, q), 1)
    causal = (jj >= ii) if reverse else (jj <= ii)
    mask_b = jnp.where(causal, 1.0, 0.0).astype(BF16)
    da_t = da.T
    dt_t = dt.T
    acs = sum(jnp.dot(mask_b, p, preferred_element_type=F32) for p in _split_bf16(da, 3))
    acs_t = sum(_dot_nt(p, mask_b) for p in _split_bf16(da_t, 3))
    total = jnp.sum(da, axis=0, keepdims=True)
    eacs = jnp.exp(acs)

    bm_b = bm.astype(BF16)
    lane = lax.broadcasted_iota(jnp.int32, (1, LANES), 1)
    in_group = [jnp.logical_and(lane >= g * SSD_STATE, lane < (g + 1) * SSD_STATE) for g in range(SSD_GROUPS)]
    cb = [_dot_nt(jnp.where(in_group[g], cm, 0.0).astype(BF16), bm_b) for g in range(SSD_GROUPS)]

    x_b = x.astype(BF16)
    h_b = h_prev.astype(BF16)
    pairs = []
    for hp in range(SSD_HEADS // 2):
        rhs = jnp.concatenate([x_b[:, hp * LANES:(hp + 1) * LANES], h_b[:, hp * LANES:(hp + 1) * LANES]], axis=0)
        outs = []
        for t in range(2):
            h = 2 * hp + t
            seg = acs[:, h:h + 1] - acs_t[h:h + 1, :]
            dec = jnp.exp(jnp.where(causal, seg, -jnp.inf))
            w = cb[h // heads_per_group] * dec * dt_t[h:h + 1, :]
            lhs = jnp.concatenate([w.astype(BF16), (cm * eacs[:, h:h + 1]).astype(BF16)], axis=1)
            outs.append(jnp.dot(lhs, rhs, preferred_element_type=F32))
        pairs.append(jnp.where(lane < SSD_HEAD_DIM, outs[0], outs[1]))
    y = jnp.concatenate(pairs, axis=1)

    to_end = jnp.exp(total - acs) * dt
    xw = (x * _expand_heads(to_end, e_ref)).astype(BF16)
    st = jnp.dot(bm.T.astype(BF16), xw, preferred_element_type=F32)
    row_g0 = lax.broadcasted_iota(jnp.int32, st.shape, 0) < SSD_STATE
    col_g0 = lax.broadcasted_iota(jnp.int32, st.shape, 1) < heads_per_group * SSD_HEAD_DIM
    chunk_decay = _expand_heads(jnp.broadcast_to(jnp.exp(total), (SUBLANES, LANES)), e_ref)[0:1]
    return y, h_prev * chunk_decay + jnp.where(row_g0 == col_g0, st, 0.0)


def _ssd_bwd_kernel(prev_ref, cur_ref, next_ref, cw_ref, cb_ref, dt_ref, dtb_ref, alog_ref, e_ref,
                    yb_ref, xc_ref, hst_ref, *, nbx, nb, bsz):
    s = pl.program_id(0)
    blk = _scan_order(s, nbx, nb)

    @pl.when(s == 0)
    def _():
        hst_ref[...] = jnp.zeros_like(hst_ref)

    seg_start = jnp.logical_or(blk == 0, blk == nbx)
    seg_end = jnp.logical_or(blk == nbx - 1, blk == nb - 1)
    xbc, h = [], []
    for bb in range(bsz):
        xbc.append(_conv_rows(prev_ref[bb], cur_ref[bb], next_ref[bb], cw_ref, cb_ref, seg_start, seg_end, True))
        xc_ref[bb] = xbc[bb]
        h.append(hst_ref[bb])
    for sub in reversed(range(cur_ref.shape[1] // SSD_CHUNK)):
        rows = slice(sub * SSD_CHUNK, (sub + 1) * SSD_CHUNK)
        for bb in range(bsz):
            y, h[bb] = _ssd_chunk(xbc[bb][rows], dt_ref[bb, rows, :], dtb_ref[...], alog_ref[...], e_ref, h[bb], True)
            yb_ref[bb, rows, :] = y
    for bb in range(bsz):
        hst_ref[bb] = h[bb]


def _ssd_fwd_kernel(xc_ref, dt_ref, dtb_ref, alog_ref, e_ref, yb_ref, z_ref, dsk_ref, ng_ref,
                    o_ref, hst_ref, *, bsz):
    @pl.when(pl.program_id(0) == 0)
    def _():
        hst_ref[...] = jnp.zeros_like(hst_ref)

    h = [hst_ref[bb] for bb in range(bsz)]
    for sub in range(xc_ref.shape[1] // SSD_CHUNK):
        rows = slice(sub * SSD_CHUNK, (sub + 1) * SSD_CHUNK)
        for bb in range(bsz):
            xbc = xc_ref[bb, rows, :]
            y, h[bb] = _ssd_chunk(xbc, dt_ref[bb, rows, :], dtb_ref[...], alog_ref[...], e_ref, h[bb], False)
            z = z_ref[bb, rows, :]
            y = (y + yb_ref[bb, rows, :] + dsk_ref[...] * xbc[:, :SSD_WIDTH]) * (z * _sigmoid(z))
            o_ref[bb, rows, :] = (_rms(y) * ng_ref[...]).astype(BF16)
    for bb in range(bsz):
        hst_ref[bb] = h[bb]


def _ssd(xbc_raw, dt, z, conv_w, conv_b, dt_bias, a_log, d_skip, norm_g, expand, *, ctx_len, q=ROW_TILE):
    bsz, lc, cch = xbc_raw.shape
    w = SSD_WIDTH
    nb = lc // q
    nbx = ctx_len // q
    hb = q // SUBLANES
    nhb = lc // SUBLANES
    pad = lambda t: jnp.pad(t, (0, LANES - SSD_HEADS)).reshape(1, LANES)
    const = lambda s: (0, 0)
    rev = lambda s: _scan_order(s, nbx, nb)
    state = [pltpu.VMEM((bsz, SSD_GROUPS * SSD_STATE, w), F32)]
    yb, xc = pl.pallas_call(
        functools.partial(_ssd_bwd_kernel, nbx=nbx, nb=nb, bsz=bsz),
        grid=(nb,),
        in_specs=[pl.BlockSpec((bsz, SUBLANES, cch), lambda s: (0, jnp.maximum(rev(s) * hb - 1, 0), 0)),
                  pl.BlockSpec((bsz, q, cch), lambda s: (0, rev(s), 0)),
                  pl.BlockSpec((bsz, SUBLANES, cch), lambda s: (0, jnp.minimum((rev(s) + 1) * hb, nhb - 1), 0)),
                  pl.BlockSpec((CONV_W, cch), const),
                  pl.BlockSpec((1, cch), const),
                  pl.BlockSpec((bsz, q, LANES), lambda s: (0, rev(s), 1)),
                  pl.BlockSpec((1, LANES), const),
                  pl.BlockSpec((1, LANES), const),
                  pl.BlockSpec((LANES, w), const)],
        out_specs=[pl.BlockSpec((bsz, q, w), lambda s: (0, rev(s), 0)),
                   pl.BlockSpec((bsz, q, cch), lambda s: (0, rev(s), 0))],
        out_shape=[jax.ShapeDtypeStruct((bsz, lc, w), F32), jax.ShapeDtypeStruct((bsz, lc, cch), F32)],
        scratch_shapes=state,
        compiler_params=_params(("arbitrary",)),
        name="ssd_bwd",
    )(xbc_raw, xbc_raw, xbc_raw, conv_w, conv_b.reshape(1, cch), dt, pad(dt_bias[1]), pad(a_log[1]), expand)
    row = lambda cols: pl.BlockSpec((bsz, q, cols), lambda s: (0, s, 0))
    return pl.pallas_call(
        functools.partial(_ssd_fwd_kernel, bsz=bsz),
        grid=(nb,),
        in_specs=[row(cch), row(LANES), pl.BlockSpec((1, LANES), const), pl.BlockSpec((1, LANES), const),
                  pl.BlockSpec((LANES, w), const), row(w), row(w), pl.BlockSpec((1, w), const),
                  pl.BlockSpec((1, w), const)],
        out_specs=row(w),
        out_shape=jax.ShapeDtypeStruct((bsz, lc, w), BF16),
        scratch_shapes=state,
        compiler_params=_params(("arbitrary",)),
        name="ssd_fwd",
    )(xc, dt, pad(dt_bias[0]), pad(a_log[0]), expand, yb, z,
      jnp.repeat(d_skip, SSD_HEAD_DIM).reshape(1, w), norm_g.reshape(1, w))


def _stack_heads(q):
    return jnp.concatenate([q[:, r * HEAD_DIM:(r + 1) * HEAD_DIM] for r in range(q.shape[1] // HEAD_DIM)], axis=0)


def _lane_parts(s):
    return [s[:, c * LANES:(c + 1) * LANES] for c in range(s.shape[1] // LANES)]


def _softmax_parts(parts, sink=None):
    pm = parts[0]
    for t in parts[1:]:
        pm = jnp.maximum(pm, t)
    m = jnp.max(pm, axis=-1, keepdims=True)
    if sink is not None:
        m = jnp.maximum(m, sink)
    mb = jnp.broadcast_to(m, pm.shape)
    ps = [jnp.exp(t - mb) for t in parts]
    lsum = ps[0]
    for t in ps[1:]:
        lsum = lsum + t
    l = jnp.sum(lsum, axis=-1, keepdims=True)
    if sink is not None:
        l = l + jnp.exp(sink - m)
    return jnp.concatenate([t.astype(BF16) for t in ps], axis=1), l


def _store_heads(o_ref, o, tq, col0=0):
    for r in range(GQA_REP):
        c0 = col0 + r * HEAD_DIM
        o_ref[0, :, c0:c0 + HEAD_DIM] = o[r * tq:(r + 1) * tq].astype(BF16)


def _attn_dense_kernel(q_ref, k_ref, v_ref, o_ref, m_scr, l_scr, acc_scr, *, tq, tk, ctx_len, lk):
    i = pl.program_id(1)
    gw = GQA_REP * HEAD_DIM
    groups = range(ATT_KV_HEADS)
    q4 = [_stack_heads(q_ref[0, :, g * gw:(g + 1) * gw]) for g in groups]
    kcol = [slice(g * HEAD_DIM, (g + 1) * HEAD_DIM) for g in groups]

    @pl.when(i < ctx_len // tq)
    def _():
        for g in groups:
            p, l = _softmax_parts(_lane_parts(_dot_nt(q4[g], k_ref[0, 0:ctx_len, kcol[g]])))
            o = jnp.dot(p, v_ref[0, 0:ctx_len, kcol[g]], preferred_element_type=F32) / l
            _store_heads(o_ref, o, tq, col0=g * gw)

    @pl.when(i >= ctx_len // tq)
    def _():
        m_scr[...] = jnp.full_like(m_scr, -jnp.inf)
        l_scr[...] = jnp.zeros_like(l_scr)
        acc_scr[...] = jnp.zeros_like(acc_scr)

        def body(j, carry):
            off = pl.multiple_of(j * tk, tk)
            for g in groups:
                parts = _lane_parts(_dot_nt(q4[g], k_ref[0, pl.ds(off, tk), kcol[g]]))
                pm = parts[0]
                for t in parts[1:]:
                    pm = jnp.maximum(pm, t)
                m_prev = m_scr[g]
                m_new = jnp.maximum(m_prev, jnp.max(pm, axis=-1, keepdims=True))
                alpha = jnp.exp(m_prev - m_new)
                ps = [jnp.exp(t - m_new) for t in parts]
                lsum = ps[0]
                for t in ps[1:]:
                    lsum = lsum + t
                l_scr[g] = alpha * l_scr[g] + lsum
                p = jnp.concatenate([t.astype(BF16) for t in ps], axis=1)
                acc_scr[g] = alpha * acc_scr[g] + jnp.dot(p, v_ref[0, pl.ds(off, tk), kcol[g]],
                                                          preferred_element_type=F32)
                m_scr[g] = m_new
            return carry

        lax.fori_loop(0, lk // tk, body, 0)
        for g in groups:
            _store_heads(o_ref, acc_scr[g] / jnp.sum(l_scr[g], axis=-1, keepdims=True), tq, col0=g * gw)


def _attn_dense(q, k, v, *, ctx_len, tq, tk):
    bsz, lc, _ = q.shape
    rows = GQA_REP * tq
    stat = pltpu.VMEM((ATT_KV_HEADS, rows, LANES), F32)
    return pl.pallas_call(
        functools.partial(_attn_dense_kernel, tq=tq, tk=tk, ctx_len=ctx_len, lk=lc),
        grid=(bsz, lc // tq),
        in_specs=[pl.BlockSpec((1, tq, ATT_WIDTH), lambda b, i: (b, i, 0)),
                  pl.BlockSpec((1, lc, KV_WIDTH), lambda b, i: (b, 0, 0)),
                  pl.BlockSpec((1, lc, KV_WIDTH), lambda b, i: (b, 0, 0))],
        out_specs=pl.BlockSpec((1, tq, ATT_WIDTH), lambda b, i: (b, i, 0)),
        out_shape=jax.ShapeDtypeStruct((bsz, lc, ATT_WIDTH), BF16),
        scratch_shapes=[stat, stat, stat],
        compiler_params=_params(("parallel", "arbitrary")),
        name="attn_dense",
    )(q, k, v)


def _attn_win_kernel(q_ref, k_ref, v_ref, sink_ref, o_ref, *, ctx_len, lat_len, tq):
    n = pl.program_id(1)
    band = tq + 2 * WINDOW
    il = n - ctx_len // tq
    lo = jnp.clip(il * tq - WINDOW, 0, lat_len - band)
    start = pl.multiple_of(ctx_len + lo, WINDOW)
    rows = lax.broadcasted_iota(jnp.int32, (GQA_REP * tq, 1), 0)
    qpos = il * tq + (rows & (tq - 1))
    kpos = lo + lax.broadcasted_iota(jnp.int32, (1, band), 1)
    valid = jnp.logical_and(jnp.abs(qpos - kpos) <= WINDOW, il >= 0)
    gw = GQA_REP * HEAD_DIM
    for g in range(ATT_KV_HEADS):
        kc0 = g * HEAD_DIM
        q4 = _stack_heads(q_ref[0, :, g * gw:(g + 1) * gw])
        s_loc = jnp.where(valid, _dot_nt(q4, k_ref[0, pl.ds(start, band), kc0:kc0 + HEAD_DIM]), -jnp.inf)
        s_ctx = _dot_nt(q4, k_ref[0, 0:ctx_len, kc0:kc0 + HEAD_DIM])
        p, l = _softmax_parts(_lane_parts(s_loc) + _lane_parts(s_ctx), sink_ref[g])
        o = (jnp.dot(p[:, :band], v_ref[0, pl.ds(start, band), kc0:kc0 + HEAD_DIM], preferred_element_type=F32)
             + jnp.dot(p[:, band:], v_ref[0, 0:ctx_len, kc0:kc0 + HEAD_DIM], preferred_element_type=F32)) / l
        _store_heads(o_ref, o, tq, col0=g * gw)


def _attn_window(q, k, v, sink, *, ctx_len, tq):
    bsz, lc, _ = q.shape
    lat_len = lc - ctx_len
    rows = GQA_REP * tq
    assert tq & (tq - 1) == 0 and ctx_len % tq == 0 and lat_len >= tq + 2 * WINDOW
    sink_col = jnp.repeat(sink.reshape(ATT_KV_HEADS, GQA_REP), tq, axis=1).reshape(ATT_KV_HEADS, rows, 1)
    return pl.pallas_call(
        functools.partial(_attn_win_kernel, ctx_len=ctx_len, lat_len=lat_len, tq=tq),
        grid=(bsz, lc // tq),
        in_specs=[pl.BlockSpec((1, tq, ATT_WIDTH), lambda b, n: (b, n, 0)),
                  pl.BlockSpec((1, lc, KV_WIDTH), lambda b, n: (b, 0, 0)),
                  pl.BlockSpec((1, lc, KV_WIDTH), lambda b, n: (b, 0, 0)),
                  pl.BlockSpec((ATT_KV_HEADS, rows, 1), lambda b, n: (0, 0, 0))],
        out_specs=pl.BlockSpec((1, tq, ATT_WIDTH), lambda b, n: (b, n, 0)),
        out_shape=jax.ShapeDtypeStruct((bsz, lc, ATT_WIDTH), BF16),
        compiler_params=_params(("parallel", "arbitrary")),
        name="attn_window",
    )(q, k, v, sink_col)


def _lru_scan_rows(a, b, carry, reverse):
    t = a.shape[0]
    pos8 = lax.broadcasted_iota(jnp.int32, (t, 1), 0) & (SUBLANES - 1)
    k = 1
    while k < SUBLANES:
        if reverse:
            keep = pos8 < SUBLANES - k
            sh = t - k
        else:
            keep = pos8 >= k
            sh = k
        a_sh = jnp.where(keep, pltpu.roll(a, sh, 0), 1.0)
        b_sh = jnp.where(keep, pltpu.roll(b, sh, 0), 0.0)
        b = b + a * b_sh
        a = a * a_sh
        k *= 2
    ngroups = t // SUBLANES
    hs = [None] * ngroups
    for g in (range(ngroups - 1, -1, -1) if reverse else range(ngroups)):
        sl = slice(g * SUBLANES, (g + 1) * SUBLANES)
        hg = a[sl] * carry + b[sl]
        carry = hg[0:1] if reverse else hg[SUBLANES - 1:SUBLANES]
        hs[g] = hg
    return jnp.concatenate(hs, axis=0), carry


def _lru_kernel(*refs, reverse):
    x_ref, wa_ref, wx_ref, ba_ref, bx_ref, lam_ref = refs[:6]
    if reverse:
        o_ref, carry_ref = refs[6:]
    else:
        hb_ref, gate_ref, o_ref, carry_ref = refs[6:]
    s = pl.program_id(1)

    @pl.when(s == 0)
    def _():
        carry_ref[...] = jnp.zeros_like(carry_ref)

    x = x_ref[0]
    x_b = x.astype(BF16)
    nslab = LRU_WIDTH // LRU_SLAB

    def gate(w_ref, b_ref):
        pre = jnp.concatenate(
            [jnp.dot(x_b[:, c * LRU_SLAB:(c + 1) * LRU_SLAB], w_ref[0, c], preferred_element_type=F32)
             for c in range(nslab)], axis=1)
        return _sigmoid(pre + b_ref[0])

    r = gate(wa_ref, ba_ref)
    ig = gate(wx_ref, bx_ref)
    log_base = -LRU_C * _softplus(-lam_ref[0])
    a = jnp.exp(r * log_base)
    y = 1.0 - a * a
    b = jnp.where(y > 0.0, y * lax.rsqrt(y), 0.0) * (ig * x)
    h, carry = _lru_scan_rows(a, b, carry_ref[...], reverse)
    carry_ref[...] = carry
    if reverse:
        o_ref[0] = h
    else:
        g = gate_ref[0]
        gelu = 0.5 * g * (1.0 + jnp.tanh(math.sqrt(2.0 / math.pi) * (g + 0.044715 * (g * g * g))))
        o_ref[0] = (gelu * (h + hb_ref[0])).astype(BF16)


def _lru(xr, gate, wa, wx, ba, bx, lam, *, ctx_len, t=ROW_TILE):
    bsz, lc, w = xr.shape
    nt = lc // t
    ct = ctx_len // t
    nslab = w // LRU_SLAB
    vec = lambda v: v.reshape(2, 1, w)

    def call(reverse, extra_in, out_dtype):
        tile = (lambda s: _scan_order(s, ct, nt)) if reverse else (lambda s: s)
        d = 1 if reverse else 0
        row = pl.BlockSpec((1, t, w), lambda b, s: (b, tile(s), 0))
        return pl.pallas_call(
            functools.partial(_lru_kernel, reverse=reverse),
            grid=(bsz, nt),
            in_specs=[row,
                      pl.BlockSpec((1, nslab, LRU_SLAB, LRU_SLAB), lambda b, s: (d, 0, 0, 0)),
                      pl.BlockSpec((1, nslab, LRU_SLAB, LRU_SLAB), lambda b, s: (d, 0, 0, 0)),
                      pl.BlockSpec((1, 1, w), lambda b, s: (d, 0, 0)),
                      pl.BlockSpec((1, 1, w), lambda b, s: (d, 0, 0)),
                      pl.BlockSpec((1, 1, w), lambda b, s: (d, 0, 0))] + [row] * len(extra_in),
            out_specs=row,
            out_shape=jax.ShapeDtypeStruct((bsz, lc, w), out_dtype),
            scratch_shapes=[pltpu.VMEM((1, w), F32)],
            compiler_params=_params(("parallel", "arbitrary")),
            name="lru_bwd" if reverse else "lru_fwd",
        )(xr, wa, wx, vec(ba), vec(bx), vec(lam), *extra_in)

    hb = call(True, (), F32)
    return call(False, (hb, gate), BF16)


def _lru_slabs(w):
    per = LRU_SLAB // LRU_BLOCK_DIM
    nslab = LRU_BLOCKS // per
    w = w.reshape(2, nslab, per, LRU_BLOCK_DIM, LRU_BLOCK_DIM)
    eye = jnp.eye(per, dtype=w.dtype)
    full = jnp.einsum('dspij,pq->dspiqj', w, eye)
    return full.reshape(2, nslab, LRU_SLAB, LRU_SLAB).astype(BF16)


def _outproj_kernel(*refs, tm, ctx_len, t0, n_src):
    ya_ref, yb_ref = refs[:2]
    x_refs = refs[2:2 + n_src]
    mod_ref, w_ref, o_ref = refs[2 + n_src:]
    i = pl.program_id(1) + t0
    half = ya_ref.shape[2]
    acc = (jnp.dot(ya_ref[0], w_ref[0:half, :], preferred_element_type=F32)
           + jnp.dot(yb_ref[0], w_ref[half:, :], preferred_element_type=F32))
    gate = _row_select(i * tm, tm, ctx_len, mod_ref, 2)
    o_ref[0] = _load_rows(x_refs, i, ctx_len // tm) + gate * acc


def _outproj(ya, yb, srcs, mod, w, *, ctx_len, latent_only, tm=ROW_TILE):
    bsz, lc, half = ya.shape
    d = w.shape[1]
    ct = ctx_len // tm
    t0 = ct if latent_only else 0
    row = lambda b, i: (b, i + t0, 0)
    if len(srcs) == 2:
        x_specs = [pl.BlockSpec((1, tm, d), lambda b, i: (b, jnp.minimum(i + t0, ct - 1), 0)),
                   pl.BlockSpec((1, tm, d), lambda b, i: (b, jnp.maximum(i + t0 - ct, 0), 0))]
    else:
        x_specs = [pl.BlockSpec((1, tm, d), row)]
    return pl.pallas_call(
        functools.partial(_outproj_kernel, tm=tm, ctx_len=ctx_len, t0=t0, n_src=len(srcs)),
        grid=(bsz, lc // tm - t0),
        in_specs=[pl.BlockSpec((1, tm, half), row), pl.BlockSpec((1, tm, half), row)] + x_specs + [
            pl.BlockSpec((1, 2, 6, d), lambda b, i: (b, 0, 0, 0)),
            pl.BlockSpec((2 * half, d), lambda b, i: (0, 0))],
        out_specs=pl.BlockSpec((1, tm, d), lambda b, i: (b, i, 0)),
        out_shape=jax.ShapeDtypeStruct((bsz, lc - t0 * tm, d), F32),
        compiler_params=_params(("parallel", "arbitrary")),
        name="outproj",
    )(ya, yb, *srcs, mod, w)


def _first_max(vals):
    m = vals[0]
    for v in vals[1:]:
        m = jnp.maximum(m, v)
    hot = []
    taken = None
    for v in vals:
        hit = v == m
        if taken is None:
            hot.append(hit)
            taken = hit
        else:
            hot.append(jnp.logical_and(hit, jnp.logical_not(taken)))
            taken = jnp.logical_or(taken, hit)
    return m, hot


def _moe_kernel(*refs, tm, ctx_len, win, final):
    x_ref, g_ref, mod_ref, wr_ref, br_ref, w13_ref, w2_ref = refs[:7]
    if final:
        gf_ref = refs[7]
    o_ref, hn_scr, gate_scr, rank_row_scr, rank_col_scr, cnt_ref, acc_scr = refs[-7:]
    i = pl.program_id(1)
    e = pl.program_id(2)

    @pl.when(e == 0)
    def _():
        hn = _rms_mod(x_ref[0], g_ref[...], i * tm, ctx_len, mod_ref, 3, 4)
        hn_scr[...] = hn.astype(BF16)
        h_hi, h_lo = _split_bf16(hn, 2)
        r_hi = jnp.dot(h_hi, wr_ref[...], preferred_element_type=F32)
        r_lo = jnp.dot(h_lo, wr_ref[...], preferred_element_type=F32)
        logits = (r_hi[:, :LANES] + r_hi[:, LANES:]) + (r_lo[:, :LANES] + r_lo[:, LANES:]) + br_ref[...]
        lt = jnp.concatenate([logits[c * LANES:(c + 1) * LANES, :].T for c in range(tm // LANES)], axis=1)
        gl = [lt[j:j + 1, :] for j in range(MOE_GROUPS)]
        gmax, ghot = _first_max(gl)
        gsum = sum(jnp.exp(v - gmax) for v in gl)
        p_grp = 1.0 / gsum
        elg = []
        for j in range(EXPERTS_PER_GROUP):
            v = jnp.zeros_like(gmax)
            for gi in range(MOE_GROUPS):
                c = MOE_GROUPS + gi * EXPERTS_PER_GROUP + j
                v = jnp.where(ghot[gi], lt[c:c + 1, :], v)
            elg.append(v)
        v1, hot1 = _first_max(elg)
        rest = [jnp.where(hh, -jnp.inf, v) for hh, v in zip(hot1, elg)]
        v2, hot2 = _first_max(rest)
        ex = jnp.exp(v2 - v1)
        w1 = p_grp / (1.0 + ex)
        w2 = p_grp * ex / (1.0 + ex)
        own = [jnp.where(hot1[j], w1, 0.0) + jnp.where(hot2[j], w2, 0.0) for j in range(EXPERTS_PER_GROUP)]
        pad_rows = jnp.zeros((LANES - MOE_GROUPS, tm), F32)
        g_t = jnp.concatenate([jnp.where(h_, 1.0, 0.0) for h_ in ghot] + [pad_rows], axis=0)
        t0 = lax.broadcasted_iota(jnp.int32, (tm, tm), 0)
        t1 = lax.broadcasted_iota(jnp.int32, (tm, tm), 1)
        before = jnp.where(t0 < t1, 1.0, 0.0).astype(BF16)
        rank_t = jnp.where(g_t > 0.0, jnp.dot(g_t.astype(BF16), before, preferred_element_type=F32), -1.0)
        rank_row_scr[...] = rank_t[0:SUBLANES]
        for gi in range(MOE_GROUPS):
            cnt_ref[gi] = jnp.sum(g_t[gi:gi + 1, :]).astype(jnp.int32)
            slab_t = jnp.concatenate([jnp.where(ghot[gi], own[j], 0.0) for j in range(EXPERTS_PER_GROUP)]
                                     + [pad_rows], axis=0)
            for c in range(tm // LANES):
                rows_c = slice(c * LANES, (c + 1) * LANES)
                slab = slab_t[:, rows_c].T
                hi, lo = _split_bf16(slab, 2)
                gate_scr[gi, rows_c, 0:LANES] = hi
                gate_scr[gi, rows_c, LANES:] = lo
                if gi == 0:
                    rank_c = rank_t[:, rows_c].T
                    for gj in range(MOE_GROUPS):
                        rank_col_scr[gj, rows_c, :] = rank_c[:, gj:gj + 1]
        acc_scr[...] = jnp.zeros_like(acc_scr)

    cnt = cnt_ref[e]
    rank_row = rank_row_scr[pl.ds(e, 1), :]
    rank_col = rank_col_scr[e]
    slot_r = lax.broadcasted_iota(jnp.int32, (win, 1), 0).astype(F32)
    slot_c = lax.broadcasted_iota(jnp.int32, (1, win), 1).astype(F32)

    def window(wi, carry):
        base = (wi * win).astype(F32)
        sel = jnp.where(rank_row - base == slot_r, 1.0, 0.0).astype(BF16)
        sel_t = jnp.where(rank_col - base == slot_c, 1.0, 0.0).astype(BF16)
        xs = jnp.dot(sel, hn_scr[...], preferred_element_type=F32).astype(BF16)
        gs = jnp.dot(sel, gate_scr[e], preferred_element_type=F32)
        gsel = gs[:, :LANES] + gs[:, LANES:]
        yw = None
        for kk in range(EXPERTS_PER_GROUP):
            au = jnp.dot(xs, w13_ref[kk], preferred_element_type=F32)
            a = au[:, :D_EXPERT]
            u = au[:, D_EXPERT:]
            hid = (a * _sigmoid(a)) * u * gsel[:, kk:kk + 1]
            part = jnp.dot(hid.astype(BF16), w2_ref[kk], preferred_element_type=F32)
            yw = part if yw is None else yw + part
        acc_scr[...] += jnp.dot(sel_t, yw.astype(BF16), preferred_element_type=F32)
        return carry

    lax.fori_loop(0, (cnt + (win - 1)) // win, window, 0)

    @pl.when(e == MOE_GROUPS - 1)
    def _():
        gate = _row_select(i * tm, tm, ctx_len, mod_ref, 5)
        out = x_ref[0] + gate * acc_scr[...]
        if final:
            out = _rms(out) * gf_ref[...]
        o_ref[0] = out


def _moe(xx, g, mod, wr, br, w13, w2, g_final, *, ctx_len):
    bsz, rows, d = xx.shape
    tm = _pick_tile(rows, MOE_TILE_PREFS)
    win = -(-(tm // MOE_GROUPS + tm // 16) // LANES) * LANES
    eb = EXPERTS_PER_GROUP
    row = lambda b, i, e: (b, i, 0)
    const = lambda b, i, e: (0, 0)
    final = g_final is not None
    return pl.pallas_call(
        functools.partial(_moe_kernel, tm=tm, ctx_len=ctx_len, win=win, final=final),
        grid=(bsz, rows // tm, MOE_GROUPS),
        in_specs=[pl.BlockSpec((1, tm, d), row),
                  pl.BlockSpec((1, d), const),
                  pl.BlockSpec((1, 2, 6, d), lambda b, i, e: (b, 0, 0, 0)),
                  pl.BlockSpec((d, 2 * LANES), const),
                  pl.BlockSpec((1, LANES), const),
                  pl.BlockSpec((eb, d, 2 * D_EXPERT), lambda b, i, e: (e, 0, 0)),
                  pl.BlockSpec((eb, D_EXPERT, d), lambda b, i, e: (e, 0, 0))]
                 + ([pl.BlockSpec((1, d), const)] if final else []),
        out_specs=pl.BlockSpec((1, tm, d), row),
        out_shape=jax.ShapeDtypeStruct((bsz, rows, d), F32),
        scratch_shapes=[pltpu.VMEM((tm, d), BF16),
                        pltpu.VMEM((MOE_GROUPS, tm, 2 * LANES), BF16),
                        pltpu.VMEM((SUBLANES, tm), F32),
                        pltpu.VMEM((MOE_GROUPS, tm, 1), F32),
                        pltpu.SMEM((MOE_GROUPS,), jnp.int32),
                        pltpu.VMEM((tm, d), F32)],
        compiler_params=_params(("parallel", "arbitrary", "arbitrary")),
        name="moe",
    )(xx, g, mod, wr, br, w13, w2, *((g_final.reshape(1, d),) if final else ()))


def _rope_tables(ctx_len, lat_len):
    rows = lat_len // GRID_W
    row = np.repeat(np.arange(rows), GRID_W).astype(np.float64)
    col = np.tile(np.arange(GRID_W), rows).astype(np.float64)
    n_freq = HEAD_DIM // 4
    inv = ROPE_THETA ** (-np.arange(n_freq, dtype=np.float64) / n_freq)
    ang = np.concatenate([row[:, None] * inv, col[:, None] * inv], axis=-1)
    cos, sin = np.cos(ang), np.sin(ang)
    cc = np.concatenate([np.ones((ctx_len, HEAD_DIM)), np.concatenate([cos, cos], axis=-1)], axis=0)
    ss = np.concatenate([np.zeros((ctx_len, HEAD_DIM)), np.concatenate([-sin, sin], axis=-1)], axis=0)
    return jnp.asarray(cc, F32), jnp.asarray(ss, F32)


def kernel(x, c, ctx, c_ctx, w_mod, b_mod, g_mix, g_ffn, moe_w_grp, moe_b_grp, moe_w_rt, moe_b_rt, moe_w1, moe_w3, moe_w2, ab_w_in, ab_w_out, ssd_conv_w, ssd_conv_b, ssd_dt_bias, ssd_a_log, ssd_d, ssd_norm_g, att_q_g, att_k_g, cd_w_in, cd_w_out, lru_conv_w, lru_conv_b, lru_w_a, lru_b_a, lru_w_x, lru_b_x, lru_lam, swa_sink, g_final):
    bsz, lat_len, d = x.shape
    ctx_len = ctx.shape[1]
    depth = w_mod.shape[0]
    lc = ctx_len + lat_len
    assert d == D_MODEL and ctx_len % ROW_TILE == 0 and lat_len % ROW_TILE == 0 and ctx_len > 0

    cc, ss = _rope_tables(ctx_len, lat_len)
    mods = _modulation(c, c_ctx, w_mod, b_mod)
    expand = jnp.asarray(np.arange(SSD_WIDTH)[None, :] // SSD_HEAD_DIM == np.arange(LANES)[:, None], BF16)
    ones_h = jnp.ones((1, HEAD_DIM), F32)
    srcs = (ctx, x)

    for i in range(depth):
        j = i // 2
        last = i == depth - 1
        m = mods[i].reshape(bsz + 1, 6, d)
        mod = jnp.stack([jnp.broadcast_to(m[0], (bsz, 6, d)), m[1:]], axis=1)
        if i % 2 == 0:
            w = ab_w_in[j]
            zc = jnp.zeros((d, LANES - SSD_HEADS), w.dtype)
            o_dt = SSD_WIDTH + SSD_CONV_CH
            o_q = o_dt + 2 * SSD_HEADS
            w_packed = jnp.concatenate([w[:, :o_dt], w[:, o_dt:o_dt + SSD_HEADS], zc,
                                        w[:, o_dt + SSD_HEADS:o_q], zc, w[:, o_q:]], axis=1).astype(BF16)
            plain = ((0, SSD_WIDTH), (SSD_WIDTH, SSD_CONV_CH), (o_dt, 2 * LANES))
            z, xbc, dt, q, k, v = _inproj(srcs, lc, g_mix[i].reshape(1, d), mod, w_packed,
                                          att_q_g[j].reshape(1, HEAD_DIM), att_k_g[j].reshape(1, HEAD_DIM), cc, ss,
                                          ctx_len=ctx_len, plain=plain, q0=o_dt + 2 * LANES, qk_norm=True)
            ya = _ssd(xbc, dt, z, ssd_conv_w[j], ssd_conv_b[j], ssd_dt_bias[j], ssd_a_log[j], ssd_d[j],
                      ssd_norm_g[j], expand, ctx_len=ctx_len)
            yb = _attn_dense(q, k, v, ctx_len=ctx_len, tq=ATT_TQ, tk=_pick_tile(lc, ATT_TK_PREFS))
            w_out = ab_w_out[j]
        else:
            plain = ((0, LRU_WIDTH), (LRU_WIDTH, LRU_WIDTH))
            gate, xr, q, k, v = _inproj(srcs, lc, g_mix[i].reshape(1, d), mod, cd_w_in[j].astype(BF16),
                                        ones_h, ones_h, cc, ss,
                                        ctx_len=ctx_len, plain=plain, q0=2 * LRU_WIDTH, qk_norm=False)
            xr = _dwconv(xr, lru_conv_w[j], lru_conv_b[j], ctx_len=ctx_len, silu=False)
            ya = _lru(xr, gate, _lru_slabs(lru_w_a[j]), _lru_slabs(lru_w_x[j]), lru_b_a[j], lru_b_x[j], lru_lam[j],
                      ctx_len=ctx_len)
            yb = _attn_window(q, k, v, swa_sink[j], ctx_len=ctx_len, tq=ATT_TQ)
            w_out = cd_w_out[j]
        xs = _outproj(ya, yb, srcs, mod, w_out.astype(BF16), ctx_len=ctx_len, latent_only=last)
        wr = jnp.concatenate([moe_w_grp[i], moe_w_rt[i],
                              jnp.zeros((d, LANES - MOE_GROUPS - N_EXPERTS), F32)], axis=1)
        br = jnp.concatenate([moe_b_grp[i], moe_b_rt[i],
                              jnp.zeros((LANES - MOE_GROUPS - N_EXPERTS,), F32)]).reshape(1, LANES)
        wr_hi = wr.astype(BF16)
        wr = jnp.concatenate([wr_hi, (wr - wr_hi.astype(F32)).astype(BF16)], axis=1)
        w13 = jnp.concatenate([moe_w1[i], moe_w3[i]], axis=-1).astype(BF16)
        xs = _moe(xs, g_ffn[i].reshape(1, d), mod, wr, br, w13, moe_w2[i].astype(BF16),
                  g_final if last else None, ctx_len=0 if last else ctx_len)
        srcs = (xs,)
    return xs
```

```python
import functools
import math

import numpy as np
import jax
import jax.numpy as jnp
from jax import lax
from jax.experimental import pallas as pl
from jax.experimental.pallas import tpu as pltpu

F32 = jnp.float32
BF16 = jnp.bfloat16

D_MODEL = 1024
GRID_W = 64
EPS = 1e-6
HEAD_DIM = 128
ATT_HEADS = 8
ATT_KV_HEADS = 2
GQA_REP = ATT_HEADS // ATT_KV_HEADS
ATT_WIDTH = ATT_HEADS * HEAD_DIM
KV_WIDTH = ATT_KV_HEADS * HEAD_DIM
ATT_SCALE = HEAD_DIM ** -0.5
LOG2E = math.log2(math.e)
ROPE_THETA = 10000.0
WINDOW = 128
SSD_HEADS = 16
SSD_HEAD_DIM = 64
SSD_WIDTH = SSD_HEADS * SSD_HEAD_DIM
SSD_GROUPS = 2
SSD_STATE = 64
SSD_CHUNK = 128
SSD_CONV_CH = SSD_WIDTH + 2 * SSD_GROUPS * SSD_STATE
LRU_WIDTH = 1024
LRU_BLOCKS = 16
LRU_BLOCK_DIM = LRU_WIDTH // LRU_BLOCKS
LRU_C = 8.0
CONV_W = 5
MOE_GROUPS = 4
EXPERTS_PER_GROUP = 4
N_EXPERTS = MOE_GROUPS * EXPERTS_PER_GROUP
D_EXPERT = 256

LANES = 128
SUBLANES = 8
LRU_SLAB = 256
VMEM_LIMIT = 56 * 1024 * 1024
ROW_TILE = 256
MOE_TILE_PREFS = (768, 512, 256)
ATT_TQ = 256
ATT_TK_PREFS = (2816, 1408, 768, 256)


def _params(sem):
    return pltpu.CompilerParams(dimension_semantics=sem, vmem_limit_bytes=VMEM_LIMIT)


def _pick_tile(n, prefs):
    for t in prefs:
        if n % t == 0:
            return t
    raise ValueError(f"no tile for {n}")


def _sigmoid(x):
    return 0.5 * jnp.tanh(0.5 * x) + 0.5


def _softplus(x):
    return jnp.maximum(x, 0.0) + jnp.log1p(jnp.exp(-jnp.abs(x)))


def _split_bf16(v, n):
    parts = []
    r = v
    for _ in range(n):
        p = r.astype(BF16)
        parts.append(p)
        r = r - p.astype(F32)
    return parts


def _dot_nt(a, b):
    return lax.dot_general(a, b, (((1,), (1,)), ((), ())), preferred_element_type=F32)


def _rms(x):
    return x * lax.rsqrt(jnp.mean(x * x, axis=-1, keepdims=True) + EPS)


def _row_select(row0, tm, ctx_len, mod_ref, k):
    if ctx_len == 0:
        return mod_ref[0, 1, k:k + 1, :]
    rows = row0 + lax.broadcasted_iota(jnp.int32, (tm, 1), 0)
    return jnp.where(rows < ctx_len, mod_ref[0, 0, k:k + 1, :], mod_ref[0, 1, k:k + 1, :])


def _rms_mod(x, g, row0, ctx_len, mod_ref, k_shift, k_scale):
    tm = x.shape[0]
    sh = _row_select(row0, tm, ctx_len, mod_ref, k_shift)
    sc = _row_select(row0, tm, ctx_len, mod_ref, k_scale)
    return _rms(x) * g * (1.0 + sc) + sh


def _scan_order(s, n_ctx, n_all):
    return jnp.where(s < n_ctx, n_ctx - 1 - s, n_all + n_ctx - 1 - s)


def _row_specs(tm, d, ct, split):
    if split:
        return [pl.BlockSpec((1, tm, d), lambda b, i: (b, jnp.minimum(i, ct - 1), 0)),
                pl.BlockSpec((1, tm, d), lambda b, i: (b, jnp.maximum(i - ct, 0), 0))]
    return [pl.BlockSpec((1, tm, d), lambda b, i: (b, i, 0))]


def _load_rows(refs, i, ct):
    if len(refs) == 2:
        return jnp.where(i < ct, refs[0][0], refs[1][0])
    return refs[0][0]


def _mod_kernel(ct_ref, w_ref, b_ref, o_ref, *, rows):
    w = w_ref[0]
    outs = []
    for r in range(rows):
        c = ct_ref[:, r:r + 1]
        act = c * _sigmoid(c)
        outs.append(jnp.sum(w * act, axis=0, keepdims=True) + b_ref[0])
    o_ref[0] = jnp.concatenate(outs, axis=0)


def _modulation(c, c_ctx, w_mod, b_mod):
    depth, d, n = w_mod.shape
    rows = c.shape[0] + 1
    ct = jnp.concatenate([c_ctx[None, :], c], axis=0).T
    tn = 1536
    return pl.pallas_call(
        functools.partial(_mod_kernel, rows=rows),
        grid=(depth, n // tn),
        in_specs=[pl.BlockSpec((d, rows), lambda i, j: (0, 0)),
                  pl.BlockSpec((1, d, tn), lambda i, j: (i, 0, j)),
                  pl.BlockSpec((1, 1, tn), lambda i, j: (i, 0, j))],
        out_specs=pl.BlockSpec((1, rows, tn), lambda i, j: (i, 0, j)),
        out_shape=jax.ShapeDtypeStruct((depth, rows, n), F32),
        compiler_params=_params(("arbitrary", "arbitrary")),
        name="modulation",
    )(ct, w_mod, b_mod.reshape(depth, 1, n))


def _inproj_kernel(*refs, tm, ctx_len, plain, q0, qk_norm, n_src):
    x_refs = refs[:n_src]
    g_ref, mod_ref, w_ref, qg_ref, kg_ref, cc_ref, ss_ref = refs[n_src:n_src + 7]
    out_refs = refs[n_src + 7:]
    i = pl.program_id(1)
    x = _load_rows(x_refs, i, ctx_len // tm)
    h = _rms_mod(x, g_ref[...], i * tm, ctx_len, mod_ref, 0, 1)
    res = jnp.dot(h.astype(BF16), w_ref[...], preferred_element_type=F32)
    for ref, (c0, cw) in zip(out_refs[:len(plain)], plain):
        ref[0] = res[:, c0:c0 + cw]
    q_ref, k_ref, v_ref = out_refs[len(plain):]
    cc = cc_ref[...]
    ss = ss_ref[...]

    def head(t, gain, scale):
        if qk_norm:
            t = _rms(t) * gain
        t = t * cc + pltpu.roll(t, HEAD_DIM // 2, 1) * ss
        return (t * scale).astype(BF16)

    for hh in range(ATT_HEADS):
        c0 = q0 + hh * HEAD_DIM
        q_ref[0, :, hh * HEAD_DIM:(hh + 1) * HEAD_DIM] = head(res[:, c0:c0 + HEAD_DIM], qg_ref[...],
                                                              ATT_SCALE * LOG2E)
    k0 = q0 + ATT_WIDTH
    for hh in range(ATT_KV_HEADS):
        c0 = k0 + hh * HEAD_DIM
        k_ref[0, :, hh * HEAD_DIM:(hh + 1) * HEAD_DIM] = head(res[:, c0:c0 + HEAD_DIM], kg_ref[...], 1.0)
    v0 = k0 + KV_WIDTH
    v_ref[0] = res[:, v0:v0 + KV_WIDTH].astype(BF16)


def _inproj(srcs, lc, g, mod, w, qg, kg, cc, ss, *, ctx_len, plain, q0, qk_norm, tm=ROW_TILE):
    bsz, _, d = srcs[0].shape
    n = w.shape[1]
    row = lambda b, i: (b, i, 0)
    const = lambda b, i: (0, 0)
    out_shape = [jax.ShapeDtypeStruct((bsz, lc, cw), F32) for _, cw in plain]
    out_specs = [pl.BlockSpec((1, tm, cw), row) for _, cw in plain]
    for cw in (ATT_WIDTH, KV_WIDTH, KV_WIDTH):
        out_shape.append(jax.ShapeDtypeStruct((bsz, lc, cw), BF16))
        out_specs.append(pl.BlockSpec((1, tm, cw), row))
    return pl.pallas_call(
        functools.partial(_inproj_kernel, tm=tm, ctx_len=ctx_len, plain=plain, q0=q0, qk_norm=qk_norm,
                          n_src=len(srcs)),
        grid=(bsz, lc // tm),
        in_specs=_row_specs(tm, d, ctx_len // tm, len(srcs) == 2) + [
            pl.BlockSpec((1, d), const),
            pl.BlockSpec((1, 2, 6, d), lambda b, i: (b, 0, 0, 0)),
            pl.BlockSpec((d, n), const),
            pl.BlockSpec((1, HEAD_DIM), const),
            pl.BlockSpec((1, HEAD_DIM), const),
            pl.BlockSpec((tm, HEAD_DIM), lambda b, i: (i, 0)),
            pl.BlockSpec((tm, HEAD_DIM), lambda b, i: (i, 0))],
        out_specs=out_specs,
        out_shape=out_shape,
        compiler_params=_params(("parallel", "arbitrary")),
        name="inproj",
    )(*srcs, g, mod, w, qg, kg, cc, ss)


def _conv_rows(prev, cur, nxt, w_ref, b_ref, seg_start, seg_end, silu):
    tm = cur.shape[0]
    prev = jnp.where(seg_start, 0.0, prev)
    nxt = jnp.where(seg_end, 0.0, nxt)
    ext = jnp.concatenate([prev, cur, nxt], axis=0)
    n = tm + 2 * SUBLANES
    pad = CONV_W // 2
    acc = jnp.broadcast_to(b_ref[...], cur.shape)
    for k in range(CONV_W):
        sh = (pad - k) % n
        r = ext if sh == 0 else pltpu.roll(ext, sh, 0)
        acc = acc + r[SUBLANES:SUBLANES + tm] * w_ref[k:k + 1, :]
    if silu:
        acc = acc * _sigmoid(acc)
    return acc


def _dwconv_kernel(prev_ref, cur_ref, next_ref, w_ref, b_ref, o_ref, *, nt, ct, silu):
    i = pl.program_id(1)
    seg_start = jnp.logical_or(i == 0, i == ct)
    seg_end = jnp.logical_or(i == ct - 1, i == nt - 1)
    o_ref[0] = _conv_rows(prev_ref[0], cur_ref[0], next_ref[0], w_ref, b_ref, seg_start, seg_end, silu)


def _dwconv(u, w, b, *, ctx_len, silu, tm=ROW_TILE):
    bsz, lc, c = u.shape
    nt = lc // tm
    hb = tm // SUBLANES
    nhb = lc // SUBLANES
    return pl.pallas_call(
        functools.partial(_dwconv_kernel, nt=nt, ct=ctx_len // tm, silu=silu),
        grid=(bsz, nt),
        in_specs=[pl.BlockSpec((1, SUBLANES, c), lambda b_, i: (b_, jnp.maximum(i * hb - 1, 0), 0)),
                  pl.BlockSpec((1, tm, c), lambda b_, i: (b_, i, 0)),
                  pl.BlockSpec((1, SUBLANES, c), lambda b_, i: (b_, jnp.minimum((i + 1) * hb, nhb - 1), 0)),
                  pl.BlockSpec((CONV_W, c), lambda b_, i: (0, 0)),
                  pl.BlockSpec((1, c), lambda b_, i: (0, 0))],
        out_specs=pl.BlockSpec((1, tm, c), lambda b_, i: (b_, i, 0)),
        out_shape=jax.ShapeDtypeStruct((bsz, lc, c), F32),
        compiler_params=_params(("parallel", "arbitrary")),
        name="dwconv",
    )(u, u, u, w, b.reshape(1, c))


def _expand_heads(v, e_ref):
    hi, lo = _split_bf16(v, 2)
    e = e_ref[...]
    return (jnp.dot(hi, e, preferred_element_type=F32) + jnp.dot(lo, e, preferred_element_type=F32))


def _ssd_chunk(xbc, dt_raw, dtb, alog, e_ref, h_prev, reverse):
    q = SSD_CHUNK
    heads_per_group = SSD_HEADS // SSD_GROUPS
    x = xbc[:, :SSD_WIDTH]
    bm = xbc[:, SSD_WIDTH:SSD_WIDTH + LANES]
    cm = xbc[:, SSD_WIDTH + LANES:]
    dt = _softplus(dt_raw + dtb)
    da = dt * (-jnp.exp(alog) * LOG2E)
    ii = lax.broadcasted_iota(jnp.int32, (q, q), 0)
    jj = lax.broadcasted_iota(jnp.int32, (q, q), 1)
    causal = (jj >= ii) if reverse else (jj <= ii)
    mask_b = jnp.where(causal, 1.0, 0.0).astype(BF16)
    da_t = da.T
    dt_t = dt.T
    acs = sum(jnp.dot(mask_b, p, preferred_element_type=F32) for p in _split_bf16(da, 3))
    acs_t = sum(_dot_nt(p, mask_b) for p in _split_bf16(da_t, 3))
    total = jnp.sum(da, axis=0, keepdims=True)
    eacs = jnp.exp2(acs)

    bm_b = bm.astype(BF16)
    lane = lax.broadcasted_iota(jnp.int32, (1, LANES), 1)
    in_group = [jnp.logical_and(lane >= g * SSD_STATE, lane < (g + 1) * SSD_STATE) for g in range(SSD_GROUPS)]
    cb = [_dot_nt(jnp.where(in_group[g], cm, 0.0).astype(BF16), bm_b) for g in range(SSD_GROUPS)]

    x_b = x.astype(BF16)
    h_b = h_prev.astype(BF16)
    pairs = []
    for hp in range(SSD_HEADS // 2):
        rhs = jnp.concatenate([x_b[:, hp * LANES:(hp + 1) * LANES], h_b[:, hp * LANES:(hp + 1) * LANES]], axis=0)
        outs = []
        for t in range(2):
            h = 2 * hp + t
            seg = acs[:, h:h + 1] - acs_t[h:h + 1, :]
            dec = jnp.exp2(jnp.where(causal, seg, -jnp.inf))
            w = cb[h // heads_per_group] * dec * dt_t[h:h + 1, :]
            lhs = jnp.concatenate([w.astype(BF16), (cm * eacs[:, h:h + 1]).astype(BF16)], axis=1)
            outs.append(jnp.dot(lhs, rhs, preferred_element_type=F32))
        pairs.append(jnp.where(lane < SSD_HEAD_DIM, outs[0], outs[1]))
    y = jnp.concatenate(pairs, axis=1)

    to_end = jnp.exp2(total - acs) * dt
    xw = (x * _expand_heads(to_end, e_ref)).astype(BF16)
    st = jnp.dot(bm.T.astype(BF16), xw, preferred_element_type=F32)
    row_g0 = lax.broadcasted_iota(jnp.int32, st.shape, 0) < SSD_STATE
    col_g0 = lax.broadcasted_iota(jnp.int32, st.shape, 1) < heads_per_group * SSD_HEAD_DIM
    chunk_decay = _expand_heads(jnp.broadcast_to(jnp.exp2(total), (SUBLANES, LANES)), e_ref)[0:1]
    return y, h_prev * chunk_decay + jnp.where(row_g0 == col_g0, st, 0.0)


def _ssd_bwd_kernel(prev_ref, cur_ref, next_ref, cw_ref, cb_ref, dt_ref, dtb_ref, alog_ref, e_ref,
                    yb_ref, xc_ref, hst_ref, *, nbx, nb, bsz):
    s = pl.program_id(0)
    blk = _scan_order(s, nbx, nb)

    @pl.when(s == 0)
    def _():
        hst_ref[...] = jnp.zeros_like(hst_ref)

    seg_start = jnp.logical_or(blk == 0, blk == nbx)
    seg_end = jnp.logical_or(blk == nbx - 1, blk == nb - 1)
    xbc, h = [], []
    for bb in range(bsz):
        xbc.append(_conv_rows(prev_ref[bb], cur_ref[bb], next_ref[bb], cw_ref, cb_ref, seg_start, seg_end, True))
        xc_ref[bb] = xbc[bb]
        h.append(hst_ref[bb])
    for sub in reversed(range(cur_ref.shape[1] // SSD_CHUNK)):
        rows = slice(sub * SSD_CHUNK, (sub + 1) * SSD_CHUNK)
        for bb in range(bsz):
            y, h[bb] = _ssd_chunk(xbc[bb][rows], dt_ref[bb, rows, :], dtb_ref[...], alog_ref[...], e_ref, h[bb], True)
            yb_ref[bb, rows, :] = y
    for bb in range(bsz):
        hst_ref[bb] = h[bb]


def _ssd_fwd_kernel(xc_ref, dt_ref, dtb_ref, alog_ref, e_ref, yb_ref, z_ref, dsk_ref, ng_ref,
                    o_ref, hst_ref, *, bsz):
    @pl.when(pl.program_id(0) == 0)
    def _():
        hst_ref[...] = jnp.zeros_like(hst_ref)

    h = [hst_ref[bb] for bb in range(bsz)]
    for sub in range(xc_ref.shape[1] // SSD_CHUNK):
        rows = slice(sub * SSD_CHUNK, (sub + 1) * SSD_CHUNK)
        for bb in range(bsz):
            xbc = xc_ref[bb, rows, :]
            y, h[bb] = _ssd_chunk(xbc, dt_ref[bb, rows, :], dtb_ref[...], alog_ref[...], e_ref, h[bb], False)
            z = z_ref[bb, rows, :]
            y = (y + yb_ref[bb, rows, :] + dsk_ref[...] * xbc[:, :SSD_WIDTH]) * (z * _sigmoid(z))
            o_ref[bb, rows, :] = (_rms(y) * ng_ref[...]).astype(BF16)
    for bb in range(bsz):
        hst_ref[bb] = h[bb]


def _ssd(xbc_raw, dt, z, conv_w, conv_b, dt_bias, a_log, d_skip, norm_g, expand, *, ctx_len, q=ROW_TILE):
    bsz, lc, cch = xbc_raw.shape
    w = SSD_WIDTH
    nb = lc // q
    nbx = ctx_len // q
    hb = q // SUBLANES
    nhb = lc // SUBLANES
    pad = lambda t: jnp.pad(t, (0, LANES - SSD_HEADS)).reshape(1, LANES)
    const = lambda s: (0, 0)
    rev = lambda s: _scan_order(s, nbx, nb)
    state = [pltpu.VMEM((bsz, SSD_GROUPS * SSD_STATE, w), F32)]
    yb, xc = pl.pallas_call(
        functools.partial(_ssd_bwd_kernel, nbx=nbx, nb=nb, bsz=bsz),
        grid=(nb,),
        in_specs=[pl.BlockSpec((bsz, SUBLANES, cch), lambda s: (0, jnp.maximum(rev(s) * hb - 1, 0), 0)),
                  pl.BlockSpec((bsz, q, cch), lambda s: (0, rev(s), 0)),
                  pl.BlockSpec((bsz, SUBLANES, cch), lambda s: (0, jnp.minimum((rev(s) + 1) * hb, nhb - 1), 0)),
                  pl.BlockSpec((CONV_W, cch), const),
                  pl.BlockSpec((1, cch), const),
                  pl.BlockSpec((bsz, q, LANES), lambda s: (0, rev(s), 1)),
                  pl.BlockSpec((1, LANES), const),
                  pl.BlockSpec((1, LANES), const),
                  pl.BlockSpec((LANES, w), const)],
        out_specs=[pl.BlockSpec((bsz, q, w), lambda s: (0, rev(s), 0)),
                   pl.BlockSpec((bsz, q, cch), lambda s: (0, rev(s), 0))],
        out_shape=[jax.ShapeDtypeStruct((bsz, lc, w), F32), jax.ShapeDtypeStruct((bsz, lc, cch), F32)],
        scratch_shapes=state,
        compiler_params=_params(("arbitrary",)),
        name="ssd_bwd",
    )(xbc_raw, xbc_raw, xbc_raw, conv_w, conv_b.reshape(1, cch), dt, pad(dt_bias[1]), pad(a_log[1]), expand)
    row = lambda cols: pl.BlockSpec((bsz, q, cols), lambda s: (0, s, 0))
    return pl.pallas_call(
        functools.partial(_ssd_fwd_kernel, bsz=bsz),
        grid=(nb,),
        in_specs=[row(cch), row(LANES), pl.BlockSpec((1, LANES), const), pl.BlockSpec((1, LANES), const),
                  pl.BlockSpec((LANES, w), const), row(w), row(w), pl.BlockSpec((1, w), const),
                  pl.BlockSpec((1, w), const)],
        out_specs=row(w),
        out_shape=jax.ShapeDtypeStruct((bsz, lc, w), BF16),
        scratch_shapes=state,
        compiler_params=_params(("arbitrary",)),
        name="ssd_fwd",
    )(xc, dt, pad(dt_bias[0]), pad(a_log[0]), expand, yb, z,
      jnp.repeat(d_skip, SSD_HEAD_DIM).reshape(1, w), norm_g.reshape(1, w))


def _stack_heads(q):
    return jnp.concatenate([q[:, r * HEAD_DIM:(r + 1) * HEAD_DIM] for r in range(q.shape[1] // HEAD_DIM)], axis=0)


def _lane_parts(s):
    return [s[:, c * LANES:(c + 1) * LANES] for c in range(s.shape[1] // LANES)]


def _softmax_parts(parts, sink=None):
    pm = parts[0]
    for t in parts[1:]:
        pm = jnp.maximum(pm, t)
    m = jnp.max(pm, axis=-1, keepdims=True)
    if sink is not None:
        m = jnp.maximum(m, sink)
    mb = jnp.broadcast_to(m, pm.shape)
    ps = [jnp.exp2(t - mb) for t in parts]
    lsum = ps[0]
    for t in ps[1:]:
        lsum = lsum + t
    l = jnp.sum(lsum, axis=-1, keepdims=True)
    if sink is not None:
        l = l + jnp.exp2(sink - m)
    return jnp.concatenate([t.astype(BF16) for t in ps], axis=1), l


def _store_heads(o_ref, o, tq, col0=0):
    for r in range(GQA_REP):
        c0 = col0 + r * HEAD_DIM
        o_ref[0, :, c0:c0 + HEAD_DIM] = o[r * tq:(r + 1) * tq].astype(BF16)


def _attn_dense_kernel(q_ref, k_ref, v_ref, o_ref, m_scr, l_scr, acc_scr, *, tq, tk, ctx_len, lk):
    i = pl.program_id(1)
    gw = GQA_REP * HEAD_DIM
    groups = range(ATT_KV_HEADS)
    q4 = [_stack_heads(q_ref[0, :, g * gw:(g + 1) * gw]) for g in groups]
    kcol = [slice(g * HEAD_DIM, (g + 1) * HEAD_DIM) for g in groups]

    @pl.when(i < ctx_len // tq)
    def _():
        for g in groups:
            p, l = _softmax_parts(_lane_parts(_dot_nt(q4[g], k_ref[0, 0:ctx_len, kcol[g]])))
            o = jnp.dot(p, v_ref[0, 0:ctx_len, kcol[g]], preferred_element_type=F32) / l
            _store_heads(o_ref, o, tq, col0=g * gw)

    @pl.when(i >= ctx_len // tq)
    def _():
        m_scr[...] = jnp.full_like(m_scr, -jnp.inf)
        l_scr[...] = jnp.zeros_like(l_scr)
        acc_scr[...] = jnp.zeros_like(acc_scr)

        def body(j, carry):
            off = pl.multiple_of(j * tk, tk)
            for g in groups:
                parts = _lane_parts(_dot_nt(q4[g], k_ref[0, pl.ds(off, tk), kcol[g]]))
                pm = parts[0]
                for t in parts[1:]:
                    pm = jnp.maximum(pm, t)
                m_prev = m_scr[g]
                m_new = jnp.maximum(m_prev, jnp.max(pm, axis=-1, keepdims=True))
                alpha = jnp.exp2(m_prev - m_new)
                ps = [jnp.exp2(t - m_new) for t in parts]
                lsum = ps[0]
                for t in ps[1:]:
                    lsum = lsum + t
                l_scr[g] = alpha * l_scr[g] + lsum
                p = jnp.concatenate([t.astype(BF16) for t in ps], axis=1)
                acc_scr[g] = alpha * acc_scr[g] + jnp.dot(p, v_ref[0, pl.ds(off, tk), kcol[g]],
                                                          preferred_element_type=F32)
                m_scr[g] = m_new
            return carry

        lax.fori_loop(0, lk // tk, body, 0)
        for g in groups:
            _store_heads(o_ref, acc_scr[g] / jnp.sum(l_scr[g], axis=-1, keepdims=True), tq, col0=g * gw)


def _attn_dense(q, k, v, *, ctx_len, tq, tk):
    bsz, lc, _ = q.shape
    rows = GQA_REP * tq
    stat = pltpu.VMEM((ATT_KV_HEADS, rows, LANES), F32)
    return pl.pallas_call(
        functools.partial(_attn_dense_kernel, tq=tq, tk=tk, ctx_len=ctx_len, lk=lc),
        grid=(bsz, lc // tq),
        in_specs=[pl.BlockSpec((1, tq, ATT_WIDTH), lambda b, i: (b, i, 0)),
                  pl.BlockSpec((1, lc, KV_WIDTH), lambda b, i: (b, 0, 0)),
                  pl.BlockSpec((1, lc, KV_WIDTH), lambda b, i: (b, 0, 0))],
        out_specs=pl.BlockSpec((1, tq, ATT_WIDTH), lambda b, i: (b, i, 0)),
        out_shape=jax.ShapeDtypeStruct((bsz, lc, ATT_WIDTH), BF16),
        scratch_shapes=[stat, stat, stat],
        compiler_params=_params(("parallel", "arbitrary")),
        name="attn_dense",
    )(q, k, v)


def _attn_win_kernel(q_ref, k_ref, v_ref, sink_ref, bias_ref, o_ref, *, ctx_len, lat_len, tq):
    n = pl.program_id(1)
    band = tq + 2 * WINDOW
    il = n - ctx_len // tq
    lo = jnp.clip(il * tq - WINDOW, 0, lat_len - band)
    start = pl.multiple_of(ctx_len + lo, WINDOW)
    bias = bias_ref[0]
    gw = GQA_REP * HEAD_DIM
    for g in range(ATT_KV_HEADS):
        kc0 = g * HEAD_DIM
        q4 = _stack_heads(q_ref[0, :, g * gw:(g + 1) * gw])
        s_loc = _dot_nt(q4, k_ref[0, pl.ds(start, band), kc0:kc0 + HEAD_DIM]) + bias
        s_ctx = _dot_nt(q4, k_ref[0, 0:ctx_len, kc0:kc0 + HEAD_DIM])
        p, l = _softmax_parts(_lane_parts(s_loc) + _lane_parts(s_ctx), sink_ref[g])
        o = (jnp.dot(p[:, :band], v_ref[0, pl.ds(start, band), kc0:kc0 + HEAD_DIM], preferred_element_type=F32)
             + jnp.dot(p[:, band:], v_ref[0, 0:ctx_len, kc0:kc0 + HEAD_DIM], preferred_element_type=F32)) / l
        _store_heads(o_ref, o, tq, col0=g * gw)


def _attn_window(q, k, v, sink, *, ctx_len, tq):
    bsz, lc, _ = q.shape
    lat_len = lc - ctx_len
    rows = GQA_REP * tq
    assert tq & (tq - 1) == 0 and ctx_len % tq == 0 and lat_len >= tq + 2 * WINDOW
    sink_col = jnp.repeat(sink.reshape(ATT_KV_HEADS, GQA_REP) * LOG2E, tq, axis=1).reshape(ATT_KV_HEADS, rows, 1)
    band = tq + 2 * WINDOW
    ct = ctx_len // tq
    nlt = lat_len // tq
    r = np.arange(tq)[:, None]
    c = np.arange(band)[None, :]
    masks = [np.abs(c - off - r) <= WINDOW for off in (0, WINDOW, band - tq)] + [np.zeros((tq, band), bool)]
    bias = jnp.asarray(np.tile(np.where(np.stack(masks), 0.0, -np.inf), (1, GQA_REP, 1)), F32)

    def variant(n):
        il = n - ct
        return jnp.where(il < 0, 3, jnp.where(il == 0, 0, jnp.where(il == nlt - 1, 2, 1)))

    return pl.pallas_call(
        functools.partial(_attn_win_kernel, ctx_len=ctx_len, lat_len=lat_len, tq=tq),
        grid=(bsz, lc // tq),
        in_specs=[pl.BlockSpec((1, tq, ATT_WIDTH), lambda b, n: (b, n, 0)),
                  pl.BlockSpec((1, lc, KV_WIDTH), lambda b, n: (b, 0, 0)),
                  pl.BlockSpec((1, lc, KV_WIDTH), lambda b, n: (b, 0, 0)),
                  pl.BlockSpec((ATT_KV_HEADS, rows, 1), lambda b, n: (0, 0, 0)),
                  pl.BlockSpec((1, rows, band), lambda b, n: (variant(n), 0, 0))],
        out_specs=pl.BlockSpec((1, tq, ATT_WIDTH), lambda b, n: (b, n, 0)),
        out_shape=jax.ShapeDtypeStruct((bsz, lc, ATT_WIDTH), BF16),
        compiler_params=_params(("parallel", "arbitrary")),
        name="attn_window",
    )(q, k, v, sink_col, bias)


def _lru_scan_rows(a, b, carry, reverse):
    t = a.shape[0]
    pos8 = lax.broadcasted_iota(jnp.int32, (t, 1), 0) & (SUBLANES - 1)
    k = 1
    while k < SUBLANES:
        if reverse:
            keep = pos8 < SUBLANES - k
            sh = t - k
        else:
            keep = pos8 >= k
            sh = k
        a_sh = jnp.where(keep, pltpu.roll(a, sh, 0), 1.0)
        b_sh = jnp.where(keep, pltpu.roll(b, sh, 0), 0.0)
        b = b + a * b_sh
        a = a * a_sh
        k *= 2
    ngroups = t // SUBLANES
    hs = [None] * ngroups
    for g in (range(ngroups - 1, -1, -1) if reverse else range(ngroups)):
        sl = slice(g * SUBLANES, (g + 1) * SUBLANES)
        hg = a[sl] * carry + b[sl]
        carry = hg[0:1] if reverse else hg[SUBLANES - 1:SUBLANES]
        hs[g] = hg
    return jnp.concatenate(hs, axis=0), carry


def _lru_kernel(*refs, reverse):
    x_ref, wa_ref, wx_ref, ba_ref, bx_ref, lam_ref = refs[:6]
    if reverse:
        o_ref, carry_ref = refs[6:]
    else:
        hb_ref, gate_ref, o_ref, carry_ref = refs[6:]
    s = pl.program_id(1)

    @pl.when(s == 0)
    def _():
        carry_ref[...] = jnp.zeros_like(carry_ref)

    x = x_ref[0]
    x_b = x.astype(BF16)
    nslab = LRU_WIDTH // LRU_SLAB

    def gate(w_ref, b_ref):
        pre = jnp.concatenate(
            [jnp.dot(x_b[:, c * LRU_SLAB:(c + 1) * LRU_SLAB], w_ref[0, c], preferred_element_type=F32)
             for c in range(nslab)], axis=1)
        return _sigmoid(pre + b_ref[0])

    r = gate(wa_ref, ba_ref)
    ig = gate(wx_ref, bx_ref)
    log2_base = (-LRU_C * LOG2E) * _softplus(-lam_ref[0])
    a = jnp.exp2(r * log2_base)
    y = 1.0 - a * a
    b = jnp.where(y > 0.0, y * lax.rsqrt(y), 0.0) * (ig * x)
    h, carry = _lru_scan_rows(a, b, carry_ref[...], reverse)
    carry_ref[...] = carry
    if reverse:
        o_ref[0] = h
    else:
        g = gate_ref[0]
        gelu = 0.5 * g * (1.0 + jnp.tanh(math.sqrt(2.0 / math.pi) * (g + 0.044715 * (g * g * g))))
        o_ref[0] = (gelu * (h + hb_ref[0])).astype(BF16)


def _lru(xr, gate, wa, wx, ba, bx, lam, *, ctx_len, t=ROW_TILE):
    bsz, lc, w = xr.shape
    nt = lc // t
    ct = ctx_len // t
    nslab = w // LRU_SLAB
    vec = lambda v: v.reshape(2, 1, w)

    def call(reverse, extra_in, out_dtype):
        tile = (lambda s: _scan_order(s, ct, nt)) if reverse else (lambda s: s)
        d = 1 if reverse else 0
        row = pl.BlockSpec((1, t, w), lambda b, s: (b, tile(s), 0))
        return pl.pallas_call(
            functools.partial(_lru_kernel, reverse=reverse),
            grid=(bsz, nt),
            in_specs=[row,
                      pl.BlockSpec((1, nslab, LRU_SLAB, LRU_SLAB), lambda b, s: (d, 0, 0, 0)),
                      pl.BlockSpec((1, nslab, LRU_SLAB, LRU_SLAB), lambda b, s: (d, 0, 0, 0)),
                      pl.BlockSpec((1, 1, w), lambda b, s: (d, 0, 0)),
                      pl.BlockSpec((1, 1, w), lambda b, s: (d, 0, 0)),
                      pl.BlockSpec((1, 1, w), lambda b, s: (d, 0, 0))] + [row] * len(extra_in),
            out_specs=row,
            out_shape=jax.ShapeDtypeStruct((bsz, lc, w), out_dtype),
            scratch_shapes=[pltpu.VMEM((1, w), F32)],
            compiler_params=_params(("parallel", "arbitrary")),
            name="lru_bwd" if reverse else "lru_fwd",
        )(xr, wa, wx, vec(ba), vec(bx), vec(lam), *extra_in)

    hb = call(True, (), F32)
    return call(False, (hb, gate), BF16)


def _lru_slabs(w):
    per = LRU_SLAB // LRU_BLOCK_DIM
    nslab = LRU_BLOCKS // per
    w = w.reshape(2, nslab, per, LRU_BLOCK_DIM, LRU_BLOCK_DIM)
    eye = jnp.eye(per, dtype=w.dtype)
    full = jnp.einsum('dspij,pq->dspiqj', w, eye)
    return full.reshape(2, nslab, LRU_SLAB, LRU_SLAB).astype(BF16)


def _outproj_kernel(*refs, tm, ctx_len, t0, n_src):
    ya_ref, yb_ref = refs[:2]
    x_refs = refs[2:2 + n_src]
    mod_ref, w_ref, o_ref = refs[2 + n_src:]
    i = pl.program_id(1) + t0
    half = ya_ref.shape[2]
    acc = (jnp.dot(ya_ref[0], w_ref[0:half, :], preferred_element_type=F32)
           + jnp.dot(yb_ref[0], w_ref[half:, :], preferred_element_type=F32))
    gate = _row_select(i * tm, tm, ctx_len, mod_ref, 2)
    o_ref[0] = _load_rows(x_refs, i, ctx_len // tm) + gate * acc


def _outproj(ya, yb, srcs, mod, w, *, ctx_len, latent_only, tm=ROW_TILE):
    bsz, lc, half = ya.shape
    d = w.shape[1]
    ct = ctx_len // tm
    t0 = ct if latent_only else 0
    row = lambda b, i: (b, i + t0, 0)
    if len(srcs) == 2:
        x_specs = [pl.BlockSpec((1, tm, d), lambda b, i: (b, jnp.minimum(i + t0, ct - 1), 0)),
                   pl.BlockSpec((1, tm, d), lambda b, i: (b, jnp.maximum(i + t0 - ct, 0), 0))]
    else:
        x_specs = [pl.BlockSpec((1, tm, d), row)]
    return pl.pallas_call(
        functools.partial(_outproj_kernel, tm=tm, ctx_len=ctx_len, t0=t0, n_src=len(srcs)),
        grid=(bsz, lc // tm - t0),
        in_specs=[pl.BlockSpec((1, tm, half), row), pl.BlockSpec((1, tm, half), row)] + x_specs + [
            pl.BlockSpec((1, 2, 6, d), lambda b, i: (b, 0, 0, 0)),
            pl.BlockSpec((2 * half, d), lambda b, i: (0, 0))],
        out_specs=pl.BlockSpec((1, tm, d), lambda b, i: (b, i, 0)),
        out_shape=jax.ShapeDtypeStruct((bsz, lc - t0 * tm, d), F32),
        compiler_params=_params(("parallel", "arbitrary")),
        name="outproj",
    )(ya, yb, *srcs, mod, w)


def _first_max(vals):
    m = vals[0]
    for v in vals[1:]:
        m = jnp.maximum(m, v)
    hot = []
    taken = None
    for v in vals:
        hit = v == m
        if taken is None:
            hot.append(hit)
            taken = hit
        else:
            hot.append(jnp.logical_and(hit, jnp.logical_not(taken)))
            taken = jnp.logical_or(taken, hit)
    return m, hot


def _moe_kernel(*refs, tm, ctx_len, wins, final):
    x_ref, g_ref, mod_ref, wr_ref, br_ref, w13_ref, w2_ref = refs[:7]
    if final:
        gf_ref = refs[7]
    o_ref, hn_scr, gate_scr, rank_row_scr, rank_col_scr, cnt_ref, acc_scr = refs[-7:]
    i = pl.program_id(1)
    e = pl.program_id(2)

    @pl.when(e == 0)
    def _():
        hn = _rms_mod(x_ref[0], g_ref[...], i * tm, ctx_len, mod_ref, 3, 4)
        hn_scr[...] = hn.astype(BF16)
        h_hi, h_lo = _split_bf16(hn, 2)
        r_hi = jnp.dot(h_hi, wr_ref[...], preferred_element_type=F32)
        r_lo = jnp.dot(h_lo, wr_ref[...], preferred_element_type=F32)
        logits = (r_hi[:, :LANES] + r_hi[:, LANES:]) + (r_lo[:, :LANES] + r_lo[:, LANES:]) + br_ref[...]
        lt = jnp.concatenate([logits[c * LANES:(c + 1) * LANES, :].T for c in range(tm // LANES)], axis=1)
        gl = [lt[j:j + 1, :] for j in range(MOE_GROUPS)]
        gmax, ghot = _first_max(gl)
        gsum = sum(jnp.exp(v - gmax) for v in gl)
        p_grp = 1.0 / gsum
        elg = []
        for j in range(EXPERTS_PER_GROUP):
            v = jnp.zeros_like(gmax)
            for gi in range(MOE_GROUPS):
                c = MOE_GROUPS + gi * EXPERTS_PER_GROUP + j
                v = jnp.where(ghot[gi], lt[c:c + 1, :], v)
            elg.append(v)
        v1, hot1 = _first_max(elg)
        rest = [jnp.where(hh, -jnp.inf, v) for hh, v in zip(hot1, elg)]
        v2, hot2 = _first_max(rest)
        ex = jnp.exp(v2 - v1)
        w1 = p_grp / (1.0 + ex)
        w2 = p_grp * ex / (1.0 + ex)
        own = [jnp.where(hot1[j], w1, 0.0) + jnp.where(hot2[j], w2, 0.0) for j in range(EXPERTS_PER_GROUP)]
        pad_rows = jnp.zeros((LANES - MOE_GROUPS, tm), F32)
        g_t = jnp.concatenate([jnp.where(h_, 1.0, 0.0) for h_ in ghot] + [pad_rows], axis=0)
        t0 = lax.broadcasted_iota(jnp.int32, (tm, tm), 0)
        t1 = lax.broadcasted_iota(jnp.int32, (tm, tm), 1)
        before = jnp.where(t0 < t1, 1.0, 0.0).astype(BF16)
        rank_t = jnp.where(g_t > 0.0, jnp.dot(g_t.astype(BF16), before, preferred_element_type=F32), -1.0)
        rank_row_scr[...] = rank_t[0:SUBLANES]
        for gi in range(MOE_GROUPS):
            cnt_ref[gi] = jnp.sum(g_t[gi:gi + 1, :]).astype(jnp.int32)
            slab_t = jnp.concatenate([jnp.where(ghot[gi], own[j], 0.0) for j in range(EXPERTS_PER_GROUP)]
                                     + [pad_rows], axis=0)
            for c in range(tm // LANES):
                rows_c = slice(c * LANES, (c + 1) * LANES)
                slab = slab_t[:, rows_c].T
                hi, lo = _split_bf16(slab, 2)
                gate_scr[gi, rows_c, 0:LANES] = hi
                gate_scr[gi, rows_c, LANES:] = lo
                if gi == 0:
                    rank_c = rank_t[:, rows_c].T
                    for gj in range(MOE_GROUPS):
                        rank_col_scr[gj, rows_c, :] = rank_c[:, gj:gj + 1]
        acc_scr[...] = jnp.zeros_like(acc_scr)

    cnt = cnt_ref[e]
    rank_row = rank_row_scr[pl.ds(e, 1), :]
    rank_col = rank_col_scr[e]

    def window(first, w_):
        base = first.astype(F32)
        slot_r = lax.broadcasted_iota(jnp.int32, (w_, 1), 0).astype(F32)
        slot_c = lax.broadcasted_iota(jnp.int32, (1, w_), 1).astype(F32)
        sel = jnp.where(rank_row - base == slot_r, 1.0, 0.0).astype(BF16)
        sel_t = jnp.where(rank_col - base == slot_c, 1.0, 0.0).astype(BF16)
        xs = jnp.dot(sel, hn_scr[...], preferred_element_type=F32).astype(BF16)
        gs = jnp.dot(sel, gate_scr[e], preferred_element_type=F32)
        gsel = gs[:, :LANES] + gs[:, LANES:]
        yw = None
        for kk in range(EXPERTS_PER_GROUP):
            au = jnp.dot(xs, w13_ref[kk], preferred_element_type=F32)
            a = au[:, :D_EXPERT]
            u = au[:, D_EXPERT:]
            hid = (a * _sigmoid(a)) * u * gsel[:, kk:kk + 1]
            part = jnp.dot(hid.astype(BF16), w2_ref[kk], preferred_element_type=F32)
            yw = part if yw is None else yw + part
        acc_scr[...] += jnp.dot(sel_t, yw.astype(BF16), preferred_element_type=F32)

    w_top = wins[-1]
    lower = 0
    for w_ in wins[:-1]:
        pl.when(jnp.logical_and(cnt > lower, cnt <= w_))(functools.partial(window, jnp.int32(0), w_))
        lower = w_

    @pl.when(cnt > lower)
    def _():
        def body(wi, carry):
            window(wi * w_top, w_top)
            return carry
        lax.fori_loop(0, (cnt + (w_top - 1)) // w_top, body, 0)

    @pl.when(e == MOE_GROUPS - 1)
    def _():
        gate = _row_select(i * tm, tm, ctx_len, mod_ref, 5)
        out = x_ref[0] + gate * acc_scr[...]
        if final:
            out = _rms(out) * gf_ref[...]
        o_ref[0] = out


def _moe(xx, g, mod, wr, br, w13, w2, g_final, *, ctx_len):
    bsz, rows, d = xx.shape
    tm = _pick_tile(rows, MOE_TILE_PREFS)
    mid = -(-(tm // MOE_GROUPS) // LANES) * LANES
    wins = (mid, mid + LANES, mid + 2 * LANES)
    eb = EXPERTS_PER_GROUP
    row = lambda b, i, e: (b, i, 0)
    const = lambda b, i, e: (0, 0)
    final = g_final is not None
    return pl.pallas_call(
        functools.partial(_moe_kernel, tm=tm, ctx_len=ctx_len, wins=wins, final=final),
        grid=(bsz, rows // tm, MOE_GROUPS),
        in_specs=[pl.BlockSpec((1, tm, d), row),
                  pl.BlockSpec((1, d), const),
                  pl.BlockSpec((1, 2, 6, d), lambda b, i, e: (b, 0, 0, 0)),
                  pl.BlockSpec((d, 2 * LANES), const),
                  pl.BlockSpec((1, LANES), const),
                  pl.BlockSpec((eb, d, 2 * D_EXPERT), lambda b, i, e: (e, 0, 0)),
                  pl.BlockSpec((eb, D_EXPERT, d), lambda b, i, e: (e, 0, 0))]
                 + ([pl.BlockSpec((1, d), const)] if final else []),
        out_specs=pl.BlockSpec((1, tm, d), row),
        out_shape=jax.ShapeDtypeStruct((bsz, rows, d), F32),
        scratch_shapes=[pltpu.VMEM((tm, d), BF16),
                        pltpu.VMEM((MOE_GROUPS, tm, 2 * LANES), BF16),
                        pltpu.VMEM((SUBLANES, tm), F32),
                        pltpu.VMEM((MOE_GROUPS, tm, 1), F32),
                        pltpu.SMEM((MOE_GROUPS,), jnp.int32),
                        pltpu.VMEM((tm, d), F32)],
        compiler_params=_params(("parallel", "arbitrary", "arbitrary")),
        name="moe",
    )(xx, g, mod, wr, br, w13, w2, *((g_final.reshape(1, d),) if final else ()))


def _rope_tables(ctx_len, lat_len):
    rows = lat_len // GRID_W
    row = np.repeat(np.arange(rows), GRID_W).astype(np.float64)
    col = np.tile(np.arange(GRID_W), rows).astype(np.float64)
    n_freq = HEAD_DIM // 4
    inv = ROPE_THETA ** (-np.arange(n_freq, dtype=np.float64) / n_freq)
    ang = np.concatenate([row[:, None] * inv, col[:, None] * inv], axis=-1)
    cos, sin = np.cos(ang), np.sin(ang)
    cc = np.concatenate([np.ones((ctx_len, HEAD_DIM)), np.concatenate([cos, cos], axis=-1)], axis=0)
    ss = np.concatenate([np.zeros((ctx_len, HEAD_DIM)), np.concatenate([-sin, sin], axis=-1)], axis=0)
    return jnp.asarray(cc, F32), jnp.asarray(ss, F32)


def kernel(x, c, ctx, c_ctx, w_mod, b_mod, g_mix, g_ffn, moe_w_grp, moe_b_grp, moe_w_rt, moe_b_rt, moe_w1, moe_w3, moe_w2, ab_w_in, ab_w_out, ssd_conv_w, ssd_conv_b, ssd_dt_bias, ssd_a_log, ssd_d, ssd_norm_g, att_q_g, att_k_g, cd_w_in, cd_w_out, lru_conv_w, lru_conv_b, lru_w_a, lru_b_a, lru_w_x, lru_b_x, lru_lam, swa_sink, g_final):
    bsz, lat_len, d = x.shape
    ctx_len = ctx.shape[1]
    depth = w_mod.shape[0]
    lc = ctx_len + lat_len
    assert d == D_MODEL and ctx_len % ROW_TILE == 0 and lat_len % ROW_TILE == 0 and ctx_len > 0

    cc, ss = _rope_tables(ctx_len, lat_len)
    mods = _modulation(c, c_ctx, w_mod, b_mod)
    expand = jnp.asarray(np.arange(SSD_WIDTH)[None, :] // SSD_HEAD_DIM == np.arange(LANES)[:, None], BF16)
    ones_h = jnp.ones((1, HEAD_DIM), F32)
    srcs = (ctx, x)

    for i in range(depth):
        j = i // 2
        last = i == depth - 1
        m = mods[i].reshape(bsz + 1, 6, d)
        mod = jnp.stack([jnp.broadcast_to(m[0], (bsz, 6, d)), m[1:]], axis=1)
        if i % 2 == 0:
            w = ab_w_in[j]
            zc = jnp.zeros((d, LANES - SSD_HEADS), w.dtype)
            o_dt = SSD_WIDTH + SSD_CONV_CH
            o_q = o_dt + 2 * SSD_HEADS
            w_packed = jnp.concatenate([w[:, :o_dt], w[:, o_dt:o_dt + SSD_HEADS], zc,
                                        w[:, o_dt + SSD_HEADS:o_q], zc, w[:, o_q:]], axis=1).astype(BF16)
            plain = ((0, SSD_WIDTH), (SSD_WIDTH, SSD_CONV_CH), (o_dt, 2 * LANES))
            z, xbc, dt, q, k, v = _inproj(srcs, lc, g_mix[i].reshape(1, d), mod, w_packed,
                                          att_q_g[j].reshape(1, HEAD_DIM), att_k_g[j].reshape(1, HEAD_DIM), cc, ss,
                                          ctx_len=ctx_len, plain=plain, q0=o_dt + 2 * LANES, qk_norm=True)
            ya = _ssd(xbc, dt, z, ssd_conv_w[j], ssd_conv_b[j], ssd_dt_bias[j], ssd_a_log[j], ssd_d[j],
                      ssd_norm_g[j], expand, ctx_len=ctx_len)
            yb = _attn_dense(q, k, v, ctx_len=ctx_len, tq=ATT_TQ, tk=_pick_tile(lc, ATT_TK_PREFS))
            w_out = ab_w_out[j]
        else:
            plain = ((0, LRU_WIDTH), (LRU_WIDTH, LRU_WIDTH))
            gate, xr, q, k, v = _inproj(srcs, lc, g_mix[i].reshape(1, d), mod, cd_w_in[j].astype(BF16),
                                        ones_h, ones_h, cc, ss,
                                        ctx_len=ctx_len, plain=plain, q0=2 * LRU_WIDTH, qk_norm=False)
            xr = _dwconv(xr, lru_conv_w[j], lru_conv_b[j], ctx_len=ctx_len, silu=False)
            ya = _lru(xr, gate, _lru_slabs(lru_w_a[j]), _lru_slabs(lru_w_x[j]), lru_b_a[j], lru_b_x[j], lru_lam[j],
                      ctx_len=ctx_len)
            yb = _attn_window(q, k, v, swa_sink[j], ctx_len=ctx_len, tq=ATT_TQ)
            w_out = cd_w_out[j]
        xs = _outproj(ya, yb, srcs, mod, w_out.astype(BF16), ctx_len=ctx_len, latent_only=last)
        wr = jnp.concatenate([moe_w_grp[i], moe_w_rt[i],
                              jnp.zeros((d, LANES - MOE_GROUPS - N_EXPERTS), F32)], axis=1)
        br = jnp.concatenate([moe_b_grp[i], moe_b_rt[i],
                              jnp.zeros((LANES - MOE_GROUPS - N_EXPERTS,), F32)]).reshape(1, LANES)
        wr_hi = wr.astype(BF16)
        wr = jnp.concatenate([wr_hi, (wr - wr_hi.astype(F32)).astype(BF16)], axis=1)
        w13 = jnp.concatenate([moe_w1[i], moe_w3[i]], axis=-1).astype(BF16)
        xs = _moe(xs, g_ffn[i].reshape(1, d), mod, wr, br, w13, moe_w2[i].astype(BF16),
                  g_final if last else None, ctx_len=0 if last else ctx_len)
        srcs = (xs,)
    return xs
```

```python
import functools
import math

import numpy as np
import jax
import jax.numpy as jnp
from jax import lax
from jax.experimental import pallas as pl
from jax.experimental.pallas import tpu as pltpu

F32 = jnp.float32
BF16 = jnp.bfloat16

D_MODEL = 1024
GRID_W = 64
EPS = 1e-6
HEAD_DIM = 128
ATT_HEADS = 8
ATT_KV_HEADS = 2
GQA_REP = ATT_HEADS // ATT_KV_HEADS
ATT_WIDTH = ATT_HEADS * HEAD_DIM
KV_WIDTH = ATT_KV_HEADS * HEAD_DIM
ATT_SCALE = HEAD_DIM ** -0.5
LOG2E = math.log2(math.e)
ROPE_THETA = 10000.0
WINDOW = 128
SSD_HEADS = 16
SSD_HEAD_DIM = 64
SSD_WIDTH = SSD_HEADS * SSD_HEAD_DIM
SSD_GROUPS = 2
SSD_STATE = 64
SSD_CHUNK = 128
SSD_CONV_CH = SSD_WIDTH + 2 * SSD_GROUPS * SSD_STATE
LRU_WIDTH = 1024
LRU_BLOCKS = 16
LRU_BLOCK_DIM = LRU_WIDTH // LRU_BLOCKS
LRU_C = 8.0
CONV_W = 5
MOE_GROUPS = 4
EXPERTS_PER_GROUP = 4
N_EXPERTS = MOE_GROUPS * EXPERTS_PER_GROUP
D_EXPERT = 256

LANES = 128
SUBLANES = 8
LRU_SLAB = 256
VMEM_LIMIT = 56 * 1024 * 1024
ROW_TILE = 256
MOE_TILE_PREFS = (768, 512, 256)
ATT_TQ = 256
ATT_TK_PREFS = (2816, 1408, 768, 256)


def _params(sem):
    return pltpu.CompilerParams(dimension_semantics=sem, vmem_limit_bytes=VMEM_LIMIT)


def _pick_tile(n, prefs):
    for t in prefs:
        if n % t == 0:
            return t
    raise ValueError(f"no tile for {n}")


def _sigmoid(x):
    return 0.5 * jnp.tanh(0.5 * x) + 0.5


def _softplus(x):
    return jnp.maximum(x, 0.0) + jnp.log1p(jnp.exp(-jnp.abs(x)))


def _split_bf16(v, n):
    parts = []
    r = v
    for _ in range(n):
        p = r.astype(BF16)
        parts.append(p)
        r = r - p.astype(F32)
    return parts


def _dot_nt(a, b):
    return lax.dot_general(a, b, (((1,), (1,)), ((), ())), preferred_element_type=F32)


def _rms(x):
    return x * lax.rsqrt(jnp.mean(x * x, axis=-1, keepdims=True) + EPS)


def _row_select(row0, tm, ctx_len, mod_ref, k):
    if ctx_len == 0:
        return mod_ref[0, 1, k:k + 1, :]
    rows = row0 + lax.broadcasted_iota(jnp.int32, (tm, 1), 0)
    return jnp.where(rows < ctx_len, mod_ref[0, 0, k:k + 1, :], mod_ref[0, 1, k:k + 1, :])


def _rms_mod(x, g, row0, ctx_len, mod_ref, k_shift, k_scale):
    tm = x.shape[0]
    sh = _row_select(row0, tm, ctx_len, mod_ref, k_shift)
    sc = _row_select(row0, tm, ctx_len, mod_ref, k_scale)
    return _rms(x) * g * (1.0 + sc) + sh


def _scan_order(s, n_ctx, n_all):
    return jnp.where(s < n_ctx, n_ctx - 1 - s, n_all + n_ctx - 1 - s)


def _row_specs(tm, d, ct, split):
    if split:
        return [pl.BlockSpec((1, tm, d), lambda b, i: (b, jnp.minimum(i, ct - 1), 0)),
                pl.BlockSpec((1, tm, d), lambda b, i: (b, jnp.maximum(i - ct, 0), 0))]
    return [pl.BlockSpec((1, tm, d), lambda b, i: (b, i, 0))]


def _load_rows(refs, i, ct):
    if len(refs) == 2:
        return jnp.where(i < ct, refs[0][0], refs[1][0])
    return refs[0][0]


def _mod_kernel(ct_ref, w_ref, b_ref, o_ref, *, rows):
    w = w_ref[0]
    outs = []
    for r in range(rows):
        c = ct_ref[:, r:r + 1]
        act = c * _sigmoid(c)
        outs.append(jnp.sum(w * act, axis=0, keepdims=True) + b_ref[0])
    o_ref[0] = jnp.concatenate(outs, axis=0)


def _modulation(c, c_ctx, w_mod, b_mod):
    depth, d, n = w_mod.shape
    rows = c.shape[0] + 1
    ct = jnp.concatenate([c_ctx[None, :], c], axis=0).T
    tn = 1536
    return pl.pallas_call(
        functools.partial(_mod_kernel, rows=rows),
        grid=(depth, n // tn),
        in_specs=[pl.BlockSpec((d, rows), lambda i, j: (0, 0)),
                  pl.BlockSpec((1, d, tn), lambda i, j: (i, 0, j)),
                  pl.BlockSpec((1, 1, tn), lambda i, j: (i, 0, j))],
        out_specs=pl.BlockSpec((1, rows, tn), lambda i, j: (i, 0, j)),
        out_shape=jax.ShapeDtypeStruct((depth, rows, n), F32),
        compiler_params=_params(("arbitrary", "arbitrary")),
        name="modulation",
    )(ct, w_mod, b_mod.reshape(depth, 1, n))


def _inproj_kernel(*refs, tm, ctx_len, plain, q0, qk_norm, n_src):
    x_refs = refs[:n_src]
    g_ref, mod_ref, w_ref, qg_ref, kg_ref, cc_ref, ss_ref = refs[n_src:n_src + 7]
    out_refs = refs[n_src + 7:]
    i = pl.program_id(1)
    x = _load_rows(x_refs, i, ctx_len // tm)
    h = _rms_mod(x, g_ref[...], i * tm, ctx_len, mod_ref, 0, 1)
    res = jnp.dot(h.astype(BF16), w_ref[...], preferred_element_type=F32)
    for ref, (c0, cw) in zip(out_refs[:len(plain)], plain):
        ref[0] = res[:, c0:c0 + cw]
    q_ref, k_ref, v_ref = out_refs[len(plain):]
    cc = cc_ref[...]
    ss = ss_ref[...]

    def head(t, gain, scale):
        if qk_norm:
            t = _rms(t) * gain
        t = t * cc + pltpu.roll(t, HEAD_DIM // 2, 1) * ss
        return (t * scale).astype(BF16)

    for hh in range(ATT_HEADS):
        c0 = q0 + hh * HEAD_DIM
        q_ref[0, :, hh * HEAD_DIM:(hh + 1) * HEAD_DIM] = head(res[:, c0:c0 + HEAD_DIM], qg_ref[...],
                                                              ATT_SCALE * LOG2E)
    k0 = q0 + ATT_WIDTH
    for hh in range(ATT_KV_HEADS):
        c0 = k0 + hh * HEAD_DIM
        k_ref[0, :, hh * HEAD_DIM:(hh + 1) * HEAD_DIM] = head(res[:, c0:c0 + HEAD_DIM], kg_ref[...], 1.0)
    v0 = k0 + KV_WIDTH
    v_ref[0] = res[:, v0:v0 + KV_WIDTH].astype(BF16)


def _inproj(srcs, lc, g, mod, w, qg, kg, cc, ss, *, ctx_len, plain, q0, qk_norm, tm=ROW_TILE):
    bsz, _, d = srcs[0].shape
    n = w.shape[1]
    row = lambda b, i: (b, i, 0)
    const = lambda b, i: (0, 0)
    out_shape = [jax.ShapeDtypeStruct((bsz, lc, cw), F32) for _, cw in plain]
    out_specs = [pl.BlockSpec((1, tm, cw), row) for _, cw in plain]
    for cw in (ATT_WIDTH, KV_WIDTH, KV_WIDTH):
        out_shape.append(jax.ShapeDtypeStruct((bsz, lc, cw), BF16))
        out_specs.append(pl.BlockSpec((1, tm, cw), row))
    return pl.pallas_call(
        functools.partial(_inproj_kernel, tm=tm, ctx_len=ctx_len, plain=plain, q0=q0, qk_norm=qk_norm,
                          n_src=len(srcs)),
        grid=(bsz, lc // tm),
        in_specs=_row_specs(tm, d, ctx_len // tm, len(srcs) == 2) + [
            pl.BlockSpec((1, d), const),
            pl.BlockSpec((1, 2, 6, d), lambda b, i: (b, 0, 0, 0)),
            pl.BlockSpec((d, n), const),
            pl.BlockSpec((1, HEAD_DIM), const),
            pl.BlockSpec((1, HEAD_DIM), const),
            pl.BlockSpec((tm, HEAD_DIM), lambda b, i: (i, 0)),
            pl.BlockSpec((tm, HEAD_DIM), lambda b, i: (i, 0))],
        out_specs=out_specs,
        out_shape=out_shape,
        compiler_params=_params(("parallel", "arbitrary")),
        name="inproj",
    )(*srcs, g, mod, w, qg, kg, cc, ss)


def _conv_rows(prev, cur, nxt, w_ref, b_ref, seg_start, seg_end, silu):
    tm, c = cur.shape
    prev = jnp.where(seg_start, 0.0, prev)
    nxt = jnp.where(seg_end, 0.0, nxt)
    ng = tm // SUBLANES
    ext = jnp.concatenate([prev, cur, nxt], axis=0).reshape(ng + 2, SUBLANES, c)
    pos8 = lax.broadcasted_iota(jnp.int32, (1, SUBLANES, 1), 1)
    pad = CONV_W // 2
    acc = b_ref[...] + cur * w_ref[pad:pad + 1, :]
    for k in range(CONV_W):
        d = k - pad
        if d == 0:
            continue
        r = pltpu.roll(ext, (-d) % SUBLANES, 1)
        if d < 0:
            tap = jnp.where(pos8 >= -d, r[1:ng + 1], r[0:ng])
        else:
            tap = jnp.where(pos8 < SUBLANES - d, r[1:ng + 1], r[2:ng + 2])
        acc = acc + tap.reshape(tm, c) * w_ref[k:k + 1, :]
    if silu:
        acc = acc * _sigmoid(acc)
    return acc


def _dwconv_kernel(prev_ref, cur_ref, next_ref, w_ref, b_ref, o_ref, *, nt, ct, silu):
    i = pl.program_id(1)
    seg_start = jnp.logical_or(i == 0, i == ct)
    seg_end = jnp.logical_or(i == ct - 1, i == nt - 1)
    o_ref[0] = _conv_rows(prev_ref[0], cur_ref[0], next_ref[0], w_ref, b_ref, seg_start, seg_end, silu)


def _dwconv(u, w, b, *, ctx_len, silu, tm=ROW_TILE):
    bsz, lc, c = u.shape
    nt = lc // tm
    hb = tm // SUBLANES
    nhb = lc // SUBLANES
    return pl.pallas_call(
        functools.partial(_dwconv_kernel, nt=nt, ct=ctx_len // tm, silu=silu),
        grid=(bsz, nt),
        in_specs=[pl.BlockSpec((1, SUBLANES, c), lambda b_, i: (b_, jnp.maximum(i * hb - 1, 0), 0)),
                  pl.BlockSpec((1, tm, c), lambda b_, i: (b_, i, 0)),
                  pl.BlockSpec((1, SUBLANES, c), lambda b_, i: (b_, jnp.minimum((i + 1) * hb, nhb - 1), 0)),
                  pl.BlockSpec((CONV_W, c), lambda b_, i: (0, 0)),
                  pl.BlockSpec((1, c), lambda b_, i: (0, 0))],
        out_specs=pl.BlockSpec((1, tm, c), lambda b_, i: (b_, i, 0)),
        out_shape=jax.ShapeDtypeStruct((bsz, lc, c), F32),
        compiler_params=_params(("parallel", "arbitrary")),
        name="dwconv",
    )(u, u, u, w, b.reshape(1, c))


def _expand_heads(v, e_ref):
    hi, lo = _split_bf16(v, 2)
    e = e_ref[...]
    return (jnp.dot(hi, e, preferred_element_type=F32) + jnp.dot(lo, e, preferred_element_type=F32))


def _ssd_chunk(xbc, dt_raw, dtb, alog, e_ref, h_prev, reverse):
    q = SSD_CHUNK
    heads_per_group = SSD_HEADS // SSD_GROUPS
    x = xbc[:, :SSD_WIDTH]
    bm = xbc[:, SSD_WIDTH:SSD_WIDTH + LANES]
    cm = xbc[:, SSD_WIDTH + LANES:]
    dt = _softplus(dt_raw + dtb)
    da = dt * (-jnp.exp(alog) * LOG2E)
    ii = lax.broadcasted_iota(jnp.int32, (q, q), 0)
    jj = lax.broadcasted_iota(jnp.int32, (q, q), 1)
    causal = (jj >= ii) if reverse else (jj <= ii)
    mask_b = jnp.where(causal, 1.0, 0.0).astype(BF16)
    da_t = da.T
    dt_t = dt.T
    acs = sum(jnp.dot(mask_b, p, preferred_element_type=F32) for p in _split_bf16(da, 3))
    acs_t = sum(_dot_nt(p, mask_b) for p in _split_bf16(da_t, 3))
    total = jnp.sum(da, axis=0, keepdims=True)
    eacs = jnp.exp2(acs)

    bm_b = bm.astype(BF16)
    lane = lax.broadcasted_iota(jnp.int32, (1, LANES), 1)
    in_group = [jnp.logical_and(lane >= g * SSD_STATE, lane < (g + 1) * SSD_STATE) for g in range(SSD_GROUPS)]
    cb = [_dot_nt(jnp.where(in_group[g], cm, 0.0).astype(BF16), bm_b) for g in range(SSD_GROUPS)]

    x_b = x.astype(BF16)
    h_b = h_prev.astype(BF16)
    pairs = []
    for hp in range(SSD_HEADS // 2):
        rhs = jnp.concatenate([x_b[:, hp * LANES:(hp + 1) * LANES], h_b[:, hp * LANES:(hp + 1) * LANES]], axis=0)
        outs = []
        for t in range(2):
            h = 2 * hp + t
            seg = acs[:, h:h + 1] - acs_t[h:h + 1, :]
            dec = jnp.exp2(jnp.where(causal, seg, -jnp.inf))
            w = cb[h // heads_per_group] * dec * dt_t[h:h + 1, :]
            lhs = jnp.concatenate([w.astype(BF16), (cm * eacs[:, h:h + 1]).astype(BF16)], axis=1)
            outs.append(jnp.dot(lhs, rhs, preferred_element_type=F32))
        pairs.append(jnp.where(lane < SSD_HEAD_DIM, outs[0], outs[1]))
    y = jnp.concatenate(pairs, axis=1)

    to_end = jnp.exp2(total - acs) * dt
    xw = (x * _expand_heads(to_end, e_ref)).astype(BF16)
    st = jnp.dot(bm.T.astype(BF16), xw, preferred_element_type=F32)
    row_g0 = lax.broadcasted_iota(jnp.int32, st.shape, 0) < SSD_STATE
    col_g0 = lax.broadcasted_iota(jnp.int32, st.shape, 1) < heads_per_group * SSD_HEAD_DIM
    chunk_decay = _expand_heads(jnp.broadcast_to(jnp.exp2(total), (SUBLANES, LANES)), e_ref)[0:1]
    return y, h_prev * chunk_decay + jnp.where(row_g0 == col_g0, st, 0.0)


def _ssd_bwd_kernel(prev_ref, cur_ref, next_ref, cw_ref, cb_ref, dt_ref, dtb_ref, alog_ref, e_ref,
                    yb_ref, xc_ref, hst_ref, *, nbx, nb, bsz):
    s = pl.program_id(0)
    blk = _scan_order(s, nbx, nb)

    @pl.when(s == 0)
    def _():
        hst_ref[...] = jnp.zeros_like(hst_ref)

    seg_start = jnp.logical_or(blk == 0, blk == nbx)
    seg_end = jnp.logical_or(blk == nbx - 1, blk == nb - 1)
    xbc, h = [], []
    for bb in range(bsz):
        xbc.append(_conv_rows(prev_ref[bb], cur_ref[bb], next_ref[bb], cw_ref, cb_ref, seg_start, seg_end, True))
        xc_ref[bb] = xbc[bb]
        h.append(hst_ref[bb])
    for sub in reversed(range(cur_ref.shape[1] // SSD_CHUNK)):
        rows = slice(sub * SSD_CHUNK, (sub + 1) * SSD_CHUNK)
        for bb in range(bsz):
            y, h[bb] = _ssd_chunk(xbc[bb][rows], dt_ref[bb, rows, :], dtb_ref[...], alog_ref[...], e_ref, h[bb], True)
            yb_ref[bb, rows, :] = y
    for bb in range(bsz):
        hst_ref[bb] = h[bb]


def _ssd_fwd_kernel(xc_ref, dt_ref, dtb_ref, alog_ref, e_ref, yb_ref, z_ref, dsk_ref, ng_ref,
                    o_ref, hst_ref, *, bsz):
    @pl.when(pl.program_id(0) == 0)
    def _():
        hst_ref[...] = jnp.zeros_like(hst_ref)

    h = [hst_ref[bb] for bb in range(bsz)]
    for sub in range(xc_ref.shape[1] // SSD_CHUNK):
        rows = slice(sub * SSD_CHUNK, (sub + 1) * SSD_CHUNK)
        for bb in range(bsz):
            xbc = xc_ref[bb, rows, :]
            y, h[bb] = _ssd_chunk(xbc, dt_ref[bb, rows, :], dtb_ref[...], alog_ref[...], e_ref, h[bb], False)
            z = z_ref[bb, rows, :]
            y = (y + yb_ref[bb, rows, :] + dsk_ref[...] * xbc[:, :SSD_WIDTH]) * (z * _sigmoid(z))
            o_ref[bb, rows, :] = (_rms(y) * ng_ref[...]).astype(BF16)
    for bb in range(bsz):
        hst_ref[bb] = h[bb]


def _ssd(xbc_raw, dt, z, conv_w, conv_b, dt_bias, a_log, d_skip, norm_g, expand, *, ctx_len, q=ROW_TILE):
    bsz, lc, cch = xbc_raw.shape
    w = SSD_WIDTH
    nb = lc // q
    nbx = ctx_len // q
    hb = q // SUBLANES
    nhb = lc // SUBLANES
    pad = lambda t: jnp.pad(t, (0, LANES - SSD_HEADS)).reshape(1, LANES)
    const = lambda s: (0, 0)
    rev = lambda s: _scan_order(s, nbx, nb)
    state = [pltpu.VMEM((bsz, SSD_GROUPS * SSD_STATE, w), F32)]
    yb, xc = pl.pallas_call(
        functools.partial(_ssd_bwd_kernel, nbx=nbx, nb=nb, bsz=bsz),
        grid=(nb,),
        in_specs=[pl.BlockSpec((bsz, SUBLANES, cch), lambda s: (0, jnp.maximum(rev(s) * hb - 1, 0), 0)),
                  pl.BlockSpec((bsz, q, cch), lambda s: (0, rev(s), 0)),
                  pl.BlockSpec((bsz, SUBLANES, cch), lambda s: (0, jnp.minimum((rev(s) + 1) * hb, nhb - 1), 0)),
                  pl.BlockSpec((CONV_W, cch), const),
                  pl.BlockSpec((1, cch), const),
                  pl.BlockSpec((bsz, q, LANES), lambda s: (0, rev(s), 1)),
                  pl.BlockSpec((1, LANES), const),
                  pl.BlockSpec((1, LANES), const),
                  pl.BlockSpec((LANES, w), const)],
        out_specs=[pl.BlockSpec((bsz, q, w), lambda s: (0, rev(s), 0)),
                   pl.BlockSpec((bsz, q, cch), lambda s: (0, rev(s), 0))],
        out_shape=[jax.ShapeDtypeStruct((bsz, lc, w), F32), jax.ShapeDtypeStruct((bsz, lc, cch), F32)],
        scratch_shapes=state,
        compiler_params=_params(("arbitrary",)),
        name="ssd_bwd",
    )(xbc_raw, xbc_raw, xbc_raw, conv_w, conv_b.reshape(1, cch), dt, pad(dt_bias[1]), pad(a_log[1]), expand)
    row = lambda cols: pl.BlockSpec((bsz, q, cols), lambda s: (0, s, 0))
    return pl.pallas_call(
        functools.partial(_ssd_fwd_kernel, bsz=bsz),
        grid=(nb,),
        in_specs=[row(cch), row(LANES), pl.BlockSpec((1, LANES), const), pl.BlockSpec((1, LANES), const),
                  pl.BlockSpec((LANES, w), const), row(w), row(w), pl.BlockSpec((1, w), const),
                  pl.BlockSpec((1, w), const)],
        out_specs=row(w),
        out_shape=jax.ShapeDtypeStruct((bsz, lc, w), BF16),
        scratch_shapes=state,
        compiler_params=_params(("arbitrary",)),
        name="ssd_fwd",
    )(xc, dt, pad(dt_bias[0]), pad(a_log[0]), expand, yb, z,
      jnp.repeat(d_skip, SSD_HEAD_DIM).reshape(1, w), norm_g.reshape(1, w))


def _stack_heads(q):
    return jnp.concatenate([q[:, r * HEAD_DIM:(r + 1) * HEAD_DIM] for r in range(q.shape[1] // HEAD_DIM)], axis=0)


def _lane_parts(s):
    return [s[:, c * LANES:(c + 1) * LANES] for c in range(s.shape[1] // LANES)]


def _softmax_parts(parts, sink=None):
    pm = parts[0]
    for t in parts[1:]:
        pm = jnp.maximum(pm, t)
    m = jnp.max(pm, axis=-1, keepdims=True)
    if sink is not None:
        m = jnp.maximum(m, sink)
    mb = jnp.broadcast_to(m, pm.shape)
    ps = [jnp.exp2(t - mb) for t in parts]
    lsum = ps[0]
    for t in ps[1:]:
        lsum = lsum + t
    l = jnp.sum(lsum, axis=-1, keepdims=True)
    if sink is not None:
        l = l + jnp.exp2(sink - m)
    return jnp.concatenate([t.astype(BF16) for t in ps], axis=1), l


def _store_heads(o_ref, o, tq, col0=0):
    for r in range(GQA_REP):
        c0 = col0 + r * HEAD_DIM
        o_ref[0, :, c0:c0 + HEAD_DIM] = o[r * tq:(r + 1) * tq].astype(BF16)


def _attn_dense_kernel(q_ref, k_ref, v_ref, o_ref, m_scr, l_scr, acc_scr, *, tq, tk, ctx_len, lk):
    i = pl.program_id(1)
    gw = GQA_REP * HEAD_DIM
    groups = range(ATT_KV_HEADS)
    q4 = [_stack_heads(q_ref[0, :, g * gw:(g + 1) * gw]) for g in groups]
    kcol = [slice(g * HEAD_DIM, (g + 1) * HEAD_DIM) for g in groups]

    @pl.when(i < ctx_len // tq)
    def _():
        for g in groups:
            p, l = _softmax_parts(_lane_parts(_dot_nt(q4[g], k_ref[0, 0:ctx_len, kcol[g]])))
            o = jnp.dot(p, v_ref[0, 0:ctx_len, kcol[g]], preferred_element_type=F32) / l
            _store_heads(o_ref, o, tq, col0=g * gw)

    @pl.when(i >= ctx_len // tq)
    def _():
        m_scr[...] = jnp.full_like(m_scr, -jnp.inf)
        l_scr[...] = jnp.zeros_like(l_scr)
        acc_scr[...] = jnp.zeros_like(acc_scr)

        def body(j, carry):
            off = pl.multiple_of(j * tk, tk)
            for g in groups:
                parts = _lane_parts(_dot_nt(q4[g], k_ref[0, pl.ds(off, tk), kcol[g]]))
                pm = parts[0]
                for t in parts[1:]:
                    pm = jnp.maximum(pm, t)
                m_prev = m_scr[g]
                m_new = jnp.maximum(m_prev, jnp.max(pm, axis=-1, keepdims=True))
                alpha = jnp.exp2(m_prev - m_new)
                ps = [jnp.exp2(t - m_new) for t in parts]
                lsum = ps[0]
                for t in ps[1:]:
                    lsum = lsum + t
                l_scr[g] = alpha * l_scr[g] + lsum
                p = jnp.concatenate([t.astype(BF16) for t in ps], axis=1)
                acc_scr[g] = alpha * acc_scr[g] + jnp.dot(p, v_ref[0, pl.ds(off, tk), kcol[g]],
                                                          preferred_element_type=F32)
                m_scr[g] = m_new
            return carry

        lax.fori_loop(0, lk // tk, body, 0)
        for g in groups:
            _store_heads(o_ref, acc_scr[g] / jnp.sum(l_scr[g], axis=-1, keepdims=True), tq, col0=g * gw)


def _attn_dense(q, k, v, *, ctx_len, tq, tk):
    bsz, lc, _ = q.shape
    rows = GQA_REP * tq
    stat = pltpu.VMEM((ATT_KV_HEADS, rows, LANES), F32)
    return pl.pallas_call(
        functools.partial(_attn_dense_kernel, tq=tq, tk=tk, ctx_len=ctx_len, lk=lc),
        grid=(bsz, lc // tq),
        in_specs=[pl.BlockSpec((1, tq, ATT_WIDTH), lambda b, i: (b, i, 0)),
                  pl.BlockSpec((1, lc, KV_WIDTH), lambda b, i: (b, 0, 0)),
                  pl.BlockSpec((1, lc, KV_WIDTH), lambda b, i: (b, 0, 0))],
        out_specs=pl.BlockSpec((1, tq, ATT_WIDTH), lambda b, i: (b, i, 0)),
        out_shape=jax.ShapeDtypeStruct((bsz, lc, ATT_WIDTH), BF16),
        scratch_shapes=[stat, stat, stat],
        compiler_params=_params(("parallel", "arbitrary")),
        name="attn_dense",
    )(q, k, v)


def _attn_win_kernel(q_ref, k_ref, v_ref, sink_ref, bias_ref, o_ref, *, ctx_len, lat_len, tq):
    n = pl.program_id(1)
    band = tq + 2 * WINDOW
    il = n - ctx_len // tq
    lo = jnp.clip(il * tq - WINDOW, 0, lat_len - band)
    start = pl.multiple_of(ctx_len + lo, WINDOW)
    bias = bias_ref[0]
    gw = GQA_REP * HEAD_DIM
    for g in range(ATT_KV_HEADS):
        kc0 = g * HEAD_DIM
        q4 = _stack_heads(q_ref[0, :, g * gw:(g + 1) * gw])
        s_loc = _dot_nt(q4, k_ref[0, pl.ds(start, band), kc0:kc0 + HEAD_DIM]) + bias
        s_ctx = _dot_nt(q4, k_ref[0, 0:ctx_len, kc0:kc0 + HEAD_DIM])
        p, l = _softmax_parts(_lane_parts(s_loc) + _lane_parts(s_ctx), sink_ref[g])
        o = (jnp.dot(p[:, :band], v_ref[0, pl.ds(start, band), kc0:kc0 + HEAD_DIM], preferred_element_type=F32)
             + jnp.dot(p[:, band:], v_ref[0, 0:ctx_len, kc0:kc0 + HEAD_DIM], preferred_element_type=F32)) / l
        _store_heads(o_ref, o, tq, col0=g * gw)


def _attn_window(q, k, v, sink, *, ctx_len, tq):
    bsz, lc, _ = q.shape
    lat_len = lc - ctx_len
    rows = GQA_REP * tq
    assert tq & (tq - 1) == 0 and ctx_len % tq == 0 and lat_len >= tq + 2 * WINDOW
    sink_col = jnp.repeat(sink.reshape(ATT_KV_HEADS, GQA_REP) * LOG2E, tq, axis=1).reshape(ATT_KV_HEADS, rows, 1)
    band = tq + 2 * WINDOW
    ct = ctx_len // tq
    nlt = lat_len // tq
    r = np.arange(tq)[:, None]
    c = np.arange(band)[None, :]
    masks = [np.abs(c - off - r) <= WINDOW for off in (0, WINDOW, band - tq)] + [np.zeros((tq, band), bool)]
    bias = jnp.asarray(np.tile(np.where(np.stack(masks), 0.0, -np.inf), (1, GQA_REP, 1)), F32)

    def variant(n):
        il = n - ct
        return jnp.where(il < 0, 3, jnp.where(il == 0, 0, jnp.where(il == nlt - 1, 2, 1)))

    return pl.pallas_call(
        functools.partial(_attn_win_kernel, ctx_len=ctx_len, lat_len=lat_len, tq=tq),
        grid=(bsz, lc // tq),
        in_specs=[pl.BlockSpec((1, tq, ATT_WIDTH), lambda b, n: (b, n, 0)),
                  pl.BlockSpec((1, lc, KV_WIDTH), lambda b, n: (b, 0, 0)),
                  pl.BlockSpec((1, lc, KV_WIDTH), lambda b, n: (b, 0, 0)),
                  pl.BlockSpec((ATT_KV_HEADS, rows, 1), lambda b, n: (0, 0, 0)),
                  pl.BlockSpec((1, rows, band), lambda b, n: (variant(n), 0, 0))],
        out_specs=pl.BlockSpec((1, tq, ATT_WIDTH), lambda b, n: (b, n, 0)),
        out_shape=jax.ShapeDtypeStruct((bsz, lc, ATT_WIDTH), BF16),
        compiler_params=_params(("parallel", "arbitrary")),
        name="attn_window",
    )(q, k, v, sink_col, bias)


def _lru_scan_rows(a, b, carry, reverse):
    t, w = a.shape
    ngroups = t // SUBLANES
    a = a.reshape(ngroups, SUBLANES, w)
    b = b.reshape(ngroups, SUBLANES, w)
    pos8 = lax.broadcasted_iota(jnp.int32, (1, SUBLANES, 1), 1)
    k = 1
    while k < SUBLANES:
        if reverse:
            keep = pos8 < SUBLANES - k
            sh = SUBLANES - k
        else:
            keep = pos8 >= k
            sh = k
        a_sh = jnp.where(keep, pltpu.roll(a, sh, 1), 1.0)
        b_sh = jnp.where(keep, pltpu.roll(b, sh, 1), 0.0)
        b = b + a * b_sh
        a = a * a_sh
        k *= 2
    a = a.reshape(t, w)
    b = b.reshape(t, w)
    hs = [None] * ngroups
    for g in (range(ngroups - 1, -1, -1) if reverse else range(ngroups)):
        sl = slice(g * SUBLANES, (g + 1) * SUBLANES)
        hg = a[sl] * carry + b[sl]
        carry = hg[0:1] if reverse else hg[SUBLANES - 1:SUBLANES]
        hs[g] = hg
    return jnp.concatenate(hs, axis=0), carry


def _lru_kernel(*refs, reverse):
    x_ref, wa_ref, wx_ref, ba_ref, bx_ref, lam_ref = refs[:6]
    if reverse:
        o_ref, carry_ref = refs[6:]
    else:
        hb_ref, gate_ref, o_ref, carry_ref = refs[6:]
    s = pl.program_id(1)

    @pl.when(s == 0)
    def _():
        carry_ref[...] = jnp.zeros_like(carry_ref)

    x = x_ref[0]
    x_b = x.astype(BF16)
    nslab = LRU_WIDTH // LRU_SLAB

    def gate(w_ref, b_ref):
        pre = jnp.concatenate(
            [jnp.dot(x_b[:, c * LRU_SLAB:(c + 1) * LRU_SLAB], w_ref[0, c], preferred_element_type=F32)
             for c in range(nslab)], axis=1)
        return _sigmoid(pre + b_ref[0])

    r = gate(wa_ref, ba_ref)
    ig = gate(wx_ref, bx_ref)
    log2_base = (-LRU_C * LOG2E) * _softplus(-lam_ref[0])
    a = jnp.exp2(r * log2_base)
    y = 1.0 - a * a
    b = jnp.where(y > 0.0, y * lax.rsqrt(y), 0.0) * (ig * x)
    h, carry = _lru_scan_rows(a, b, carry_ref[...], reverse)
    carry_ref[...] = carry
    if reverse:
        o_ref[0] = h
    else:
        g = gate_ref[0]
        gelu = 0.5 * g * (1.0 + jnp.tanh(math.sqrt(2.0 / math.pi) * (g + 0.044715 * (g * g * g))))
        o_ref[0] = (gelu * (h + hb_ref[0])).astype(BF16)


def _lru(xr, gate, wa, wx, ba, bx, lam, *, ctx_len, t=ROW_TILE):
    bsz, lc, w = xr.shape
    nt = lc // t
    ct = ctx_len // t
    nslab = w // LRU_SLAB
    vec = lambda v: v.reshape(2, 1, w)

    def call(reverse, extra_in, out_dtype):
        tile = (lambda s: _scan_order(s, ct, nt)) if reverse else (lambda s: s)
        d = 1 if reverse else 0
        row = pl.BlockSpec((1, t, w), lambda b, s: (b, tile(s), 0))
        return pl.pallas_call(
            functools.partial(_lru_kernel, reverse=reverse),
            grid=(bsz, nt),
            in_specs=[row,
                      pl.BlockSpec((1, nslab, LRU_SLAB, LRU_SLAB), lambda b, s: (d, 0, 0, 0)),
                      pl.BlockSpec((1, nslab, LRU_SLAB, LRU_SLAB), lambda b, s: (d, 0, 0, 0)),
                      pl.BlockSpec((1, 1, w), lambda b, s: (d, 0, 0)),
                      pl.BlockSpec((1, 1, w), lambda b, s: (d, 0, 0)),
                      pl.BlockSpec((1, 1, w), lambda b, s: (d, 0, 0))] + [row] * len(extra_in),
            out_specs=row,
            out_shape=jax.ShapeDtypeStruct((bsz, lc, w), out_dtype),
            scratch_shapes=[pltpu.VMEM((1, w), F32)],
            compiler_params=_params(("parallel", "arbitrary")),
            name="lru_bwd" if reverse else "lru_fwd",
        )(xr, wa, wx, vec(ba), vec(bx), vec(lam), *extra_in)

    hb = call(True, (), F32)
    return call(False, (hb, gate), BF16)


def _lru_slabs(w):
    per = LRU_SLAB // LRU_BLOCK_DIM
    nslab = LRU_BLOCKS // per
    w = w.reshape(2, nslab, per, LRU_BLOCK_DIM, LRU_BLOCK_DIM)
    eye = jnp.eye(per, dtype=w.dtype)
    full = jnp.einsum('dspij,pq->dspiqj', w, eye)
    return full.reshape(2, nslab, LRU_SLAB, LRU_SLAB).astype(BF16)


def _outproj_kernel(*refs, tm, ctx_len, t0, n_src):
    ya_ref, yb_ref = refs[:2]
    x_refs = refs[2:2 + n_src]
    mod_ref, w_ref, o_ref = refs[2 + n_src:]
    i = pl.program_id(1) + t0
    half = ya_ref.shape[2]
    acc = (jnp.dot(ya_ref[0], w_ref[0:half, :], preferred_element_type=F32)
           + jnp.dot(yb_ref[0], w_ref[half:, :], preferred_element_type=F32))
    gate = _row_select(i * tm, tm, ctx_len, mod_ref, 2)
    o_ref[0] = _load_rows(x_refs, i, ctx_len // tm) + gate * acc


def _outproj(ya, yb, srcs, mod, w, *, ctx_len, latent_only, tm=ROW_TILE):
    bsz, lc, half = ya.shape
    d = w.shape[1]
    ct = ctx_len // tm
    t0 = ct if latent_only else 0
    row = lambda b, i: (b, i + t0, 0)
    if len(srcs) == 2:
        x_specs = [pl.BlockSpec((1, tm, d), lambda b, i: (b, jnp.minimum(i + t0, ct - 1), 0)),
                   pl.BlockSpec((1, tm, d), lambda b, i: (b, jnp.maximum(i + t0 - ct, 0), 0))]
    else:
        x_specs = [pl.BlockSpec((1, tm, d), row)]
    return pl.pallas_call(
        functools.partial(_outproj_kernel, tm=tm, ctx_len=ctx_len, t0=t0, n_src=len(srcs)),
        grid=(bsz, lc // tm - t0),
        in_specs=[pl.BlockSpec((1, tm, half), row), pl.BlockSpec((1, tm, half), row)] + x_specs + [
            pl.BlockSpec((1, 2, 6, d), lambda b, i: (b, 0, 0, 0)),
            pl.BlockSpec((2 * half, d), lambda b, i: (0, 0))],
        out_specs=pl.BlockSpec((1, tm, d), lambda b, i: (b, i, 0)),
        out_shape=jax.ShapeDtypeStruct((bsz, lc - t0 * tm, d), F32),
        compiler_params=_params(("parallel", "arbitrary")),
        name="outproj",
    )(ya, yb, *srcs, mod, w)


def _first_max(vals):
    m = vals[0]
    for v in vals[1:]:
        m = jnp.maximum(m, v)
    hot = []
    taken = None
    for v in vals:
        hit = v == m
        if taken is None:
            hot.append(hit)
            taken = hit
        else:
            hot.append(jnp.logical_and(hit, jnp.logical_not(taken)))
            taken = jnp.logical_or(taken, hit)
    return m, hot


def _moe_kernel(*refs, tm, ctx_len, wins, final):
    x_ref, g_ref, mod_ref, wr_ref, br_ref, w1_ref, w3_ref, w2_ref = refs[:8]
    if final:
        gf_ref = refs[8]
    o_ref, hn_scr, gate_scr, rank_row_scr, rank_col_scr, cnt_ref, acc_scr = refs[-7:]
    i = pl.program_id(1)
    e = pl.program_id(2)

    @pl.when(e == 0)
    def _():
        hn = _rms_mod(x_ref[0], g_ref[...], i * tm, ctx_len, mod_ref, 3, 4)
        hn_scr[...] = hn.astype(BF16)
        h_hi, h_lo = _split_bf16(hn, 2)
        r_hi = jnp.dot(h_hi, wr_ref[...], preferred_element_type=F32)
        r_lo = jnp.dot(h_lo, wr_ref[...], preferred_element_type=F32)
        logits = (r_hi[:, :LANES] + r_hi[:, LANES:]) + (r_lo[:, :LANES] + r_lo[:, LANES:]) + br_ref[...]
        lt = jnp.concatenate([logits[c * LANES:(c + 1) * LANES, :].T for c in range(tm // LANES)], axis=1)
        gl = [lt[j:j + 1, :] for j in range(MOE_GROUPS)]
        gmax, ghot = _first_max(gl)
        gsum = sum(jnp.exp(v - gmax) for v in gl)
        p_grp = 1.0 / gsum
        elg = []
        for j in range(EXPERTS_PER_GROUP):
            v = jnp.zeros_like(gmax)
            for gi in range(MOE_GROUPS):
                c = MOE_GROUPS + gi * EXPERTS_PER_GROUP + j
                v = jnp.where(ghot[gi], lt[c:c + 1, :], v)
            elg.append(v)
        v1, hot1 = _first_max(elg)
        rest = [jnp.where(hh, -jnp.inf, v) for hh, v in zip(hot1, elg)]
        v2, hot2 = _first_max(rest)
        ex = jnp.exp(v2 - v1)
        w1 = p_grp / (1.0 + ex)
        w2 = p_grp * ex / (1.0 + ex)
        own = [jnp.where(hot1[j], w1, 0.0) + jnp.where(hot2[j], w2, 0.0) for j in range(EXPERTS_PER_GROUP)]
        pad_rows = jnp.zeros((LANES - MOE_GROUPS, tm), F32)
        g_t = jnp.concatenate([jnp.where(h_, 1.0, 0.0) for h_ in ghot] + [pad_rows], axis=0)
        t0 = lax.broadcasted_iota(jnp.int32, (tm, tm), 0)
        t1 = lax.broadcasted_iota(jnp.int32, (tm, tm), 1)
        before = jnp.where(t0 < t1, 1.0, 0.0).astype(BF16)
        rank_t = jnp.where(g_t > 0.0, jnp.dot(g_t.astype(BF16), before, preferred_element_type=F32), -1.0)
        rank_row_scr[...] = rank_t[0:SUBLANES]
        for gi in range(MOE_GROUPS):
            cnt_ref[gi] = jnp.sum(g_t[gi:gi + 1, :]).astype(jnp.int32)
            slab_t = jnp.concatenate([jnp.where(ghot[gi], own[j], 0.0) for j in range(EXPERTS_PER_GROUP)]
                                     + [pad_rows], axis=0)
            for c in range(tm // LANES):
                rows_c = slice(c * LANES, (c + 1) * LANES)
                slab = slab_t[:, rows_c].T
                hi, lo = _split_bf16(slab, 2)
                gate_scr[gi, rows_c, 0:LANES] = hi
                gate_scr[gi, rows_c, LANES:] = lo
                if gi == 0:
                    rank_c = rank_t[:, rows_c].T
                    for gj in range(MOE_GROUPS):
                        rank_col_scr[gj, rows_c, :] = rank_c[:, gj:gj + 1]
        acc_scr[...] = jnp.zeros_like(acc_scr)

    cnt = cnt_ref[e]
    rank_row = rank_row_scr[pl.ds(e, 1), :]
    rank_col = rank_col_scr[e]

    def window(first, w_):
        base = first.astype(F32)
        slot_r = lax.broadcasted_iota(jnp.int32, (w_, 1), 0).astype(F32)
        slot_c = lax.broadcasted_iota(jnp.int32, (1, w_), 1).astype(F32)
        sel = jnp.where(rank_row - base == slot_r, 1.0, 0.0).astype(BF16)
        sel_t = jnp.where(rank_col - base == slot_c, 1.0, 0.0).astype(BF16)
        xs = jnp.dot(sel, hn_scr[...], preferred_element_type=F32).astype(BF16)
        gs = jnp.dot(sel, gate_scr[e], preferred_element_type=F32)
        gsel = gs[:, :LANES] + gs[:, LANES:]
        yw = None
        for kk in range(EXPERTS_PER_GROUP):
            a = jnp.dot(xs, w1_ref[kk], preferred_element_type=F32)
            u = jnp.dot(xs, w3_ref[kk], preferred_element_type=F32)
            hid = (a * _sigmoid(a)) * u * gsel[:, kk:kk + 1]
            part = jnp.dot(hid.astype(BF16), w2_ref[kk], preferred_element_type=F32)
            yw = part if yw is None else yw + part
        acc_scr[...] += jnp.dot(sel_t, yw.astype(BF16), preferred_element_type=F32)

    w_top = wins[-1]
    lower = 0
    for w_ in wins[:-1]:
        pl.when(jnp.logical_and(cnt > lower, cnt <= w_))(functools.partial(window, jnp.int32(0), w_))
        lower = w_

    @pl.when(cnt > lower)
    def _():
        def body(wi, carry):
            window(wi * w_top, w_top)
            return carry
        lax.fori_loop(0, (cnt + (w_top - 1)) // w_top, body, 0)

    @pl.when(e == MOE_GROUPS - 1)
    def _():
        gate = _row_select(i * tm, tm, ctx_len, mod_ref, 5)
        out = x_ref[0] + gate * acc_scr[...]
        if final:
            out = _rms(out) * gf_ref[...]
        o_ref[0] = out


def _moe(xx, g, mod, wr, br, w1, w3, w2, g_final, *, ctx_len):
    bsz, rows, d = xx.shape
    tm = _pick_tile(rows, MOE_TILE_PREFS)
    mid = -(-(tm // MOE_GROUPS) // LANES) * LANES
    wins = (mid, mid + LANES, mid + 2 * LANES)
    eb = EXPERTS_PER_GROUP
    row = lambda b, i, e: (b, i, 0)
    const = lambda b, i, e: (0, 0)
    final = g_final is not None
    return pl.pallas_call(
        functools.partial(_moe_kernel, tm=tm, ctx_len=ctx_len, wins=wins, final=final),
        grid=(bsz, rows // tm, MOE_GROUPS),
        in_specs=[pl.BlockSpec((1, tm, d), row),
                  pl.BlockSpec((1, d), const),
                  pl.BlockSpec((1, 2, 6, d), lambda b, i, e: (b, 0, 0, 0)),
                  pl.BlockSpec((d, 2 * LANES), const),
                  pl.BlockSpec((1, LANES), const),
                  pl.BlockSpec((eb, d, D_EXPERT), lambda b, i, e: (e, 0, 0)),
                  pl.BlockSpec((eb, d, D_EXPERT), lambda b, i, e: (e, 0, 0)),
                  pl.BlockSpec((eb, D_EXPERT, d), lambda b, i, e: (e, 0, 0))]
                 + ([pl.BlockSpec((1, d), const)] if final else []),
        out_specs=pl.BlockSpec((1, tm, d), row),
        out_shape=jax.ShapeDtypeStruct((bsz, rows, d), F32),
        scratch_shapes=[pltpu.VMEM((tm, d), BF16),
                        pltpu.VMEM((MOE_GROUPS, tm, 2 * LANES), BF16),
                        pltpu.VMEM((SUBLANES, tm), F32),
                        pltpu.VMEM((MOE_GROUPS, tm, 1), F32),
                        pltpu.SMEM((MOE_GROUPS,), jnp.int32),
                        pltpu.VMEM((tm, d), F32)],
        compiler_params=_params(("parallel", "arbitrary", "arbitrary")),
        name="moe",
    )(xx, g, mod, wr, br, w1, w3, w2, *((g_final.reshape(1, d),) if final else ()))


def _rope_tables(ctx_len, lat_len):
    rows = lat_len // GRID_W
    row = np.repeat(np.arange(rows), GRID_W).astype(np.float64)
    col = np.tile(np.arange(GRID_W), rows).astype(np.float64)
    n_freq = HEAD_DIM // 4
    inv = ROPE_THETA ** (-np.arange(n_freq, dtype=np.float64) / n_freq)
    ang = np.concatenate([row[:, None] * inv, col[:, None] * inv], axis=-1)
    cos, sin = np.cos(ang), np.sin(ang)
    cc = np.concatenate([np.ones((ctx_len, HEAD_DIM)), np.concatenate([cos, cos], axis=-1)], axis=0)
    ss = np.concatenate([np.zeros((ctx_len, HEAD_DIM)), np.concatenate([-sin, sin], axis=-1)], axis=0)
    return jnp.asarray(cc, F32), jnp.asarray(ss, F32)


def kernel(x, c, ctx, c_ctx, w_mod, b_mod, g_mix, g_ffn, moe_w_grp, moe_b_grp, moe_w_rt, moe_b_rt, moe_w1, moe_w3, moe_w2, ab_w_in, ab_w_out, ssd_conv_w, ssd_conv_b, ssd_dt_bias, ssd_a_log, ssd_d, ssd_norm_g, att_q_g, att_k_g, cd_w_in, cd_w_out, lru_conv_w, lru_conv_b, lru_w_a, lru_b_a, lru_w_x, lru_b_x, lru_lam, swa_sink, g_final):
    bsz, lat_len, d = x.shape
    ctx_len = ctx.shape[1]
    depth = w_mod.shape[0]
    lc = ctx_len + lat_len
    assert d == D_MODEL and ctx_len % ROW_TILE == 0 and lat_len % ROW_TILE == 0 and ctx_len > 0

    cc, ss = _rope_tables(ctx_len, lat_len)
    mods = _modulation(c, c_ctx, w_mod, b_mod)
    expand = jnp.asarray(np.arange(SSD_WIDTH)[None, :] // SSD_HEAD_DIM == np.arange(LANES)[:, None], BF16)
    ones_h = jnp.ones((1, HEAD_DIM), F32)
    srcs = (ctx, x)

    for i in range(depth):
        j = i // 2
        last = i == depth - 1
        m = mods[i].reshape(bsz + 1, 6, d)
        mod = jnp.stack([jnp.broadcast_to(m[0], (bsz, 6, d)), m[1:]], axis=1)
        if i % 2 == 0:
            w = ab_w_in[j]
            zc = jnp.zeros((d, LANES - SSD_HEADS), w.dtype)
            o_dt = SSD_WIDTH + SSD_CONV_CH
            o_q = o_dt + 2 * SSD_HEADS
            w_packed = jnp.concatenate([w[:, :o_dt], w[:, o_dt:o_dt + SSD_HEADS], zc,
                                        w[:, o_dt + SSD_HEADS:o_q], zc, w[:, o_q:]], axis=1).astype(BF16)
            plain = ((0, SSD_WIDTH), (SSD_WIDTH, SSD_CONV_CH), (o_dt, 2 * LANES))
            z, xbc, dt, q, k, v = _inproj(srcs, lc, g_mix[i].reshape(1, d), mod, w_packed,
                                          att_q_g[j].reshape(1, HEAD_DIM), att_k_g[j].reshape(1, HEAD_DIM), cc, ss,
                                          ctx_len=ctx_len, plain=plain, q0=o_dt + 2 * LANES, qk_norm=True)
            ya = _ssd(xbc, dt, z, ssd_conv_w[j], ssd_conv_b[j], ssd_dt_bias[j], ssd_a_log[j], ssd_d[j],
                      ssd_norm_g[j], expand, ctx_len=ctx_len)
            yb = _attn_dense(q, k, v, ctx_len=ctx_len, tq=ATT_TQ, tk=_pick_tile(lc, ATT_TK_PREFS))
            w_out = ab_w_out[j]
        else:
            plain = ((0, LRU_WIDTH), (LRU_WIDTH, LRU_WIDTH))
            gate, xr, q, k, v = _inproj(srcs, lc, g_mix[i].reshape(1, d), mod, cd_w_in[j].astype(BF16),
                                        ones_h, ones_h, cc, ss,
                                        ctx_len=ctx_len, plain=plain, q0=2 * LRU_WIDTH, qk_norm=False)
            xr = _dwconv(xr, lru_conv_w[j], lru_conv_b[j], ctx_len=ctx_len, silu=False)
            ya = _lru(xr, gate, _lru_slabs(lru_w_a[j]), _lru_slabs(lru_w_x[j]), lru_b_a[j], lru_b_x[j], lru_lam[j],
                      ctx_len=ctx_len)
            yb = _attn_window(q, k, v, swa_sink[j], ctx_len=ctx_len, tq=ATT_TQ)
            w_out = cd_w_out[j]
        xs = _outproj(ya, yb, srcs, mod, w_out.astype(BF16), ctx_len=ctx_len, latent_only=last)
        wr = jnp.concatenate([moe_w_grp[i], moe_w_rt[i],
                              jnp.zeros((d, LANES - MOE_GROUPS - N_EXPERTS), F32)], axis=1)
        br = jnp.concatenate([moe_b_grp[i], moe_b_rt[i],
                              jnp.zeros((LANES - MOE_GROUPS - N_EXPERTS,), F32)]).reshape(1, LANES)
        wr_hi = wr.astype(BF16)
        wr = jnp.concatenate([wr_hi, (wr - wr_hi.astype(F32)).astype(BF16)], axis=1)
        xs = _moe(xs, g_ffn[i].reshape(1, d), mod, wr, br, moe_w1[i].astype(BF16), moe_w3[i].astype(BF16),
                  moe_w2[i].astype(BF16),
                  g_final if last else None, ctx_len=0 if last else ctx_len)
        srcs = (xs,)
    return xs
```

```python
import functools
import math

import numpy as np
import jax
import jax.numpy as jnp
from jax import lax
from jax.experimental import pallas as pl
from jax.experimental.pallas import tpu as pltpu

F32 = jnp.float32
BF16 = jnp.bfloat16

D_MODEL = 1024
GRID_W = 64
EPS = 1e-6
HEAD_DIM = 128
ATT_HEADS = 8
ATT_KV_HEADS = 2
GQA_REP = ATT_HEADS // ATT_KV_HEADS
ATT_WIDTH = ATT_HEADS * HEAD_DIM
KV_WIDTH = ATT_KV_HEADS * HEAD_DIM
ATT_SCALE = HEAD_DIM ** -0.5
LOG2E = math.log2(math.e)
ROPE_THETA = 10000.0
WINDOW = 128
SSD_HEADS = 16
SSD_HEAD_DIM = 64
SSD_WIDTH = SSD_HEADS * SSD_HEAD_DIM
SSD_GROUPS = 2
SSD_STATE = 64
SSD_CHUNK = 128
SSD_CONV_CH = SSD_WIDTH + 2 * SSD_GROUPS * SSD_STATE
LRU_WIDTH = 1024
LRU_BLOCKS = 16
LRU_BLOCK_DIM = LRU_WIDTH // LRU_BLOCKS
LRU_C = 8.0
CONV_W = 5
MOE_GROUPS = 4
EXPERTS_PER_GROUP = 4
N_EXPERTS = MOE_GROUPS * EXPERTS_PER_GROUP
D_EXPERT = 256

LANES = 128
SUBLANES = 8
LRU_SLAB = 256
VMEM_LIMIT = 56 * 1024 * 1024
ROW_TILE = 256
MOE_TILE_PREFS = (768, 512, 256)
ATT_TQ = 256
ATT_TK_PREFS = (2816, 1408, 768, 256)


def _params(sem):
    return pltpu.CompilerParams(dimension_semantics=sem, vmem_limit_bytes=VMEM_LIMIT)


def _pick_tile(n, prefs):
    for t in prefs:
        if n % t == 0:
            return t
    raise ValueError(f"no tile for {n}")


def _sigmoid(x):
    return 0.5 * jnp.tanh(0.5 * x) + 0.5


def _softplus(x):
    return jnp.maximum(x, 0.0) + jnp.log1p(jnp.exp(-jnp.abs(x)))


def _split_bf16(v, n):
    parts = []
    r = v
    for _ in range(n):
        p = r.astype(BF16)
        parts.append(p)
        r = r - p.astype(F32)
    return parts


def _dot_nt(a, b):
    return lax.dot_general(a, b, (((1,), (1,)), ((), ())), preferred_element_type=F32)


def _rms(x):
    return x * lax.rsqrt(jnp.mean(x * x, axis=-1, keepdims=True) + EPS)


def _row_select(row0, tm, ctx_len, mod_ref, k):
    if ctx_len == 0:
        return mod_ref[0, 1, k:k + 1, :]
    rows = row0 + lax.broadcasted_iota(jnp.int32, (tm, 1), 0)
    return jnp.where(rows < ctx_len, mod_ref[0, 0, k:k + 1, :], mod_ref[0, 1, k:k + 1, :])


def _rms_mod(x, g, row0, ctx_len, mod_ref, k_shift, k_scale):
    tm = x.shape[0]
    sh = _row_select(row0, tm, ctx_len, mod_ref, k_shift)
    sc = _row_select(row0, tm, ctx_len, mod_ref, k_scale)
    return _rms(x) * g * (1.0 + sc) + sh


def _scan_order(s, n_ctx, n_all):
    return jnp.where(s < n_ctx, n_ctx - 1 - s, n_all + n_ctx - 1 - s)


def _row_specs(tm, d, ct, split):
    if split:
        return [pl.BlockSpec((1, tm, d), lambda b, i: (b, jnp.minimum(i, ct - 1), 0)),
                pl.BlockSpec((1, tm, d), lambda b, i: (b, jnp.maximum(i - ct, 0), 0))]
    return [pl.BlockSpec((1, tm, d), lambda b, i: (b, i, 0))]


def _load_rows(refs, i, ct):
    if len(refs) == 2:
        return jnp.where(i < ct, refs[0][0], refs[1][0])
    return refs[0][0]


def _mod_kernel(ct_ref, w_ref, b_ref, o_ref, *, rows):
    w = w_ref[0]
    outs = []
    for r in range(rows):
        c = ct_ref[:, r:r + 1]
        act = c * _sigmoid(c)
        outs.append(jnp.sum(w * act, axis=0, keepdims=True) + b_ref[0])
    o_ref[0] = jnp.concatenate(outs, axis=0)


def _modulation(c, c_ctx, w_mod, b_mod):
    depth, d, n = w_mod.shape
    rows = c.shape[0] + 1
    ct = jnp.concatenate([c_ctx[None, :], c], axis=0).T
    tn = 1536
    return pl.pallas_call(
        functools.partial(_mod_kernel, rows=rows),
        grid=(depth, n // tn),
        in_specs=[pl.BlockSpec((d, rows), lambda i, j: (0, 0)),
                  pl.BlockSpec((1, d, tn), lambda i, j: (i, 0, j)),
                  pl.BlockSpec((1, 1, tn), lambda i, j: (i, 0, j))],
        out_specs=pl.BlockSpec((1, rows, tn), lambda i, j: (i, 0, j)),
        out_shape=jax.ShapeDtypeStruct((depth, rows, n), F32),
        compiler_params=_params(("arbitrary", "arbitrary")),
        name="modulation",
    )(ct, w_mod, b_mod.reshape(depth, 1, n))


def _inproj_kernel(*refs, tm, ctx_len, plain, q0, qk_norm, n_src, n_w):
    x_refs = refs[:n_src]
    g_ref, mod_ref = refs[n_src:n_src + 2]
    w_refs = refs[n_src + 2:n_src + 2 + n_w]
    qg_ref, kg_ref, cc_ref, ss_ref = refs[n_src + 2 + n_w:n_src + 6 + n_w]
    out_refs = refs[n_src + 6 + n_w:]
    i = pl.program_id(1)
    x = _load_rows(x_refs, i, ctx_len // tm)
    h = _rms_mod(x, g_ref[...], i * tm, ctx_len, mod_ref, 0, 1).astype(BF16)
    res = jnp.concatenate([jnp.dot(h, w_ref[...], preferred_element_type=F32) for w_ref in w_refs], axis=1)
    for ref, (c0, cw) in zip(out_refs[:len(plain)], plain):
        ref[0] = res[:, c0:c0 + cw]
    q_ref, k_ref, v_ref = out_refs[len(plain):]
    cc = cc_ref[...]
    ss = ss_ref[...]

    def head(t, gain, scale):
        if qk_norm:
            t = _rms(t) * gain
        t = t * cc + pltpu.roll(t, HEAD_DIM // 2, 1) * ss
        return (t * scale).astype(BF16)

    for hh in range(ATT_HEADS):
        c0 = q0 + hh * HEAD_DIM
        q_ref[0, :, hh * HEAD_DIM:(hh + 1) * HEAD_DIM] = head(res[:, c0:c0 + HEAD_DIM], qg_ref[...],
                                                              ATT_SCALE * LOG2E)
    k0 = q0 + ATT_WIDTH
    for hh in range(ATT_KV_HEADS):
        c0 = k0 + hh * HEAD_DIM
        k_ref[0, :, hh * HEAD_DIM:(hh + 1) * HEAD_DIM] = head(res[:, c0:c0 + HEAD_DIM], kg_ref[...], 1.0)
    v0 = k0 + KV_WIDTH
    v_ref[0] = res[:, v0:v0 + KV_WIDTH].astype(BF16)


def _inproj(srcs, lc, g, mod, ws, qg, kg, cc, ss, *, ctx_len, plain, q0, qk_norm, tm=ROW_TILE):
    bsz, _, d = srcs[0].shape
    row = lambda b, i: (b, i, 0)
    const = lambda b, i: (0, 0)
    out_shape = [jax.ShapeDtypeStruct((bsz, lc, cw), F32) for _, cw in plain]
    out_specs = [pl.BlockSpec((1, tm, cw), row) for _, cw in plain]
    for cw in (ATT_WIDTH, KV_WIDTH, KV_WIDTH):
        out_shape.append(jax.ShapeDtypeStruct((bsz, lc, cw), BF16))
        out_specs.append(pl.BlockSpec((1, tm, cw), row))
    return pl.pallas_call(
        functools.partial(_inproj_kernel, tm=tm, ctx_len=ctx_len, plain=plain, q0=q0, qk_norm=qk_norm,
                          n_src=len(srcs), n_w=len(ws)),
        grid=(bsz, lc // tm),
        in_specs=_row_specs(tm, d, ctx_len // tm, len(srcs) == 2) + [
            pl.BlockSpec((1, d), const),
            pl.BlockSpec((1, 2, 6, d), lambda b, i: (b, 0, 0, 0))] + [
            pl.BlockSpec(w.shape, const) for w in ws] + [
            pl.BlockSpec((1, HEAD_DIM), const),
            pl.BlockSpec((1, HEAD_DIM), const),
            pl.BlockSpec((tm, HEAD_DIM), lambda b, i: (i, 0)),
            pl.BlockSpec((tm, HEAD_DIM), lambda b, i: (i, 0))],
        out_specs=out_specs,
        out_shape=out_shape,
        compiler_params=_params(("parallel", "arbitrary")),
        name="inproj",
    )(*srcs, g, mod, *ws, qg, kg, cc, ss)


def _conv_rows(prev, cur, nxt, w_ref, b_ref, seg_start, seg_end, silu):
    tm, c = cur.shape
    prev = jnp.where(seg_start, 0.0, prev)
    nxt = jnp.where(seg_end, 0.0, nxt)
    ng = tm // SUBLANES
    ext = jnp.concatenate([prev, cur, nxt], axis=0).reshape(ng + 2, SUBLANES, c)
    pos8 = lax.broadcasted_iota(jnp.int32, (1, SUBLANES, 1), 1)
    pad = CONV_W // 2
    acc = b_ref[...] + cur * w_ref[pad:pad + 1, :]
    for k in range(CONV_W):
        d = k - pad
        if d == 0:
            continue
        r = pltpu.roll(ext, (-d) % SUBLANES, 1)
        if d < 0:
            tap = jnp.where(pos8 >= -d, r[1:ng + 1], r[0:ng])
        else:
            tap = jnp.where(pos8 < SUBLANES - d, r[1:ng + 1], r[2:ng + 2])
        acc = acc + tap.reshape(tm, c) * w_ref[k:k + 1, :]
    if silu:
        acc = acc * _sigmoid(acc)
    return acc


def _dwconv_kernel(prev_ref, cur_ref, next_ref, w_ref, b_ref, o_ref, *, nt, ct, silu):
    i = pl.program_id(1)
    seg_start = jnp.logical_or(i == 0, i == ct)
    seg_end = jnp.logical_or(i == ct - 1, i == nt - 1)
    o_ref[0] = _conv_rows(prev_ref[0], cur_ref[0], next_ref[0], w_ref, b_ref, seg_start, seg_end, silu)


def _dwconv(u, w, b, *, ctx_len, silu, tm=ROW_TILE):
    bsz, lc, c = u.shape
    nt = lc // tm
    hb = tm // SUBLANES
    nhb = lc // SUBLANES
    return pl.pallas_call(
        functools.partial(_dwconv_kernel, nt=nt, ct=ctx_len // tm, silu=silu),
        grid=(bsz, nt),
        in_specs=[pl.BlockSpec((1, SUBLANES, c), lambda b_, i: (b_, jnp.maximum(i * hb - 1, 0), 0)),
                  pl.BlockSpec((1, tm, c), lambda b_, i: (b_, i, 0)),
                  pl.BlockSpec((1, SUBLANES, c), lambda b_, i: (b_, jnp.minimum((i + 1) * hb, nhb - 1), 0)),
                  pl.BlockSpec((CONV_W, c), lambda b_, i: (0, 0)),
                  pl.BlockSpec((1, c), lambda b_, i: (0, 0))],
        out_specs=pl.BlockSpec((1, tm, c), lambda b_, i: (b_, i, 0)),
        out_shape=jax.ShapeDtypeStruct((bsz, lc, c), F32),
        compiler_params=_params(("parallel", "arbitrary")),
        name="dwconv",
    )(u, u, u, w, b.reshape(1, c))


def _expand_heads(v, e_ref):
    hi, lo = _split_bf16(v, 2)
    e = e_ref[...]
    return (jnp.dot(hi, e, preferred_element_type=F32) + jnp.dot(lo, e, preferred_element_type=F32))


def _ssd_chunk(xbc, dt_raw, dtb, alog, e_ref, h_prev, reverse):
    q = SSD_CHUNK
    heads_per_group = SSD_HEADS // SSD_GROUPS
    x = xbc[:, :SSD_WIDTH]
    bm = xbc[:, SSD_WIDTH:SSD_WIDTH + LANES]
    cm = xbc[:, SSD_WIDTH + LANES:]
    dt = _softplus(dt_raw + dtb)
    da = dt * (-jnp.exp(alog) * LOG2E)
    ii = lax.broadcasted_iota(jnp.int32, (q, q), 0)
    jj = lax.broadcasted_iota(jnp.int32, (q, q), 1)
    causal = (jj >= ii) if reverse else (jj <= ii)
    mask_b = jnp.where(causal, 1.0, 0.0).astype(BF16)
    da_t = da.T
    dt_t = dt.T
    acs = sum(jnp.dot(mask_b, p, preferred_element_type=F32) for p in _split_bf16(da, 3))
    acs_t = sum(_dot_nt(p, mask_b) for p in _split_bf16(da_t, 3))
    total = jnp.sum(da, axis=0, keepdims=True)
    eacs = jnp.exp2(acs)

    bm_b = bm.astype(BF16)
    lane = lax.broadcasted_iota(jnp.int32, (1, LANES), 1)
    in_group = [jnp.logical_and(lane >= g * SSD_STATE, lane < (g + 1) * SSD_STATE) for g in range(SSD_GROUPS)]
    cb = [_dot_nt(jnp.where(in_group[g], cm, 0.0).astype(BF16), bm_b) for g in range(SSD_GROUPS)]

    x_b = x.astype(BF16)
    h_b = h_prev.astype(BF16)
    pairs = []
    for hp in range(SSD_HEADS // 2):
        rhs = jnp.concatenate([x_b[:, hp * LANES:(hp + 1) * LANES], h_b[:, hp * LANES:(hp + 1) * LANES]], axis=0)
        outs = []
        for t in range(2):
            h = 2 * hp + t
            seg = acs[:, h:h + 1] - acs_t[h:h + 1, :]
            dec = jnp.exp2(jnp.where(causal, seg, -jnp.inf))
            w = cb[h // heads_per_group] * dec * dt_t[h:h + 1, :]
            lhs = jnp.concatenate([w.astype(BF16), (cm * eacs[:, h:h + 1]).astype(BF16)], axis=1)
            outs.append(jnp.dot(lhs, rhs, preferred_element_type=F32))
        pairs.append(jnp.where(lane < SSD_HEAD_DIM, outs[0], outs[1]))
    y = jnp.concatenate(pairs, axis=1)

    to_end = jnp.exp2(total - acs) * dt
    xw = (x * _expand_heads(to_end, e_ref)).astype(BF16)
    st = jnp.dot(bm.T.astype(BF16), xw, preferred_element_type=F32)
    row_g0 = lax.broadcasted_iota(jnp.int32, st.shape, 0) < SSD_STATE
    col_g0 = lax.broadcasted_iota(jnp.int32, st.shape, 1) < heads_per_group * SSD_HEAD_DIM
    chunk_decay = _expand_heads(jnp.broadcast_to(jnp.exp2(total), (SUBLANES, LANES)), e_ref)[0:1]
    return y, h_prev * chunk_decay + jnp.where(row_g0 == col_g0, st, 0.0)


def _ssd_bwd_kernel(prev_ref, cur_ref, next_ref, cw_ref, cb_ref, dt_ref, dtb_ref, alog_ref, e_ref,
                    yb_ref, xc_ref, hst_ref, *, nbx, nb, bsz):
    s = pl.program_id(0)
    blk = _scan_order(s, nbx, nb)

    @pl.when(s == 0)
    def _():
        hst_ref[...] = jnp.zeros_like(hst_ref)

    seg_start = jnp.logical_or(blk == 0, blk == nbx)
    seg_end = jnp.logical_or(blk == nbx - 1, blk == nb - 1)
    xbc, h = [], []
    for bb in range(bsz):
        xbc.append(_conv_rows(prev_ref[bb], cur_ref[bb], next_ref[bb], cw_ref, cb_ref, seg_start, seg_end, True))
        xc_ref[bb] = xbc[bb]
        h.append(hst_ref[bb])
    for sub in reversed(range(cur_ref.shape[1] // SSD_CHUNK)):
        rows = slice(sub * SSD_CHUNK, (sub + 1) * SSD_CHUNK)
        for bb in range(bsz):
            y, h[bb] = _ssd_chunk(xbc[bb][rows], dt_ref[bb, rows, :], dtb_ref[...], alog_ref[...], e_ref, h[bb], True)
            yb_ref[bb, rows, :] = y
    for bb in range(bsz):
        hst_ref[bb] = h[bb]


def _ssd_fwd_kernel(xc_ref, dt_ref, dtb_ref, alog_ref, e_ref, yb_ref, z_ref, dsk_ref, ng_ref,
                    o_ref, hst_ref, *, bsz):
    @pl.when(pl.program_id(0) == 0)
    def _():
        hst_ref[...] = jnp.zeros_like(hst_ref)

    h = [hst_ref[bb] for bb in range(bsz)]
    for sub in range(xc_ref.shape[1] // SSD_CHUNK):
        rows = slice(sub * SSD_CHUNK, (sub + 1) * SSD_CHUNK)
        for bb in range(bsz):
            xbc = xc_ref[bb, rows, :]
            y, h[bb] = _ssd_chunk(xbc, dt_ref[bb, rows, :], dtb_ref[...], alog_ref[...], e_ref, h[bb], False)
            z = z_ref[bb, rows, :]
            y = (y + yb_ref[bb, rows, :] + dsk_ref[...] * xbc[:, :SSD_WIDTH]) * (z * _sigmoid(z))
            o_ref[bb, rows, :] = (_rms(y) * ng_ref[...]).astype(BF16)
    for bb in range(bsz):
        hst_ref[bb] = h[bb]


def _ssd(xbc_raw, dt, z, conv_w, conv_b, dt_bias, a_log, d_skip, norm_g, expand, *, ctx_len, q=ROW_TILE):
    bsz, lc, cch = xbc_raw.shape
    w = SSD_WIDTH
    nb = lc // q
    nbx = ctx_len // q
    hb = q // SUBLANES
    nhb = lc // SUBLANES
    pad = lambda t: jnp.pad(t, (0, LANES - SSD_HEADS)).reshape(1, LANES)
    const = lambda s: (0, 0)
    rev = lambda s: _scan_order(s, nbx, nb)
    state = [pltpu.VMEM((bsz, SSD_GROUPS * SSD_STATE, w), F32)]
    yb, xc = pl.pallas_call(
        functools.partial(_ssd_bwd_kernel, nbx=nbx, nb=nb, bsz=bsz),
        grid=(nb,),
        in_specs=[pl.BlockSpec((bsz, SUBLANES, cch), lambda s: (0, jnp.maximum(rev(s) * hb - 1, 0), 0)),
                  pl.BlockSpec((bsz, q, cch), lambda s: (0, rev(s), 0)),
                  pl.BlockSpec((bsz, SUBLANES, cch), lambda s: (0, jnp.minimum((rev(s) + 1) * hb, nhb - 1), 0)),
                  pl.BlockSpec((CONV_W, cch), const),
                  pl.BlockSpec((1, cch), const),
                  pl.BlockSpec((bsz, q, LANES), lambda s: (0, rev(s), 1)),
                  pl.BlockSpec((1, LANES), const),
                  pl.BlockSpec((1, LANES), const),
                  pl.BlockSpec((LANES, w), const)],
        out_specs=[pl.BlockSpec((bsz, q, w), lambda s: (0, rev(s), 0)),
                   pl.BlockSpec((bsz, q, cch), lambda s: (0, rev(s), 0))],
        out_shape=[jax.ShapeDtypeStruct((bsz, lc, w), F32), jax.ShapeDtypeStruct((bsz, lc, cch), F32)],
        scratch_shapes=state,
        compiler_params=_params(("arbitrary",)),
        name="ssd_bwd",
    )(xbc_raw, xbc_raw, xbc_raw, conv_w, conv_b.reshape(1, cch), dt, pad(dt_bias[1]), pad(a_log[1]), expand)
    row = lambda cols: pl.BlockSpec((bsz, q, cols), lambda s: (0, s, 0))
    return pl.pallas_call(
        functools.partial(_ssd_fwd_kernel, bsz=bsz),
        grid=(nb,),
        in_specs=[row(cch), row(LANES), pl.BlockSpec((1, LANES), const), pl.BlockSpec((1, LANES), const),
                  pl.BlockSpec((LANES, w), const), row(w), row(w), pl.BlockSpec((1, w), const),
                  pl.BlockSpec((1, w), const)],
        out_specs=row(w),
        out_shape=jax.ShapeDtypeStruct((bsz, lc, w), BF16),
        scratch_shapes=state,
        compiler_params=_params(("arbitrary",)),
        name="ssd_fwd",
    )(xc, dt, pad(dt_bias[0]), pad(a_log[0]), expand, yb, z,
      jnp.repeat(d_skip, SSD_HEAD_DIM).reshape(1, w), norm_g.reshape(1, w))


def _stack_heads(q):
    return jnp.concatenate([q[:, r * HEAD_DIM:(r + 1) * HEAD_DIM] for r in range(q.shape[1] // HEAD_DIM)], axis=0)


def _lane_parts(s):
    return [s[:, c * LANES:(c + 1) * LANES] for c in range(s.shape[1] // LANES)]


def _softmax_parts(parts, sink=None):
    pm = parts[0]
    for t in parts[1:]:
        pm = jnp.maximum(pm, t)
    m = jnp.max(pm, axis=-1, keepdims=True)
    if sink is not None:
        m = jnp.maximum(m, sink)
    mb = jnp.broadcast_to(m, pm.shape)
    ps = [jnp.exp2(t - mb) for t in parts]
    lsum = ps[0]
    for t in ps[1:]:
        lsum = lsum + t
    l = jnp.sum(lsum, axis=-1, keepdims=True)
    if sink is not None:
        l = l + jnp.exp2(sink - m)
    return jnp.concatenate([t.astype(BF16) for t in ps], axis=1), l


def _store_heads(o_ref, o, tq, col0=0):
    for r in range(GQA_REP):
        c0 = col0 + r * HEAD_DIM
        o_ref[0, :, c0:c0 + HEAD_DIM] = o[r * tq:(r + 1) * tq].astype(BF16)


def _attn_dense_kernel(q_ref, k_ref, v_ref, o_ref, m_scr, l_scr, acc_scr, *, tq, tk, ctx_len, lk):
    i = pl.program_id(1)
    gw = GQA_REP * HEAD_DIM
    groups = range(ATT_KV_HEADS)
    q4 = [_stack_heads(q_ref[0, :, g * gw:(g + 1) * gw]) for g in groups]
    kcol = [slice(g * HEAD_DIM, (g + 1) * HEAD_DIM) for g in groups]

    @pl.when(i < ctx_len // tq)
    def _():
        for g in groups:
            p, l = _softmax_parts(_lane_parts(_dot_nt(q4[g], k_ref[0, 0:ctx_len, kcol[g]])))
            o = jnp.dot(p, v_ref[0, 0:ctx_len, kcol[g]], preferred_element_type=F32) / l
            _store_heads(o_ref, o, tq, col0=g * gw)

    @pl.when(i >= ctx_len // tq)
    def _():
        m_scr[...] = jnp.full_like(m_scr, -jnp.inf)
        l_scr[...] = jnp.zeros_like(l_scr)
        acc_scr[...] = jnp.zeros_like(acc_scr)

        def body(j, carry):
            off = pl.multiple_of(j * tk, tk)
            for g in groups:
                parts = _lane_parts(_dot_nt(q4[g], k_ref[0, pl.ds(off, tk), kcol[g]]))
                pm = parts[0]
                for t in parts[1:]:
                    pm = jnp.maximum(pm, t)
                m_prev = m_scr[g]
                m_new = jnp.maximum(m_prev, jnp.max(pm, axis=-1, keepdims=True))
                alpha = jnp.exp2(m_prev - m_new)
                ps = [jnp.exp2(t - m_new) for t in parts]
                lsum = ps[0]
                for t in ps[1:]:
                    lsum = lsum + t
                l_scr[g] = alpha * l_scr[g] + lsum
                p = jnp.concatenate([t.astype(BF16) for t in ps], axis=1)
                acc_scr[g] = alpha * acc_scr[g] + jnp.dot(p, v_ref[0, pl.ds(off, tk), kcol[g]],
                                                          preferred_element_type=F32)
                m_scr[g] = m_new
            return carry

        lax.fori_loop(0, lk // tk, body, 0)
        for g in groups:
            _store_heads(o_ref, acc_scr[g] / jnp.sum(l_scr[g], axis=-1, keepdims=True), tq, col0=g * gw)


def _attn_dense(q, k, v, *, ctx_len, tq, tk):
    bsz, lc, _ = q.shape
    rows = GQA_REP * tq
    stat = pltpu.VMEM((ATT_KV_HEADS, rows, LANES), F32)
    return pl.pallas_call(
        functools.partial(_attn_dense_kernel, tq=tq, tk=tk, ctx_len=ctx_len, lk=lc),
        grid=(bsz, lc // tq),
        in_specs=[pl.BlockSpec((1, tq, ATT_WIDTH), lambda b, i: (b, i, 0)),
                  pl.BlockSpec((1, lc, KV_WIDTH), lambda b, i: (b, 0, 0)),
                  pl.BlockSpec((1, lc, KV_WIDTH), lambda b, i: (b, 0, 0))],
        out_specs=pl.BlockSpec((1, tq, ATT_WIDTH), lambda b, i: (b, i, 0)),
        out_shape=jax.ShapeDtypeStruct((bsz, lc, ATT_WIDTH), BF16),
        scratch_shapes=[stat, stat, stat],
        compiler_params=_params(("parallel", "arbitrary")),
        name="attn_dense",
    )(q, k, v)


def _attn_win_kernel(q_ref, k_ref, v_ref, sink_ref, bias_ref, o_ref, *, ctx_len, lat_len, tq):
    n = pl.program_id(1)
    band = tq + 2 * WINDOW
    il = n - ctx_len // tq
    lo = jnp.clip(il * tq - WINDOW, 0, lat_len - band)
    start = pl.multiple_of(ctx_len + lo, WINDOW)
    bias = bias_ref[0]
    gw = GQA_REP * HEAD_DIM
    for g in range(ATT_KV_HEADS):
        kc0 = g * HEAD_DIM
        q4 = _stack_heads(q_ref[0, :, g * gw:(g + 1) * gw])
        s_loc = _dot_nt(q4, k_ref[0, pl.ds(start, band), kc0:kc0 + HEAD_DIM]) + bias
        s_ctx = _dot_nt(q4, k_ref[0, 0:ctx_len, kc0:kc0 + HEAD_DIM])
        p, l = _softmax_parts(_lane_parts(s_loc) + _lane_parts(s_ctx), sink_ref[g])
        o = (jnp.dot(p[:, :band], v_ref[0, pl.ds(start, band), kc0:kc0 + HEAD_DIM], preferred_element_type=F32)
             + jnp.dot(p[:, band:], v_ref[0, 0:ctx_len, kc0:kc0 + HEAD_DIM], preferred_element_type=F32)) / l
        _store_heads(o_ref, o, tq, col0=g * gw)


def _attn_window(q, k, v, sink, *, ctx_len, tq):
    bsz, lc, _ = q.shape
    lat_len = lc - ctx_len
    rows = GQA_REP * tq
    assert tq & (tq - 1) == 0 and ctx_len % tq == 0 and lat_len >= tq + 2 * WINDOW
    sink_col = jnp.repeat(sink.reshape(ATT_KV_HEADS, GQA_REP) * LOG2E, tq, axis=1).reshape(ATT_KV_HEADS, rows, 1)
    band = tq + 2 * WINDOW
    ct = ctx_len // tq
    nlt = lat_len // tq
    r = np.arange(tq)[:, None]
    c = np.arange(band)[None, :]
    masks = [np.abs(c - off - r) <= WINDOW for off in (0, WINDOW, band - tq)] + [np.zeros((tq, band), bool)]
    bias = jnp.asarray(np.tile(np.where(np.stack(masks), 0.0, -np.inf), (1, GQA_REP, 1)), F32)

    def variant(n):
        il = n - ct
        return jnp.where(il < 0, 3, jnp.where(il == 0, 0, jnp.where(il == nlt - 1, 2, 1)))

    return pl.pallas_call(
        functools.partial(_attn_win_kernel, ctx_len=ctx_len, lat_len=lat_len, tq=tq),
        grid=(bsz, lc // tq),
        in_specs=[pl.BlockSpec((1, tq, ATT_WIDTH), lambda b, n: (b, n, 0)),
                  pl.BlockSpec((1, lc, KV_WIDTH), lambda b, n: (b, 0, 0)),
                  pl.BlockSpec((1, lc, KV_WIDTH), lambda b, n: (b, 0, 0)),
                  pl.BlockSpec((ATT_KV_HEADS, rows, 1), lambda b, n: (0, 0, 0)),
                  pl.BlockSpec((1, rows, band), lambda b, n: (variant(n), 0, 0))],
        out_specs=pl.BlockSpec((1, tq, ATT_WIDTH), lambda b, n: (b, n, 0)),
        out_shape=jax.ShapeDtypeStruct((bsz, lc, ATT_WIDTH), BF16),
        compiler_params=_params(("parallel", "arbitrary")),
        name="attn_window",
    )(q, k, v, sink_col, bias)


def _lru_scan_rows(a, b, carry, reverse):
    t, w = a.shape
    ngroups = t // SUBLANES
    a = a.reshape(ngroups, SUBLANES, w)
    b = b.reshape(ngroups, SUBLANES, w)
    pos8 = lax.broadcasted_iota(jnp.int32, (1, SUBLANES, 1), 1)
    k = 1
    while k < SUBLANES:
        if reverse:
            keep = pos8 < SUBLANES - k
            sh = SUBLANES - k
        else:
            keep = pos8 >= k
            sh = k
        a_sh = jnp.where(keep, pltpu.roll(a, sh, 1), 1.0)
        b_sh = jnp.where(keep, pltpu.roll(b, sh, 1), 0.0)
        b = b + a * b_sh
        a = a * a_sh
        k *= 2
    a = a.reshape(t, w)
    b = b.reshape(t, w)
    hs = [None] * ngroups
    for g in (range(ngroups - 1, -1, -1) if reverse else range(ngroups)):
        sl = slice(g * SUBLANES, (g + 1) * SUBLANES)
        hg = a[sl] * carry + b[sl]
        carry = hg[0:1] if reverse else hg[SUBLANES - 1:SUBLANES]
        hs[g] = hg
    return jnp.concatenate(hs, axis=0), carry


def _lru_kernel(*refs, reverse):
    x_ref, wa_ref, wx_ref, ba_ref, bx_ref, lam_ref = refs[:6]
    if reverse:
        o_ref, carry_ref = refs[6:]
    else:
        hb_ref, gate_ref, o_ref, carry_ref = refs[6:]
    s = pl.program_id(1)

    @pl.when(s == 0)
    def _():
        carry_ref[...] = jnp.zeros_like(carry_ref)

    x = x_ref[0]
    x_b = x.astype(BF16)
    nslab = LRU_WIDTH // LRU_SLAB

    def gate(w_ref, b_ref):
        pre = jnp.concatenate(
            [jnp.dot(x_b[:, c * LRU_SLAB:(c + 1) * LRU_SLAB], w_ref[0, c], preferred_element_type=F32)
             for c in range(nslab)], axis=1)
        return _sigmoid(pre + b_ref[0])

    r = gate(wa_ref, ba_ref)
    ig = gate(wx_ref, bx_ref)
    log2_base = (-LRU_C * LOG2E) * _softplus(-lam_ref[0])
    a = jnp.exp2(r * log2_base)
    y = 1.0 - a * a
    b = jnp.where(y > 0.0, y * lax.rsqrt(y), 0.0) * (ig * x)
    h, carry = _lru_scan_rows(a, b, carry_ref[...], reverse)
    carry_ref[...] = carry
    if reverse:
        o_ref[0] = h
    else:
        g = gate_ref[0]
        gelu = 0.5 * g * (1.0 + jnp.tanh(math.sqrt(2.0 / math.pi) * (g + 0.044715 * (g * g * g))))
        o_ref[0] = (gelu * (h + hb_ref[0])).astype(BF16)


def _lru(xr, gate, wa, wx, ba, bx, lam, *, ctx_len, t=ROW_TILE):
    bsz, lc, w = xr.shape
    nt = lc // t
    ct = ctx_len // t
    nslab = w // LRU_SLAB
    vec = lambda v: v.reshape(2, 1, w)

    def call(reverse, extra_in, out_dtype):
        tile = (lambda s: _scan_order(s, ct, nt)) if reverse else (lambda s: s)
        d = 1 if reverse else 0
        row = pl.BlockSpec((1, t, w), lambda b, s: (b, tile(s), 0))
        return pl.pallas_call(
            functools.partial(_lru_kernel, reverse=reverse),
            grid=(bsz, nt),
            in_specs=[row,
                      pl.BlockSpec((1, nslab, LRU_SLAB, LRU_SLAB), lambda b, s: (d, 0, 0, 0)),
                      pl.BlockSpec((1, nslab, LRU_SLAB, LRU_SLAB), lambda b, s: (d, 0, 0, 0)),
                      pl.BlockSpec((1, 1, w), lambda b, s: (d, 0, 0)),
                      pl.BlockSpec((1, 1, w), lambda b, s: (d, 0, 0)),
                      pl.BlockSpec((1, 1, w), lambda b, s: (d, 0, 0))] + [row] * len(extra_in),
            out_specs=row,
            out_shape=jax.ShapeDtypeStruct((bsz, lc, w), out_dtype),
            scratch_shapes=[pltpu.VMEM((1, w), F32)],
            compiler_params=_params(("parallel", "arbitrary")),
            name="lru_bwd" if reverse else "lru_fwd",
        )(xr, wa, wx, vec(ba), vec(bx), vec(lam), *extra_in)

    hb = call(True, (), F32)
    return call(False, (hb, gate), BF16)


def _lru_slabs(w):
    per = LRU_SLAB // LRU_BLOCK_DIM
    nslab = LRU_BLOCKS // per
    w = w.reshape(2, nslab, per, LRU_BLOCK_DIM, LRU_BLOCK_DIM)
    eye = jnp.eye(per, dtype=w.dtype)
    full = jnp.einsum('dspij,pq->dspiqj', w, eye)
    return full.reshape(2, nslab, LRU_SLAB, LRU_SLAB).astype(BF16)


def _outproj_kernel(*refs, tm, ctx_len, t0, n_src):
    ya_ref, yb_ref = refs[:2]
    x_refs = refs[2:2 + n_src]
    mod_ref, w_ref, o_ref = refs[2 + n_src:]
    i = pl.program_id(1) + t0
    half = ya_ref.shape[2]
    acc = (jnp.dot(ya_ref[0], w_ref[0:half, :], preferred_element_type=F32)
           + jnp.dot(yb_ref[0], w_ref[half:, :], preferred_element_type=F32))
    gate = _row_select(i * tm, tm, ctx_len, mod_ref, 2)
    o_ref[0] = _load_rows(x_refs, i, ctx_len // tm) + gate * acc


def _outproj(ya, yb, srcs, mod, w, *, ctx_len, latent_only, tm=ROW_TILE):
    bsz, lc, half = ya.shape
    d = w.shape[1]
    ct = ctx_len // tm
    t0 = ct if latent_only else 0
    row = lambda b, i: (b, i + t0, 0)
    if len(srcs) == 2:
        x_specs = [pl.BlockSpec((1, tm, d), lambda b, i: (b, jnp.minimum(i + t0, ct - 1), 0)),
                   pl.BlockSpec((1, tm, d), lambda b, i: (b, jnp.maximum(i + t0 - ct, 0), 0))]
    else:
        x_specs = [pl.BlockSpec((1, tm, d), row)]
    return pl.pallas_call(
        functools.partial(_outproj_kernel, tm=tm, ctx_len=ctx_len, t0=t0, n_src=len(srcs)),
        grid=(bsz, lc // tm - t0),
        in_specs=[pl.BlockSpec((1, tm, half), row), pl.BlockSpec((1, tm, half), row)] + x_specs + [
            pl.BlockSpec((1, 2, 6, d), lambda b, i: (b, 0, 0, 0)),
            pl.BlockSpec((2 * half, d), lambda b, i: (0, 0))],
        out_specs=pl.BlockSpec((1, tm, d), lambda b, i: (b, i, 0)),
        out_shape=jax.ShapeDtypeStruct((bsz, lc - t0 * tm, d), F32),
        compiler_params=_params(("parallel", "arbitrary")),
        name="outproj",
    )(ya, yb, *srcs, mod, w)


def _first_max(vals):
    m = vals[0]
    for v in vals[1:]:
        m = jnp.maximum(m, v)
    hot = []
    taken = None
    for v in vals:
        hit = v == m
        if taken is None:
            hot.append(hit)
            taken = hit
        else:
            hot.append(jnp.logical_and(hit, jnp.logical_not(taken)))
            taken = jnp.logical_or(taken, hit)
    return m, hot


def _moe_kernel(*refs, tm, ctx_len, wins, final):
    x_ref, g_ref, mod_ref, wr_ref, br_ref, w1_ref, w3_ref, w2_ref = refs[:8]
    if final:
        gf_ref = refs[8]
    o_ref, hn_scr, gate_scr, rank_row_scr, rank_col_scr, cnt_ref, acc_scr = refs[-7:]
    i = pl.program_id(1)
    e = pl.program_id(2)

    @pl.when(e == 0)
    def _():
        hn = _rms_mod(x_ref[0], g_ref[...], i * tm, ctx_len, mod_ref, 3, 4)
        hn_scr[...] = hn.astype(BF16)
        h_hi, h_lo = _split_bf16(hn, 2)
        r_hi = jnp.dot(h_hi, wr_ref[...], preferred_element_type=F32)
        r_lo = jnp.dot(h_lo, wr_ref[...], preferred_element_type=F32)
        logits = (r_hi[:, :LANES] + r_hi[:, LANES:]) + (r_lo[:, :LANES] + r_lo[:, LANES:]) + br_ref[...]
        lt = jnp.concatenate([logits[c * LANES:(c + 1) * LANES, :].T for c in range(tm // LANES)], axis=1)
        gl = [lt[j:j + 1, :] for j in range(MOE_GROUPS)]
        gmax, ghot = _first_max(gl)
        gsum = sum(jnp.exp(v - gmax) for v in gl)
        p_grp = 1.0 / gsum
        elg = []
        for j in range(EXPERTS_PER_GROUP):
            v = jnp.zeros_like(gmax)
            for gi in range(MOE_GROUPS):
                c = MOE_GROUPS + gi * EXPERTS_PER_GROUP + j
                v = jnp.where(ghot[gi], lt[c:c + 1, :], v)
            elg.append(v)
        v1, hot1 = _first_max(elg)
        rest = [jnp.where(hh, -jnp.inf, v) for hh, v in zip(hot1, elg)]
        v2, hot2 = _first_max(rest)
        ex = jnp.exp(v2 - v1)
        w1 = p_grp / (1.0 + ex)
        w2 = p_grp * ex / (1.0 + ex)
        own = [jnp.where(hot1[j], w1, 0.0) + jnp.where(hot2[j], w2, 0.0) for j in range(EXPERTS_PER_GROUP)]
        pad_rows = jnp.zeros((LANES - MOE_GROUPS, tm), F32)
        g_t = jnp.concatenate([jnp.where(h_, 1.0, 0.0) for h_ in ghot] + [pad_rows], axis=0)
        t0 = lax.broadcasted_iota(jnp.int32, (tm, tm), 0)
        t1 = lax.broadcasted_iota(jnp.int32, (tm, tm), 1)
        before = jnp.where(t0 < t1, 1.0, 0.0).astype(BF16)
        rank_t = jnp.where(g_t > 0.0, jnp.dot(g_t.astype(BF16), before, preferred_element_type=F32), -1.0)
        rank_row_scr[...] = rank_t[0:SUBLANES]
        for gi in range(MOE_GROUPS):
            cnt_ref[gi] = jnp.sum(g_t[gi:gi + 1, :]).astype(jnp.int32)
            slab_t = jnp.concatenate([jnp.where(ghot[gi], own[j], 0.0) for j in range(EXPERTS_PER_GROUP)]
                                     + [pad_rows], axis=0)
            for c in range(tm // LANES):
                rows_c = slice(c * LANES, (c + 1) * LANES)
                slab = slab_t[:, rows_c].T
                hi, lo = _split_bf16(slab, 2)
                gate_scr[gi, rows_c, 0:LANES] = hi
                gate_scr[gi, rows_c, LANES:] = lo
                if gi == 0:
                    rank_c = rank_t[:, rows_c].T
                    for gj in range(MOE_GROUPS):
                        rank_col_scr[gj, rows_c, :] = rank_c[:, gj:gj + 1]
        acc_scr[...] = jnp.zeros_like(acc_scr)

    cnt = cnt_ref[e]
    rank_row = rank_row_scr[pl.ds(e, 1), :]
    rank_col = rank_col_scr[e]

    def window(first, w_):
        base = first.astype(F32)
        slot_r = lax.broadcasted_iota(jnp.int32, (w_, 1), 0).astype(F32)
        slot_c = lax.broadcasted_iota(jnp.int32, (1, w_), 1).astype(F32)
        sel = jnp.where(rank_row - base == slot_r, 1.0, 0.0).astype(BF16)
        sel_t = jnp.where(rank_col - base == slot_c, 1.0, 0.0).astype(BF16)
        xs = jnp.dot(sel, hn_scr[...], preferred_element_type=F32).astype(BF16)
        gs = jnp.dot(sel, gate_scr[e], preferred_element_type=F32)
        gsel = gs[:, :LANES] + gs[:, LANES:]
        yw = None
        for kk in range(EXPERTS_PER_GROUP):
            a = jnp.dot(xs, w1_ref[0, kk], preferred_element_type=F32)
            u = jnp.dot(xs, w3_ref[0, kk], preferred_element_type=F32)
            hid = (a * _sigmoid(a)) * u * gsel[:, kk:kk + 1]
            part = jnp.dot(hid.astype(BF16), w2_ref[0, kk], preferred_element_type=F32)
            yw = part if yw is None else yw + part
        acc_scr[...] += jnp.dot(sel_t, yw.astype(BF16), preferred_element_type=F32)

    w_top = wins[-1]
    lower = 0
    for w_ in wins[:-1]:
        pl.when(jnp.logical_and(cnt > lower, cnt <= w_))(functools.partial(window, jnp.int32(0), w_))
        lower = w_

    @pl.when(cnt > lower)
    def _():
        def body(wi, carry):
            window(wi * w_top, w_top)
            return carry
        lax.fori_loop(0, (cnt + (w_top - 1)) // w_top, body, 0)

    @pl.when(e == MOE_GROUPS - 1)
    def _():
        gate = _row_select(i * tm, tm, ctx_len, mod_ref, 5)
        out = x_ref[0] + gate * acc_scr[...]
        if final:
            out = _rms(out) * gf_ref[...]
        o_ref[0] = out


def _moe(xx, g, mod, wr, br, w1, w3, w2, layer, g_final, *, ctx_len):
    bsz, rows, d = xx.shape
    tm = _pick_tile(rows, MOE_TILE_PREFS)
    mid = -(-(tm // MOE_GROUPS) // LANES) * LANES
    wins = (mid, mid + LANES, mid + 2 * LANES)
    eb = EXPERTS_PER_GROUP
    row = lambda b, i, e: (b, i, 0)
    const = lambda b, i, e: (0, 0)
    final = g_final is not None
    return pl.pallas_call(
        functools.partial(_moe_kernel, tm=tm, ctx_len=ctx_len, wins=wins, final=final),
        grid=(bsz, rows // tm, MOE_GROUPS),
        in_specs=[pl.BlockSpec((1, tm, d), row),
                  pl.BlockSpec((1, d), const),
                  pl.BlockSpec((1, 2, 6, d), lambda b, i, e: (b, 0, 0, 0)),
                  pl.BlockSpec((d, 2 * LANES), const),
                  pl.BlockSpec((1, LANES), const),
                  pl.BlockSpec((1, eb, d, D_EXPERT), lambda b, i, e: (layer, e, 0, 0)),
                  pl.BlockSpec((1, eb, d, D_EXPERT), lambda b, i, e: (layer, e, 0, 0)),
                  pl.BlockSpec((1, eb, D_EXPERT, d), lambda b, i, e: (layer, e, 0, 0))]
                 + ([pl.BlockSpec((1, d), const)] if final else []),
        out_specs=pl.BlockSpec((1, tm, d), row),
        out_shape=jax.ShapeDtypeStruct((bsz, rows, d), F32),
        scratch_shapes=[pltpu.VMEM((tm, d), BF16),
                        pltpu.VMEM((MOE_GROUPS, tm, 2 * LANES), BF16),
                        pltpu.VMEM((SUBLANES, tm), F32),
                        pltpu.VMEM((MOE_GROUPS, tm, 1), F32),
                        pltpu.SMEM((MOE_GROUPS,), jnp.int32),
                        pltpu.VMEM((tm, d), F32)],
        compiler_params=_params(("parallel", "arbitrary", "arbitrary")),
        name="moe",
    )(xx, g, mod, wr, br, w1, w3, w2, *((g_final.reshape(1, d),) if final else ()))


def _rope_tables(ctx_len, lat_len):
    rows = lat_len // GRID_W
    row = np.repeat(np.arange(rows), GRID_W).astype(np.float64)
    col = np.tile(np.arange(GRID_W), rows).astype(np.float64)
    n_freq = HEAD_DIM // 4
    inv = ROPE_THETA ** (-np.arange(n_freq, dtype=np.float64) / n_freq)
    ang = np.concatenate([row[:, None] * inv, col[:, None] * inv], axis=-1)
    cos, sin = np.cos(ang), np.sin(ang)
    cc = np.concatenate([np.ones((ctx_len, HEAD_DIM)), np.concatenate([cos, cos], axis=-1)], axis=0)
    ss = np.concatenate([np.zeros((ctx_len, HEAD_DIM)), np.concatenate([-sin, sin], axis=-1)], axis=0)
    return jnp.asarray(cc, F32), jnp.asarray(ss, F32)


def kernel(x, c, ctx, c_ctx, w_mod, b_mod, g_mix, g_ffn, moe_w_grp, moe_b_grp, moe_w_rt, moe_b_rt, moe_w1, moe_w3, moe_w2, ab_w_in, ab_w_out, ssd_conv_w, ssd_conv_b, ssd_dt_bias, ssd_a_log, ssd_d, ssd_norm_g, att_q_g, att_k_g, cd_w_in, cd_w_out, lru_conv_w, lru_conv_b, lru_w_a, lru_b_a, lru_w_x, lru_b_x, lru_lam, swa_sink, g_final):
    bsz, lat_len, d = x.shape
    ctx_len = ctx.shape[1]
    depth = w_mod.shape[0]
    lc = ctx_len + lat_len
    assert d == D_MODEL and ctx_len % ROW_TILE == 0 and lat_len % ROW_TILE == 0 and ctx_len > 0

    cc, ss = _rope_tables(ctx_len, lat_len)
    mods = _modulation(c, c_ctx, w_mod, b_mod)
    expand = jnp.asarray(np.arange(SSD_WIDTH)[None, :] // SSD_HEAD_DIM == np.arange(LANES)[:, None], BF16)
    ones_h = jnp.ones((1, HEAD_DIM), F32)
    w1_b, w3_b, w2_b = moe_w1.astype(BF16), moe_w3.astype(BF16), moe_w2.astype(BF16)
    srcs = (ctx, x)

    for i in range(depth):
        j = i // 2
        last = i == depth - 1
        m = mods[i].reshape(bsz + 1, 6, d)
        mod = jnp.stack([jnp.broadcast_to(m[0], (bsz, 6, d)), m[1:]], axis=1)
        if i % 2 == 0:
            w = ab_w_in[j]
            zc = jnp.zeros((d, LANES - SSD_HEADS), w.dtype)
            o_dt = SSD_WIDTH + SSD_CONV_CH
            o_q = o_dt + 2 * SSD_HEADS
            w_dt = jnp.concatenate([w[:, o_dt:o_dt + SSD_HEADS], zc, w[:, o_dt + SSD_HEADS:o_q], zc], axis=1)
            ws = (w[:, :o_dt].astype(BF16), w_dt.astype(BF16), w[:, o_q:].astype(BF16))
            plain = ((0, SSD_WIDTH), (SSD_WIDTH, SSD_CONV_CH), (o_dt, 2 * LANES))
            z, xbc, dt, q, k, v = _inproj(srcs, lc, g_mix[i].reshape(1, d), mod, ws,
                                          att_q_g[j].reshape(1, HEAD_DIM), att_k_g[j].reshape(1, HEAD_DIM), cc, ss,
                                          ctx_len=ctx_len, plain=plain, q0=o_dt + 2 * LANES, qk_norm=True)
            ya = _ssd(xbc, dt, z, ssd_conv_w[j], ssd_conv_b[j], ssd_dt_bias[j], ssd_a_log[j], ssd_d[j],
                      ssd_norm_g[j], expand, ctx_len=ctx_len)
            yb = _attn_dense(q, k, v, ctx_len=ctx_len, tq=ATT_TQ, tk=_pick_tile(lc, ATT_TK_PREFS))
            w_out = ab_w_out[j]
        else:
            plain = ((0, LRU_WIDTH), (LRU_WIDTH, LRU_WIDTH))
            gate, xr, q, k, v = _inproj(srcs, lc, g_mix[i].reshape(1, d), mod, (cd_w_in[j].astype(BF16),),
                                        ones_h, ones_h, cc, ss,
                                        ctx_len=ctx_len, plain=plain, q0=2 * LRU_WIDTH, qk_norm=False)
            xr = _dwconv(xr, lru_conv_w[j], lru_conv_b[j], ctx_len=ctx_len, silu=False)
            ya = _lru(xr, gate, _lru_slabs(lru_w_a[j]), _lru_slabs(lru_w_x[j]), lru_b_a[j], lru_b_x[j], lru_lam[j],
                      ctx_len=ctx_len)
            yb = _attn_window(q, k, v, swa_sink[j], ctx_len=ctx_len, tq=ATT_TQ)
            w_out = cd_w_out[j]
        xs = _outproj(ya, yb, srcs, mod, w_out.astype(BF16), ctx_len=ctx_len, latent_only=last)
        wr = jnp.concatenate([moe_w_grp[i], moe_w_rt[i],
                              jnp.zeros((d, LANES - MOE_GROUPS - N_EXPERTS), F32)], axis=1)
        br = jnp.concatenate([moe_b_grp[i], moe_b_rt[i],
                              jnp.zeros((LANES - MOE_GROUPS - N_EXPERTS,), F32)]).reshape(1, LANES)
        wr_hi = wr.astype(BF16)
        wr = jnp.concatenate([wr_hi, (wr - wr_hi.astype(F32)).astype(BF16)], axis=1)
        xs = _moe(xs, g_ffn[i].reshape(1, d), mod, wr, br, w1_b, w3_b, w2_b, i,
                  g_final if last else None, ctx_len=0 if last else ctx_len)
        srcs = (xs,)
    return xs
```

```python
import functools
import math

import numpy as np
import jax
import jax.numpy as jnp
from jax import lax
from jax.experimental import pallas as pl
from jax.experimental.pallas import tpu as pltpu

F32 = jnp.float32
BF16 = jnp.bfloat16

D_MODEL = 1024
GRID_W = 64
EPS = 1e-6
HEAD_DIM = 128
ATT_HEADS = 8
ATT_KV_HEADS = 2
GQA_REP = ATT_HEADS // ATT_KV_HEADS
ATT_WIDTH = ATT_HEADS * HEAD_DIM
KV_WIDTH = ATT_KV_HEADS * HEAD_DIM
ATT_SCALE = HEAD_DIM ** -0.5
LOG2E = math.log2(math.e)
ROPE_THETA = 10000.0
WINDOW = 128
SSD_HEADS = 16
SSD_HEAD_DIM = 64
SSD_WIDTH = SSD_HEADS * SSD_HEAD_DIM
SSD_GROUPS = 2
SSD_STATE = 64
SSD_CHUNK = 128
SSD_CONV_CH = SSD_WIDTH + 2 * SSD_GROUPS * SSD_STATE
LRU_WIDTH = 1024
LRU_BLOCKS = 16
LRU_BLOCK_DIM = LRU_WIDTH // LRU_BLOCKS
LRU_C = 8.0
CONV_W = 5
MOE_GROUPS = 4
EXPERTS_PER_GROUP = 4
N_EXPERTS = MOE_GROUPS * EXPERTS_PER_GROUP
D_EXPERT = 256

LANES = 128
SUBLANES = 8
MXU_TILE = 256
LRU_SLAB = MXU_TILE
V7X_VMEM_BYTES = 64 * 1024 * 1024
VMEM_LIMIT = V7X_VMEM_BYTES - 8 * 1024 * 1024
MOD_COL_TILE = 1536
ROW_TILE = 256
MOE_TILE_PREFS = (768, 512, 256)
ATT_TQ = 256
ATT_TK_PREFS = (2816, 1408, 768, 256)


def _params(sem):
    return pltpu.CompilerParams(dimension_semantics=sem, vmem_limit_bytes=VMEM_LIMIT)


def _pick_tile(n, prefs):
    for t in prefs:
        if n % t == 0:
            return t
    raise ValueError(f"no tile for {n}")


def _sigmoid(x):
    return 0.5 * jnp.tanh(0.5 * x) + 0.5


def _softplus(x):
    return jnp.maximum(x, 0.0) + jnp.log1p(jnp.exp(-jnp.abs(x)))


def _split_bf16(v, n):
    parts = []
    r = v
    for _ in range(n):
        p = r.astype(BF16)
        parts.append(p)
        r = r - p.astype(F32)
    return parts


def _dot_nt(a, b):
    return lax.dot_general(a, b, (((1,), (1,)), ((), ())), preferred_element_type=F32)


def _rms(x):
    return x * lax.rsqrt(jnp.mean(x * x, axis=-1, keepdims=True) + EPS)


def _row_select(row0, tm, ctx_len, mod_ref, k):
    if ctx_len == 0:
        return mod_ref[0, 1, k:k + 1, :]
    rows = row0 + lax.broadcasted_iota(jnp.int32, (tm, 1), 0)
    return jnp.where(rows < ctx_len, mod_ref[0, 0, k:k + 1, :], mod_ref[0, 1, k:k + 1, :])


def _rms_mod(x, g, row0, ctx_len, mod_ref, k_shift, k_scale):
    tm = x.shape[0]
    sh = _row_select(row0, tm, ctx_len, mod_ref, k_shift)
    sc = _row_select(row0, tm, ctx_len, mod_ref, k_scale)
    return _rms(x) * g * (1.0 + sc) + sh


def _scan_order(s, n_ctx, n_all):
    return jnp.where(s < n_ctx, n_ctx - 1 - s, n_all + n_ctx - 1 - s)


def _row_specs(tm, d, ct, split):
    if split:
        return [pl.BlockSpec((1, tm, d), lambda b, i: (b, jnp.minimum(i, ct - 1), 0)),
                pl.BlockSpec((1, tm, d), lambda b, i: (b, jnp.maximum(i - ct, 0), 0))]
    return [pl.BlockSpec((1, tm, d), lambda b, i: (b, i, 0))]


def _load_rows(refs, i, ct):
    if len(refs) == 2:
        return jnp.where(i < ct, refs[0][0], refs[1][0])
    return refs[0][0]


def _mod_kernel(ct_ref, w_ref, b_ref, o_ref, *, rows):
    w = w_ref[0]
    outs = []
    for r in range(rows):
        c = ct_ref[:, r:r + 1]
        act = c * _sigmoid(c)
        outs.append(jnp.sum(w * act, axis=0, keepdims=True) + b_ref[0])
    o_ref[0] = jnp.concatenate(outs, axis=0)


def _modulation(c, c_ctx, w_mod, b_mod):
    depth, d, n = w_mod.shape
    rows = c.shape[0] + 1
    ct = jnp.concatenate([c_ctx[None, :], c], axis=0).T
    tn = MOD_COL_TILE
    return pl.pallas_call(
        functools.partial(_mod_kernel, rows=rows),
        grid=(depth, n // tn),
        in_specs=[pl.BlockSpec((d, rows), lambda i, j: (0, 0)),
                  pl.BlockSpec((1, d, tn), lambda i, j: (i, 0, j)),
                  pl.BlockSpec((1, 1, tn), lambda i, j: (i, 0, j))],
        out_specs=pl.BlockSpec((1, rows, tn), lambda i, j: (i, 0, j)),
        out_shape=jax.ShapeDtypeStruct((depth, rows, n), F32),
        compiler_params=_params(("arbitrary", "arbitrary")),
        name="modulation",
    )(ct, w_mod, b_mod.reshape(depth, 1, n))


def _inproj_kernel(*refs, tm, ctx_len, plain, q0, qk_norm, n_src, n_w):
    x_refs = refs[:n_src]
    g_ref, mod_ref = refs[n_src:n_src + 2]
    w_refs = refs[n_src + 2:n_src + 2 + n_w]
    qg_ref, kg_ref, cc_ref, ss_ref = refs[n_src + 2 + n_w:n_src + 6 + n_w]
    out_refs = refs[n_src + 6 + n_w:]
    i = pl.program_id(1)
    x = _load_rows(x_refs, i, ctx_len // tm)
    h = _rms_mod(x, g_ref[...], i * tm, ctx_len, mod_ref, 0, 1).astype(BF16)
    res = jnp.concatenate([jnp.dot(h, w_ref[...], preferred_element_type=F32) for w_ref in w_refs], axis=1)
    for ref, (c0, cw) in zip(out_refs[:len(plain)], plain):
        ref[0] = res[:, c0:c0 + cw]
    q_ref, k_ref, v_ref = out_refs[len(plain):]
    cc = cc_ref[...]
    ss = ss_ref[...]

    def head(t, gain, scale):
        if qk_norm:
            t = _rms(t) * gain
        t = t * cc + pltpu.roll(t, HEAD_DIM // 2, 1) * ss
        return (t * scale).astype(BF16)

    for hh in range(ATT_HEADS):
        c0 = q0 + hh * HEAD_DIM
        q_ref[0, :, hh * HEAD_DIM:(hh + 1) * HEAD_DIM] = head(res[:, c0:c0 + HEAD_DIM], qg_ref[...],
                                                              ATT_SCALE * LOG2E)
    k0 = q0 + ATT_WIDTH
    for hh in range(ATT_KV_HEADS):
        c0 = k0 + hh * HEAD_DIM
        k_ref[0, :, hh * HEAD_DIM:(hh + 1) * HEAD_DIM] = head(res[:, c0:c0 + HEAD_DIM], kg_ref[...], 1.0)
    v0 = k0 + KV_WIDTH
    v_ref[0] = res[:, v0:v0 + KV_WIDTH].astype(BF16)


def _inproj(srcs, lc, g, mod, ws, qg, kg, cc, ss, *, ctx_len, plain, q0, qk_norm, tm=ROW_TILE):
    bsz, _, d = srcs[0].shape
    row = lambda b, i: (b, i, 0)
    const = lambda b, i: (0, 0)
    out_shape = [jax.ShapeDtypeStruct((bsz, lc, cw), F32) for _, cw in plain]
    out_specs = [pl.BlockSpec((1, tm, cw), row) for _, cw in plain]
    for cw in (ATT_WIDTH, KV_WIDTH, KV_WIDTH):
        out_shape.append(jax.ShapeDtypeStruct((bsz, lc, cw), BF16))
        out_specs.append(pl.BlockSpec((1, tm, cw), row))
    return pl.pallas_call(
        functools.partial(_inproj_kernel, tm=tm, ctx_len=ctx_len, plain=plain, q0=q0, qk_norm=qk_norm,
                          n_src=len(srcs), n_w=len(ws)),
        grid=(bsz, lc // tm),
        in_specs=_row_specs(tm, d, ctx_len // tm, len(srcs) == 2) + [
            pl.BlockSpec((1, d), const),
            pl.BlockSpec((1, 2, 6, d), lambda b, i: (b, 0, 0, 0))] + [
            pl.BlockSpec(w.shape, const) for w in ws] + [
            pl.BlockSpec((1, HEAD_DIM), const),
            pl.BlockSpec((1, HEAD_DIM), const),
            pl.BlockSpec((tm, HEAD_DIM), lambda b, i: (i, 0)),
            pl.BlockSpec((tm, HEAD_DIM), lambda b, i: (i, 0))],
        out_specs=out_specs,
        out_shape=out_shape,
        compiler_params=_params(("parallel", "arbitrary")),
        name="inproj",
    )(*srcs, g, mod, *ws, qg, kg, cc, ss)


def _conv_rows(prev, cur, nxt, w_ref, b_ref, seg_start, seg_end, silu):
    tm, c = cur.shape
    prev = jnp.where(seg_start, 0.0, prev)
    nxt = jnp.where(seg_end, 0.0, nxt)
    ng = tm // SUBLANES
    ext = jnp.concatenate([prev, cur, nxt], axis=0).reshape(ng + 2, SUBLANES, c)
    pos8 = lax.broadcasted_iota(jnp.int32, (1, SUBLANES, 1), 1)
    pad = CONV_W // 2
    acc = b_ref[...] + cur * w_ref[pad:pad + 1, :]
    for k in range(CONV_W):
        d = k - pad
        if d == 0:
            continue
        r = pltpu.roll(ext, (-d) % SUBLANES, 1)
        if d < 0:
            tap = jnp.where(pos8 >= -d, r[1:ng + 1], r[0:ng])
        else:
            tap = jnp.where(pos8 < SUBLANES - d, r[1:ng + 1], r[2:ng + 2])
        acc = acc + tap.reshape(tm, c) * w_ref[k:k + 1, :]
    if silu:
        acc = acc * _sigmoid(acc)
    return acc


def _dwconv_kernel(prev_ref, cur_ref, next_ref, w_ref, b_ref, o_ref, *, nt, ct, silu):
    i = pl.program_id(1)
    seg_start = jnp.logical_or(i == 0, i == ct)
    seg_end = jnp.logical_or(i == ct - 1, i == nt - 1)
    o_ref[0] = _conv_rows(prev_ref[0], cur_ref[0], next_ref[0], w_ref, b_ref, seg_start, seg_end, silu)


def _dwconv(u, w, b, *, ctx_len, silu, tm=ROW_TILE):
    bsz, lc, c = u.shape
    nt = lc // tm
    hb = tm // SUBLANES
    nhb = lc // SUBLANES
    return pl.pallas_call(
        functools.partial(_dwconv_kernel, nt=nt, ct=ctx_len // tm, silu=silu),
        grid=(bsz, nt),
        in_specs=[pl.BlockSpec((1, SUBLANES, c), lambda b_, i: (b_, jnp.maximum(i * hb - 1, 0), 0)),
                  pl.BlockSpec((1, tm, c), lambda b_, i: (b_, i, 0)),
                  pl.BlockSpec((1, SUBLANES, c), lambda b_, i: (b_, jnp.minimum((i + 1) * hb, nhb - 1), 0)),
                  pl.BlockSpec((CONV_W, c), lambda b_, i: (0, 0)),
                  pl.BlockSpec((1, c), lambda b_, i: (0, 0))],
        out_specs=pl.BlockSpec((1, tm, c), lambda b_, i: (b_, i, 0)),
        out_shape=jax.ShapeDtypeStruct((bsz, lc, c), F32),
        compiler_params=_params(("parallel", "arbitrary")),
        name="dwconv",
    )(u, u, u, w, b.reshape(1, c))


def _expand_heads(v, e_ref):
    hi, lo = _split_bf16(v, 2)
    e = e_ref[...]
    return (jnp.dot(hi, e, preferred_element_type=F32) + jnp.dot(lo, e, preferred_element_type=F32))


def _ssd_chunk(xbc, dt_raw, dtb, alog, e_ref, h_prev, reverse):
    q = SSD_CHUNK
    heads_per_group = SSD_HEADS // SSD_GROUPS
    x = xbc[:, :SSD_WIDTH]
    bm = xbc[:, SSD_WIDTH:SSD_WIDTH + LANES]
    cm = xbc[:, SSD_WIDTH + LANES:]
    dt = _softplus(dt_raw + dtb)
    da = dt * (-jnp.exp(alog) * LOG2E)
    ii = lax.broadcasted_iota(jnp.int32, (q, q), 0)
    jj = lax.broadcasted_iota(jnp.int32, (q, q), 1)
    causal = (jj >= ii) if reverse else (jj <= ii)
    mask_b = jnp.where(causal, 1.0, 0.0).astype(BF16)
    da_t = da.T
    dt_t = dt.T
    acs = sum(jnp.dot(mask_b, p, preferred_element_type=F32) for p in _split_bf16(da, 3))
    acs_t = sum(_dot_nt(p, mask_b) for p in _split_bf16(da_t, 3))
    total = jnp.sum(da, axis=0, keepdims=True)
    eacs = jnp.exp2(acs)

    bm_b = bm.astype(BF16)
    lane = lax.broadcasted_iota(jnp.int32, (1, LANES), 1)
    in_group = [jnp.logical_and(lane >= g * SSD_STATE, lane < (g + 1) * SSD_STATE) for g in range(SSD_GROUPS)]
    cb = [_dot_nt(jnp.where(in_group[g], cm, 0.0).astype(BF16), bm_b) for g in range(SSD_GROUPS)]

    x_b = x.astype(BF16)
    h_b = h_prev.astype(BF16)
    pairs = []
    for hp in range(SSD_HEADS // 2):
        rhs = jnp.concatenate([x_b[:, hp * LANES:(hp + 1) * LANES], h_b[:, hp * LANES:(hp + 1) * LANES]], axis=0)
        outs = []
        for t in range(2):
            h = 2 * hp + t
            seg = acs[:, h:h + 1] - acs_t[h:h + 1, :]
            dec = jnp.exp2(jnp.where(causal, seg, -jnp.inf))
            w = cb[h // heads_per_group] * dec * dt_t[h:h + 1, :]
            lhs = jnp.concatenate([w.astype(BF16), (cm * eacs[:, h:h + 1]).astype(BF16)], axis=1)
            outs.append(jnp.dot(lhs, rhs, preferred_element_type=F32))
        pairs.append(jnp.where(lane < SSD_HEAD_DIM, outs[0], outs[1]))
    y = jnp.concatenate(pairs, axis=1)

    to_end = jnp.exp2(total - acs) * dt
    xw = (x * _expand_heads(to_end, e_ref)).astype(BF16)
    st = jnp.dot(bm.T.astype(BF16), xw, preferred_element_type=F32)
    row_g0 = lax.broadcasted_iota(jnp.int32, st.shape, 0) < SSD_STATE
    col_g0 = lax.broadcasted_iota(jnp.int32, st.shape, 1) < heads_per_group * SSD_HEAD_DIM
    chunk_decay = _expand_heads(jnp.broadcast_to(jnp.exp2(total), (SUBLANES, LANES)), e_ref)[0:1]
    return y, h_prev * chunk_decay + jnp.where(row_g0 == col_g0, st, 0.0)


def _ssd_bwd_kernel(prev_ref, cur_ref, next_ref, cw_ref, cb_ref, dt_ref, dtb_ref, alog_ref, e_ref,
                    yb_ref, xc_ref, hst_ref, *, nbx, nb, bsz):
    s = pl.program_id(0)
    blk = _scan_order(s, nbx, nb)

    @pl.when(s == 0)
    def _():
        hst_ref[...] = jnp.zeros_like(hst_ref)

    seg_start = jnp.logical_or(blk == 0, blk == nbx)
    seg_end = jnp.logical_or(blk == nbx - 1, blk == nb - 1)
    xbc, h = [], []
    for bb in range(bsz):
        xbc.append(_conv_rows(prev_ref[bb], cur_ref[bb], next_ref[bb], cw_ref, cb_ref, seg_start, seg_end, True))
        xc_ref[bb] = xbc[bb]
        h.append(hst_ref[bb])
    for sub in reversed(range(cur_ref.shape[1] // SSD_CHUNK)):
        rows = slice(sub * SSD_CHUNK, (sub + 1) * SSD_CHUNK)
        for bb in range(bsz):
            y, h[bb] = _ssd_chunk(xbc[bb][rows], dt_ref[bb, rows, :], dtb_ref[...], alog_ref[...], e_ref, h[bb], True)
            yb_ref[bb, rows, :] = y
    for bb in range(bsz):
        hst_ref[bb] = h[bb]


def _ssd_fwd_kernel(xc_ref, dt_ref, dtb_ref, alog_ref, e_ref, yb_ref, z_ref, dsk_ref, ng_ref,
                    o_ref, hst_ref, *, bsz):
    @pl.when(pl.program_id(0) == 0)
    def _():
        hst_ref[...] = jnp.zeros_like(hst_ref)

    h = [hst_ref[bb] for bb in range(bsz)]
    for sub in range(xc_ref.shape[1] // SSD_CHUNK):
        rows = slice(sub * SSD_CHUNK, (sub + 1) * SSD_CHUNK)
        for bb in range(bsz):
            xbc = xc_ref[bb, rows, :]
            y, h[bb] = _ssd_chunk(xbc, dt_ref[bb, rows, :], dtb_ref[...], alog_ref[...], e_ref, h[bb], False)
            z = z_ref[bb, rows, :]
            y = (y + yb_ref[bb, rows, :] + dsk_ref[...] * xbc[:, :SSD_WIDTH]) * (z * _sigmoid(z))
            o_ref[bb, rows, :] = (_rms(y) * ng_ref[...]).astype(BF16)
    for bb in range(bsz):
        hst_ref[bb] = h[bb]


def _ssd(xbc_raw, dt, z, conv_w, conv_b, dt_bias, a_log, d_skip, norm_g, expand, *, ctx_len, q=ROW_TILE):
    bsz, lc, cch = xbc_raw.shape
    w = SSD_WIDTH
    nb = lc // q
    nbx = ctx_len // q
    hb = q // SUBLANES
    nhb = lc // SUBLANES
    pad = lambda t: jnp.pad(t, (0, LANES - SSD_HEADS)).reshape(1, LANES)
    const = lambda s: (0, 0)
    rev = lambda s: _scan_order(s, nbx, nb)
    state = [pltpu.VMEM((bsz, SSD_GROUPS * SSD_STATE, w), F32)]
    yb, xc = pl.pallas_call(
        functools.partial(_ssd_bwd_kernel, nbx=nbx, nb=nb, bsz=bsz),
        grid=(nb,),
        in_specs=[pl.BlockSpec((bsz, SUBLANES, cch), lambda s: (0, jnp.maximum(rev(s) * hb - 1, 0), 0)),
                  pl.BlockSpec((bsz, q, cch), lambda s: (0, rev(s), 0)),
                  pl.BlockSpec((bsz, SUBLANES, cch), lambda s: (0, jnp.minimum((rev(s) + 1) * hb, nhb - 1), 0)),
                  pl.BlockSpec((CONV_W, cch), const),
                  pl.BlockSpec((1, cch), const),
                  pl.BlockSpec((bsz, q, LANES), lambda s: (0, rev(s), 1)),
                  pl.BlockSpec((1, LANES), const),
                  pl.BlockSpec((1, LANES), const),
                  pl.BlockSpec((LANES, w), const)],
        out_specs=[pl.BlockSpec((bsz, q, w), lambda s: (0, rev(s), 0)),
                   pl.BlockSpec((bsz, q, cch), lambda s: (0, rev(s), 0))],
        out_shape=[jax.ShapeDtypeStruct((bsz, lc, w), F32), jax.ShapeDtypeStruct((bsz, lc, cch), F32)],
        scratch_shapes=state,
        compiler_params=_params(("arbitrary",)),
        name="ssd_bwd",
    )(xbc_raw, xbc_raw, xbc_raw, conv_w, conv_b.reshape(1, cch), dt, pad(dt_bias[1]), pad(a_log[1]), expand)
    row = lambda cols: pl.BlockSpec((bsz, q, cols), lambda s: (0, s, 0))
    return pl.pallas_call(
        functools.partial(_ssd_fwd_kernel, bsz=bsz),
        grid=(nb,),
        in_specs=[row(cch), row(LANES), pl.BlockSpec((1, LANES), const), pl.BlockSpec((1, LANES), const),
                  pl.BlockSpec((LANES, w), const), row(w), row(w), pl.BlockSpec((1, w), const),
                  pl.BlockSpec((1, w), const)],
        out_specs=row(w),
        out_shape=jax.ShapeDtypeStruct((bsz, lc, w), BF16),
        scratch_shapes=state,
        compiler_params=_params(("arbitrary",)),
        name="ssd_fwd",
    )(xc, dt, pad(dt_bias[0]), pad(a_log[0]), expand, yb, z,
      jnp.repeat(d_skip, SSD_HEAD_DIM).reshape(1, w), norm_g.reshape(1, w))


def _stack_heads(q):
    return jnp.concatenate([q[:, r * HEAD_DIM:(r + 1) * HEAD_DIM] for r in range(q.shape[1] // HEAD_DIM)], axis=0)


def _lane_parts(s):
    return [s[:, c * LANES:(c + 1) * LANES] for c in range(s.shape[1] // LANES)]


def _softmax_parts(parts, sink=None):
    pm = parts[0]
    for t in parts[1:]:
        pm = jnp.maximum(pm, t)
    m = jnp.max(pm, axis=-1, keepdims=True)
    if sink is not None:
        m = jnp.maximum(m, sink)
    mb = jnp.broadcast_to(m, pm.shape)
    ps = [jnp.exp2(t - mb) for t in parts]
    lsum = ps[0]
    for t in ps[1:]:
        lsum = lsum + t
    l = jnp.sum(lsum, axis=-1, keepdims=True)
    if sink is not None:
        l = l + jnp.exp2(sink - m)
    return jnp.concatenate([t.astype(BF16) for t in ps], axis=1), l


def _store_heads(o_ref, o, tq, col0=0):
    for r in range(GQA_REP):
        c0 = col0 + r * HEAD_DIM
        o_ref[0, :, c0:c0 + HEAD_DIM] = o[r * tq:(r + 1) * tq].astype(BF16)


def _attn_dense_kernel(q_ref, k_ref, v_ref, o_ref, m_scr, l_scr, acc_scr, *, tq, tk, ctx_len, lk):
    i = pl.program_id(1)
    gw = GQA_REP * HEAD_DIM
    groups = range(ATT_KV_HEADS)
    q4 = [_stack_heads(q_ref[0, :, g * gw:(g + 1) * gw]) for g in groups]
    kcol = [slice(g * HEAD_DIM, (g + 1) * HEAD_DIM) for g in groups]

    @pl.when(i < ctx_len // tq)
    def _():
        for g in groups:
            p, l = _softmax_parts(_lane_parts(_dot_nt(q4[g], k_ref[0, 0:ctx_len, kcol[g]])))
            o = jnp.dot(p, v_ref[0, 0:ctx_len, kcol[g]], preferred_element_type=F32) / l
            _store_heads(o_ref, o, tq, col0=g * gw)

    @pl.when(i >= ctx_len // tq)
    def _():
        m_scr[...] = jnp.full_like(m_scr, -jnp.inf)
        l_scr[...] = jnp.zeros_like(l_scr)
        acc_scr[...] = jnp.zeros_like(acc_scr)

        def body(j, carry):
            off = pl.multiple_of(j * tk, tk)
            for g in groups:
                parts = _lane_parts(_dot_nt(q4[g], k_ref[0, pl.ds(off, tk), kcol[g]]))
                pm = parts[0]
                for t in parts[1:]:
                    pm = jnp.maximum(pm, t)
                m_prev = m_scr[g]
                m_new = jnp.maximum(m_prev, jnp.max(pm, axis=-1, keepdims=True))
                alpha = jnp.exp2(m_prev - m_new)
                ps = [jnp.exp2(t - m_new) for t in parts]
                lsum = ps[0]
                for t in ps[1:]:
                    lsum = lsum + t
                l_scr[g] = alpha * l_scr[g] + lsum
                p = jnp.concatenate([t.astype(BF16) for t in ps], axis=1)
                acc_scr[g] = alpha * acc_scr[g] + jnp.dot(p, v_ref[0, pl.ds(off, tk), kcol[g]],
                                                          preferred_element_type=F32)
                m_scr[g] = m_new
            return carry

        lax.fori_loop(0, lk // tk, body, 0)
        for g in groups:
            _store_heads(o_ref, acc_scr[g] / jnp.sum(l_scr[g], axis=-1, keepdims=True), tq, col0=g * gw)


def _attn_dense(q, k, v, *, ctx_len, tq, tk):
    bsz, lc, _ = q.shape
    rows = GQA_REP * tq
    stat = pltpu.VMEM((ATT_KV_HEADS, rows, LANES), F32)
    return pl.pallas_call(
        functools.partial(_attn_dense_kernel, tq=tq, tk=tk, ctx_len=ctx_len, lk=lc),
        grid=(bsz, lc // tq),
        in_specs=[pl.BlockSpec((1, tq, ATT_WIDTH), lambda b, i: (b, i, 0)),
                  pl.BlockSpec((1, lc, KV_WIDTH), lambda b, i: (b, 0, 0)),
                  pl.BlockSpec((1, lc, KV_WIDTH), lambda b, i: (b, 0, 0))],
        out_specs=pl.BlockSpec((1, tq, ATT_WIDTH), lambda b, i: (b, i, 0)),
        out_shape=jax.ShapeDtypeStruct((bsz, lc, ATT_WIDTH), BF16),
        scratch_shapes=[stat, stat, stat],
        compiler_params=_params(("parallel", "arbitrary")),
        name="attn_dense",
    )(q, k, v)


def _attn_win_kernel(q_ref, k_ref, v_ref, sink_ref, bias_ref, o_ref, *, ctx_len, lat_len, tq):
    n = pl.program_id(1)
    band = tq + 2 * WINDOW
    il = n - ctx_len // tq
    lo = jnp.clip(il * tq - WINDOW, 0, lat_len - band)
    start = pl.multiple_of(ctx_len + lo, WINDOW)
    bias = bias_ref[0]
    gw = GQA_REP * HEAD_DIM
    for g in range(ATT_KV_HEADS):
        kc0 = g * HEAD_DIM
        q4 = _stack_heads(q_ref[0, :, g * gw:(g + 1) * gw])
        s_loc = _dot_nt(q4, k_ref[0, pl.ds(start, band), kc0:kc0 + HEAD_DIM]) + bias
        s_ctx = _dot_nt(q4, k_ref[0, 0:ctx_len, kc0:kc0 + HEAD_DIM])
        p, l = _softmax_parts(_lane_parts(s_loc) + _lane_parts(s_ctx), sink_ref[g])
        o = (jnp.dot(p[:, :band], v_ref[0, pl.ds(start, band), kc0:kc0 + HEAD_DIM], preferred_element_type=F32)
             + jnp.dot(p[:, band:], v_ref[0, 0:ctx_len, kc0:kc0 + HEAD_DIM], preferred_element_type=F32)) / l
        _store_heads(o_ref, o, tq, col0=g * gw)


def _attn_window(q, k, v, sink, *, ctx_len, tq):
    bsz, lc, _ = q.shape
    lat_len = lc - ctx_len
    rows = GQA_REP * tq
    assert tq & (tq - 1) == 0 and ctx_len % tq == 0 and lat_len >= tq + 2 * WINDOW
    sink_col = jnp.repeat(sink.reshape(ATT_KV_HEADS, GQA_REP) * LOG2E, tq, axis=1).reshape(ATT_KV_HEADS, rows, 1)
    band = tq + 2 * WINDOW
    ct = ctx_len // tq
    nlt = lat_len // tq
    r = np.arange(tq)[:, None]
    c = np.arange(band)[None, :]
    masks = [np.abs(c - off - r) <= WINDOW for off in (0, WINDOW, band - tq)] + [np.zeros((tq, band), bool)]
    bias = jnp.asarray(np.tile(np.where(np.stack(masks), 0.0, -np.inf), (1, GQA_REP, 1)), F32)

    def variant(n):
        il = n - ct
        return jnp.where(il < 0, 3, jnp.where(il == 0, 0, jnp.where(il == nlt - 1, 2, 1)))

    return pl.pallas_call(
        functools.partial(_attn_win_kernel, ctx_len=ctx_len, lat_len=lat_len, tq=tq),
        grid=(bsz, lc // tq),
        in_specs=[pl.BlockSpec((1, tq, ATT_WIDTH), lambda b, n: (b, n, 0)),
                  pl.BlockSpec((1, lc, KV_WIDTH), lambda b, n: (b, 0, 0)),
                  pl.BlockSpec((1, lc, KV_WIDTH), lambda b, n: (b, 0, 0)),
                  pl.BlockSpec((ATT_KV_HEADS, rows, 1), lambda b, n: (0, 0, 0)),
                  pl.BlockSpec((1, rows, band), lambda b, n: (variant(n), 0, 0))],
        out_specs=pl.BlockSpec((1, tq, ATT_WIDTH), lambda b, n: (b, n, 0)),
        out_shape=jax.ShapeDtypeStruct((bsz, lc, ATT_WIDTH), BF16),
        compiler_params=_params(("parallel", "arbitrary")),
        name="attn_window",
    )(q, k, v, sink_col, bias)


def _lru_scan_rows(a, b, carry, reverse):
    t, w = a.shape
    ngroups = t // SUBLANES
    a = a.reshape(ngroups, SUBLANES, w)
    b = b.reshape(ngroups, SUBLANES, w)
    pos8 = lax.broadcasted_iota(jnp.int32, (1, SUBLANES, 1), 1)
    k = 1
    while k < SUBLANES:
        if reverse:
            keep = pos8 < SUBLANES - k
            sh = SUBLANES - k
        else:
            keep = pos8 >= k
            sh = k
        a_sh = jnp.where(keep, pltpu.roll(a, sh, 1), 1.0)
        b_sh = jnp.where(keep, pltpu.roll(b, sh, 1), 0.0)
        b = b + a * b_sh
        a = a * a_sh
        k *= 2
    a = a.reshape(t, w)
    b = b.reshape(t, w)
    hs = [None] * ngroups
    for g in (range(ngroups - 1, -1, -1) if reverse else range(ngroups)):
        sl = slice(g * SUBLANES, (g + 1) * SUBLANES)
        hg = a[sl] * carry + b[sl]
        carry = hg[0:1] if reverse else hg[SUBLANES - 1:SUBLANES]
        hs[g] = hg
    return jnp.concatenate(hs, axis=0), carry


def _lru_kernel(*refs, reverse):
    x_ref, wa_ref, wx_ref, ba_ref, bx_ref, lam_ref = refs[:6]
    if reverse:
        o_ref, carry_ref = refs[6:]
    else:
        hb_ref, gate_ref, o_ref, carry_ref = refs[6:]
    s = pl.program_id(1)

    @pl.when(s == 0)
    def _():
        carry_ref[...] = jnp.zeros_like(carry_ref)

    x = x_ref[0]
    x_b = x.astype(BF16)
    nslab = LRU_WIDTH // LRU_SLAB

    def gate(w_ref, b_ref):
        pre = jnp.concatenate(
            [jnp.dot(x_b[:, c * LRU_SLAB:(c + 1) * LRU_SLAB], w_ref[0, c], preferred_element_type=F32)
             for c in range(nslab)], axis=1)
        return _sigmoid(pre + b_ref[0])

    r = gate(wa_ref, ba_ref)
    ig = gate(wx_ref, bx_ref)
    log2_base = (-LRU_C * LOG2E) * _softplus(-lam_ref[0])
    a = jnp.exp2(r * log2_base)
    y = 1.0 - a * a
    b = jnp.where(y > 0.0, y * lax.rsqrt(y), 0.0) * (ig * x)
    h, carry = _lru_scan_rows(a, b, carry_ref[...], reverse)
    carry_ref[...] = carry
    if reverse:
        o_ref[0] = h
    else:
        g = gate_ref[0]
        gelu = 0.5 * g * (1.0 + jnp.tanh(math.sqrt(2.0 / math.pi) * (g + 0.044715 * (g * g * g))))
        o_ref[0] = (gelu * (h + hb_ref[0])).astype(BF16)


def _lru(xr, gate, wa, wx, ba, bx, lam, *, ctx_len, t=ROW_TILE):
    bsz, lc, w = xr.shape
    nt = lc // t
    ct = ctx_len // t
    nslab = w // LRU_SLAB
    vec = lambda v: v.reshape(2, 1, w)

    def call(reverse, extra_in, out_dtype):
        tile = (lambda s: _scan_order(s, ct, nt)) if reverse else (lambda s: s)
        d = 1 if reverse else 0
        row = pl.BlockSpec((1, t, w), lambda b, s: (b, tile(s), 0))
        return pl.pallas_call(
            functools.partial(_lru_kernel, reverse=reverse),
            grid=(bsz, nt),
            in_specs=[row,
                      pl.BlockSpec((1, nslab, LRU_SLAB, LRU_SLAB), lambda b, s: (d, 0, 0, 0)),
                      pl.BlockSpec((1, nslab, LRU_SLAB, LRU_SLAB), lambda b, s: (d, 0, 0, 0)),
                      pl.BlockSpec((1, 1, w), lambda b, s: (d, 0, 0)),
                      pl.BlockSpec((1, 1, w), lambda b, s: (d, 0, 0)),
                      pl.BlockSpec((1, 1, w), lambda b, s: (d, 0, 0))] + [row] * len(extra_in),
            out_specs=row,
            out_shape=jax.ShapeDtypeStruct((bsz, lc, w), out_dtype),
            scratch_shapes=[pltpu.VMEM((1, w), F32)],
            compiler_params=_params(("parallel", "arbitrary")),
            name="lru_bwd" if reverse else "lru_fwd",
        )(xr, wa, wx, vec(ba), vec(bx), vec(lam), *extra_in)

    hb = call(True, (), F32)
    return call(False, (hb, gate), BF16)


def _lru_slabs(w):
    per = LRU_SLAB // LRU_BLOCK_DIM
    nslab = LRU_BLOCKS // per
    w = w.reshape(2, nslab, per, LRU_BLOCK_DIM, LRU_BLOCK_DIM)
    eye = jnp.eye(per, dtype=w.dtype)
    full = jnp.einsum('dspij,pq->dspiqj', w, eye)
    return full.reshape(2, nslab, LRU_SLAB, LRU_SLAB).astype(BF16)


def _outproj_kernel(*refs, tm, ctx_len, t0, n_src):
    ya_ref, yb_ref = refs[:2]
    x_refs = refs[2:2 + n_src]
    mod_ref, w_ref, o_ref = refs[2 + n_src:]
    i = pl.program_id(1) + t0
    half = ya_ref.shape[2]
    acc = (jnp.dot(ya_ref[0], w_ref[0:half, :], preferred_element_type=F32)
           + jnp.dot(yb_ref[0], w_ref[half:, :], preferred_element_type=F32))
    gate = _row_select(i * tm, tm, ctx_len, mod_ref, 2)
    o_ref[0] = _load_rows(x_refs, i, ctx_len // tm) + gate * acc


def _outproj(ya, yb, srcs, mod, w, *, ctx_len, latent_only, tm=ROW_TILE):
    bsz, lc, half = ya.shape
    d = w.shape[1]
    ct = ctx_len // tm
    t0 = ct if latent_only else 0
    row = lambda b, i: (b, i + t0, 0)
    if len(srcs) == 2:
        x_specs = [pl.BlockSpec((1, tm, d), lambda b, i: (b, jnp.minimum(i + t0, ct - 1), 0)),
                   pl.BlockSpec((1, tm, d), lambda b, i: (b, jnp.maximum(i + t0 - ct, 0), 0))]
    else:
        x_specs = [pl.BlockSpec((1, tm, d), row)]
    return pl.pallas_call(
        functools.partial(_outproj_kernel, tm=tm, ctx_len=ctx_len, t0=t0, n_src=len(srcs)),
        grid=(bsz, lc // tm - t0),
        in_specs=[pl.BlockSpec((1, tm, half), row), pl.BlockSpec((1, tm, half), row)] + x_specs + [
            pl.BlockSpec((1, 2, 6, d), lambda b, i: (b, 0, 0, 0)),
            pl.BlockSpec((2 * half, d), lambda b, i: (0, 0))],
        out_specs=pl.BlockSpec((1, tm, d), lambda b, i: (b, i, 0)),
        out_shape=jax.ShapeDtypeStruct((bsz, lc - t0 * tm, d), F32),
        compiler_params=_params(("parallel", "arbitrary")),
        name="outproj",
    )(ya, yb, *srcs, mod, w)


def _first_max(vals):
    m = vals[0]
    for v in vals[1:]:
        m = jnp.maximum(m, v)
    hot = []
    taken = None
    for v in vals:
        hit = v == m
        if taken is None:
            hot.append(hit)
            taken = hit
        else:
            hot.append(jnp.logical_and(hit, jnp.logical_not(taken)))
            taken = jnp.logical_or(taken, hit)
    return m, hot


def _moe_kernel(*refs, tm, ctx_len, wins, final):
    x_ref, g_ref, mod_ref, wr_ref, br_ref, w1_ref, w3_ref, w2_ref = refs[:8]
    if final:
        gf_ref = refs[8]
    o_ref, hn_scr, gate_scr, rank_row_scr, rank_col_scr, cnt_ref, acc_scr = refs[-7:]
    i = pl.program_id(1)
    e = pl.program_id(2)

    @pl.when(e == 0)
    def _():
        hn = _rms_mod(x_ref[0], g_ref[...], i * tm, ctx_len, mod_ref, 3, 4)
        hn_scr[...] = hn.astype(BF16)
        h_hi, h_lo = _split_bf16(hn, 2)
        r_hi = jnp.dot(h_hi, wr_ref[...], preferred_element_type=F32)
        r_lo = jnp.dot(h_lo, wr_ref[...], preferred_element_type=F32)
        logits = (r_hi[:, :LANES] + r_hi[:, LANES:]) + (r_lo[:, :LANES] + r_lo[:, LANES:]) + br_ref[...]
        lt = jnp.concatenate([logits[c * LANES:(c + 1) * LANES, :].T for c in range(tm // LANES)], axis=1)
        gl = [lt[j:j + 1, :] for j in range(MOE_GROUPS)]
        gmax, ghot = _first_max(gl)
        gsum = sum(jnp.exp(v - gmax) for v in gl)
        p_grp = 1.0 / gsum
        elg = []
        for j in range(EXPERTS_PER_GROUP):
            v = jnp.zeros_like(gmax)
            for gi in range(MOE_GROUPS):
                c = MOE_GROUPS + gi * EXPERTS_PER_GROUP + j
                v = jnp.where(ghot[gi], lt[c:c + 1, :], v)
            elg.append(v)
        v1, hot1 = _first_max(elg)
        rest = [jnp.where(hh, -jnp.inf, v) for hh, v in zip(hot1, elg)]
        v2, hot2 = _first_max(rest)
        ex = jnp.exp(v2 - v1)
        w1 = p_grp / (1.0 + ex)
        w2 = p_grp * ex / (1.0 + ex)
        own = [jnp.where(hot1[j], w1, 0.0) + jnp.where(hot2[j], w2, 0.0) for j in range(EXPERTS_PER_GROUP)]
        pad_rows = jnp.zeros((LANES - MOE_GROUPS, tm), F32)
        g_t = jnp.concatenate([jnp.where(h_, 1.0, 0.0) for h_ in ghot] + [pad_rows], axis=0)
        t0 = lax.broadcasted_iota(jnp.int32, (tm, tm), 0)
        t1 = lax.broadcasted_iota(jnp.int32, (tm, tm), 1)
        before = jnp.where(t0 < t1, 1.0, 0.0).astype(BF16)
        rank_t = jnp.where(g_t > 0.0, jnp.dot(g_t.astype(BF16), before, preferred_element_type=F32), -1.0)
        rank_row_scr[...] = rank_t[0:SUBLANES]
        for gi in range(MOE_GROUPS):
            cnt_ref[gi] = jnp.sum(g_t[gi:gi + 1, :]).astype(jnp.int32)
            slab_t = jnp.concatenate([jnp.where(ghot[gi], own[j], 0.0) for j in range(EXPERTS_PER_GROUP)]
                                     + [pad_rows], axis=0)
            for c in range(tm // LANES):
                rows_c = slice(c * LANES, (c + 1) * LANES)
                slab = slab_t[:, rows_c].T
                hi, lo = _split_bf16(slab, 2)
                gate_scr[gi, rows_c, 0:LANES] = hi
                gate_scr[gi, rows_c, LANES:] = lo
                if gi == 0:
                    rank_c = rank_t[:, rows_c].T
                    for gj in range(MOE_GROUPS):
                        rank_col_scr[gj, rows_c, :] = rank_c[:, gj:gj + 1]
        acc_scr[...] = jnp.zeros_like(acc_scr)

    cnt = cnt_ref[e]
    rank_row = rank_row_scr[pl.ds(e, 1), :]
    rank_col = rank_col_scr[e]

    def window(first, w_):
        base = first.astype(F32)
        slot_r = lax.broadcasted_iota(jnp.int32, (w_, 1), 0).astype(F32)
        slot_c = lax.broadcasted_iota(jnp.int32, (1, w_), 1).astype(F32)
        sel = jnp.where(rank_row - base == slot_r, 1.0, 0.0).astype(BF16)
        sel_t = jnp.where(rank_col - base == slot_c, 1.0, 0.0).astype(BF16)
        xs = jnp.dot(sel, hn_scr[...], preferred_element_type=F32).astype(BF16)
        gs = jnp.dot(sel, gate_scr[e], preferred_element_type=F32)
        gsel = gs[:, :LANES] + gs[:, LANES:]
        yw = None
        for kk in range(EXPERTS_PER_GROUP):
            a = jnp.dot(xs, w1_ref[0, kk], preferred_element_type=F32)
            u = jnp.dot(xs, w3_ref[0, kk], preferred_element_type=F32)
            hid = (a * _sigmoid(a)) * u * gsel[:, kk:kk + 1]
            part = jnp.dot(hid.astype(BF16), w2_ref[0, kk], preferred_element_type=F32)
            yw = part if yw is None else yw + part
        acc_scr[...] += jnp.dot(sel_t, yw.astype(BF16), preferred_element_type=F32)

    w_top = wins[-1]
    lower = 0
    for w_ in wins[:-1]:
        pl.when(jnp.logical_and(cnt > lower, cnt <= w_))(functools.partial(window, jnp.int32(0), w_))
        lower = w_

    @pl.when(cnt > lower)
    def _():
        def body(wi, carry):
            window(wi * w_top, w_top)
            return carry
        lax.fori_loop(0, (cnt + (w_top - 1)) // w_top, body, 0)

    @pl.when(e == MOE_GROUPS - 1)
    def _():
        gate = _row_select(i * tm, tm, ctx_len, mod_ref, 5)
        out = x_ref[0] + gate * acc_scr[...]
        if final:
            out = _rms(out) * gf_ref[...]
        o_ref[0] = out


def _moe(xx, g, mod, wr, br, w1, w3, w2, layer, g_final, *, ctx_len):
    bsz, rows, d = xx.shape
    tm = _pick_tile(rows, MOE_TILE_PREFS)
    mid = -(-(tm // MOE_GROUPS) // LANES) * LANES
    wins = (mid, mid + LANES, mid + 2 * LANES)
    eb = EXPERTS_PER_GROUP
    row = lambda b, i, e: (b, i, 0)
    const = lambda b, i, e: (0, 0)
    final = g_final is not None
    return pl.pallas_call(
        functools.partial(_moe_kernel, tm=tm, ctx_len=ctx_len, wins=wins, final=final),
        grid=(bsz, rows // tm, MOE_GROUPS),
        in_specs=[pl.BlockSpec((1, tm, d), row),
                  pl.BlockSpec((1, d), const),
                  pl.BlockSpec((1, 2, 6, d), lambda b, i, e: (b, 0, 0, 0)),
                  pl.BlockSpec((d, 2 * LANES), const),
                  pl.BlockSpec((1, LANES), const),
                  pl.BlockSpec((1, eb, d, D_EXPERT), lambda b, i, e: (layer, e, 0, 0)),
                  pl.BlockSpec((1, eb, d, D_EXPERT), lambda b, i, e: (layer, e, 0, 0)),
                  pl.BlockSpec((1, eb, D_EXPERT, d), lambda b, i, e: (layer, e, 0, 0))]
                 + ([pl.BlockSpec((1, d), const)] if final else []),
        out_specs=pl.BlockSpec((1, tm, d), row),
        out_shape=jax.ShapeDtypeStruct((bsz, rows, d), F32),
        scratch_shapes=[pltpu.VMEM((tm, d), BF16),
                        pltpu.VMEM((MOE_GROUPS, tm, 2 * LANES), BF16),
                        pltpu.VMEM((SUBLANES, tm), F32),
                        pltpu.VMEM((MOE_GROUPS, tm, 1), F32),
                        pltpu.SMEM((MOE_GROUPS,), jnp.int32),
                        pltpu.VMEM((tm, d), F32)],
        compiler_params=_params(("parallel", "arbitrary", "arbitrary")),
        name="moe",
    )(xx, g, mod, wr, br, w1, w3, w2, *((g_final.reshape(1, d),) if final else ()))


def _rope_tables(ctx_len, lat_len):
    rows = lat_len // GRID_W
    row = np.repeat(np.arange(rows), GRID_W).astype(np.float64)
    col = np.tile(np.arange(GRID_W), rows).astype(np.float64)
    n_freq = HEAD_DIM // 4
    inv = ROPE_THETA ** (-np.arange(n_freq, dtype=np.float64) / n_freq)
    ang = np.concatenate([row[:, None] * inv, col[:, None] * inv], axis=-1)
    cos, sin = np.cos(ang), np.sin(ang)
    cc = np.concatenate([np.ones((ctx_len, HEAD_DIM)), np.concatenate([cos, cos], axis=-1)], axis=0)
    ss = np.concatenate([np.zeros((ctx_len, HEAD_DIM)), np.concatenate([-sin, sin], axis=-1)], axis=0)
    return jnp.asarray(cc, F32), jnp.asarray(ss, F32)


def kernel(x, c, ctx, c_ctx, w_mod, b_mod, g_mix, g_ffn, moe_w_grp, moe_b_grp, moe_w_rt, moe_b_rt, moe_w1, moe_w3, moe_w2, ab_w_in, ab_w_out, ssd_conv_w, ssd_conv_b, ssd_dt_bias, ssd_a_log, ssd_d, ssd_norm_g, att_q_g, att_k_g, cd_w_in, cd_w_out, lru_conv_w, lru_conv_b, lru_w_a, lru_b_a, lru_w_x, lru_b_x, lru_lam, swa_sink, g_final):
    bsz, lat_len, d = x.shape
    ctx_len = ctx.shape[1]
    depth = w_mod.shape[0]
    lc = ctx_len + lat_len
    assert d == D_MODEL and ctx_len % ROW_TILE == 0 and lat_len % ROW_TILE == 0 and ctx_len > 0

    cc, ss = _rope_tables(ctx_len, lat_len)
    mods = _modulation(c, c_ctx, w_mod, b_mod)
    expand = jnp.asarray(np.arange(SSD_WIDTH)[None, :] // SSD_HEAD_DIM == np.arange(LANES)[:, None], BF16)
    ones_h = jnp.ones((1, HEAD_DIM), F32)
    w1_b, w3_b, w2_b = moe_w1.astype(BF16), moe_w3.astype(BF16), moe_w2.astype(BF16)
    srcs = (ctx, x)

    for i in range(depth):
        j = i // 2
        last = i == depth - 1
        m = mods[i].reshape(bsz + 1, 6, d)
        mod = jnp.stack([jnp.broadcast_to(m[0], (bsz, 6, d)), m[1:]], axis=1)
        if i % 2 == 0:
            w = ab_w_in[j]
            zc = jnp.zeros((d, LANES - SSD_HEADS), w.dtype)
            o_dt = SSD_WIDTH + SSD_CONV_CH
            o_q = o_dt + 2 * SSD_HEADS
            w_dt = jnp.concatenate([w[:, o_dt:o_dt + SSD_HEADS], zc, w[:, o_dt + SSD_HEADS:o_q], zc], axis=1)
            ws = (w[:, :o_dt].astype(BF16), w_dt.astype(BF16), w[:, o_q:].astype(BF16))
            plain = ((0, SSD_WIDTH), (SSD_WIDTH, SSD_CONV_CH), (o_dt, 2 * LANES))
            z, xbc, dt, q, k, v = _inproj(srcs, lc, g_mix[i].reshape(1, d), mod, ws,
                                          att_q_g[j].reshape(1, HEAD_DIM), att_k_g[j].reshape(1, HEAD_DIM), cc, ss,
                                          ctx_len=ctx_len, plain=plain, q0=o_dt + 2 * LANES, qk_norm=True)
            ya = _ssd(xbc, dt, z, ssd_conv_w[j], ssd_conv_b[j], ssd_dt_bias[j], ssd_a_log[j], ssd_d[j],
                      ssd_norm_g[j], expand, ctx_len=ctx_len)
            yb = _attn_dense(q, k, v, ctx_len=ctx_len, tq=ATT_TQ, tk=_pick_tile(lc, ATT_TK_PREFS))
            w_out = ab_w_out[j]
        else:
            plain = ((0, LRU_WIDTH), (LRU_WIDTH, LRU_WIDTH))
            gate, xr, q, k, v = _inproj(srcs, lc, g_mix[i].reshape(1, d), mod, (cd_w_in[j].astype(BF16),),
                                        ones_h, ones_h, cc, ss,
                                        ctx_len=ctx_len, plain=plain, q0=2 * LRU_WIDTH, qk_norm=False)
            xr = _dwconv(xr, lru_conv_w[j], lru_conv_b[j], ctx_len=ctx_len, silu=False)
            ya = _lru(xr, gate, _lru_slabs(lru_w_a[j]), _lru_slabs(lru_w_x[j]), lru_b_a[j], lru_b_x[j], lru_lam[j],
                      ctx_len=ctx_len)
            yb = _attn_window(q, k, v, swa_sink[j], ctx_len=ctx_len, tq=ATT_TQ)
            w_out = cd_w_out[j]
        xs = _outproj(ya, yb, srcs, mod, w_out.astype(BF16), ctx_len=ctx_len, latent_only=last)
        wr = jnp.concatenate([moe_w_grp[i], moe_w_rt[i],
                              jnp.zeros((d, LANES - MOE_GROUPS - N_EXPERTS), F32)], axis=1)
        br = jnp.concatenate([moe_b_grp[i], moe_b_rt[i],
                              jnp.zeros((LANES - MOE_GROUPS - N_EXPERTS,), F32)]).reshape(1, LANES)
        wr_hi = wr.astype(BF16)
        wr = jnp.concatenate([wr_hi, (wr - wr_hi.astype(F32)).astype(BF16)], axis=1)
        xs = _moe(xs, g_ffn[i].reshape(1, d), mod, wr, br, w1_b, w3_b, w2_b, i,
                  g_final if last else None, ctx_len=0 if last else ctx_len)
        srcs = (xs,)
    return xs
```

```python
import functools
import math

import numpy as np
import jax
import jax.numpy as jnp
from jax import lax
from jax.experimental import pallas as pl
from jax.experimental.pallas import tpu as pltpu

F32 = jnp.float32
BF16 = jnp.bfloat16

D_MODEL = 1024
GRID_W = 64
EPS = 1e-6
HEAD_DIM = 128
ATT_HEADS = 8
ATT_KV_HEADS = 2
GQA_REP = ATT_HEADS // ATT_KV_HEADS
ATT_WIDTH = ATT_HEADS * HEAD_DIM
KV_WIDTH = ATT_KV_HEADS * HEAD_DIM
ATT_SCALE = HEAD_DIM ** -0.5
LOG2E = math.log2(math.e)
ROPE_THETA = 10000.0
WINDOW = 128
SSD_HEADS = 16
SSD_HEAD_DIM = 64
SSD_WIDTH = SSD_HEADS * SSD_HEAD_DIM
SSD_GROUPS = 2
SSD_STATE = 64
SSD_CHUNK = 128
SSD_CONV_CH = SSD_WIDTH + 2 * SSD_GROUPS * SSD_STATE
LRU_WIDTH = 1024
LRU_BLOCKS = 16
LRU_BLOCK_DIM = LRU_WIDTH // LRU_BLOCKS
LRU_C = 8.0
CONV_W = 5
MOE_GROUPS = 4
EXPERTS_PER_GROUP = 4
N_EXPERTS = MOE_GROUPS * EXPERTS_PER_GROUP
D_EXPERT = 256

LANES = 128
SUBLANES = 8
MXU_TILE = 256
LRU_SLAB = MXU_TILE
V7X_VMEM_BYTES = 64 * 1024 * 1024
VMEM_LIMIT = V7X_VMEM_BYTES - 8 * 1024 * 1024
MOD_COL_TILE = 1536
ROW_TILE = 256
MOE_TILE_PREFS = (768, 512, 256)
ATT_TQ = 256
ATT_TK_PREFS = (2816, 1408, 768, 256)


def _params(sem):
    return pltpu.CompilerParams(dimension_semantics=sem, vmem_limit_bytes=VMEM_LIMIT)


def _pick_tile(n, prefs):
    for t in prefs:
        if n % t == 0:
            return t
    raise ValueError(f"no tile for {n}")


def _sigmoid(x):
    return 0.5 * jnp.tanh(0.5 * x) + 0.5


def _softplus(x):
    return jnp.maximum(x, 0.0) + jnp.log1p(jnp.exp(-jnp.abs(x)))


def _split_bf16(v, n):
    parts = []
    r = v
    for _ in range(n):
        p = r.astype(BF16)
        parts.append(p)
        r = r - p.astype(F32)
    return parts


def _dot_nt(a, b):
    return lax.dot_general(a, b, (((1,), (1,)), ((), ())), preferred_element_type=F32)


def _rms(x):
    return x * lax.rsqrt(jnp.mean(x * x, axis=-1, keepdims=True) + EPS)


def _row_select(row0, tm, ctx_len, mod_ref, k):
    if ctx_len == 0:
        return mod_ref[0, 1, k:k + 1, :]
    rows = row0 + lax.broadcasted_iota(jnp.int32, (tm, 1), 0)
    return jnp.where(rows < ctx_len, mod_ref[0, 0, k:k + 1, :], mod_ref[0, 1, k:k + 1, :])


def _rms_mod(x, g, row0, ctx_len, mod_ref, k_shift, k_scale):
    tm = x.shape[0]
    sh = _row_select(row0, tm, ctx_len, mod_ref, k_shift)
    sc = _row_select(row0, tm, ctx_len, mod_ref, k_scale)
    return _rms(x) * g * (1.0 + sc) + sh


def _scan_order(s, n_ctx, n_all):
    return jnp.where(s < n_ctx, n_ctx - 1 - s, n_all + n_ctx - 1 - s)


def _row_specs(tm, d, ct, split):
    if split:
        return [pl.BlockSpec((1, tm, d), lambda b, i: (b, jnp.minimum(i, ct - 1), 0)),
                pl.BlockSpec((1, tm, d), lambda b, i: (b, jnp.maximum(i - ct, 0), 0))]
    return [pl.BlockSpec((1, tm, d), lambda b, i: (b, i, 0))]


def _load_rows(refs, i, ct):
    if len(refs) == 2:
        return jnp.where(i < ct, refs[0][0], refs[1][0])
    return refs[0][0]


def _mod_kernel(ct_ref, w_ref, b_ref, o_ref, *, rows):
    w = w_ref[0]
    outs = []
    for r in range(rows):
        c = ct_ref[:, r:r + 1]
        act = c * _sigmoid(c)
        outs.append(jnp.sum(w * act, axis=0, keepdims=True) + b_ref[0])
    o_ref[0] = jnp.concatenate(outs, axis=0)


def _modulation(c, c_ctx, w_mod, b_mod):
    depth, d, n = w_mod.shape
    rows = c.shape[0] + 1
    ct = jnp.concatenate([c_ctx[None, :], c], axis=0).T
    tn = MOD_COL_TILE
    return pl.pallas_call(
        functools.partial(_mod_kernel, rows=rows),
        grid=(depth, n // tn),
        in_specs=[pl.BlockSpec((d, rows), lambda i, j: (0, 0)),
                  pl.BlockSpec((1, d, tn), lambda i, j: (i, 0, j)),
                  pl.BlockSpec((1, 1, tn), lambda i, j: (i, 0, j))],
        out_specs=pl.BlockSpec((1, rows, tn), lambda i, j: (i, 0, j)),
        out_shape=jax.ShapeDtypeStruct((depth, rows, n), F32),
        compiler_params=_params(("arbitrary", "arbitrary")),
        name="modulation",
    )(ct, w_mod, b_mod.reshape(depth, 1, n))


def _inproj_kernel(*refs, tm, ctx_len, plain, q0, qk_norm, n_src, n_w):
    x_refs = refs[:n_src]
    g_ref, mod_ref = refs[n_src:n_src + 2]
    w_refs = refs[n_src + 2:n_src + 2 + n_w]
    qg_ref, kg_ref, cc_ref, ss_ref = refs[n_src + 2 + n_w:n_src + 6 + n_w]
    out_refs = refs[n_src + 6 + n_w:]
    i = pl.program_id(1)
    x = _load_rows(x_refs, i, ctx_len // tm)
    h = _rms_mod(x, g_ref[...], i * tm, ctx_len, mod_ref, 0, 1).astype(BF16)
    res = jnp.concatenate([jnp.dot(h, w_ref[...], preferred_element_type=F32) for w_ref in w_refs], axis=1)
    for ref, (c0, cw) in zip(out_refs[:len(plain)], plain):
        ref[0] = res[:, c0:c0 + cw]
    q_ref, k_ref, v_ref = out_refs[len(plain):]
    cc = cc_ref[...]
    ss = ss_ref[...]

    def head(t, gain, scale):
        if qk_norm:
            t = _rms(t) * gain
        t = t * cc + pltpu.roll(t, HEAD_DIM // 2, 1) * ss
        return (t * scale).astype(BF16)

    for hh in range(ATT_HEADS):
        c0 = q0 + hh * HEAD_DIM
        q_ref[0, :, hh * HEAD_DIM:(hh + 1) * HEAD_DIM] = head(res[:, c0:c0 + HEAD_DIM], qg_ref[...],
                                                              ATT_SCALE * LOG2E)
    k0 = q0 + ATT_WIDTH
    for hh in range(ATT_KV_HEADS):
        c0 = k0 + hh * HEAD_DIM
        k_ref[0, :, hh * HEAD_DIM:(hh + 1) * HEAD_DIM] = head(res[:, c0:c0 + HEAD_DIM], kg_ref[...], 1.0)
    v0 = k0 + KV_WIDTH
    v_ref[0] = res[:, v0:v0 + KV_WIDTH].astype(BF16)


def _inproj(srcs, lc, g, mod, ws, qg, kg, cc, ss, *, ctx_len, plain, q0, qk_norm, tm=ROW_TILE):
    bsz, _, d = srcs[0].shape
    row = lambda b, i: (b, i, 0)
    const = lambda b, i: (0, 0)
    out_shape = [jax.ShapeDtypeStruct((bsz, lc, cw), F32) for _, cw in plain]
    out_specs = [pl.BlockSpec((1, tm, cw), row) for _, cw in plain]
    for cw in (ATT_WIDTH, KV_WIDTH, KV_WIDTH):
        out_shape.append(jax.ShapeDtypeStruct((bsz, lc, cw), BF16))
        out_specs.append(pl.BlockSpec((1, tm, cw), row))
    return pl.pallas_call(
        functools.partial(_inproj_kernel, tm=tm, ctx_len=ctx_len, plain=plain, q0=q0, qk_norm=qk_norm,
                          n_src=len(srcs), n_w=len(ws)),
        grid=(bsz, lc // tm),
        in_specs=_row_specs(tm, d, ctx_len // tm, len(srcs) == 2) + [
            pl.BlockSpec((1, d), const),
            pl.BlockSpec((1, 2, 6, d), lambda b, i: (b, 0, 0, 0))] + [
            pl.BlockSpec(w.shape, const) for w in ws] + [
            pl.BlockSpec((1, HEAD_DIM), const),
            pl.BlockSpec((1, HEAD_DIM), const),
            pl.BlockSpec((tm, HEAD_DIM), lambda b, i: (i, 0)),
            pl.BlockSpec((tm, HEAD_DIM), lambda b, i: (i, 0))],
        out_specs=out_specs,
        out_shape=out_shape,
        compiler_params=_params(("parallel", "arbitrary")),
        name="inproj",
    )(*srcs, g, mod, *ws, qg, kg, cc, ss)


def _conv_rows(prev, cur, nxt, w_ref, b_ref, seg_start, seg_end, silu):
    tm, c = cur.shape
    prev = jnp.where(seg_start, 0.0, prev)
    nxt = jnp.where(seg_end, 0.0, nxt)
    ng = tm // SUBLANES
    ext = jnp.concatenate([prev, cur, nxt], axis=0).reshape(ng + 2, SUBLANES, c)
    pos8 = lax.broadcasted_iota(jnp.int32, (1, SUBLANES, 1), 1)
    pad = CONV_W // 2
    acc = b_ref[...] + cur * w_ref[pad:pad + 1, :]
    for k in range(CONV_W):
        d = k - pad
        if d == 0:
            continue
        r = pltpu.roll(ext, (-d) % SUBLANES, 1)
        if d < 0:
            tap = jnp.where(pos8 >= -d, r[1:ng + 1], r[0:ng])
        else:
            tap = jnp.where(pos8 < SUBLANES - d, r[1:ng + 1], r[2:ng + 2])
        acc = acc + tap.reshape(tm, c) * w_ref[k:k + 1, :]
    if silu:
        acc = acc * _sigmoid(acc)
    return acc


def _dwconv_kernel(prev_ref, cur_ref, next_ref, w_ref, b_ref, o_ref, *, nt, ct, silu):
    i = pl.program_id(1)
    seg_start = jnp.logical_or(i == 0, i == ct)
    seg_end = jnp.logical_or(i == ct - 1, i == nt - 1)
    o_ref[0] = _conv_rows(prev_ref[0], cur_ref[0], next_ref[0], w_ref, b_ref, seg_start, seg_end, silu)


def _dwconv(u, w, b, *, ctx_len, silu, tm=ROW_TILE):
    bsz, lc, c = u.shape
    nt = lc // tm
    hb = tm // SUBLANES
    nhb = lc // SUBLANES
    return pl.pallas_call(
        functools.partial(_dwconv_kernel, nt=nt, ct=ctx_len // tm, silu=silu),
        grid=(bsz, nt),
        in_specs=[pl.BlockSpec((1, SUBLANES, c), lambda b_, i: (b_, jnp.maximum(i * hb - 1, 0), 0)),
                  pl.BlockSpec((1, tm, c), lambda b_, i: (b_, i, 0)),
                  pl.BlockSpec((1, SUBLANES, c), lambda b_, i: (b_, jnp.minimum((i + 1) * hb, nhb - 1), 0)),
                  pl.BlockSpec((CONV_W, c), lambda b_, i: (0, 0)),
                  pl.BlockSpec((1, c), lambda b_, i: (0, 0))],
        out_specs=pl.BlockSpec((1, tm, c), lambda b_, i: (b_, i, 0)),
        out_shape=jax.ShapeDtypeStruct((bsz, lc, c), F32),
        compiler_params=_params(("parallel", "arbitrary")),
        name="dwconv",
    )(u, u, u, w, b.reshape(1, c))


def _expand_heads(v, e_ref):
    hi, lo = _split_bf16(v, 2)
    e = e_ref[...]
    return (jnp.dot(hi, e, preferred_element_type=F32) + jnp.dot(lo, e, preferred_element_type=F32))


def _ssd_chunk(xbc, dt_raw, dtb, alog, e_ref, h_prev, reverse):
    q = SSD_CHUNK
    heads_per_group = SSD_HEADS // SSD_GROUPS
    x = xbc[:, :SSD_WIDTH]
    bm = xbc[:, SSD_WIDTH:SSD_WIDTH + LANES]
    cm = xbc[:, SSD_WIDTH + LANES:]
    dt = _softplus(dt_raw + dtb)
    da = dt * (-jnp.exp(alog) * LOG2E)
    ii = lax.broadcasted_iota(jnp.int32, (q, q), 0)
    jj = lax.broadcasted_iota(jnp.int32, (q, q), 1)
    causal = (jj >= ii) if reverse else (jj <= ii)
    mask_b = jnp.where(causal, 1.0, 0.0).astype(BF16)
    da_t = da.T
    dt_t = dt.T
    acs = sum(jnp.dot(mask_b, p, preferred_element_type=F32) for p in _split_bf16(da, 3))
    acs_t = sum(_dot_nt(p, mask_b) for p in _split_bf16(da_t, 3))
    total = jnp.sum(da, axis=0, keepdims=True)
    eacs = jnp.exp2(acs)

    bm_b = bm.astype(BF16)
    lane = lax.broadcasted_iota(jnp.int32, (1, LANES), 1)
    in_group = [jnp.logical_and(lane >= g * SSD_STATE, lane < (g + 1) * SSD_STATE) for g in range(SSD_GROUPS)]
    cb = [_dot_nt(jnp.where(in_group[g], cm, 0.0).astype(BF16), bm_b) for g in range(SSD_GROUPS)]

    x_b = x.astype(BF16)
    h_b = h_prev.astype(BF16)
    pairs = []
    for hp in range(SSD_HEADS // 2):
        rhs = jnp.concatenate([x_b[:, hp * LANES:(hp + 1) * LANES], h_b[:, hp * LANES:(hp + 1) * LANES]], axis=0)
        outs = []
        for t in range(2):
            h = 2 * hp + t
            seg = acs[:, h:h + 1] - acs_t[h:h + 1, :]
            dec = jnp.exp2(jnp.where(causal, seg, -jnp.inf))
            w = cb[h // heads_per_group] * dec * dt_t[h:h + 1, :]
            lhs = jnp.concatenate([w.astype(BF16), (cm * eacs[:, h:h + 1]).astype(BF16)], axis=1)
            outs.append(jnp.dot(lhs, rhs, preferred_element_type=F32))
        pairs.append(jnp.where(lane < SSD_HEAD_DIM, outs[0], outs[1]))
    y = jnp.concatenate(pairs, axis=1)

    to_end = jnp.exp2(total - acs) * dt
    xw = (x * _expand_heads(to_end, e_ref)).astype(BF16)
    st = jnp.dot(bm.T.astype(BF16), xw, preferred_element_type=F32)
    row_g0 = lax.broadcasted_iota(jnp.int32, st.shape, 0) < SSD_STATE
    col_g0 = lax.broadcasted_iota(jnp.int32, st.shape, 1) < heads_per_group * SSD_HEAD_DIM
    chunk_decay = _expand_heads(jnp.broadcast_to(jnp.exp2(total), (SUBLANES, LANES)), e_ref)[0:1]
    return y, h_prev * chunk_decay + jnp.where(row_g0 == col_g0, st, 0.0)


def _ssd_bwd_kernel(prev_ref, cur_ref, next_ref, cw_ref, cb_ref, dt_ref, dtb_ref, alog_ref, e_ref,
                    yb_ref, xc_ref, hst_ref, *, nbx, nb, bsz):
    s = pl.program_id(0)
    blk = _scan_order(s, nbx, nb)

    @pl.when(s == 0)
    def _():
        hst_ref[...] = jnp.zeros_like(hst_ref)

    seg_start = jnp.logical_or(blk == 0, blk == nbx)
    seg_end = jnp.logical_or(blk == nbx - 1, blk == nb - 1)
    xbc, h = [], []
    for bb in range(bsz):
        xbc.append(_conv_rows(prev_ref[bb], cur_ref[bb], next_ref[bb], cw_ref, cb_ref, seg_start, seg_end, True))
        xc_ref[bb] = xbc[bb]
        h.append(hst_ref[bb])
    for sub in reversed(range(cur_ref.shape[1] // SSD_CHUNK)):
        rows = slice(sub * SSD_CHUNK, (sub + 1) * SSD_CHUNK)
        for bb in range(bsz):
            y, h[bb] = _ssd_chunk(xbc[bb][rows], dt_ref[bb, rows, :], dtb_ref[...], alog_ref[...], e_ref, h[bb], True)
            yb_ref[bb, rows, :] = y
    for bb in range(bsz):
        hst_ref[bb] = h[bb]


def _ssd_fwd_kernel(xc_ref, dt_ref, dtb_ref, alog_ref, e_ref, yb_ref, z_ref, dsk_ref, ng_ref,
                    o_ref, hst_ref, *, bsz):
    @pl.when(pl.program_id(0) == 0)
    def _():
        hst_ref[...] = jnp.zeros_like(hst_ref)

    h = [hst_ref[bb] for bb in range(bsz)]
    for sub in range(xc_ref.shape[1] // SSD_CHUNK):
        rows = slice(sub * SSD_CHUNK, (sub + 1) * SSD_CHUNK)
        for bb in range(bsz):
            xbc = xc_ref[bb, rows, :]
            y, h[bb] = _ssd_chunk(xbc, dt_ref[bb, rows, :], dtb_ref[...], alog_ref[...], e_ref, h[bb], False)
            z = z_ref[bb, rows, :]
            y = (y + yb_ref[bb, rows, :] + dsk_ref[...] * xbc[:, :SSD_WIDTH]) * (z * _sigmoid(z))
            o_ref[bb, rows, :] = (_rms(y) * ng_ref[...]).astype(BF16)
    for bb in range(bsz):
        hst_ref[bb] = h[bb]


def _ssd(xbc_raw, dt, z, conv_w, conv_b, dt_bias, a_log, d_skip, norm_g, expand, *, ctx_len, q=ROW_TILE):
    bsz, lc, cch = xbc_raw.shape
    w = SSD_WIDTH
    nb = lc // q
    nbx = ctx_len // q
    hb = q // SUBLANES
    nhb = lc // SUBLANES
    pad = lambda t: jnp.pad(t, (0, LANES - SSD_HEADS)).reshape(1, LANES)
    const = lambda s: (0, 0)
    rev = lambda s: _scan_order(s, nbx, nb)
    state = [pltpu.VMEM((bsz, SSD_GROUPS * SSD_STATE, w), F32)]
    yb, xc = pl.pallas_call(
        functools.partial(_ssd_bwd_kernel, nbx=nbx, nb=nb, bsz=bsz),
        grid=(nb,),
        in_specs=[pl.BlockSpec((bsz, SUBLANES, cch), lambda s: (0, jnp.maximum(rev(s) * hb - 1, 0), 0)),
                  pl.BlockSpec((bsz, q, cch), lambda s: (0, rev(s), 0)),
                  pl.BlockSpec((bsz, SUBLANES, cch), lambda s: (0, jnp.minimum((rev(s) + 1) * hb, nhb - 1), 0)),
                  pl.BlockSpec((CONV_W, cch), const),
                  pl.BlockSpec((1, cch), const),
                  pl.BlockSpec((bsz, q, LANES), lambda s: (0, rev(s), 1)),
                  pl.BlockSpec((1, LANES), const),
                  pl.BlockSpec((1, LANES), const),
                  pl.BlockSpec((LANES, w), const)],
        out_specs=[pl.BlockSpec((bsz, q, w), lambda s: (0, rev(s), 0)),
                   pl.BlockSpec((bsz, q, cch), lambda s: (0, rev(s), 0))],
        out_shape=[jax.ShapeDtypeStruct((bsz, lc, w), F32), jax.ShapeDtypeStruct((bsz, lc, cch), F32)],
        scratch_shapes=state,
        compiler_params=_params(("arbitrary",)),
        name="ssd_bwd",
    )(xbc_raw, xbc_raw, xbc_raw, conv_w, conv_b.reshape(1, cch), dt, pad(dt_bias[1]), pad(a_log[1]), expand)
    row = lambda cols: pl.BlockSpec((bsz, q, cols), lambda s: (0, s, 0))
    return pl.pallas_call(
        functools.partial(_ssd_fwd_kernel, bsz=bsz),
        grid=(nb,),
        in_specs=[row(cch), row(LANES), pl.BlockSpec((1, LANES), const), pl.BlockSpec((1, LANES), const),
                  pl.BlockSpec((LANES, w), const), row(w), row(w), pl.BlockSpec((1, w), const),
                  pl.BlockSpec((1, w), const)],
        out_specs=row(w),
        out_shape=jax.ShapeDtypeStruct((bsz, lc, w), BF16),
        scratch_shapes=state,
        compiler_params=_params(("arbitrary",)),
        name="ssd_fwd",
    )(xc, dt, pad(dt_bias[0]), pad(a_log[0]), expand, yb, z,
      jnp.repeat(d_skip, SSD_HEAD_DIM).reshape(1, w), norm_g.reshape(1, w))


def _stack_heads(q):
    return jnp.concatenate([q[:, r * HEAD_DIM:(r + 1) * HEAD_DIM] for r in range(q.shape[1] // HEAD_DIM)], axis=0)


def _lane_parts(s):
    return [s[:, c * LANES:(c + 1) * LANES] for c in range(s.shape[1] // LANES)]


def _softmax_parts(parts, sink=None):
    pm = parts[0]
    for t in parts[1:]:
        pm = jnp.maximum(pm, t)
    m = jnp.max(pm, axis=-1, keepdims=True)
    if sink is not None:
        m = jnp.maximum(m, sink)
    mb = jnp.broadcast_to(m, pm.shape)
    ps = [jnp.exp2(t - mb) for t in parts]
    lsum = ps[0]
    for t in ps[1:]:
        lsum = lsum + t
    l = jnp.sum(lsum, axis=-1, keepdims=True)
    if sink is not None:
        l = l + jnp.exp2(sink - m)
    return jnp.concatenate([t.astype(BF16) for t in ps], axis=1), l


def _store_heads(o_ref, o, tq, col0=0):
    for r in range(GQA_REP):
        c0 = col0 + r * HEAD_DIM
        o_ref[0, :, c0:c0 + HEAD_DIM] = o[r * tq:(r + 1) * tq].astype(BF16)


def _attn_dense_kernel(q_ref, k_ref, v_ref, o_ref, m_scr, l_scr, acc_scr, *, tq, tk, ctx_len, lk):
    i = pl.program_id(1)
    gw = GQA_REP * HEAD_DIM
    groups = range(ATT_KV_HEADS)
    q4 = [_stack_heads(q_ref[0, :, g * gw:(g + 1) * gw]) for g in groups]
    kcol = [slice(g * HEAD_DIM, (g + 1) * HEAD_DIM) for g in groups]

    @pl.when(i < ctx_len // tq)
    def _():
        for g in groups:
            p, l = _softmax_parts(_lane_parts(_dot_nt(q4[g], k_ref[0, 0:ctx_len, kcol[g]])))
            o = jnp.dot(p, v_ref[0, 0:ctx_len, kcol[g]], preferred_element_type=F32) / l
            _store_heads(o_ref, o, tq, col0=g * gw)

    @pl.when(i >= ctx_len // tq)
    def _():
        m_scr[...] = jnp.full_like(m_scr, -jnp.inf)
        l_scr[...] = jnp.zeros_like(l_scr)
        acc_scr[...] = jnp.zeros_like(acc_scr)

        def body(j, carry):
            off = pl.multiple_of(j * tk, tk)
            for g in groups:
                parts = _lane_parts(_dot_nt(q4[g], k_ref[0, pl.ds(off, tk), kcol[g]]))
                pm = parts[0]
                for t in parts[1:]:
                    pm = jnp.maximum(pm, t)
                m_prev = m_scr[g]
                m_new = jnp.maximum(m_prev, jnp.max(pm, axis=-1, keepdims=True))
                alpha = jnp.exp2(m_prev - m_new)
                ps = [jnp.exp2(t - m_new) for t in parts]
                lsum = ps[0]
                for t in ps[1:]:
                    lsum = lsum + t
                l_scr[g] = alpha * l_scr[g] + lsum
                p = jnp.concatenate([t.astype(BF16) for t in ps], axis=1)
                acc_scr[g] = alpha * acc_scr[g] + jnp.dot(p, v_ref[0, pl.ds(off, tk), kcol[g]],
                                                          preferred_element_type=F32)
                m_scr[g] = m_new
            return carry

        lax.fori_loop(0, lk // tk, body, 0, unroll=True)
        for g in groups:
            _store_heads(o_ref, acc_scr[g] / jnp.sum(l_scr[g], axis=-1, keepdims=True), tq, col0=g * gw)


def _attn_dense(q, k, v, *, ctx_len, tq, tk):
    bsz, lc, _ = q.shape
    rows = GQA_REP * tq
    stat = pltpu.VMEM((ATT_KV_HEADS, rows, LANES), F32)
    return pl.pallas_call(
        functools.partial(_attn_dense_kernel, tq=tq, tk=tk, ctx_len=ctx_len, lk=lc),
        grid=(bsz, lc // tq),
        in_specs=[pl.BlockSpec((1, tq, ATT_WIDTH), lambda b, i: (b, i, 0)),
                  pl.BlockSpec((1, lc, KV_WIDTH), lambda b, i: (b, 0, 0)),
                  pl.BlockSpec((1, lc, KV_WIDTH), lambda b, i: (b, 0, 0))],
        out_specs=pl.BlockSpec((1, tq, ATT_WIDTH), lambda b, i: (b, i, 0)),
        out_shape=jax.ShapeDtypeStruct((bsz, lc, ATT_WIDTH), BF16),
        scratch_shapes=[stat, stat, stat],
        compiler_params=_params(("parallel", "arbitrary")),
        name="attn_dense",
    )(q, k, v)


def _attn_win_kernel(q_ref, k_ref, v_ref, sink_ref, bias_ref, o_ref, *, ctx_len, lat_len, tq):
    n = pl.program_id(1)
    band = tq + 2 * WINDOW
    il = n - ctx_len // tq
    lo = jnp.clip(il * tq - WINDOW, 0, lat_len - band)
    start = pl.multiple_of(ctx_len + lo, WINDOW)
    bias = bias_ref[0]
    gw = GQA_REP * HEAD_DIM
    for g in range(ATT_KV_HEADS):
        kc0 = g * HEAD_DIM
        q4 = _stack_heads(q_ref[0, :, g * gw:(g + 1) * gw])
        s_loc = _dot_nt(q4, k_ref[0, pl.ds(start, band), kc0:kc0 + HEAD_DIM]) + bias
        s_ctx = _dot_nt(q4, k_ref[0, 0:ctx_len, kc0:kc0 + HEAD_DIM])
        p, l = _softmax_parts(_lane_parts(s_loc) + _lane_parts(s_ctx), sink_ref[g])
        o = (jnp.dot(p[:, :band], v_ref[0, pl.ds(start, band), kc0:kc0 + HEAD_DIM], preferred_element_type=F32)
             + jnp.dot(p[:, band:], v_ref[0, 0:ctx_len, kc0:kc0 + HEAD_DIM], preferred_element_type=F32)) / l
        _store_heads(o_ref, o, tq, col0=g * gw)


def _attn_window(q, k, v, sink, *, ctx_len, tq):
    bsz, lc, _ = q.shape
    lat_len = lc - ctx_len
    rows = GQA_REP * tq
    assert tq & (tq - 1) == 0 and ctx_len % tq == 0 and lat_len >= tq + 2 * WINDOW
    sink_col = jnp.repeat(sink.reshape(ATT_KV_HEADS, GQA_REP) * LOG2E, tq, axis=1).reshape(ATT_KV_HEADS, rows, 1)
    band = tq + 2 * WINDOW
    ct = ctx_len // tq
    nlt = lat_len // tq
    r = np.arange(tq)[:, None]
    c = np.arange(band)[None, :]
    masks = [np.abs(c - off - r) <= WINDOW for off in (0, WINDOW, band - tq)] + [np.zeros((tq, band), bool)]
    bias = jnp.asarray(np.tile(np.where(np.stack(masks), 0.0, -np.inf), (1, GQA_REP, 1)), F32)

    def variant(n):
        il = n - ct
        return jnp.where(il < 0, 3, jnp.where(il == 0, 0, jnp.where(il == nlt - 1, 2, 1)))

    return pl.pallas_call(
        functools.partial(_attn_win_kernel, ctx_len=ctx_len, lat_len=lat_len, tq=tq),
        grid=(bsz, lc // tq),
        in_specs=[pl.BlockSpec((1, tq, ATT_WIDTH), lambda b, n: (b, n, 0)),
                  pl.BlockSpec((1, lc, KV_WIDTH), lambda b, n: (b, 0, 0)),
                  pl.BlockSpec((1, lc, KV_WIDTH), lambda b, n: (b, 0, 0)),
                  pl.BlockSpec((ATT_KV_HEADS, rows, 1), lambda b, n: (0, 0, 0)),
                  pl.BlockSpec((1, rows, band), lambda b, n: (variant(n), 0, 0))],
        out_specs=pl.BlockSpec((1, tq, ATT_WIDTH), lambda b, n: (b, n, 0)),
        out_shape=jax.ShapeDtypeStruct((bsz, lc, ATT_WIDTH), BF16),
        compiler_params=_params(("parallel", "arbitrary")),
        name="attn_window",
    )(q, k, v, sink_col, bias)


def _lru_scan_rows(a, b, carry, reverse):
    t, w = a.shape
    ngroups = t // SUBLANES
    a = a.reshape(ngroups, SUBLANES, w)
    b = b.reshape(ngroups, SUBLANES, w)
    pos8 = lax.broadcasted_iota(jnp.int32, (1, SUBLANES, 1), 1)
    k = 1
    while k < SUBLANES:
        if reverse:
            keep = pos8 < SUBLANES - k
            sh = SUBLANES - k
        else:
            keep = pos8 >= k
            sh = k
        a_sh = jnp.where(keep, pltpu.roll(a, sh, 1), 1.0)
        b_sh = jnp.where(keep, pltpu.roll(b, sh, 1), 0.0)
        b = b + a * b_sh
        a = a * a_sh
        k *= 2
    a = a.reshape(t, w)
    b = b.reshape(t, w)
    hs = [None] * ngroups
    for g in (range(ngroups - 1, -1, -1) if reverse else range(ngroups)):
        sl = slice(g * SUBLANES, (g + 1) * SUBLANES)
        hg = a[sl] * carry + b[sl]
        carry = hg[0:1] if reverse else hg[SUBLANES - 1:SUBLANES]
        hs[g] = hg
    return jnp.concatenate(hs, axis=0), carry


def _lru_kernel(*refs, reverse):
    x_ref, wa_ref, wx_ref, ba_ref, bx_ref, lam_ref = refs[:6]
    if reverse:
        o_ref, carry_ref = refs[6:]
    else:
        hb_ref, gate_ref, o_ref, carry_ref = refs[6:]
    s = pl.program_id(1)

    @pl.when(s == 0)
    def _():
        carry_ref[...] = jnp.zeros_like(carry_ref)

    x = x_ref[0]
    x_b = x.astype(BF16)
    nslab = LRU_WIDTH // LRU_SLAB

    def gate(w_ref, b_ref):
        pre = jnp.concatenate(
            [jnp.dot(x_b[:, c * LRU_SLAB:(c + 1) * LRU_SLAB], w_ref[0, c], preferred_element_type=F32)
             for c in range(nslab)], axis=1)
        return _sigmoid(pre + b_ref[0])

    r = gate(wa_ref, ba_ref)
    ig = gate(wx_ref, bx_ref)
    log2_base = (-LRU_C * LOG2E) * _softplus(-lam_ref[0])
    a = jnp.exp2(r * log2_base)
    y = 1.0 - a * a
    b = jnp.where(y > 0.0, y * lax.rsqrt(y), 0.0) * (ig * x)
    h, carry = _lru_scan_rows(a, b, carry_ref[...], reverse)
    carry_ref[...] = carry
    if reverse:
        o_ref[0] = h
    else:
        g = gate_ref[0]
        gelu = 0.5 * g * (1.0 + jnp.tanh(math.sqrt(2.0 / math.pi) * (g + 0.044715 * (g * g * g))))
        o_ref[0] = (gelu * (h + hb_ref[0])).astype(BF16)


def _lru(xr, gate, wa, wx, ba, bx, lam, *, ctx_len, t=ROW_TILE):
    bsz, lc, w = xr.shape
    nt = lc // t
    ct = ctx_len // t
    nslab = w // LRU_SLAB
    vec = lambda v: v.reshape(2, 1, w)

    def call(reverse, extra_in, out_dtype):
        tile = (lambda s: _scan_order(s, ct, nt)) if reverse else (lambda s: s)
        d = 1 if reverse else 0
        row = pl.BlockSpec((1, t, w), lambda b, s: (b, tile(s), 0))
        return pl.pallas_call(
            functools.partial(_lru_kernel, reverse=reverse),
            grid=(bsz, nt),
            in_specs=[row,
                      pl.BlockSpec((1, nslab, LRU_SLAB, LRU_SLAB), lambda b, s: (d, 0, 0, 0)),
                      pl.BlockSpec((1, nslab, LRU_SLAB, LRU_SLAB), lambda b, s: (d, 0, 0, 0)),
                      pl.BlockSpec((1, 1, w), lambda b, s: (d, 0, 0)),
                      pl.BlockSpec((1, 1, w), lambda b, s: (d, 0, 0)),
                      pl.BlockSpec((1, 1, w), lambda b, s: (d, 0, 0))] + [row] * len(extra_in),
            out_specs=row,
            out_shape=jax.ShapeDtypeStruct((bsz, lc, w), out_dtype),
            scratch_shapes=[pltpu.VMEM((1, w), F32)],
            compiler_params=_params(("parallel", "arbitrary")),
            name="lru_bwd" if reverse else "lru_fwd",
        )(xr, wa, wx, vec(ba), vec(bx), vec(lam), *extra_in)

    hb = call(True, (), F32)
    return call(False, (hb, gate), BF16)


def _lru_slabs(w):
    per = LRU_SLAB // LRU_BLOCK_DIM
    nslab = LRU_BLOCKS // per
    w = w.reshape(2, nslab, per, LRU_BLOCK_DIM, LRU_BLOCK_DIM)
    eye = jnp.eye(per, dtype=w.dtype)
    full = jnp.einsum('dspij,pq->dspiqj', w, eye)
    return full.reshape(2, nslab, LRU_SLAB, LRU_SLAB).astype(BF16)


def _outproj_kernel(*refs, tm, ctx_len, t0, n_src):
    ya_ref, yb_ref = refs[:2]
    x_refs = refs[2:2 + n_src]
    mod_ref, w_ref, o_ref = refs[2 + n_src:]
    i = pl.program_id(1) + t0
    half = ya_ref.shape[2]
    acc = (jnp.dot(ya_ref[0], w_ref[0:half, :], preferred_element_type=F32)
           + jnp.dot(yb_ref[0], w_ref[half:, :], preferred_element_type=F32))
    gate = _row_select(i * tm, tm, ctx_len, mod_ref, 2)
    o_ref[0] = _load_rows(x_refs, i, ctx_len // tm) + gate * acc


def _outproj(ya, yb, srcs, mod, w, *, ctx_len, latent_only, tm=ROW_TILE):
    bsz, lc, half = ya.shape
    d = w.shape[1]
    ct = ctx_len // tm
    t0 = ct if latent_only else 0
    row = lambda b, i: (b, i + t0, 0)
    if len(srcs) == 2:
        x_specs = [pl.BlockSpec((1, tm, d), lambda b, i: (b, jnp.minimum(i + t0, ct - 1), 0)),
                   pl.BlockSpec((1, tm, d), lambda b, i: (b, jnp.maximum(i + t0 - ct, 0), 0))]
    else:
        x_specs = [pl.BlockSpec((1, tm, d), row)]
    return pl.pallas_call(
        functools.partial(_outproj_kernel, tm=tm, ctx_len=ctx_len, t0=t0, n_src=len(srcs)),
        grid=(bsz, lc // tm - t0),
        in_specs=[pl.BlockSpec((1, tm, half), row), pl.BlockSpec((1, tm, half), row)] + x_specs + [
            pl.BlockSpec((1, 2, 6, d), lambda b, i: (b, 0, 0, 0)),
            pl.BlockSpec((2 * half, d), lambda b, i: (0, 0))],
        out_specs=pl.BlockSpec((1, tm, d), lambda b, i: (b, i, 0)),
        out_shape=jax.ShapeDtypeStruct((bsz, lc - t0 * tm, d), F32),
        compiler_params=_params(("parallel", "arbitrary")),
        name="outproj",
    )(ya, yb, *srcs, mod, w)


def _first_max(vals):
    m = vals[0]
    for v in vals[1:]:
        m = jnp.maximum(m, v)
    hot = []
    taken = None
    for v in vals:
        hit = v == m
        if taken is None:
            hot.append(hit)
            taken = hit
        else:
            hot.append(jnp.logical_and(hit, jnp.logical_not(taken)))
            taken = jnp.logical_or(taken, hit)
    return m, hot


def _moe_kernel(*refs, tm, ctx_len, wins, final):
    x_ref, g_ref, mod_ref, wr_ref, br_ref, w1_ref, w3_ref, w2_ref = refs[:8]
    if final:
        gf_ref = refs[8]
    o_ref, hn_scr, gate_scr, rank_row_scr, rank_col_scr, cnt_ref, acc_scr = refs[-7:]
    i = pl.program_id(1)
    e = pl.program_id(2)

    @pl.when(e == 0)
    def _():
        hn = _rms_mod(x_ref[0], g_ref[...], i * tm, ctx_len, mod_ref, 3, 4)
        hn_scr[...] = hn.astype(BF16)
        h_hi, h_lo = _split_bf16(hn, 2)
        r_hi = jnp.dot(h_hi, wr_ref[...], preferred_element_type=F32)
        r_lo = jnp.dot(h_lo, wr_ref[...], preferred_element_type=F32)
        logits = (r_hi[:, :LANES] + r_hi[:, LANES:]) + (r_lo[:, :LANES] + r_lo[:, LANES:]) + br_ref[...]
        lt = jnp.concatenate([logits[c * LANES:(c + 1) * LANES, :].T for c in range(tm // LANES)], axis=1)
        gl = [lt[j:j + 1, :] for j in range(MOE_GROUPS)]
        gmax, ghot = _first_max(gl)
        gsum = sum(jnp.exp(v - gmax) for v in gl)
        p_grp = 1.0 / gsum
        elg = []
        for j in range(EXPERTS_PER_GROUP):
            v = jnp.zeros_like(gmax)
            for gi in range(MOE_GROUPS):
                c = MOE_GROUPS + gi * EXPERTS_PER_GROUP + j
                v = jnp.where(ghot[gi], lt[c:c + 1, :], v)
            elg.append(v)
        v1, hot1 = _first_max(elg)
        rest = [jnp.where(hh, -jnp.inf, v) for hh, v in zip(hot1, elg)]
        v2, hot2 = _first_max(rest)
        ex = jnp.exp(v2 - v1)
        w1 = p_grp / (1.0 + ex)
        w2 = p_grp * ex / (1.0 + ex)
        own = [jnp.where(hot1[j], w1, 0.0) + jnp.where(hot2[j], w2, 0.0) for j in range(EXPERTS_PER_GROUP)]
        pad_rows = jnp.zeros((LANES - MOE_GROUPS, tm), F32)
        g_t = jnp.concatenate([jnp.where(h_, 1.0, 0.0) for h_ in ghot] + [pad_rows], axis=0)
        t0 = lax.broadcasted_iota(jnp.int32, (tm, tm), 0)
        t1 = lax.broadcasted_iota(jnp.int32, (tm, tm), 1)
        before = jnp.where(t0 < t1, 1.0, 0.0).astype(BF16)
        rank_t = jnp.where(g_t > 0.0, jnp.dot(g_t.astype(BF16), before, preferred_element_type=F32), -1.0)
        rank_row_scr[...] = rank_t[0:SUBLANES]
        for gi in range(MOE_GROUPS):
            cnt_ref[gi] = jnp.sum(g_t[gi:gi + 1, :]).astype(jnp.int32)
            slab_t = jnp.concatenate([jnp.where(ghot[gi], own[j], 0.0) for j in range(EXPERTS_PER_GROUP)]
                                     + [pad_rows], axis=0)
            for c in range(tm // LANES):
                rows_c = slice(c * LANES, (c + 1) * LANES)
                slab = slab_t[:, rows_c].T
                hi, lo = _split_bf16(slab, 2)
                gate_scr[gi, rows_c, 0:LANES] = hi
                gate_scr[gi, rows_c, LANES:] = lo
                if gi == 0:
                    rank_c = rank_t[:, rows_c].T
                    for gj in range(MOE_GROUPS):
                        rank_col_scr[gj, rows_c, :] = rank_c[:, gj:gj + 1]
        acc_scr[...] = jnp.zeros_like(acc_scr)

    cnt = cnt_ref[e]
    rank_row = rank_row_scr[pl.ds(e, 1), :]
    rank_col = rank_col_scr[e]

    def window(first, w_):
        base = first.astype(F32)
        slot_r = lax.broadcasted_iota(jnp.int32, (w_, 1), 0).astype(F32)
        slot_c = lax.broadcasted_iota(jnp.int32, (1, w_), 1).astype(F32)
        sel = jnp.where(rank_row - base == slot_r, 1.0, 0.0).astype(BF16)
        sel_t = jnp.where(rank_col - base == slot_c, 1.0, 0.0).astype(BF16)
        xs = jnp.dot(sel, hn_scr[...], preferred_element_type=F32).astype(BF16)
        gs = jnp.dot(sel, gate_scr[e], preferred_element_type=F32)
        gsel = gs[:, :LANES] + gs[:, LANES:]
        yw = None
        for kk in range(EXPERTS_PER_GROUP):
            a = jnp.dot(xs, w1_ref[0, kk], preferred_element_type=F32)
            u = jnp.dot(xs, w3_ref[0, kk], preferred_element_type=F32)
            hid = (a * _sigmoid(a)) * u * gsel[:, kk:kk + 1]
            part = jnp.dot(hid.astype(BF16), w2_ref[0, kk], preferred_element_type=F32)
            yw = part if yw is None else yw + part
        acc_scr[...] += jnp.dot(sel_t, yw.astype(BF16), preferred_element_type=F32)

    w_top = wins[-1]
    lower = 0
    for w_ in wins[:-1]:
        pl.when(jnp.logical_and(cnt > lower, cnt <= w_))(functools.partial(window, jnp.int32(0), w_))
        lower = w_

    @pl.when(cnt > lower)
    def _():
        def body(wi, carry):
            window(wi * w_top, w_top)
            return carry
        lax.fori_loop(0, (cnt + (w_top - 1)) // w_top, body, 0)

    @pl.when(e == MOE_GROUPS - 1)
    def _():
        gate = _row_select(i * tm, tm, ctx_len, mod_ref, 5)
        out = x_ref[0] + gate * acc_scr[...]
        if final:
            out = _rms(out) * gf_ref[...]
        o_ref[0] = out


def _moe(xx, g, mod, wr, br, w1, w3, w2, layer, g_final, *, ctx_len):
    bsz, rows, d = xx.shape
    tm = _pick_tile(rows, MOE_TILE_PREFS)
    mid = -(-(tm // MOE_GROUPS) // LANES) * LANES
    wins = (mid, mid + LANES, mid + 2 * LANES)
    eb = EXPERTS_PER_GROUP
    row = lambda b, i, e: (b, i, 0)
    const = lambda b, i, e: (0, 0)
    final = g_final is not None
    return pl.pallas_call(
        functools.partial(_moe_kernel, tm=tm, ctx_len=ctx_len, wins=wins, final=final),
        grid=(bsz, rows // tm, MOE_GROUPS),
        in_specs=[pl.BlockSpec((1, tm, d), row),
                  pl.BlockSpec((1, d), const),
                  pl.BlockSpec((1, 2, 6, d), lambda b, i, e: (b, 0, 0, 0)),
                  pl.BlockSpec((d, 2 * LANES), const),
                  pl.BlockSpec((1, LANES), const),
                  pl.BlockSpec((1, eb, d, D_EXPERT), lambda b, i, e: (layer, e, 0, 0)),
                  pl.BlockSpec((1, eb, d, D_EXPERT), lambda b, i, e: (layer, e, 0, 0)),
                  pl.BlockSpec((1, eb, D_EXPERT, d), lambda b, i, e: (layer, e, 0, 0))]
                 + ([pl.BlockSpec((1, d), const)] if final else []),
        out_specs=pl.BlockSpec((1, tm, d), row),
        out_shape=jax.ShapeDtypeStruct((bsz, rows, d), F32),
        scratch_shapes=[pltpu.VMEM((tm, d), BF16),
                        pltpu.VMEM((MOE_GROUPS, tm, 2 * LANES), BF16),
                        pltpu.VMEM((SUBLANES, tm), F32),
                        pltpu.VMEM((MOE_GROUPS, tm, 1), F32),
                        pltpu.SMEM((MOE_GROUPS,), jnp.int32),
                        pltpu.VMEM((tm, d), F32)],
        compiler_params=_params(("parallel", "arbitrary", "arbitrary")),
        name="moe",
    )(xx, g, mod, wr, br, w1, w3, w2, *((g_final.reshape(1, d),) if final else ()))


def _rope_tables(ctx_len, lat_len):
    rows = lat_len // GRID_W
    row = np.repeat(np.arange(rows), GRID_W).astype(np.float64)
    col = np.tile(np.arange(GRID_W), rows).astype(np.float64)
    n_freq = HEAD_DIM // 4
    inv = ROPE_THETA ** (-np.arange(n_freq, dtype=np.float64) / n_freq)
    ang = np.concatenate([row[:, None] * inv, col[:, None] * inv], axis=-1)
    cos, sin = np.cos(ang), np.sin(ang)
    cc = np.concatenate([np.ones((ctx_len, HEAD_DIM)), np.concatenate([cos, cos], axis=-1)], axis=0)
    ss = np.concatenate([np.zeros((ctx_len, HEAD_DIM)), np.concatenate([-sin, sin], axis=-1)], axis=0)
    return jnp.asarray(cc, F32), jnp.asarray(ss, F32)


def kernel(x, c, ctx, c_ctx, w_mod, b_mod, g_mix, g_ffn, moe_w_grp, moe_b_grp, moe_w_rt, moe_b_rt, moe_w1, moe_w3, moe_w2, ab_w_in, ab_w_out, ssd_conv_w, ssd_conv_b, ssd_dt_bias, ssd_a_log, ssd_d, ssd_norm_g, att_q_g, att_k_g, cd_w_in, cd_w_out, lru_conv_w, lru_conv_b, lru_w_a, lru_b_a, lru_w_x, lru_b_x, lru_lam, swa_sink, g_final):
    bsz, lat_len, d = x.shape
    ctx_len = ctx.shape[1]
    depth = w_mod.shape[0]
    lc = ctx_len + lat_len
    assert d == D_MODEL and ctx_len % ROW_TILE == 0 and lat_len % ROW_TILE == 0 and ctx_len > 0

    cc, ss = _rope_tables(ctx_len, lat_len)
    mods = _modulation(c, c_ctx, w_mod, b_mod)
    expand = jnp.asarray(np.arange(SSD_WIDTH)[None, :] // SSD_HEAD_DIM == np.arange(LANES)[:, None], BF16)
    ones_h = jnp.ones((1, HEAD_DIM), F32)
    w1_b, w3_b, w2_b = moe_w1.astype(BF16), moe_w3.astype(BF16), moe_w2.astype(BF16)
    srcs = (ctx, x)

    for i in range(depth):
        j = i // 2
        last = i == depth - 1
        m = mods[i].reshape(bsz + 1, 6, d)
        mod = jnp.stack([jnp.broadcast_to(m[0], (bsz, 6, d)), m[1:]], axis=1)
        if i % 2 == 0:
            w = ab_w_in[j]
            zc = jnp.zeros((d, LANES - SSD_HEADS), w.dtype)
            o_dt = SSD_WIDTH + SSD_CONV_CH
            o_q = o_dt + 2 * SSD_HEADS
            w_dt = jnp.concatenate([w[:, o_dt:o_dt + SSD_HEADS], zc, w[:, o_dt + SSD_HEADS:o_q], zc], axis=1)
            ws = (w[:, :o_dt].astype(BF16), w_dt.astype(BF16), w[:, o_q:].astype(BF16))
            plain = ((0, SSD_WIDTH), (SSD_WIDTH, SSD_CONV_CH), (o_dt, 2 * LANES))
            z, xbc, dt, q, k, v = _inproj(srcs, lc, g_mix[i].reshape(1, d), mod, ws,
                                          att_q_g[j].reshape(1, HEAD_DIM), att_k_g[j].reshape(1, HEAD_DIM), cc, ss,
                                          ctx_len=ctx_len, plain=plain, q0=o_dt + 2 * LANES, qk_norm=True)
            ya = _ssd(xbc, dt, z, ssd_conv_w[j], ssd_conv_b[j], ssd_dt_bias[j], ssd_a_log[j], ssd_d[j],
                      ssd_norm_g[j], expand, ctx_len=ctx_len)
            yb = _attn_dense(q, k, v, ctx_len=ctx_len, tq=ATT_TQ, tk=_pick_tile(lc, ATT_TK_PREFS))
            w_out = ab_w_out[j]
        else:
            plain = ((0, LRU_WIDTH), (LRU_WIDTH, LRU_WIDTH))
            gate, xr, q, k, v = _inproj(srcs, lc, g_mix[i].reshape(1, d), mod, (cd_w_in[j].astype(BF16),),
                                        ones_h, ones_h, cc, ss,
                                        ctx_len=ctx_len, plain=plain, q0=2 * LRU_WIDTH, qk_norm=False)
            xr = _dwconv(xr, lru_conv_w[j], lru_conv_b[j], ctx_len=ctx_len, silu=False)
            ya = _lru(xr, gate, _lru_slabs(lru_w_a[j]), _lru_slabs(lru_w_x[j]), lru_b_a[j], lru_b_x[j], lru_lam[j],
                      ctx_len=ctx_len)
            yb = _attn_window(q, k, v, swa_sink[j], ctx_len=ctx_len, tq=ATT_TQ)
            w_out = cd_w_out[j]
        xs = _outproj(ya, yb, srcs, mod, w_out.astype(BF16), ctx_len=ctx_len, latent_only=last)
        wr = jnp.concatenate([moe_w_grp[i], moe_w_rt[i],
                              jnp.zeros((d, LANES - MOE_GROUPS - N_EXPERTS), F32)], axis=1)
        br = jnp.concatenate([moe_b_grp[i], moe_b_rt[i],
                              jnp.zeros((LANES - MOE_GROUPS - N_EXPERTS,), F32)]).reshape(1, LANES)
        wr_hi = wr.astype(BF16)
        wr = jnp.concatenate([wr_hi, (wr - wr_hi.astype(F32)).astype(BF16)], axis=1)
        xs = _moe(xs, g_ffn[i].reshape(1, d), mod, wr, br, w1_b, w3_b, w2_b, i,
                  g_final if last else None, ctx_len=0 if last else ctx_len)
        srcs = (xs,)
    return xs
```

```python
import functools
import math

import numpy as np
import jax
import jax.numpy as jnp
from jax import lax
from jax.experimental import pallas as pl
from jax.experimental.pallas import tpu as pltpu

F32 = jnp.float32
BF16 = jnp.bfloat16

D_MODEL = 1024
GRID_W = 64
EPS = 1e-6
HEAD_DIM = 128
ATT_HEADS = 8
ATT_KV_HEADS = 2
GQA_REP = ATT_HEADS // ATT_KV_HEADS
ATT_WIDTH = ATT_HEADS * HEAD_DIM
KV_WIDTH = ATT_KV_HEADS * HEAD_DIM
ATT_SCALE = HEAD_DIM ** -0.5
LOG2E = math.log2(math.e)
ROPE_THETA = 10000.0
WINDOW = 128
SSD_HEADS = 16
SSD_HEAD_DIM = 64
SSD_WIDTH = SSD_HEADS * SSD_HEAD_DIM
SSD_GROUPS = 2
SSD_STATE = 64
SSD_CHUNK = 128
SSD_CONV_CH = SSD_WIDTH + 2 * SSD_GROUPS * SSD_STATE
LRU_WIDTH = 1024
LRU_BLOCKS = 16
LRU_BLOCK_DIM = LRU_WIDTH // LRU_BLOCKS
LRU_C = 8.0
CONV_W = 5
MOE_GROUPS = 4
EXPERTS_PER_GROUP = 4
N_EXPERTS = MOE_GROUPS * EXPERTS_PER_GROUP
D_EXPERT = 256

LANES = 128
SUBLANES = 8
MXU_TILE = 256
LRU_SLAB = MXU_TILE
V7X_VMEM_BYTES = 64 * 1024 * 1024
VMEM_LIMIT = V7X_VMEM_BYTES - 8 * 1024 * 1024
MOD_COL_TILE = 1536
ROW_TILE = 256
MOE_TILE_PREFS = (1024, 768, 512, 256)
ATT_TQ = 256
ATT_TK_PREFS = (2816, 1408, 768, 256)


def _params(sem):
    return pltpu.CompilerParams(dimension_semantics=sem, vmem_limit_bytes=VMEM_LIMIT)


def _pick_tile(n, prefs):
    for t in prefs:
        if n % t == 0:
            return t
    raise ValueError(f"no tile for {n}")


def _sigmoid(x):
    return 0.5 * jnp.tanh(0.5 * x) + 0.5


def _softplus(x):
    return jnp.maximum(x, 0.0) + jnp.log1p(jnp.exp(-jnp.abs(x)))


def _split_bf16(v, n):
    parts = []
    r = v
    for _ in range(n):
        p = r.astype(BF16)
        parts.append(p)
        r = r - p.astype(F32)
    return parts


def _dot_nt(a, b):
    return lax.dot_general(a, b, (((1,), (1,)), ((), ())), preferred_element_type=F32)


def _rms(x):
    return x * lax.rsqrt(jnp.mean(x * x, axis=-1, keepdims=True) + EPS)


def _row_select(row0, tm, ctx_len, mod_ref, k):
    if ctx_len == 0:
        return mod_ref[0, 1, k:k + 1, :]
    rows = row0 + lax.broadcasted_iota(jnp.int32, (tm, 1), 0)
    return jnp.where(rows < ctx_len, mod_ref[0, 0, k:k + 1, :], mod_ref[0, 1, k:k + 1, :])


def _rms_mod(x, g, row0, ctx_len, mod_ref, k_shift, k_scale):
    tm = x.shape[0]
    sh = _row_select(row0, tm, ctx_len, mod_ref, k_shift)
    sc = _row_select(row0, tm, ctx_len, mod_ref, k_scale)
    return _rms(x) * g * (1.0 + sc) + sh


def _scan_order(s, n_ctx, n_all):
    return jnp.where(s < n_ctx, n_ctx - 1 - s, n_all + n_ctx - 1 - s)


def _row_specs(tm, d, ct, split):
    if split:
        return [pl.BlockSpec((1, tm, d), lambda b, i: (b, jnp.minimum(i, ct - 1), 0)),
                pl.BlockSpec((1, tm, d), lambda b, i: (b, jnp.maximum(i - ct, 0), 0))]
    return [pl.BlockSpec((1, tm, d), lambda b, i: (b, i, 0))]


def _load_rows(refs, i, ct):
    if len(refs) == 2:
        return jnp.where(i < ct, refs[0][0], refs[1][0])
    return refs[0][0]


def _mod_kernel(ct_ref, w_ref, b_ref, o_ref, *, rows):
    w = w_ref[0]
    outs = []
    for r in range(rows):
        c = ct_ref[:, r:r + 1]
        act = c * _sigmoid(c)
        outs.append(jnp.sum(w * act, axis=0, keepdims=True) + b_ref[0])
    o_ref[0] = jnp.concatenate(outs, axis=0)


def _modulation(c, c_ctx, w_mod, b_mod):
    depth, d, n = w_mod.shape
    rows = c.shape[0] + 1
    ct = jnp.concatenate([c_ctx[None, :], c], axis=0).T
    tn = MOD_COL_TILE
    return pl.pallas_call(
        functools.partial(_mod_kernel, rows=rows),
        grid=(depth, n // tn),
        in_specs=[pl.BlockSpec((d, rows), lambda i, j: (0, 0)),
                  pl.BlockSpec((1, d, tn), lambda i, j: (i, 0, j)),
                  pl.BlockSpec((1, 1, tn), lambda i, j: (i, 0, j))],
        out_specs=pl.BlockSpec((1, rows, tn), lambda i, j: (i, 0, j)),
        out_shape=jax.ShapeDtypeStruct((depth, rows, n), F32),
        compiler_params=_params(("arbitrary", "arbitrary")),
        name="modulation",
    )(ct, w_mod, b_mod.reshape(depth, 1, n))


def _inproj_kernel(*refs, tm, ctx_len, plain, q0, qk_norm, n_src, n_w):
    x_refs = refs[:n_src]
    g_ref, mod_ref = refs[n_src:n_src + 2]
    w_refs = refs[n_src + 2:n_src + 2 + n_w]
    qg_ref, kg_ref, cc_ref, ss_ref = refs[n_src + 2 + n_w:n_src + 6 + n_w]
    out_refs = refs[n_src + 6 + n_w:]
    i = pl.program_id(1)
    x = _load_rows(x_refs, i, ctx_len // tm)
    h = _rms_mod(x, g_ref[...], i * tm, ctx_len, mod_ref, 0, 1).astype(BF16)
    res = jnp.concatenate([jnp.dot(h, w_ref[...], preferred_element_type=F32) for w_ref in w_refs], axis=1)
    for ref, (c0, cw) in zip(out_refs[:len(plain)], plain):
        ref[0] = res[:, c0:c0 + cw]
    q_ref, k_ref, v_ref = out_refs[len(plain):]
    cc = cc_ref[...]
    ss = ss_ref[...]

    def head(t, gain, scale):
        if qk_norm:
            t = _rms(t) * gain
        t = t * cc + pltpu.roll(t, HEAD_DIM // 2, 1) * ss
        return (t * scale).astype(BF16)

    for hh in range(ATT_HEADS):
        c0 = q0 + hh * HEAD_DIM
        q_ref[0, :, hh * HEAD_DIM:(hh + 1) * HEAD_DIM] = head(res[:, c0:c0 + HEAD_DIM], qg_ref[...],
                                                              ATT_SCALE * LOG2E)
    k0 = q0 + ATT_WIDTH
    for hh in range(ATT_KV_HEADS):
        c0 = k0 + hh * HEAD_DIM
        k_ref[0, :, hh * HEAD_DIM:(hh + 1) * HEAD_DIM] = head(res[:, c0:c0 + HEAD_DIM], kg_ref[...], 1.0)
    v0 = k0 + KV_WIDTH
    v_ref[0] = res[:, v0:v0 + KV_WIDTH].astype(BF16)


def _inproj(srcs, lc, g, mod, ws, qg, kg, cc, ss, *, ctx_len, plain, q0, qk_norm, tm=ROW_TILE):
    bsz, _, d = srcs[0].shape
    row = lambda b, i: (b, i, 0)
    const = lambda b, i: (0, 0)
    out_shape = [jax.ShapeDtypeStruct((bsz, lc, cw), F32) for _, cw in plain]
    out_specs = [pl.BlockSpec((1, tm, cw), row) for _, cw in plain]
    for cw in (ATT_WIDTH, KV_WIDTH, KV_WIDTH):
        out_shape.append(jax.ShapeDtypeStruct((bsz, lc, cw), BF16))
        out_specs.append(pl.BlockSpec((1, tm, cw), row))
    return pl.pallas_call(
        functools.partial(_inproj_kernel, tm=tm, ctx_len=ctx_len, plain=plain, q0=q0, qk_norm=qk_norm,
                          n_src=len(srcs), n_w=len(ws)),
        grid=(bsz, lc // tm),
        in_specs=_row_specs(tm, d, ctx_len // tm, len(srcs) == 2) + [
            pl.BlockSpec((1, d), const),
            pl.BlockSpec((1, 2, 6, d), lambda b, i: (b, 0, 0, 0))] + [
            pl.BlockSpec(w.shape, const) for w in ws] + [
            pl.BlockSpec((1, HEAD_DIM), const),
            pl.BlockSpec((1, HEAD_DIM), const),
            pl.BlockSpec((tm, HEAD_DIM), lambda b, i: (i, 0)),
            pl.BlockSpec((tm, HEAD_DIM), lambda b, i: (i, 0))],
        out_specs=out_specs,
        out_shape=out_shape,
        compiler_params=_params(("parallel", "arbitrary")),
        name="inproj",
    )(*srcs, g, mod, *ws, qg, kg, cc, ss)


def _conv_rows(prev, cur, nxt, w_ref, b_ref, seg_start, seg_end, silu):
    tm, c = cur.shape
    prev = jnp.where(seg_start, 0.0, prev)
    nxt = jnp.where(seg_end, 0.0, nxt)
    ng = tm // SUBLANES
    ext = jnp.concatenate([prev, cur, nxt], axis=0).reshape(ng + 2, SUBLANES, c)
    pos8 = lax.broadcasted_iota(jnp.int32, (1, SUBLANES, 1), 1)
    pad = CONV_W // 2
    acc = b_ref[...] + cur * w_ref[pad:pad + 1, :]
    for k in range(CONV_W):
        d = k - pad
        if d == 0:
            continue
        r = pltpu.roll(ext, (-d) % SUBLANES, 1)
        if d < 0:
            tap = jnp.where(pos8 >= -d, r[1:ng + 1], r[0:ng])
        else:
            tap = jnp.where(pos8 < SUBLANES - d, r[1:ng + 1], r[2:ng + 2])
        acc = acc + tap.reshape(tm, c) * w_ref[k:k + 1, :]
    if silu:
        acc = acc * _sigmoid(acc)
    return acc


def _dwconv_kernel(prev_ref, cur_ref, next_ref, w_ref, b_ref, o_ref, *, nt, ct, silu):
    i = pl.program_id(1)
    seg_start = jnp.logical_or(i == 0, i == ct)
    seg_end = jnp.logical_or(i == ct - 1, i == nt - 1)
    o_ref[0] = _conv_rows(prev_ref[0], cur_ref[0], next_ref[0], w_ref, b_ref, seg_start, seg_end, silu)


def _dwconv(u, w, b, *, ctx_len, silu, tm=ROW_TILE):
    bsz, lc, c = u.shape
    nt = lc // tm
    hb = tm // SUBLANES
    nhb = lc // SUBLANES
    return pl.pallas_call(
        functools.partial(_dwconv_kernel, nt=nt, ct=ctx_len // tm, silu=silu),
        grid=(bsz, nt),
        in_specs=[pl.BlockSpec((1, SUBLANES, c), lambda b_, i: (b_, jnp.maximum(i * hb - 1, 0), 0)),
                  pl.BlockSpec((1, tm, c), lambda b_, i: (b_, i, 0)),
                  pl.BlockSpec((1, SUBLANES, c), lambda b_, i: (b_, jnp.minimum((i + 1) * hb, nhb - 1), 0)),
                  pl.BlockSpec((CONV_W, c), lambda b_, i: (0, 0)),
                  pl.BlockSpec((1, c), lambda b_, i: (0, 0))],
        out_specs=pl.BlockSpec((1, tm, c), lambda b_, i: (b_, i, 0)),
        out_shape=jax.ShapeDtypeStruct((bsz, lc, c), F32),
        compiler_params=_params(("parallel", "arbitrary")),
        name="dwconv",
    )(u, u, u, w, b.reshape(1, c))


def _expand_heads(v, e_ref):
    hi, lo = _split_bf16(v, 2)
    e = e_ref[...]
    return (jnp.dot(hi, e, preferred_element_type=F32) + jnp.dot(lo, e, preferred_element_type=F32))


def _ssd_chunk(xbc, dt_raw, dtb, alog, e_ref, h_prev, reverse):
    q = SSD_CHUNK
    heads_per_group = SSD_HEADS // SSD_GROUPS
    x = xbc[:, :SSD_WIDTH]
    bm = xbc[:, SSD_WIDTH:SSD_WIDTH + LANES]
    cm = xbc[:, SSD_WIDTH + LANES:]
    dt = _softplus(dt_raw + dtb)
    da = dt * (-jnp.exp(alog) * LOG2E)
    ii = lax.broadcasted_iota(jnp.int32, (q, q), 0)
    jj = lax.broadcasted_iota(jnp.int32, (q, q), 1)
    causal = (jj >= ii) if reverse else (jj <= ii)
    mask_b = jnp.where(causal, 1.0, 0.0).astype(BF16)
    da_t = da.T
    dt_t = dt.T
    acs = sum(jnp.dot(mask_b, p, preferred_element_type=F32) for p in _split_bf16(da, 3))
    acs_t = sum(_dot_nt(p, mask_b) for p in _split_bf16(da_t, 3))
    total = jnp.sum(da, axis=0, keepdims=True)
    eacs = jnp.exp2(acs)

    bm_b = bm.astype(BF16)
    lane = lax.broadcasted_iota(jnp.int32, (1, LANES), 1)
    in_group = [jnp.logical_and(lane >= g * SSD_STATE, lane < (g + 1) * SSD_STATE) for g in range(SSD_GROUPS)]
    cb = [_dot_nt(jnp.where(in_group[g], cm, 0.0).astype(BF16), bm_b) for g in range(SSD_GROUPS)]

    x_b = x.astype(BF16)
    h_b = h_prev.astype(BF16)
    pairs = []
    for hp in range(SSD_HEADS // 2):
        rhs = jnp.concatenate([x_b[:, hp * LANES:(hp + 1) * LANES], h_b[:, hp * LANES:(hp + 1) * LANES]], axis=0)
        outs = []
        for t in range(2):
            h = 2 * hp + t
            seg = acs[:, h:h + 1] - acs_t[h:h + 1, :]
            dec = jnp.exp2(jnp.where(causal, seg, -jnp.inf))
            w = cb[h // heads_per_group] * dec * dt_t[h:h + 1, :]
            lhs = jnp.concatenate([w.astype(BF16), (cm * eacs[:, h:h + 1]).astype(BF16)], axis=1)
            outs.append(jnp.dot(lhs, rhs, preferred_element_type=F32))
        pairs.append(jnp.where(lane < SSD_HEAD_DIM, outs[0], outs[1]))
    y = jnp.concatenate(pairs, axis=1)

    to_end = jnp.exp2(total - acs) * dt
    xw = (x * _expand_heads(to_end, e_ref)).astype(BF16)
    st = jnp.dot(bm.T.astype(BF16), xw, preferred_element_type=F32)
    row_g0 = lax.broadcasted_iota(jnp.int32, st.shape, 0) < SSD_STATE
    col_g0 = lax.broadcasted_iota(jnp.int32, st.shape, 1) < heads_per_group * SSD_HEAD_DIM
    chunk_decay = _expand_heads(jnp.broadcast_to(jnp.exp2(total), (SUBLANES, LANES)), e_ref)[0:1]
    return y, h_prev * chunk_decay + jnp.where(row_g0 == col_g0, st, 0.0)


def _ssd_bwd_kernel(prev_ref, cur_ref, next_ref, cw_ref, cb_ref, dt_ref, dtb_ref, alog_ref, e_ref,
                    yb_ref, xc_ref, hst_ref, *, nbx, nb, bsz):
    s = pl.program_id(0)
    blk = _scan_order(s, nbx, nb)

    @pl.when(s == 0)
    def _():
        hst_ref[...] = jnp.zeros_like(hst_ref)

    seg_start = jnp.logical_or(blk == 0, blk == nbx)
    seg_end = jnp.logical_or(blk == nbx - 1, blk == nb - 1)
    xbc, h = [], []
    for bb in range(bsz):
        xbc.append(_conv_rows(prev_ref[bb], cur_ref[bb], next_ref[bb], cw_ref, cb_ref, seg_start, seg_end, True))
        xc_ref[bb] = xbc[bb]
        h.append(hst_ref[bb])
    for sub in reversed(range(cur_ref.shape[1] // SSD_CHUNK)):
        rows = slice(sub * SSD_CHUNK, (sub + 1) * SSD_CHUNK)
        for bb in range(bsz):
            y, h[bb] = _ssd_chunk(xbc[bb][rows], dt_ref[bb, rows, :], dtb_ref[...], alog_ref[...], e_ref, h[bb], True)
            yb_ref[bb, rows, :] = y
    for bb in range(bsz):
        hst_ref[bb] = h[bb]


def _ssd_fwd_kernel(xc_ref, dt_ref, dtb_ref, alog_ref, e_ref, yb_ref, z_ref, dsk_ref, ng_ref,
                    o_ref, hst_ref, *, bsz):
    @pl.when(pl.program_id(0) == 0)
    def _():
        hst_ref[...] = jnp.zeros_like(hst_ref)

    h = [hst_ref[bb] for bb in range(bsz)]
    for sub in range(xc_ref.shape[1] // SSD_CHUNK):
        rows = slice(sub * SSD_CHUNK, (sub + 1) * SSD_CHUNK)
        for bb in range(bsz):
            xbc = xc_ref[bb, rows, :]
            y, h[bb] = _ssd_chunk(xbc, dt_ref[bb, rows, :], dtb_ref[...], alog_ref[...], e_ref, h[bb], False)
            z = z_ref[bb, rows, :]
            y = (y + yb_ref[bb, rows, :] + dsk_ref[...] * xbc[:, :SSD_WIDTH]) * (z * _sigmoid(z))
            o_ref[bb, rows, :] = (_rms(y) * ng_ref[...]).astype(BF16)
    for bb in range(bsz):
        hst_ref[bb] = h[bb]


def _ssd(xbc_raw, dt, z, conv_w, conv_b, dt_bias, a_log, d_skip, norm_g, expand, *, ctx_len, q=ROW_TILE):
    bsz, lc, cch = xbc_raw.shape
    w = SSD_WIDTH
    nb = lc // q
    nbx = ctx_len // q
    hb = q // SUBLANES
    nhb = lc // SUBLANES
    pad = lambda t: jnp.pad(t, (0, LANES - SSD_HEADS)).reshape(1, LANES)
    const = lambda s: (0, 0)
    rev = lambda s: _scan_order(s, nbx, nb)
    state = [pltpu.VMEM((bsz, SSD_GROUPS * SSD_STATE, w), F32)]
    yb, xc = pl.pallas_call(
        functools.partial(_ssd_bwd_kernel, nbx=nbx, nb=nb, bsz=bsz),
        grid=(nb,),
        in_specs=[pl.BlockSpec((bsz, SUBLANES, cch), lambda s: (0, jnp.maximum(rev(s) * hb - 1, 0), 0)),
                  pl.BlockSpec((bsz, q, cch), lambda s: (0, rev(s), 0)),
                  pl.BlockSpec((bsz, SUBLANES, cch), lambda s: (0, jnp.minimum((rev(s) + 1) * hb, nhb - 1), 0)),
                  pl.BlockSpec((CONV_W, cch), const),
                  pl.BlockSpec((1, cch), const),
                  pl.BlockSpec((bsz, q, LANES), lambda s: (0, rev(s), 1)),
                  pl.BlockSpec((1, LANES), const),
                  pl.BlockSpec((1, LANES), const),
                  pl.BlockSpec((LANES, w), const)],
        out_specs=[pl.BlockSpec((bsz, q, w), lambda s: (0, rev(s), 0)),
                   pl.BlockSpec((bsz, q, cch), lambda s: (0, rev(s), 0))],
        out_shape=[jax.ShapeDtypeStruct((bsz, lc, w), F32), jax.ShapeDtypeStruct((bsz, lc, cch), F32)],
        scratch_shapes=state,
        compiler_params=_params(("arbitrary",)),
        name="ssd_bwd",
    )(xbc_raw, xbc_raw, xbc_raw, conv_w, conv_b.reshape(1, cch), dt, pad(dt_bias[1]), pad(a_log[1]), expand)
    row = lambda cols: pl.BlockSpec((bsz, q, cols), lambda s: (0, s, 0))
    return pl.pallas_call(
        functools.partial(_ssd_fwd_kernel, bsz=bsz),
        grid=(nb,),
        in_specs=[row(cch), row(LANES), pl.BlockSpec((1, LANES), const), pl.BlockSpec((1, LANES), const),
                  pl.BlockSpec((LANES, w), const), row(w), row(w), pl.BlockSpec((1, w), const),
                  pl.BlockSpec((1, w), const)],
        out_specs=row(w),
        out_shape=jax.ShapeDtypeStruct((bsz, lc, w), BF16),
        scratch_shapes=state,
        compiler_params=_params(("arbitrary",)),
        name="ssd_fwd",
    )(xc, dt, pad(dt_bias[0]), pad(a_log[0]), expand, yb, z,
      jnp.repeat(d_skip, SSD_HEAD_DIM).reshape(1, w), norm_g.reshape(1, w))


def _stack_heads(q):
    return jnp.concatenate([q[:, r * HEAD_DIM:(r + 1) * HEAD_DIM] for r in range(q.shape[1] // HEAD_DIM)], axis=0)


def _lane_parts(s):
    return [s[:, c * LANES:(c + 1) * LANES] for c in range(s.shape[1] // LANES)]


def _softmax_parts(parts, sink=None):
    pm = parts[0]
    for t in parts[1:]:
        pm = jnp.maximum(pm, t)
    m = jnp.max(pm, axis=-1, keepdims=True)
    if sink is not None:
        m = jnp.maximum(m, sink)
    mb = jnp.broadcast_to(m, pm.shape)
    ps = [jnp.exp2(t - mb) for t in parts]
    lsum = ps[0]
    for t in ps[1:]:
        lsum = lsum + t
    l = jnp.sum(lsum, axis=-1, keepdims=True)
    if sink is not None:
        l = l + jnp.exp2(sink - m)
    return jnp.concatenate([t.astype(BF16) for t in ps], axis=1), l


def _store_heads(o_ref, o, tq, col0=0):
    for r in range(GQA_REP):
        c0 = col0 + r * HEAD_DIM
        o_ref[0, :, c0:c0 + HEAD_DIM] = o[r * tq:(r + 1) * tq].astype(BF16)


def _attn_dense_kernel(q_ref, k_ref, v_ref, o_ref, m_scr, l_scr, acc_scr, *, tq, tk, ctx_len, lk):
    i = pl.program_id(1)
    gw = GQA_REP * HEAD_DIM
    groups = range(ATT_KV_HEADS)
    q4 = [_stack_heads(q_ref[0, :, g * gw:(g + 1) * gw]) for g in groups]
    kcol = [slice(g * HEAD_DIM, (g + 1) * HEAD_DIM) for g in groups]

    @pl.when(i < ctx_len // tq)
    def _():
        for g in groups:
            p, l = _softmax_parts(_lane_parts(_dot_nt(q4[g], k_ref[0, 0:ctx_len, kcol[g]])))
            o = jnp.dot(p, v_ref[0, 0:ctx_len, kcol[g]], preferred_element_type=F32) / l
            _store_heads(o_ref, o, tq, col0=g * gw)

    @pl.when(i >= ctx_len // tq)
    def _():
        m_scr[...] = jnp.full_like(m_scr, -jnp.inf)
        l_scr[...] = jnp.zeros_like(l_scr)
        acc_scr[...] = jnp.zeros_like(acc_scr)

        def body(j, carry):
            off = pl.multiple_of(j * tk, tk)
            for g in groups:
                parts = _lane_parts(_dot_nt(q4[g], k_ref[0, pl.ds(off, tk), kcol[g]]))
                pm = parts[0]
                for t in parts[1:]:
                    pm = jnp.maximum(pm, t)
                m_prev = m_scr[g]
                m_new = jnp.maximum(m_prev, jnp.max(pm, axis=-1, keepdims=True))
                alpha = jnp.exp2(m_prev - m_new)
                ps = [jnp.exp2(t - m_new) for t in parts]
                lsum = ps[0]
                for t in ps[1:]:
                    lsum = lsum + t
                l_scr[g] = alpha * l_scr[g] + lsum
                p = jnp.concatenate([t.astype(BF16) for t in ps], axis=1)
                acc_scr[g] = alpha * acc_scr[g] + jnp.dot(p, v_ref[0, pl.ds(off, tk), kcol[g]],
                                                          preferred_element_type=F32)
                m_scr[g] = m_new
            return carry

        lax.fori_loop(0, lk // tk, body, 0, unroll=True)
        for g in groups:
            _store_heads(o_ref, acc_scr[g] / jnp.sum(l_scr[g], axis=-1, keepdims=True), tq, col0=g * gw)


def _attn_dense(q, k, v, *, ctx_len, tq, tk):
    bsz, lc, _ = q.shape
    rows = GQA_REP * tq
    stat = pltpu.VMEM((ATT_KV_HEADS, rows, LANES), F32)
    return pl.pallas_call(
        functools.partial(_attn_dense_kernel, tq=tq, tk=tk, ctx_len=ctx_len, lk=lc),
        grid=(bsz, lc // tq),
        in_specs=[pl.BlockSpec((1, tq, ATT_WIDTH), lambda b, i: (b, i, 0)),
                  pl.BlockSpec((1, lc, KV_WIDTH), lambda b, i: (b, 0, 0)),
                  pl.BlockSpec((1, lc, KV_WIDTH), lambda b, i: (b, 0, 0))],
        out_specs=pl.BlockSpec((1, tq, ATT_WIDTH), lambda b, i: (b, i, 0)),
        out_shape=jax.ShapeDtypeStruct((bsz, lc, ATT_WIDTH), BF16),
        scratch_shapes=[stat, stat, stat],
        compiler_params=_params(("parallel", "arbitrary")),
        name="attn_dense",
    )(q, k, v)


def _attn_win_kernel(q_ref, k_ref, v_ref, sink_ref, bias_ref, o_ref, *, ctx_len, lat_len, tq):
    n = pl.program_id(1)
    band = tq + 2 * WINDOW
    il = n - ctx_len // tq
    lo = jnp.clip(il * tq - WINDOW, 0, lat_len - band)
    start = pl.multiple_of(ctx_len + lo, WINDOW)
    bias = bias_ref[0]
    gw = GQA_REP * HEAD_DIM
    for g in range(ATT_KV_HEADS):
        kc0 = g * HEAD_DIM
        q4 = _stack_heads(q_ref[0, :, g * gw:(g + 1) * gw])
        s_loc = _dot_nt(q4, k_ref[0, pl.ds(start, band), kc0:kc0 + HEAD_DIM]) + bias
        s_ctx = _dot_nt(q4, k_ref[0, 0:ctx_len, kc0:kc0 + HEAD_DIM])
        p, l = _softmax_parts(_lane_parts(s_loc) + _lane_parts(s_ctx), sink_ref[g])
        o = (jnp.dot(p[:, :band], v_ref[0, pl.ds(start, band), kc0:kc0 + HEAD_DIM], preferred_element_type=F32)
             + jnp.dot(p[:, band:], v_ref[0, 0:ctx_len, kc0:kc0 + HEAD_DIM], preferred_element_type=F32)) / l
        _store_heads(o_ref, o, tq, col0=g * gw)


def _attn_window(q, k, v, sink, *, ctx_len, tq):
    bsz, lc, _ = q.shape
    lat_len = lc - ctx_len
    rows = GQA_REP * tq
    assert tq & (tq - 1) == 0 and ctx_len % tq == 0 and lat_len >= tq + 2 * WINDOW
    sink_col = jnp.repeat(sink.reshape(ATT_KV_HEADS, GQA_REP) * LOG2E, tq, axis=1).reshape(ATT_KV_HEADS, rows, 1)
    band = tq + 2 * WINDOW
    ct = ctx_len // tq
    nlt = lat_len // tq
    r = np.arange(tq)[:, None]
    c = np.arange(band)[None, :]
    masks = [np.abs(c - off - r) <= WINDOW for off in (0, WINDOW, band - tq)] + [np.zeros((tq, band), bool)]
    bias = jnp.asarray(np.tile(np.where(np.stack(masks), 0.0, -np.inf), (1, GQA_REP, 1)), F32)

    def variant(n):
        il = n - ct
        return jnp.where(il < 0, 3, jnp.where(il == 0, 0, jnp.where(il == nlt - 1, 2, 1)))

    return pl.pallas_call(
        functools.partial(_attn_win_kernel, ctx_len=ctx_len, lat_len=lat_len, tq=tq),
        grid=(bsz, lc // tq),
        in_specs=[pl.BlockSpec((1, tq, ATT_WIDTH), lambda b, n: (b, n, 0)),
                  pl.BlockSpec((1, lc, KV_WIDTH), lambda b, n: (b, 0, 0)),
                  pl.BlockSpec((1, lc, KV_WIDTH), lambda b, n: (b, 0, 0)),
                  pl.BlockSpec((ATT_KV_HEADS, rows, 1), lambda b, n: (0, 0, 0)),
                  pl.BlockSpec((1, rows, band), lambda b, n: (variant(n), 0, 0))],
        out_specs=pl.BlockSpec((1, tq, ATT_WIDTH), lambda b, n: (b, n, 0)),
        out_shape=jax.ShapeDtypeStruct((bsz, lc, ATT_WIDTH), BF16),
        compiler_params=_params(("parallel", "arbitrary")),
        name="attn_window",
    )(q, k, v, sink_col, bias)


def _lru_scan_rows(a, b, carry, reverse):
    t, w = a.shape
    ngroups = t // SUBLANES
    a = a.reshape(ngroups, SUBLANES, w)
    b = b.reshape(ngroups, SUBLANES, w)
    pos8 = lax.broadcasted_iota(jnp.int32, (1, SUBLANES, 1), 1)
    k = 1
    while k < SUBLANES:
        if reverse:
            keep = pos8 < SUBLANES - k
            sh = SUBLANES - k
        else:
            keep = pos8 >= k
            sh = k
        a_sh = jnp.where(keep, pltpu.roll(a, sh, 1), 1.0)
        b_sh = jnp.where(keep, pltpu.roll(b, sh, 1), 0.0)
        b = b + a * b_sh
        a = a * a_sh
        k *= 2
    a = a.reshape(t, w)
    b = b.reshape(t, w)
    hs = [None] * ngroups
    for g in (range(ngroups - 1, -1, -1) if reverse else range(ngroups)):
        sl = slice(g * SUBLANES, (g + 1) * SUBLANES)
        hg = a[sl] * carry + b[sl]
        carry = hg[0:1] if reverse else hg[SUBLANES - 1:SUBLANES]
        hs[g] = hg
    return jnp.concatenate(hs, axis=0), carry


def _lru_kernel(*refs, reverse):
    x_ref, wa_ref, wx_ref, ba_ref, bx_ref, lam_ref = refs[:6]
    if reverse:
        o_ref, carry_ref = refs[6:]
    else:
        hb_ref, gate_ref, o_ref, carry_ref = refs[6:]
    s = pl.program_id(1)

    @pl.when(s == 0)
    def _():
        carry_ref[...] = jnp.zeros_like(carry_ref)

    x = x_ref[0]
    x_b = x.astype(BF16)
    nslab = LRU_WIDTH // LRU_SLAB

    def gate(w_ref, b_ref):
        pre = jnp.concatenate(
            [jnp.dot(x_b[:, c * LRU_SLAB:(c + 1) * LRU_SLAB], w_ref[0, c], preferred_element_type=F32)
             for c in range(nslab)], axis=1)
        return _sigmoid(pre + b_ref[0])

    r = gate(wa_ref, ba_ref)
    ig = gate(wx_ref, bx_ref)
    log2_base = (-LRU_C * LOG2E) * _softplus(-lam_ref[0])
    a = jnp.exp2(r * log2_base)
    y = 1.0 - a * a
    b = jnp.where(y > 0.0, y * lax.rsqrt(y), 0.0) * (ig * x)
    h, carry = _lru_scan_rows(a, b, carry_ref[...], reverse)
    carry_ref[...] = carry
    if reverse:
        o_ref[0] = h
    else:
        g = gate_ref[0]
        gelu = 0.5 * g * (1.0 + jnp.tanh(math.sqrt(2.0 / math.pi) * (g + 0.044715 * (g * g * g))))
        o_ref[0] = (gelu * (h + hb_ref[0])).astype(BF16)


def _lru(xr, gate, wa, wx, ba, bx, lam, *, ctx_len, t=ROW_TILE):
    bsz, lc, w = xr.shape
    nt = lc // t
    ct = ctx_len // t
    nslab = w // LRU_SLAB
    vec = lambda v: v.reshape(2, 1, w)

    def call(reverse, extra_in, out_dtype):
        tile = (lambda s: _scan_order(s, ct, nt)) if reverse else (lambda s: s)
        d = 1 if reverse else 0
        row = pl.BlockSpec((1, t, w), lambda b, s: (b, tile(s), 0))
        return pl.pallas_call(
            functools.partial(_lru_kernel, reverse=reverse),
            grid=(bsz, nt),
            in_specs=[row,
                      pl.BlockSpec((1, nslab, LRU_SLAB, LRU_SLAB), lambda b, s: (d, 0, 0, 0)),
                      pl.BlockSpec((1, nslab, LRU_SLAB, LRU_SLAB), lambda b, s: (d, 0, 0, 0)),
                      pl.BlockSpec((1, 1, w), lambda b, s: (d, 0, 0)),
                      pl.BlockSpec((1, 1, w), lambda b, s: (d, 0, 0)),
                      pl.BlockSpec((1, 1, w), lambda b, s: (d, 0, 0))] + [row] * len(extra_in),
            out_specs=row,
            out_shape=jax.ShapeDtypeStruct((bsz, lc, w), out_dtype),
            scratch_shapes=[pltpu.VMEM((1, w), F32)],
            compiler_params=_params(("parallel", "arbitrary")),
            name="lru_bwd" if reverse else "lru_fwd",
        )(xr, wa, wx, vec(ba), vec(bx), vec(lam), *extra_in)

    hb = call(True, (), F32)
    return call(False, (hb, gate), BF16)


def _lru_slabs(w):
    per = LRU_SLAB // LRU_BLOCK_DIM
    nslab = LRU_BLOCKS // per
    w = w.reshape(2, nslab, per, LRU_BLOCK_DIM, LRU_BLOCK_DIM)
    eye = jnp.eye(per, dtype=w.dtype)
    full = jnp.einsum('dspij,pq->dspiqj', w, eye)
    return full.reshape(2, nslab, LRU_SLAB, LRU_SLAB).astype(BF16)


def _outproj_kernel(*refs, tm, ctx_len, t0, n_src):
    ya_ref, yb_ref = refs[:2]
    x_refs = refs[2:2 + n_src]
    mod_ref, w_ref, o_ref = refs[2 + n_src:]
    i = pl.program_id(1) + t0
    half = ya_ref.shape[2]
    acc = (jnp.dot(ya_ref[0], w_ref[0:half, :], preferred_element_type=F32)
           + jnp.dot(yb_ref[0], w_ref[half:, :], preferred_element_type=F32))
    gate = _row_select(i * tm, tm, ctx_len, mod_ref, 2)
    o_ref[0] = _load_rows(x_refs, i, ctx_len // tm) + gate * acc


def _outproj(ya, yb, srcs, mod, w, *, ctx_len, latent_only, tm=ROW_TILE):
    bsz, lc, half = ya.shape
    d = w.shape[1]
    ct = ctx_len // tm
    t0 = ct if latent_only else 0
    row = lambda b, i: (b, i + t0, 0)
    if len(srcs) == 2:
        x_specs = [pl.BlockSpec((1, tm, d), lambda b, i: (b, jnp.minimum(i + t0, ct - 1), 0)),
                   pl.BlockSpec((1, tm, d), lambda b, i: (b, jnp.maximum(i + t0 - ct, 0), 0))]
    else:
        x_specs = [pl.BlockSpec((1, tm, d), row)]
    return pl.pallas_call(
        functools.partial(_outproj_kernel, tm=tm, ctx_len=ctx_len, t0=t0, n_src=len(srcs)),
        grid=(bsz, lc // tm - t0),
        in_specs=[pl.BlockSpec((1, tm, half), row), pl.BlockSpec((1, tm, half), row)] + x_specs + [
            pl.BlockSpec((1, 2, 6, d), lambda b, i: (b, 0, 0, 0)),
            pl.BlockSpec((2 * half, d), lambda b, i: (0, 0))],
        out_specs=pl.BlockSpec((1, tm, d), lambda b, i: (b, i, 0)),
        out_shape=jax.ShapeDtypeStruct((bsz, lc - t0 * tm, d), F32),
        compiler_params=_params(("parallel", "arbitrary")),
        name="outproj",
    )(ya, yb, *srcs, mod, w)


def _first_max(vals):
    m = vals[0]
    for v in vals[1:]:
        m = jnp.maximum(m, v)
    hot = []
    taken = None
    for v in vals:
        hit = v == m
        if taken is None:
            hot.append(hit)
            taken = hit
        else:
            hot.append(jnp.logical_and(hit, jnp.logical_not(taken)))
            taken = jnp.logical_or(taken, hit)
    return m, hot


def _moe_kernel(*refs, tm, ctx_len, wins, final):
    x_ref, g_ref, mod_ref, wr_ref, br_ref, w1_ref, w3_ref, w2_ref = refs[:8]
    if final:
        gf_ref = refs[8]
    o_ref, hn_scr, gate_scr, rank_row_scr, rank_col_scr, cnt_ref, acc_scr = refs[-7:]
    i = pl.program_id(1)
    e = pl.program_id(2)

    @pl.when(e == 0)
    def _():
        hn = _rms_mod(x_ref[0], g_ref[...], i * tm, ctx_len, mod_ref, 3, 4)
        hn_scr[...] = hn.astype(BF16)
        h_hi, h_lo = _split_bf16(hn, 2)
        r_hi = jnp.dot(h_hi, wr_ref[...], preferred_element_type=F32)
        r_lo = jnp.dot(h_lo, wr_ref[...], preferred_element_type=F32)
        logits = (r_hi[:, :LANES] + r_hi[:, LANES:]) + (r_lo[:, :LANES] + r_lo[:, LANES:]) + br_ref[...]
        lt = jnp.concatenate([logits[c * LANES:(c + 1) * LANES, :].T for c in range(tm // LANES)], axis=1)
        gl = [lt[j:j + 1, :] for j in range(MOE_GROUPS)]
        gmax, ghot = _first_max(gl)
        gsum = sum(jnp.exp(v - gmax) for v in gl)
        p_grp = 1.0 / gsum
        elg = []
        for j in range(EXPERTS_PER_GROUP):
            v = jnp.zeros_like(gmax)
            for gi in range(MOE_GROUPS):
                c = MOE_GROUPS + gi * EXPERTS_PER_GROUP + j
                v = jnp.where(ghot[gi], lt[c:c + 1, :], v)
            elg.append(v)
        v1, hot1 = _first_max(elg)
        rest = [jnp.where(hh, -jnp.inf, v) for hh, v in zip(hot1, elg)]
        v2, hot2 = _first_max(rest)
        ex = jnp.exp(v2 - v1)
        w1 = p_grp / (1.0 + ex)
        w2 = p_grp * ex / (1.0 + ex)
        own = [jnp.where(hot1[j], w1, 0.0) + jnp.where(hot2[j], w2, 0.0) for j in range(EXPERTS_PER_GROUP)]
        pad_rows = jnp.zeros((LANES - MOE_GROUPS, tm), F32)
        g_t = jnp.concatenate([jnp.where(h_, 1.0, 0.0) for h_ in ghot] + [pad_rows], axis=0)
        t0 = lax.broadcasted_iota(jnp.int32, (tm, tm), 0)
        t1 = lax.broadcasted_iota(jnp.int32, (tm, tm), 1)
        before = jnp.where(t0 < t1, 1.0, 0.0).astype(BF16)
        rank_t = jnp.where(g_t > 0.0, jnp.dot(g_t.astype(BF16), before, preferred_element_type=F32), -1.0)
        rank_row_scr[...] = rank_t[0:SUBLANES]
        for gi in range(MOE_GROUPS):
            cnt_ref[gi] = jnp.sum(g_t[gi:gi + 1, :]).astype(jnp.int32)
            slab_t = jnp.concatenate([jnp.where(ghot[gi], own[j], 0.0) for j in range(EXPERTS_PER_GROUP)]
                                     + [pad_rows], axis=0)
            for c in range(tm // LANES):
                rows_c = slice(c * LANES, (c + 1) * LANES)
                slab = slab_t[:, rows_c].T
                hi, lo = _split_bf16(slab, 2)
                gate_scr[gi, rows_c, 0:LANES] = hi
                gate_scr[gi, rows_c, LANES:] = lo
                if gi == 0:
                    rank_c = rank_t[:, rows_c].T
                    for gj in range(MOE_GROUPS):
                        rank_col_scr[gj, rows_c, :] = rank_c[:, gj:gj + 1]
        acc_scr[...] = jnp.zeros_like(acc_scr)

    cnt = cnt_ref[e]
    rank_row = rank_row_scr[pl.ds(e, 1), :]
    rank_col = rank_col_scr[e]

    def window(first, w_):
        base = first.astype(F32)
        slot_r = lax.broadcasted_iota(jnp.int32, (w_, 1), 0).astype(F32)
        slot_c = lax.broadcasted_iota(jnp.int32, (1, w_), 1).astype(F32)
        sel = jnp.where(rank_row - base == slot_r, 1.0, 0.0).astype(BF16)
        sel_t = jnp.where(rank_col - base == slot_c, 1.0, 0.0).astype(BF16)
        xs = jnp.dot(sel, hn_scr[...], preferred_element_type=F32).astype(BF16)
        gs = jnp.dot(sel, gate_scr[e], preferred_element_type=F32)
        gsel = gs[:, :LANES] + gs[:, LANES:]
        yw = None
        for kk in range(EXPERTS_PER_GROUP):
            a = jnp.dot(xs, w1_ref[0, kk], preferred_element_type=F32)
            u = jnp.dot(xs, w3_ref[0, kk], preferred_element_type=F32)
            hid = (a * _sigmoid(a)) * u * gsel[:, kk:kk + 1]
            part = jnp.dot(hid.astype(BF16), w2_ref[0, kk], preferred_element_type=F32)
            yw = part if yw is None else yw + part
        acc_scr[...] += jnp.dot(sel_t, yw.astype(BF16), preferred_element_type=F32)

    w_top = wins[-1]
    lower = 0
    for w_ in wins[:-1]:
        pl.when(jnp.logical_and(cnt > lower, cnt <= w_))(functools.partial(window, jnp.int32(0), w_))
        lower = w_

    @pl.when(cnt > lower)
    def _():
        def body(wi, carry):
            window(wi * w_top, w_top)
            return carry
        lax.fori_loop(0, (cnt + (w_top - 1)) // w_top, body, 0)

    @pl.when(e == MOE_GROUPS - 1)
    def _():
        gate = _row_select(i * tm, tm, ctx_len, mod_ref, 5)
        out = x_ref[0] + gate * acc_scr[...]
        if final:
            out = _rms(out) * gf_ref[...]
        o_ref[0] = out


def _moe(xx, g, mod, wr, br, w1, w3, w2, layer, g_final, *, ctx_len):
    bsz, rows, d = xx.shape
    tm = _pick_tile(rows, MOE_TILE_PREFS)
    mid = -(-(tm // MOE_GROUPS) // LANES) * LANES
    wins = (mid, mid + LANES, mid + 2 * LANES)
    eb = EXPERTS_PER_GROUP
    row = lambda b, i, e: (b, i, 0)
    const = lambda b, i, e: (0, 0)
    final = g_final is not None
    return pl.pallas_call(
        functools.partial(_moe_kernel, tm=tm, ctx_len=ctx_len, wins=wins, final=final),
        grid=(bsz, rows // tm, MOE_GROUPS),
        in_specs=[pl.BlockSpec((1, tm, d), row),
                  pl.BlockSpec((1, d), const),
                  pl.BlockSpec((1, 2, 6, d), lambda b, i, e: (b, 0, 0, 0)),
                  pl.BlockSpec((d, 2 * LANES), const),
                  pl.BlockSpec((1, LANES), const),
                  pl.BlockSpec((1, eb, d, D_EXPERT), lambda b, i, e: (layer, e, 0, 0)),
                  pl.BlockSpec((1, eb, d, D_EXPERT), lambda b, i, e: (layer, e, 0, 0)),
                  pl.BlockSpec((1, eb, D_EXPERT, d), lambda b, i, e: (layer, e, 0, 0))]
                 + ([pl.BlockSpec((1, d), const)] if final else []),
        out_specs=pl.BlockSpec((1, tm, d), row),
        out_shape=jax.ShapeDtypeStruct((bsz, rows, d), F32),
        scratch_shapes=[pltpu.VMEM((tm, d), BF16),
                        pltpu.VMEM((MOE_GROUPS, tm, 2 * LANES), BF16),
                        pltpu.VMEM((SUBLANES, tm), F32),
                        pltpu.VMEM((MOE_GROUPS, tm, 1), F32),
                        pltpu.SMEM((MOE_GROUPS,), jnp.int32),
                        pltpu.VMEM((tm, d), F32)],
        compiler_params=_params(("parallel", "arbitrary", "arbitrary")),
        name="moe",
    )(xx, g, mod, wr, br, w1, w3, w2, *((g_final.reshape(1, d),) if final else ()))


def _rope_tables(ctx_len, lat_len):
    rows = lat_len // GRID_W
    row = np.repeat(np.arange(rows), GRID_W).astype(np.float64)
    col = np.tile(np.arange(GRID_W), rows).astype(np.float64)
    n_freq = HEAD_DIM // 4
    inv = ROPE_THETA ** (-np.arange(n_freq, dtype=np.float64) / n_freq)
    ang = np.concatenate([row[:, None] * inv, col[:, None] * inv], axis=-1)
    cos, sin = np.cos(ang), np.sin(ang)
    cc = np.concatenate([np.ones((ctx_len, HEAD_DIM)), np.concatenate([cos, cos], axis=-1)], axis=0)
    ss = np.concatenate([np.zeros((ctx_len, HEAD_DIM)), np.concatenate([-sin, sin], axis=-1)], axis=0)
    return jnp.asarray(cc, F32), jnp.asarray(ss, F32)


def kernel(x, c, ctx, c_ctx, w_mod, b_mod, g_mix, g_ffn, moe_w_grp, moe_b_grp, moe_w_rt, moe_b_rt, moe_w1, moe_w3, moe_w2, ab_w_in, ab_w_out, ssd_conv_w, ssd_conv_b, ssd_dt_bias, ssd_a_log, ssd_d, ssd_norm_g, att_q_g, att_k_g, cd_w_in, cd_w_out, lru_conv_w, lru_conv_b, lru_w_a, lru_b_a, lru_w_x, lru_b_x, lru_lam, swa_sink, g_final):
    bsz, lat_len, d = x.shape
    ctx_len = ctx.shape[1]
    depth = w_mod.shape[0]
    lc = ctx_len + lat_len
    assert d == D_MODEL and ctx_len % ROW_TILE == 0 and lat_len % ROW_TILE == 0 and ctx_len > 0

    cc, ss = _rope_tables(ctx_len, lat_len)
    mods = _modulation(c, c_ctx, w_mod, b_mod)
    expand = jnp.asarray(np.arange(SSD_WIDTH)[None, :] // SSD_HEAD_DIM == np.arange(LANES)[:, None], BF16)
    ones_h = jnp.ones((1, HEAD_DIM), F32)
    w1_b, w3_b, w2_b = moe_w1.astype(BF16), moe_w3.astype(BF16), moe_w2.astype(BF16)
    srcs = (ctx, x)

    for i in range(depth):
        j = i // 2
        last = i == depth - 1
        m = mods[i].reshape(bsz + 1, 6, d)
        mod = jnp.stack([jnp.broadcast_to(m[0], (bsz, 6, d)), m[1:]], axis=1)
        if i % 2 == 0:
            w = ab_w_in[j]
            zc = jnp.zeros((d, LANES - SSD_HEADS), w.dtype)
            o_dt = SSD_WIDTH + SSD_CONV_CH
            o_q = o_dt + 2 * SSD_HEADS
            w_dt = jnp.concatenate([w[:, o_dt:o_dt + SSD_HEADS], zc, w[:, o_dt + SSD_HEADS:o_q], zc], axis=1)
            ws = (w[:, :o_dt].astype(BF16), w_dt.astype(BF16), w[:, o_q:].astype(BF16))
            plain = ((0, SSD_WIDTH), (SSD_WIDTH, SSD_CONV_CH), (o_dt, 2 * LANES))
            z, xbc, dt, q, k, v = _inproj(srcs, lc, g_mix[i].reshape(1, d), mod, ws,
                                          att_q_g[j].reshape(1, HEAD_DIM), att_k_g[j].reshape(1, HEAD_DIM), cc, ss,
                                          ctx_len=ctx_len, plain=plain, q0=o_dt + 2 * LANES, qk_norm=True)
            ya = _ssd(xbc, dt, z, ssd_conv_w[j], ssd_conv_b[j], ssd_dt_bias[j], ssd_a_log[j], ssd_d[j],
                      ssd_norm_g[j], expand, ctx_len=ctx_len)
            yb = _attn_dense(q, k, v, ctx_len=ctx_len, tq=ATT_TQ, tk=_pick_tile(lc, ATT_TK_PREFS))
            w_out = ab_w_out[j]
        else:
            plain = ((0, LRU_WIDTH), (LRU_WIDTH, LRU_WIDTH))
            gate, xr, q, k, v = _inproj(srcs, lc, g_mix[i].reshape(1, d), mod, (cd_w_in[j].astype(BF16),),
                                        ones_h, ones_h, cc, ss,
                                        ctx_len=ctx_len, plain=plain, q0=2 * LRU_WIDTH, qk_norm=False)
            xr = _dwconv(xr, lru_conv_w[j], lru_conv_b[j], ctx_len=ctx_len, silu=False)
            ya = _lru(xr, gate, _lru_slabs(lru_w_a[j]), _lru_slabs(lru_w_x[j]), lru_b_a[j], lru_b_x[j], lru_lam[j],
                      ctx_len=ctx_len)
            yb = _attn_window(q, k, v, swa_sink[j], ctx_len=ctx_len, tq=ATT_TQ)
            w_out = cd_w_out[j]
        xs = _outproj(ya, yb, srcs, mod, w_out.astype(BF16), ctx_len=ctx_len, latent_only=last)
        wr = jnp.concatenate([moe_w_grp[i], moe_w_rt[i],
                              jnp.zeros((d, LANES - MOE_GROUPS - N_EXPERTS), F32)], axis=1)
        br = jnp.concatenate([moe_b_grp[i], moe_b_rt[i],
                              jnp.zeros((LANES - MOE_GROUPS - N_EXPERTS,), F32)]).reshape(1, LANES)
        wr_hi = wr.astype(BF16)
        wr = jnp.concatenate([wr_hi, (wr - wr_hi.astype(F32)).astype(BF16)], axis=1)
        xs = _moe(xs, g_ffn[i].reshape(1, d), mod, wr, br, w1_b, w3_b, w2_b, i,
                  g_final if last else None, ctx_len=0 if last else ctx_len)
        srcs = (xs,)
    return xs
```
